```python
import jax
import jax.numpy as jnp
from jax import lax
import numpy as np

D_MODEL = 1024
BATCH = 16
SEQ = 4096
DEPTH = 2

CTX_LEN = 256
GRID_W = 64
HEAD_DIM = 64
A_HEADS = 8
A_KV_HEADS = 2
A_GROUP = A_HEADS // A_KV_HEADS
B_HEADS = 4
B_Q_RANK = 192
B_KV_RANK = 128
B_NOPE = 64
B_ROPE = 32
B_V = 64
C_HEADS = 4
C_DK = 64
C_DV = 64
D_FF = 4 * D_MODEL
A_OUT = A_HEADS * HEAD_DIM
B_OUT = B_HEADS * B_V
C_OUT = C_HEADS * C_DV
D_MIX = A_OUT + B_OUT + C_OUT
IN_SIZES = (A_HEADS * HEAD_DIM, A_KV_HEADS * HEAD_DIM, A_KV_HEADS * HEAD_DIM,
            B_Q_RANK, B_KV_RANK, B_ROPE,
            C_HEADS * C_DK, C_HEADS * C_DK, C_HEADS * C_DK, C_HEADS * C_DV, C_HEADS * C_DV)
D_IN = sum(IN_SIZES)
SPLIT_AT = tuple(int(v) for v in np.cumsum(IN_SIZES)[:-1])
Q_BLOCK = 128
SCAN_CHUNK = 64
ROPE_THETA = 10000.0
EPS = 1e-6
F_TINY = 1e-30
F32 = jnp.float32

kernel_name = 'hybrid_parallel_heads_dit_block'


def rms_norm(x, gain):
    xf = x.astype(F32)
    y = xf * lax.rsqrt(jnp.mean(xf * xf, axis=-1, keepdims=True) + EPS)
    return (y * gain.astype(F32)).astype(x.dtype)


def modulate(h, shift, scale):
    return h * (1 + scale) + shift


def axial_rope(row_ids, col_ids, rot_dim):
    n_freq = rot_dim // 4
    inv = ROPE_THETA ** (-jnp.arange(n_freq, dtype=F32) / n_freq)
    ang = jnp.concatenate([row_ids.astype(F32)[:, None] * inv, col_ids.astype(F32)[:, None] * inv], axis=-1)
    return jnp.cos(ang)[:, None, :], jnp.sin(ang)[:, None, :]


def apply_rope(x, rope):
    cos, sin = rope
    xf = x.astype(F32)
    x1, x2 = jnp.split(xf, 2, axis=-1)
    return jnp.concatenate([x1 * cos - x2 * sin, x2 * cos + x1 * sin], axis=-1).astype(x.dtype)


def forget_gate(z, lb):
    zf = z.astype(F32)
    f = lb + (1.0 - lb) * jax.nn.sigmoid(zf)
    log_f = jnp.log(jnp.maximum(f, F_TINY))
    k = (1.0 - lb) * jax.nn.sigmoid(-zf)
    return log_f, k.astype(z.dtype)


def block_attention(q, k, v, scale):
    bsz, t = q.shape[:2]
    nb = t // Q_BLOCK
    qb = jnp.moveaxis(q.reshape(bsz, nb, Q_BLOCK, *q.shape[2:]), 1, 0)

    def one_block(qi):
        s = jnp.einsum('bqhgd,bkhd->bhgqk', qi, k, preferred_element_type=F32) * scale
        p = jax.nn.softmax(s, axis=-1).astype(v.dtype)
        return jnp.einsum('bhgqk,bkhv->bqhgv', p, v)

    out = lax.map(one_block, qb)
    return jnp.moveaxis(out, 0, 1).reshape(bsz, t, *out.shape[3:])


def gla_chunk_scan(q, k, v, g, s0):
    bsz, n_tok, n_h, dk = q.shape
    dv = v.shape[-1]
    n_chunk = n_tok // SCAN_CHUNK

    def to_chunks(a):
        return a.astype(F32).reshape(bsz, n_chunk, SCAN_CHUNK, n_h, a.shape[-1]).transpose(1, 0, 3, 2, 4)

    lower_tri = jnp.tril(jnp.ones((SCAN_CHUNK, SCAN_CHUNK), dtype=bool))[:, :, None]

    def step(state, inp):
        q_, k_, v_, g_ = inp
        b = jnp.cumsum(g_, axis=-2)
        diff = b[..., :, None, :] - b[..., None, :, :]
        decay = jnp.where(lower_tri, jnp.exp(jnp.where(lower_tri, diff, 0.0)), 0.0)
        scores = jnp.einsum('bhtd,bhsd,bhtsd->bhts', q_, k_, decay)
        o = jnp.einsum('bhts,bhsv->bhtv', scores, v_) + jnp.einsum('bhtd,bhdv->bhtv', q_ * jnp.exp(b), state)
        b_last = b[..., -1:, :]
        new_state = state * jnp.exp(b_last[..., 0, :])[..., None] + jnp.einsum('bhsd,bhsv->bhdv', k_ * jnp.exp(b_last - b), v_)
        return new_state, o

    s_fin, o = lax.scan(step, s0, (to_chunks(q), to_chunks(k), to_chunks(v), to_chunks(g)))
    o = o.transpose(1, 0, 3, 2, 4).reshape(bsz, n_tok, n_h, dv)
    return o.astype(v.dtype), s_fin


def mixer_features(h, w_in, a_q_norm, a_k_norm, b_q_norm, w_q_up, b_kv_norm, w_kv_up, lb, rope_a, rope_b):
    bsz, t, _ = h.shape
    aq, ak, av, bqd, bkvd, bkr, cq, cff, cfb, ci, cg = jnp.split(h @ w_in, SPLIT_AT, axis=-1)
    aq = rms_norm(aq.reshape(bsz, t, A_HEADS, HEAD_DIM), a_q_norm)
    ak = rms_norm(ak.reshape(bsz, t, A_KV_HEADS, HEAD_DIM), a_k_norm)
    av = av.reshape(bsz, t, A_KV_HEADS, HEAD_DIM)
    bq = (rms_norm(bqd, b_q_norm) @ w_q_up).reshape(bsz, t, B_HEADS, B_NOPE + B_ROPE)
    bkv = (rms_norm(bkvd, b_kv_norm) @ w_kv_up).reshape(bsz, t, B_HEADS, B_NOPE + B_V)
    bq_nope, bq_pe = jnp.split(bq, [B_NOPE], axis=-1)
    bk_nope, bv = jnp.split(bkv, [B_NOPE], axis=-1)
    bk_pe = bkr[:, :, None, :]
    if rope_a is not None:
        aq = apply_rope(aq, rope_a)
        ak = apply_rope(ak, rope_a)
        bq_pe = apply_rope(bq_pe, rope_b)
        bk_pe = apply_rope(bk_pe, rope_b)
    bq = jnp.concatenate([bq_nope, bq_pe], axis=-1)
    bk = jnp.concatenate([bk_nope, jnp.broadcast_to(bk_pe, (bsz, t, B_HEADS, B_ROPE))], axis=-1)
    heads = lambda a: a.reshape(bsz, t, C_HEADS, -1)
    g_f, k_f = forget_gate(cff, lb[0])
    g_b, k_b = forget_gate(cfb, lb[1])
    return {'aq': aq, 'ak': ak, 'av': av, 'bq': bq, 'bk': bk, 'bv': bv,
            'cq': heads(jax.nn.silu(cq)), 'cv': heads(ci), 'cgate': heads(cg),
            'ck_f': heads(k_f), 'cg_f': heads(g_f), 'ck_b': heads(k_b), 'cg_b': heads(g_b)}


def attention_groups(fq, fkv_list):
    bsz, t = fq['aq'].shape[:2]
    cat = lambda name: jnp.concatenate([f[name] for f in fkv_list], axis=1)
    ya = block_attention(fq['aq'].reshape(bsz, t, A_KV_HEADS, A_GROUP, HEAD_DIM), cat('ak'), cat('av'), HEAD_DIM ** -0.5)
    yb = block_attention(fq['bq'][:, :, :, None, :], cat('bk'), cat('bv'), (B_NOPE + B_ROPE) ** -0.5)
    return ya.reshape(bsz, t, A_OUT), yb.reshape(bsz, t, B_OUT)


def hgrn2_bidirectional(fl, fc, out_norm, need_ctx_out):
    bsz = fl['cq'].shape[0]
    zero = jnp.zeros((bsz, C_HEADS, C_DK, C_DV), F32)
    rev = lambda a: jnp.flip(a, axis=1)
    oc_f, s_f = gla_chunk_scan(fc['cq'], fc['ck_f'], fc['cv'], fc['cg_f'], zero)
    oc_b, s_b = gla_chunk_scan(rev(fc['cq']), rev(fc['ck_b']), rev(fc['cv']), rev(fc['cg_b']), zero)
    ol_f, _ = gla_chunk_scan(fl['cq'], fl['ck_f'], fl['cv'], fl['cg_f'], s_f)
    ol_b, _ = gla_chunk_scan(rev(fl['cq']), rev(fl['ck_b']), rev(fl['cv']), rev(fl['cg_b']), s_b)

    def readout(o, gate):
        y = rms_norm(o, out_norm) * jax.nn.silu(gate)
        return y.reshape(*y.shape[:2], C_OUT)

    y_lat = readout(ol_f + rev(ol_b), fl['cgate'])
    y_ctx = readout(oc_f + rev(oc_b), fc['cgate']) if need_ctx_out else None
    return y_lat, y_ctx


def sq_relu_mlp(h, w1, w2):
    return jnp.square(jax.nn.relu(h @ w1)) @ w2


def _fwd_setup_inputs(seed: int = 0) -> dict:
    key = jax.random.key(seed)
    ks = jax.random.split(key, 24)
    nrm = lambda k, shape, scale: jax.random.normal(k, shape, F32) * scale
    gain = lambda k, shape: 1.0 + 0.05 * jax.random.normal(k, shape, F32)
    return {
        'x': nrm(ks[0], (BATCH, SEQ, D_MODEL), 1.0),
        'c': nrm(ks[1], (BATCH, D_MODEL), 1.0),
        'ctx': nrm(ks[2], (BATCH, CTX_LEN, D_MODEL), 1.0),
        'c_ctx': nrm(ks[3], (D_MODEL,), 1.0),
        'w_ada': nrm(ks[4], (DEPTH, D_MODEL, 6 * D_MODEL), 0.5 * D_MODEL ** -0.5),
        'b_ada': nrm(ks[5], (DEPTH, 6 * D_MODEL), 0.02),
        'g_pre_mix': gain(ks[6], (DEPTH, D_MODEL)),
        'g_post_mix': gain(ks[7], (DEPTH, D_MODEL)),
        'g_pre_ffn': gain(ks[8], (DEPTH, D_MODEL)),
        'g_post_ffn': gain(ks[9], (DEPTH, D_MODEL)),
        'w_in': nrm(ks[10], (DEPTH, D_MODEL, D_IN), D_MODEL ** -0.5),
        'a_q_norm': gain(ks[11], (DEPTH, HEAD_DIM)),
        'a_k_norm': gain(ks[12], (DEPTH, HEAD_DIM)),
        'b_q_norm': gain(ks[13], (DEPTH, B_Q_RANK)),
        'w_q_up': nrm(ks[14], (DEPTH, B_Q_RANK, B_HEADS * (B_NOPE + B_ROPE)), B_Q_RANK ** -0.5),
        'b_kv_norm': gain(ks[15], (DEPTH, B_KV_RANK)),
        'w_kv_up': nrm(ks[16], (DEPTH, B_KV_RANK, B_HEADS * (B_NOPE + B_V)), B_KV_RANK ** -0.5),
        'c_lower_bounds': nrm(ks[17], (DEPTH, 2, C_HEADS * C_DK), 0.5),
        'c_out_norm': gain(ks[18], (DEPTH, C_DV)),
        'w_out': nrm(ks[19], (DEPTH, D_MIX, D_MODEL), D_MIX ** -0.5),
        'w_ff1': nrm(ks[20], (DEPTH, D_MODEL, D_FF), D_MODEL ** -0.5),
        'w_ff2': nrm(ks[21], (DEPTH, D_FF, D_MODEL), D_FF ** -0.5),
    }


def _fwd_reference(x, c, ctx, c_ctx, w_ada, b_ada, g_pre_mix, g_post_mix, g_pre_ffn, g_post_ffn,
              w_in, a_q_norm, a_k_norm, b_q_norm, w_q_up, b_kv_norm, w_kv_up,
              c_lower_bounds, c_out_norm, w_out, w_ff1, w_ff2):
    n_lat = x.shape[1]
    rows = n_lat // GRID_W
    row_ids = jnp.repeat(jnp.arange(rows, dtype=jnp.int32), GRID_W)
    col_ids = jnp.tile(jnp.arange(GRID_W, dtype=jnp.int32), rows)
    rope_a = axial_rope(row_ids, col_ids, HEAD_DIM)
    rope_b = axial_rope(row_ids, col_ids, B_ROPE)
    p_lb = jax.nn.softmax(c_lower_bounds.astype(F32), axis=0)
    lower = jnp.cumsum(p_lb, axis=0) - p_lb[:1]

    xc = ctx
    for l in range(DEPTH):
        need_ctx = l < DEPTH - 1
        sh_m, sc_m, gt_m, sh_f, sc_f, gt_f = jnp.split((jax.nn.silu(c) @ w_ada[l] + b_ada[l])[:, None, :], 6, axis=-1)
        csh_m, csc_m, cgt_m, csh_f, csc_f, cgt_f = jnp.split((jax.nn.silu(c_ctx) @ w_ada[l] + b_ada[l])[None, None, :], 6, axis=-1)
        feat_args = (w_in[l], a_q_norm[l], a_k_norm[l], b_q_norm[l], w_q_up[l], b_kv_norm[l], w_kv_up[l], lower[l])

        f_lat = mixer_features(modulate(rms_norm(x, g_pre_mix[l]), sh_m, sc_m), *feat_args, rope_a, rope_b)
        f_ctx = mixer_features(modulate(rms_norm(xc, g_pre_mix[l]), csh_m, csc_m), *feat_args, None, None)
        ya, yb = attention_groups(f_lat, [f_lat, f_ctx])
        yc, yc_ctx = hgrn2_bidirectional(f_lat, f_ctx, c_out_norm[l], need_ctx)
        x = x + gt_m * rms_norm(jnp.concatenate([ya, yb, yc], axis=-1) @ w_out[l], g_post_mix[l])

        x = x + gt_f * rms_norm(sq_relu_mlp(modulate(rms_norm(x, g_pre_ffn[l]), sh_f, sc_f), w_ff1[l], w_ff2[l]), g_post_ffn[l])

        if need_ctx:
            ya_c, yb_c = attention_groups(f_ctx, [f_ctx])
            xc = xc + cgt_m * rms_norm(jnp.concatenate([ya_c, yb_c, yc_ctx], axis=-1) @ w_out[l], g_post_mix[l])
            xc = xc + cgt_f * rms_norm(sq_relu_mlp(modulate(rms_norm(xc, g_pre_ffn[l]), csh_f, csc_f), w_ff1[l], w_ff2[l]), g_post_ffn[l])
    return x


import jax as _jax
import jax.numpy as _jnp

TWIN_FORMAT = 'train_step'
FWD_PARAMS = ['x', 'c', 'ctx', 'c_ctx', 'w_ada', 'b_ada', 'g_pre_mix', 'g_post_mix', 'g_pre_ffn', 'g_post_ffn', 'w_in', 'a_q_norm', 'a_k_norm', 'b_q_norm', 'w_q_up', 'b_kv_norm', 'w_kv_up', 'c_lower_bounds', 'c_out_norm', 'w_out', 'w_ff1', 'w_ff2']
TWIN_WEIGHTS = ['c_ctx', 'w_ada', 'b_ada', 'g_pre_mix', 'g_post_mix', 'g_pre_ffn', 'g_post_ffn', 'w_in', 'a_q_norm', 'a_k_norm', 'b_q_norm', 'w_q_up', 'b_kv_norm', 'w_kv_up', 'c_lower_bounds', 'c_out_norm', 'w_out', 'w_ff1', 'w_ff2']
TWIN_DIFF_INPUT = 'x'
TWIN_INPUTS = ['x', 'c', 'ctx', 'c_ctx', 'w_ada', 'b_ada', 'g_pre_mix', 'g_post_mix', 'g_pre_ffn', 'g_post_ffn', 'w_in', 'a_q_norm', 'a_k_norm', 'b_q_norm', 'w_q_up', 'b_kv_norm', 'w_kv_up', 'c_lower_bounds', 'c_out_norm', 'w_out', 'w_ff1', 'w_ff2', 'loss_target', 'm_c_ctx', 'm_w_ada', 'm_b_ada', 'm_g_pre_mix', 'm_g_post_mix', 'm_g_pre_ffn', 'm_g_post_ffn', 'm_w_in', 'm_a_q_norm', 'm_a_k_norm', 'm_b_q_norm', 'm_w_q_up', 'm_b_kv_norm', 'm_w_kv_up', 'm_c_lower_bounds', 'm_c_out_norm', 'm_w_out', 'm_w_ff1', 'm_w_ff2', 'v_c_ctx', 'v_w_ada', 'v_b_ada', 'v_g_pre_mix', 'v_g_post_mix', 'v_g_pre_ffn', 'v_g_post_ffn', 'v_w_in', 'v_a_q_norm', 'v_a_k_norm', 'v_b_q_norm', 'v_w_q_up', 'v_b_kv_norm', 'v_w_kv_up', 'v_c_lower_bounds', 'v_c_out_norm', 'v_w_out', 'v_w_ff1', 'v_w_ff2']
TWIN_OUTPUTS = ['loss', 'grad_x', 'grad_c_ctx', 'grad_w_ada', 'grad_b_ada', 'grad_g_pre_mix', 'grad_g_post_mix', 'grad_g_pre_ffn', 'grad_g_post_ffn', 'grad_w_in', 'grad_a_q_norm', 'grad_a_k_norm', 'grad_b_q_norm', 'grad_w_q_up', 'grad_b_kv_norm', 'grad_w_kv_up', 'grad_c_lower_bounds', 'grad_c_out_norm', 'grad_w_out', 'grad_w_ff1', 'grad_w_ff2', 'delta_c_ctx', 'delta_w_ada', 'delta_b_ada', 'delta_g_pre_mix', 'delta_g_post_mix', 'delta_g_pre_ffn', 'delta_g_post_ffn', 'delta_w_in', 'delta_a_q_norm', 'delta_a_k_norm', 'delta_b_q_norm', 'delta_w_q_up', 'delta_b_kv_norm', 'delta_w_kv_up', 'delta_c_lower_bounds', 'delta_c_out_norm', 'delta_w_out', 'delta_w_ff1', 'delta_w_ff2', 'new_m_c_ctx', 'new_m_w_ada', 'new_m_b_ada', 'new_m_g_pre_mix', 'new_m_g_post_mix', 'new_m_g_pre_ffn', 'new_m_g_post_ffn', 'new_m_w_in', 'new_m_a_q_norm', 'new_m_a_k_norm', 'new_m_b_q_norm', 'new_m_w_q_up', 'new_m_b_kv_norm', 'new_m_w_kv_up', 'new_m_c_lower_bounds', 'new_m_c_out_norm', 'new_m_w_out', 'new_m_w_ff1', 'new_m_w_ff2', 'new_v_c_ctx', 'new_v_w_ada', 'new_v_b_ada', 'new_v_g_pre_mix', 'new_v_g_post_mix', 'new_v_g_pre_ffn', 'new_v_g_post_ffn', 'new_v_w_in', 'new_v_a_q_norm', 'new_v_a_k_norm', 'new_v_b_q_norm', 'new_v_w_q_up', 'new_v_b_kv_norm', 'new_v_w_kv_up', 'new_v_c_lower_bounds', 'new_v_c_out_norm', 'new_v_w_out', 'new_v_w_ff1', 'new_v_w_ff2']
TWIN_LEAF_KINDS = {'loss': 'loss', 'grad_x': 'grad_x', 'grad_c_ctx': 'grad_w', 'grad_w_ada': 'grad_w', 'grad_b_ada': 'grad_w', 'grad_g_pre_mix': 'grad_w', 'grad_g_post_mix': 'grad_w', 'grad_g_pre_ffn': 'grad_w', 'grad_g_post_ffn': 'grad_w', 'grad_w_in': 'grad_w', 'grad_a_q_norm': 'grad_w', 'grad_a_k_norm': 'grad_w', 'grad_b_q_norm': 'grad_w', 'grad_w_q_up': 'grad_w', 'grad_b_kv_norm': 'grad_w', 'grad_w_kv_up': 'grad_w', 'grad_c_lower_bounds': 'grad_w', 'grad_c_out_norm': 'grad_w', 'grad_w_out': 'grad_w', 'grad_w_ff1': 'grad_w', 'grad_w_ff2': 'grad_w', 'delta_c_ctx': 'delta_w', 'delta_w_ada': 'delta_w', 'delta_b_ada': 'delta_w', 'delta_g_pre_mix': 'delta_w', 'delta_g_post_mix': 'delta_w', 'delta_g_pre_ffn': 'delta_w', 'delta_g_post_ffn': 'delta_w', 'delta_w_in': 'delta_w', 'delta_a_q_norm': 'delta_w', 'delta_a_k_norm': 'delta_w', 'delta_b_q_norm': 'delta_w', 'delta_w_q_up': 'delta_w', 'delta_b_kv_norm': 'delta_w', 'delta_w_kv_up': 'delta_w', 'delta_c_lower_bounds': 'delta_w', 'delta_c_out_norm': 'delta_w', 'delta_w_out': 'delta_w', 'delta_w_ff1': 'delta_w', 'delta_w_ff2': 'delta_w', 'new_m_c_ctx': 'new_m', 'new_m_w_ada': 'new_m', 'new_m_b_ada': 'new_m', 'new_m_g_pre_mix': 'new_m', 'new_m_g_post_mix': 'new_m', 'new_m_g_pre_ffn': 'new_m', 'new_m_g_post_ffn': 'new_m', 'new_m_w_in': 'new_m', 'new_m_a_q_norm': 'new_m', 'new_m_a_k_norm': 'new_m', 'new_m_b_q_norm': 'new_m', 'new_m_w_q_up': 'new_m', 'new_m_b_kv_norm': 'new_m', 'new_m_w_kv_up': 'new_m', 'new_m_c_lower_bounds': 'new_m', 'new_m_c_out_norm': 'new_m', 'new_m_w_out': 'new_m', 'new_m_w_ff1': 'new_m', 'new_m_w_ff2': 'new_m', 'new_v_c_ctx': 'new_v', 'new_v_w_ada': 'new_v', 'new_v_b_ada': 'new_v', 'new_v_g_pre_mix': 'new_v', 'new_v_g_post_mix': 'new_v', 'new_v_g_pre_ffn': 'new_v', 'new_v_g_post_ffn': 'new_v', 'new_v_w_in': 'new_v', 'new_v_a_q_norm': 'new_v', 'new_v_a_k_norm': 'new_v', 'new_v_b_q_norm': 'new_v', 'new_v_w_q_up': 'new_v', 'new_v_b_kv_norm': 'new_v', 'new_v_w_kv_up': 'new_v', 'new_v_c_lower_bounds': 'new_v', 'new_v_c_out_norm': 'new_v', 'new_v_w_out': 'new_v', 'new_v_w_ff1': 'new_v', 'new_v_w_ff2': 'new_v'}


def _forward(args):
    return _fwd_reference(*[args[k] for k in FWD_PARAMS])


def _output_shape():
    out = _jax.eval_shape(lambda: _forward(_fwd_setup_inputs(0)))
    return out.shape, out.dtype

N_MICROBATCH = 1
ADAM_LR = 0.001
ADAM_B1 = 0.9
ADAM_B2 = 0.999
ADAM_EPS = 1e-08
ADAM_WD = 0.01
ADAM_STEP = 10
PER_EXAMPLE_BATCH_AXIS = {'x': 0, 'c': 0, 'ctx': 0, 'loss_target': 0}
SHARED_INPUTS = []
_WEIGHT_DTYPES = {'c_ctx': _jnp.float32, 'w_ada': _jnp.float32, 'b_ada': _jnp.float32, 'g_pre_mix': _jnp.float32, 'g_post_mix': _jnp.float32, 'g_pre_ffn': _jnp.float32, 'g_post_ffn': _jnp.float32, 'w_in': _jnp.float32, 'a_q_norm': _jnp.float32, 'a_k_norm': _jnp.float32, 'b_q_norm': _jnp.float32, 'w_q_up': _jnp.float32, 'b_kv_norm': _jnp.float32, 'w_kv_up': _jnp.float32, 'c_lower_bounds': _jnp.float32, 'c_out_norm': _jnp.float32, 'w_out': _jnp.float32, 'w_ff1': _jnp.float32, 'w_ff2': _jnp.float32}
MOMENT_SCALE = {'c_ctx': 2.916438e-01, 'w_ada': 5.039154e+00, 'b_ada': 9.131040e+00, 'g_pre_mix': 8.264251e-01, 'g_post_mix': 8.441950e+00, 'g_pre_ffn': 4.585704e-01, 'g_post_ffn': 8.242968e+00, 'w_in': 1.856833e+00, 'a_q_norm': 1.946901e-01, 'a_k_norm': 1.864240e-01, 'b_q_norm': 2.490899e-01, 'w_q_up': 1.528836e-01, 'b_kv_norm': 3.651285e+00, 'w_kv_up': 1.841327e+00, 'c_lower_bounds': 1.703590e-02, 'c_out_norm': 2.473226e+00, 'w_out': 2.597129e+00, 'w_ff1': 4.233895e-01, 'w_ff2': 2.064561e+00}


def _to_microbatches(a, axis):
    t = _jnp.moveaxis(a, axis, 0)
    t = t.reshape((N_MICROBATCH, t.shape[0] // N_MICROBATCH) + t.shape[1:])
    return _jnp.moveaxis(t, 1, axis + 1)


def setup_inputs(seed: int = 0) -> dict:
    inp = _fwd_setup_inputs(seed)
    key = _jax.random.fold_in(_jax.random.key(seed), 7919)
    shape, _ = _output_shape()
    out = dict(inp)
    out["loss_target"] = _jax.random.normal(_jax.random.fold_in(key, 0), shape, _jnp.float32)
    for i, name in enumerate(TWIN_WEIGHTS):
        w = inp[name].astype(_jnp.float32)
        if MOMENT_SCALE is None:
            s = _jnp.sqrt(_jnp.mean(_jnp.square(w)) + 1e-30)
        else:
            s = MOMENT_SCALE[name]
        km, kv = _jax.random.split(_jax.random.fold_in(key, i + 1))
        out[name] = w
        out["m_" + name] = s * _jax.random.normal(km, w.shape, _jnp.float32)
        out["v_" + name] = (s * s) * _jax.random.uniform(kv, w.shape, _jnp.float32, 0.5, 1.5)
    if N_MICROBATCH > 1:
        for name, axis in PER_EXAMPLE_BATCH_AXIS.items():
            out[name] = _to_microbatches(out[name], axis)
    return {'x': out['x'], 'c': out['c'], 'ctx': out['ctx'], 'c_ctx': out['c_ctx'], 'w_ada': out['w_ada'], 'b_ada': out['b_ada'], 'g_pre_mix': out['g_pre_mix'], 'g_post_mix': out['g_post_mix'], 'g_pre_ffn': out['g_pre_ffn'], 'g_post_ffn': out['g_post_ffn'], 'w_in': out['w_in'], 'a_q_norm': out['a_q_norm'], 'a_k_norm': out['a_k_norm'], 'b_q_norm': out['b_q_norm'], 'w_q_up': out['w_q_up'], 'b_kv_norm': out['b_kv_norm'], 'w_kv_up': out['w_kv_up'], 'c_lower_bounds': out['c_lower_bounds'], 'c_out_norm': out['c_out_norm'], 'w_out': out['w_out'], 'w_ff1': out['w_ff1'], 'w_ff2': out['w_ff2'], 'loss_target': out['loss_target'], 'm_c_ctx': out['m_c_ctx'], 'm_w_ada': out['m_w_ada'], 'm_b_ada': out['m_b_ada'], 'm_g_pre_mix': out['m_g_pre_mix'], 'm_g_post_mix': out['m_g_post_mix'], 'm_g_pre_ffn': out['m_g_pre_ffn'], 'm_g_post_ffn': out['m_g_post_ffn'], 'm_w_in': out['m_w_in'], 'm_a_q_norm': out['m_a_q_norm'], 'm_a_k_norm': out['m_a_k_norm'], 'm_b_q_norm': out['m_b_q_norm'], 'm_w_q_up': out['m_w_q_up'], 'm_b_kv_norm': out['m_b_kv_norm'], 'm_w_kv_up': out['m_w_kv_up'], 'm_c_lower_bounds': out['m_c_lower_bounds'], 'm_c_out_norm': out['m_c_out_norm'], 'm_w_out': out['m_w_out'], 'm_w_ff1': out['m_w_ff1'], 'm_w_ff2': out['m_w_ff2'], 'v_c_ctx': out['v_c_ctx'], 'v_w_ada': out['v_w_ada'], 'v_b_ada': out['v_b_ada'], 'v_g_pre_mix': out['v_g_pre_mix'], 'v_g_post_mix': out['v_g_post_mix'], 'v_g_pre_ffn': out['v_g_pre_ffn'], 'v_g_post_ffn': out['v_g_post_ffn'], 'v_w_in': out['v_w_in'], 'v_a_q_norm': out['v_a_q_norm'], 'v_a_k_norm': out['v_a_k_norm'], 'v_b_q_norm': out['v_b_q_norm'], 'v_w_q_up': out['v_w_q_up'], 'v_b_kv_norm': out['v_b_kv_norm'], 'v_w_kv_up': out['v_w_kv_up'], 'v_c_lower_bounds': out['v_c_lower_bounds'], 'v_c_out_norm': out['v_c_out_norm'], 'v_w_out': out['v_w_out'], 'v_w_ff1': out['v_w_ff1'], 'v_w_ff2': out['v_w_ff2']}


def _loss(weights, diff, rest, loss_target):
    with _jax.named_scope("forward"):
        args = {**rest, TWIN_DIFF_INPUT: diff, **{k: w.astype(_WEIGHT_DTYPES[k]) for k, w in weights.items()}}
        y = _forward(args)
    with _jax.named_scope("loss_head"):
        err = _jnp.square(y.astype(_jnp.float32) - loss_target)
        return 0.5 * _jnp.sum(_jnp.mean(err, axis=-1)) if err.ndim else 0.5 * err


def _adamw(w, g, m, v):
    m = ADAM_B1 * m + (1.0 - ADAM_B1) * g
    v = ADAM_B2 * v + (1.0 - ADAM_B2) * _jnp.square(g)
    m_hat = m / (1.0 - ADAM_B1 ** ADAM_STEP)
    v_hat = v / (1.0 - ADAM_B2 ** ADAM_STEP)
    delta = -ADAM_LR * (m_hat / (_jnp.sqrt(v_hat) + ADAM_EPS) + ADAM_WD * w)
    return delta, m, v


def reference(x, c, ctx, c_ctx, w_ada, b_ada, g_pre_mix, g_post_mix, g_pre_ffn, g_post_ffn, w_in, a_q_norm, a_k_norm, b_q_norm, w_q_up, b_kv_norm, w_kv_up, c_lower_bounds, c_out_norm, w_out, w_ff1, w_ff2, loss_target, m_c_ctx, m_w_ada, m_b_ada, m_g_pre_mix, m_g_post_mix, m_g_pre_ffn, m_g_post_ffn, m_w_in, m_a_q_norm, m_a_k_norm, m_b_q_norm, m_w_q_up, m_b_kv_norm, m_w_kv_up, m_c_lower_bounds, m_c_out_norm, m_w_out, m_w_ff1, m_w_ff2, v_c_ctx, v_w_ada, v_b_ada, v_g_pre_mix, v_g_post_mix, v_g_pre_ffn, v_g_post_ffn, v_w_in, v_a_q_norm, v_a_k_norm, v_b_q_norm, v_w_q_up, v_b_kv_norm, v_w_kv_up, v_c_lower_bounds, v_c_out_norm, v_w_out, v_w_ff1, v_w_ff2):
    given = dict(x=x, c=c, ctx=ctx, c_ctx=c_ctx, w_ada=w_ada, b_ada=b_ada, g_pre_mix=g_pre_mix, g_post_mix=g_post_mix, g_pre_ffn=g_pre_ffn, g_post_ffn=g_post_ffn, w_in=w_in, a_q_norm=a_q_norm, a_k_norm=a_k_norm, b_q_norm=b_q_norm, w_q_up=w_q_up, b_kv_norm=b_kv_norm, w_kv_up=w_kv_up, c_lower_bounds=c_lower_bounds, c_out_norm=c_out_norm, w_out=w_out, w_ff1=w_ff1, w_ff2=w_ff2, loss_target=loss_target, m_c_ctx=m_c_ctx, m_w_ada=m_w_ada, m_b_ada=m_b_ada, m_g_pre_mix=m_g_pre_mix, m_g_post_mix=m_g_post_mix, m_g_pre_ffn=m_g_pre_ffn, m_g_post_ffn=m_g_post_ffn, m_w_in=m_w_in, m_a_q_norm=m_a_q_norm, m_a_k_norm=m_a_k_norm, m_b_q_norm=m_b_q_norm, m_w_q_up=m_w_q_up, m_b_kv_norm=m_b_kv_norm, m_w_kv_up=m_w_kv_up, m_c_lower_bounds=m_c_lower_bounds, m_c_out_norm=m_c_out_norm, m_w_out=m_w_out, m_w_ff1=m_w_ff1, m_w_ff2=m_w_ff2, v_c_ctx=v_c_ctx, v_w_ada=v_w_ada, v_b_ada=v_b_ada, v_g_pre_mix=v_g_pre_mix, v_g_post_mix=v_g_post_mix, v_g_pre_ffn=v_g_pre_ffn, v_g_post_ffn=v_g_post_ffn, v_w_in=v_w_in, v_a_q_norm=v_a_q_norm, v_a_k_norm=v_a_k_norm, v_b_q_norm=v_b_q_norm, v_w_q_up=v_w_q_up, v_b_kv_norm=v_b_kv_norm, v_w_kv_up=v_w_kv_up, v_c_lower_bounds=v_c_lower_bounds, v_c_out_norm=v_c_out_norm, v_w_out=v_w_out, v_w_ff1=v_w_ff1, v_w_ff2=v_w_ff2)
    weights = {n: given[n] for n in TWIN_WEIGHTS}
    shared = {n: given[n] for n in SHARED_INPUTS}
    per_example = {n: given[n] for n in ['x', 'c', 'ctx']}
    grad_fn = _jax.value_and_grad(_loss, argnums=(0, 1))

    def one_microbatch(ex, loss_target):
        ex = dict(ex)
        diff = ex.pop(TWIN_DIFF_INPUT)
        return grad_fn(weights, diff, {**shared, **ex}, loss_target)

    if N_MICROBATCH == 1:
        loss, (grad_w, grad_x) = one_microbatch(per_example, given["loss_target"])
    else:
        def body(carry, xs):
            loss_sum, grad_sum = carry
            l_k, (gw_k, gx_k) = one_microbatch(xs[0], xs[1])
            with _jax.named_scope("update"):
                return (loss_sum + l_k, _jax.tree.map(_jnp.add, grad_sum, gw_k)), gx_k

        init = (_jnp.zeros((), _jnp.float32), _jax.tree.map(_jnp.zeros_like, weights))
        (loss, grad_w), grad_x = _jax.lax.scan(body, init, (per_example, given["loss_target"]))
    with _jax.named_scope("update"):
        delta_w, new_m, new_v = {}, {}, {}
        for n in TWIN_WEIGHTS:
            delta_w[n], new_m[n], new_v[n] = _adamw(weights[n], grad_w[n], given["m_" + n], given["v_" + n])
    return (loss, grad_x, *[grad_w[n] for n in TWIN_WEIGHTS], *[delta_w[n] for n in TWIN_WEIGHTS],
            *[new_m[n] for n in TWIN_WEIGHTS], *[new_v[n] for n in TWIN_WEIGHTS])
```

```python
import functools
from typing import Any, Callable, NamedTuple

import numpy as np
import jax
import jax.numpy as jnp
from jax import lax
from jax.experimental import pallas as pl
from jax.experimental.pallas import tpu as pltpu

F32 = jnp.float32
BF16 = jnp.bfloat16
HIGHEST = lax.Precision.HIGHEST

GRID_W = 64
HEAD_DIM = 64
A_HEADS, A_KV_HEADS = 8, 2
A_GROUP = A_HEADS // A_KV_HEADS
B_HEADS, B_Q_RANK, B_KV_RANK, B_NOPE, B_ROPE, B_V = 4, 192, 128, 64, 32, 64
C_HEADS, C_DK, C_DV = 4, 64, 64
SCAN_CHUNK = 64
ROPE_THETA = 10000.0
EPS = 1e-6
F_TINY = 1e-30
ADAM_LR, ADAM_B1, ADAM_B2, ADAM_EPS, ADAM_WD, ADAM_STEP = 0.001, 0.9, 0.999, 1e-08, 0.01, 10

IN_SIZES = (512, 128, 128, 192, 128, 32, 256, 256, 256, 256, 256)
IN_PAD = (512, 128, 128, 256, 128, 128, 256, 256, 256, 256, 256)
IN_OFF = tuple(int(v) for v in np.cumsum((0,) + IN_PAD)[:-1])
D_IN_PAD = int(sum(IN_PAD))

LANE = 128
SUBLANE = 8
TOKEN_TILE = 256
ATT_A_TQ = 64
ATT_B_TQ = 256
MM_TILE = 512
MM_TK = 2048
MM_TK_TOKENS = 512
VMEM_LIMIT = 56 * 1024 * 1024
MESH_AXES = ("x", "y", "c")
N_DEV = 8
N_CHIP = 4
ADA_ROWS = 24


class Arg(NamedTuple):
    arr: Any
    block: tuple
    imap: Callable
    kind: str
    first: Callable = None


class Out(NamedTuple):
    shape: tuple
    block: tuple
    imap: Callable


def _cparams():
    return pltpu.CompilerParams(vmem_limit_bytes=VMEM_LIMIT)


def tile_op(name, fn, grid, args, outs):
    n_in, n_out = len(args), len(outs)
    in_specs = [pl.BlockSpec(a.block, a.imap) for a in args]
    out_specs = [pl.BlockSpec(o.block, o.imap) for o in outs]
    out_shape = [jax.ShapeDtypeStruct(o.shape, F32) for o in outs]
    diff = [i for i, a in enumerate(args) if a.kind != "const"]

    def fwd_call(*arrays):
        def body(*refs):
            ids = tuple(pl.program_id(i) for i in range(len(grid)))
            res = fn(ids, *[r[...] for r in refs[:n_in]])
            for r, o in zip(refs[n_in:], res):
                r[...] = o

        return pl.pallas_call(body, grid=grid, in_specs=in_specs, out_specs=out_specs, out_shape=out_shape,
                              name=name + "_fwd", compiler_params=_cparams())(*arrays)

    def bwd_call(arrays, cts):
        def body(*refs):
            ids = tuple(pl.program_id(i) for i in range(len(grid)))
            vals = [r[...] for r in refs[:n_in]]
            ct = tuple(r[...] for r in refs[n_in:n_in + n_out])
            drefs = refs[n_in + n_out:]

            def g(*dv):
                full = list(vals)
                for i, v in zip(diff, dv):
                    full[i] = v
                return tuple(fn(ids, *full))

            _, vjp = jax.vjp(g, *[vals[i] for i in diff])
            ds = vjp(ct)
            for i, d, r in zip(diff, ds, drefs):
                if args[i].kind == "tile":
                    r[...] = d
                else:
                    is_first = args[i].first(ids)

                    @pl.when(is_first)
                    def _(r=r, d=d):
                        r[...] = d

                    @pl.when(jnp.logical_not(is_first))
                    def _(r=r, d=d):
                        r[...] += d

        d_specs = [in_specs[i] for i in diff]
        d_shape = [jax.ShapeDtypeStruct(arrays[i].shape, F32) for i in diff]
        return pl.pallas_call(body, grid=grid, in_specs=in_specs + out_specs, out_specs=d_specs, out_shape=d_shape,
                              name=name + "_bwd", compiler_params=_cparams())(*arrays, *cts)

    @jax.custom_vjp
    def op(*arrays):
        return tuple(fwd_call(*arrays))

    def op_fwd(*arrays):
        return tuple(fwd_call(*arrays)), arrays

    def op_bwd(arrays, cts):
        ds = bwd_call(arrays, cts)
        res, k = [], 0
        for i, a in enumerate(args):
            if a.kind == "const":
                res.append(jnp.zeros_like(arrays[i]))
            else:
                res.append(ds[k])
                k += 1
        return tuple(res)

    op.defvjp(op_fwd, op_bwd)
    return op(*[a.arr for a in args])


def _pick(n, cap):
    if n <= cap:
        return n
    best = None
    for t in range(LANE, cap + 1, LANE):
        if n % t == 0:
            best = t
    assert best is not None, (n, cap)
    return best


_MM_DIMS = {"nn": ((1,), (0,)), "nt": ((1,), (1,)), "tn": ((0,), (0,))}


def _mm(name, a, b, mode):
    if mode == "nn":
        (m, k), n = a.shape, b.shape[1]
    elif mode == "nt":
        (m, k), n = a.shape, b.shape[0]
    else:
        (k, m), n = a.shape, b.shape[1]
    tm, tn = _pick(m, MM_TILE), _pick(n, MM_TILE)
    tk = _pick(k, MM_TK_TOKENS if mode == "tn" else MM_TK)
    grid = (n // tn, m // tm, k // tk)
    if mode == "tn":
        a_spec = pl.BlockSpec((tk, tm), lambda j, i, kk: (kk, i))
    else:
        a_spec = pl.BlockSpec((tm, tk), lambda j, i, kk: (i, kk))
    if mode == "nt":
        b_spec = pl.BlockSpec((tn, tk), lambda j, i, kk: (j, kk))
    else:
        b_spec = pl.BlockSpec((tk, tn), lambda j, i, kk: (kk, j))
    dims = (_MM_DIMS[mode], ((), ()))

    def body(a_ref, b_ref, o_ref):
        p = lax.dot_general(a_ref[...].astype(BF16), b_ref[...].astype(BF16), dims, preferred_element_type=F32)
        kk = pl.program_id(2)

        @pl.when(kk == 0)
        def _():
            o_ref[...] = p

        @pl.when(kk != 0)
        def _():
            o_ref[...] += p

    return pl.pallas_call(body, grid=grid, in_specs=[a_spec, b_spec],
                          out_specs=pl.BlockSpec((tm, tn), lambda j, i, kk: (i, j)),
                          out_shape=jax.ShapeDtypeStruct((m, n), F32), name=name, compiler_params=_cparams())(a, b)


def linear(name, a, w):
    @jax.custom_vjp
    def op(a, w):
        return _mm(name + "_fwd", a, w, "nn")

    def op_fwd(a, w):
        return _mm(name + "_fwd", a, w, "nn"), (a, w)

    def op_bwd(res, g):
        a, w = res
        return _mm(name + "_da", g, w, "nt"), _mm(name + "_dw", a, g, "tn")

    op.defvjp(op_fwd, op_bwd)
    return op(a, w)


def _rms(x, g):
    return x * lax.rsqrt(jnp.mean(x * x, axis=-1, keepdims=True) + EPS) * g


def _sigmoid(z):
    return 1.0 / (1.0 + jnp.exp(-z))


def _silu(z):
    return z * _sigmoid(z)


def _rope(y, cos, sin_signed, swap):
    return y * cos + jnp.dot(y, swap, precision=HIGHEST, preferred_element_type=F32) * sin_signed


def _softmax_rows(s):
    m = jnp.max(s, axis=-1, keepdims=True)
    e = jnp.exp(s - m)
    return e / jnp.sum(e, axis=-1, keepdims=True)


def _dot_bf16(a, b, dims=((1,), (0,))):
    return lax.dot_general(a.astype(BF16), b.astype(BF16), (dims, ((), ())), preferred_element_type=F32)


def _gla_step(state, q, k, v, g, reverse):
    c, d = q.shape
    row = lax.broadcasted_iota(jnp.int32, (c, c), 0)
    col = lax.broadcasted_iota(jnp.int32, (c, c), 1)
    tri = (row <= col) if reverse else (row >= col)
    b = jnp.dot(tri.astype(F32), g, precision=HIGHEST, preferred_element_type=F32)
    r3 = lax.broadcasted_iota(jnp.int32, (c, c, d), 0)
    c3 = lax.broadcasted_iota(jnp.int32, (c, c, d), 1)
    tri3 = (r3 <= c3) if reverse else (r3 >= c3)
    diff = b[:, None, :] - b[None, :, :]
    decay = jnp.where(tri3, jnp.exp(jnp.where(tri3, diff, 0.0)), 0.0)
    scores = jnp.sum(q[:, None, :] * k[None, :, :] * decay, axis=-1)
    o = jnp.dot(scores, v, preferred_element_type=F32) + jnp.dot(q * jnp.exp(b), state, preferred_element_type=F32)
    b_end = b[0:1, :] if reverse else b[c - 1:c, :]
    kd = k * jnp.exp(b_end - b)
    new_state = state * jnp.exp(b_end).reshape(d, 1) + lax.dot_general(kd, v, (((0,), (0,)), ((), ())), preferred_element_type=F32)
    return new_state, o


def gla_scan(name, q, k, v, g, reverse, n_lat_chunks):
    bsz, nh, t, d = q.shape
    c = SCAN_CHUNK
    n = t // c

    def chunk_of(j):
        return (n - 1 - j) if reverse else lax.rem(j + n_lat_chunks, n)

    blk = (None, None, c, d)
    st_blk = (None, None, None, d, d)

    def fwd_call(q, k, v, g):
        def body(q_ref, k_ref, v_ref, g_ref, o_ref, states_ref, st):
            @pl.when(pl.program_id(2) == 0)
            def _():
                st[...] = jnp.zeros_like(st)
            s = st[...]
            states_ref[...] = s
            ns, o = _gla_step(s, q_ref[...], k_ref[...], v_ref[...], g_ref[...], reverse)
            st[...] = ns
            o_ref[...] = o

        spec = pl.BlockSpec(blk, lambda b, h, j: (b, h, chunk_of(j), 0))
        return pl.pallas_call(
            body, grid=(bsz, nh, n), in_specs=[spec] * 4,
            out_specs=[spec, pl.BlockSpec(st_blk, lambda b, h, j: (b, h, j, 0, 0))],
            out_shape=[jax.ShapeDtypeStruct(q.shape, F32), jax.ShapeDtypeStruct((bsz, nh, n, d, d), F32)],
            scratch_shapes=[pltpu.VMEM((d, d), F32)], name=name + "_fwd", compiler_params=_cparams())(q, k, v, g)

    def bwd_call(q, k, v, g, states, do):
        def body(q_ref, k_ref, v_ref, g_ref, s_ref, do_ref, dq_ref, dk_ref, dv_ref, dg_ref, dst):
            @pl.when(pl.program_id(2) == 0)
            def _():
                dst[...] = jnp.zeros_like(dst)
            step = functools.partial(_gla_step, reverse=reverse)
            _, vjp = jax.vjp(step, s_ref[...], q_ref[...], k_ref[...], v_ref[...], g_ref[...])
            ds, dq, dk, dv, dg = vjp((dst[...], do_ref[...]))
            dst[...] = ds
            dq_ref[...] = dq
            dk_ref[...] = dk
            dv_ref[...] = dv
            dg_ref[...] = dg

        spec = pl.BlockSpec(blk, lambda b, h, jj: (b, h, chunk_of(n - 1 - jj), 0))
        s_spec = pl.BlockSpec(st_blk, lambda b, h, jj: (b, h, n - 1 - jj, 0, 0))
        return pl.pallas_call(
            body, grid=(bsz, nh, n), in_specs=[spec] * 4 + [s_spec, spec], out_specs=[spec] * 4,
            out_shape=[jax.ShapeDtypeStruct(q.shape, F32)] * 4,
            scratch_shapes=[pltpu.VMEM((d, d), F32)], name=name + "_bwd", compiler_params=_cparams())(q, k, v, g, states, do)

    @jax.custom_vjp
    def op(q, k, v, g):
        return fwd_call(q, k, v, g)[0]

    def op_fwd(q, k, v, g):
        o, states = fwd_call(q, k, v, g)
        return o, (q, k, v, g, states)

    def op_bwd(res, do):
        return tuple(bwd_call(*res, do))

    op.defvjp(op_fwd, op_bwd)
    return op(q, k, v, g)


def _zero_ids(ids):
    z = ids[0] == 0
    for i in ids[1:]:
        z = jnp.logical_and(z, i == 0)
    return z


def _token_grid(x, n_lat_tiles):
    bsz, t, d = x.shape
    nt = t // TOKEN_TILE
    row = lambda w: ((None, TOKEN_TILE, w), lambda b, i: (b, i, 0))
    mod_block = (None, None, 6, d)
    mod_imap = lambda b, i: (b, (i >= n_lat_tiles).astype(jnp.int32), 0, 0)
    mod_first = lambda ids: jnp.logical_or(ids[1] == 0, ids[1] == n_lat_tiles)
    return bsz, t, d, nt, row, (mod_block, mod_imap, mod_first)


def premod(name, x, gain, mods, r0, n_lat_tiles):
    bsz, t, d, nt, row, (mb, mi, mf) = _token_grid(x, n_lat_tiles)

    def fn(ids, xb, gb, mod):
        return (_rms(xb, gb) * (1.0 + mod[r0 + 1:r0 + 2]) + mod[r0:r0 + 1],)

    args = [Arg(x, *row(d), "tile"), Arg(gain, (1, d), lambda b, i: (0, 0), "acc", _zero_ids), Arg(mods, mb, mi, "acc", mf)]
    return tile_op(name, fn, (bsz, nt), args, [Out(x.shape, *row(d))])[0]


def resid(name, x, y, gain, mods, r, n_lat_tiles):
    bsz, t, d, nt, row, (mb, mi, mf) = _token_grid(x, n_lat_tiles)

    def fn(ids, xb, yb, gb, mod):
        return (xb + mod[r:r + 1] * _rms(yb, gb),)

    args = [Arg(x, *row(d), "tile"), Arg(y, *row(d), "tile"), Arg(gain, (1, d), lambda b, i: (0, 0), "acc", _zero_ids),
            Arg(mods, mb, mi, "acc", mf)]
    return tile_op(name, fn, (bsz, nt), args, [Out(x.shape, *row(d))])[0]


def sq_relu(name, u):
    m, n = u.shape
    spec = ((TOKEN_TILE, n), lambda i: (i, 0))
    fn = lambda ids, ub: (jnp.square(jnp.maximum(ub, 0.0)),)
    return tile_op(name, fn, (m // TOKEN_TILE,), [Arg(u, *spec, "tile")], [Out(u.shape, *spec)])[0]


def _head_spec(w):
    return (None, None, TOKEN_TILE, w), lambda b, h, i: (b, h, i, 0)


def a_prep(name, x, gain, cos, sin, swap):
    bsz, nh, t, d = x.shape
    tab = ((TOKEN_TILE, d), lambda b, h, i: (i, 0))

    def fn(ids, xb, gb, cb, sb, pb):
        return (_rope(_rms(xb, gb), cb, sb, pb),)

    args = [Arg(x, *_head_spec(d), "tile"), Arg(gain, (1, d), lambda b, h, i: (0, 0), "acc", _zero_ids),
            Arg(cos, *tab, "const"), Arg(sin, *tab, "const"), Arg(swap, (d, d), lambda b, h, i: (0, 0), "const")]
    return tile_op(name, fn, (bsz, nh, t // TOKEN_TILE), args, [Out(x.shape, *_head_spec(d))])[0]


def b_prep(name, bqd, bkvd, bkr, bqn, wq_nope, wq_pe, bkvn, wkv_nope, wkv_v, cos_q, sin_q, swap_q, cos_k, sin_k, swap_k):
    bsz, t, _ = bqd.shape
    row = lambda w: ((None, TOKEN_TILE, w), lambda b, i: (b, i, 0))
    whole = lambda a: (a.shape, lambda b, i: (0,) * a.ndim)
    tab = lambda w: ((TOKEN_TILE, w), lambda b, i: (i, 0))

    def fn(ids, qd, kvd, kr, qn, wqn, wqp, kvn, wkn, wkv, cq, sq, pq, ck, sk, pk):
        hq = _rms(qd, qn)
        hkv = _rms(kvd, kvn)
        return (_dot_bf16(hq, wqn), _rope(_dot_bf16(hq, wqp), cq, sq, pq), _dot_bf16(hkv, wkn), _dot_bf16(hkv, wkv),
                _rope(kr, ck, sk, pk))

    params = [bqn, wq_nope, wq_pe, bkvn, wkv_nope, wkv_v]
    args = [Arg(bqd, *row(B_Q_RANK), "tile"), Arg(bkvd, *row(B_KV_RANK), "tile"), Arg(bkr, *row(B_ROPE), "tile")]
    args += [Arg(p, *whole(p), "acc", _zero_ids) for p in params]
    args += [Arg(cos_q, *tab(cos_q.shape[1]), "const"), Arg(sin_q, *tab(cos_q.shape[1]), "const"), Arg(swap_q, *whole(swap_q), "const"),
             Arg(cos_k, *tab(B_ROPE), "const"), Arg(sin_k, *tab(B_ROPE), "const"), Arg(swap_k, *whole(swap_k), "const")]
    widths = (B_HEADS * B_NOPE, B_HEADS * B_ROPE, B_HEADS * B_NOPE, B_HEADS * B_V, B_ROPE)
    outs = [Out((bsz, t, w), *row(w)) for w in widths]
    return tile_op(name, fn, (bsz, t // TOKEN_TILE), args, outs)


def c_prep(name, cq, cff, cfb, clb, layer):
    bsz, nh, t, d = cq.shape
    depth = clb.shape[1]
    spec = ((None, None, TOKEN_TILE, d), lambda h, b, i: (b, h, i, 0))

    def fn(ids, q, zf, zb, lbs):
        m = lbs[0]
        for j in range(1, depth):
            m = jnp.maximum(m, lbs[j])
        e = [jnp.exp(lbs[j] - m) for j in range(depth)]
        tot = e[0]
        for j in range(1, depth):
            tot = tot + e[j]
        p = [ej / tot for ej in e]
        cum = p[0]
        for j in range(1, layer + 1):
            cum = cum + p[j]
        lower = cum - p[0]

        def gate(z, lb):
            f = lb + (1.0 - lb) * _sigmoid(z)
            return jnp.log(jnp.maximum(f, F_TINY)), (1.0 - lb) * _sigmoid(-z)

        gf, kf = gate(zf, lower[0:1])
        gb, kb = gate(zb, lower[1:2])
        return _silu(q), kf, gf, kb, gb

    first = lambda ids: jnp.logical_and(ids[1] == 0, ids[2] == 0)
    args = [Arg(cq, *spec, "tile"), Arg(cff, *spec, "tile"), Arg(cfb, *spec, "tile"),
            Arg(clb, (None, depth, 2, d), lambda h, b, i: (h, 0, 0, 0), "acc", first)]
    return tile_op(name, fn, (nh, bsz, t // TOKEN_TILE), args, [Out(cq.shape, *spec)] * 5)


def c_readout(name, o_f, o_b, gate, gain):
    bsz, nh, t, d = o_f.shape

    def fn(ids, of, ob, gt, gn):
        return (_rms(of + ob, gn) * _silu(gt),)

    args = [Arg(o_f, *_head_spec(d), "tile"), Arg(o_b, *_head_spec(d), "tile"), Arg(gate, *_head_spec(d), "tile"),
            Arg(gain, (1, d), lambda b, h, i: (0, 0), "acc", _zero_ids)]
    return tile_op(name, fn, (bsz, nh, t // TOKEN_TILE), args, [Out(o_f.shape, *_head_spec(d))])[0]


def _key_mask(rows, t, tq, i, n_lat):
    key = lax.broadcasted_iota(jnp.int32, (rows, t), 1)
    return jnp.logical_or(i * tq < n_lat, key >= n_lat)


def attention_a(name, q, k, v, n_lat):
    bsz, hk, grp, t, d = q.shape
    tq = ATT_A_TQ
    scale = HEAD_DIM ** -0.5
    q_spec = ((None, None, grp, tq, d), lambda b, h, i: (b, h, 0, i, 0))
    kv_spec = ((None, None, t, d), lambda b, h, i: (b, h, 0, 0))

    def fn(ids, qb, kb, vb):
        qq = qb.reshape(grp * tq, d)
        s = _dot_bf16(qq, kb, ((1,), (1,))) * scale
        s = jnp.where(_key_mask(grp * tq, t, tq, ids[2], n_lat), s, -1e30)
        return (_dot_bf16(_softmax_rows(s), vb).reshape(grp, tq, d),)

    first = lambda ids: ids[2] == 0
    args = [Arg(q, *q_spec, "tile"), Arg(k, *kv_spec, "acc", first), Arg(v, *kv_spec, "acc", first)]
    return tile_op(name, fn, (bsz, hk, t // tq), args, [Out(q.shape, *q_spec)])[0]


def attention_b(name, qn, qp, kn, kp, v, n_lat):
    bsz, nh, t, _ = qn.shape
    tq = ATT_B_TQ
    scale = (B_NOPE + B_ROPE) ** -0.5
    q_spec = lambda w: ((None, None, tq, w), lambda b, h, i: (b, h, i, 0))
    kv_spec = lambda w: ((None, None, t, w), lambda b, h, i: (b, h, 0, 0))

    def fn(ids, qnb, qpb, knb, kpb, vb):
        s = (_dot_bf16(qnb, knb, ((1,), (1,))) + _dot_bf16(qpb, kpb, ((1,), (1,)))) * scale
        s = jnp.where(_key_mask(tq, t, tq, ids[2], n_lat), s, -1e30)
        return (_dot_bf16(_softmax_rows(s), vb),)

    first = lambda ids: ids[2] == 0
    first_b = lambda ids: jnp.logical_and(ids[1] == 0, ids[2] == 0)
    args = [Arg(qn, *q_spec(B_NOPE), "tile"), Arg(qp, *q_spec(B_ROPE), "tile"), Arg(kn, *kv_spec(B_NOPE), "acc", first),
            Arg(kp, (None, t, B_ROPE), lambda b, h, i: (b, 0, 0), "acc", first_b), Arg(v, *kv_spec(B_V), "acc", first)]
    return tile_op(name, fn, (bsz, nh, t // tq), args, [Out((bsz, nh, t, B_V), *q_spec(B_V))])[0]


def ada_op(cc, w, b):
    depth, d, n = w.shape

    def fn(ids, ccb, wb, bb):
        return (_dot_bf16(_silu(ccb), wb) + bb,)

    args = [Arg(cc, cc.shape, lambda l: (0, 0), "acc", lambda ids: ids[0] == 0),
            Arg(w, (None, d, n), lambda l: (l, 0, 0), "tile"), Arg(b, (None, 1, n), lambda l: (l, 0, 0), "tile")]
    return tile_op("ada", fn, (depth,), args, [Out((depth, cc.shape[0], n), (None, cc.shape[0], n), lambda l: (l, 0, 0))])[0]


def loss_op(xu, target, n_lat_tiles):
    bsz, t, d = xu.shape

    def body(x_ref, t_ref, dx_ref, l_ref):
        b, i = pl.program_id(0), pl.program_id(1)

        @pl.when(jnp.logical_and(b == 0, i == 0))
        def _():
            l_ref[...] = jnp.zeros_like(l_ref)

        @pl.when(i < n_lat_tiles)
        def _():
            err = x_ref[...] - t_ref[...]
            dx_ref[...] = err * (1.0 / d)
            l_ref[...] += 0.5 * jnp.sum(jnp.mean(err * err, axis=-1))

        @pl.when(i >= n_lat_tiles)
        def _():
            dx_ref[...] = jnp.zeros_like(dx_ref)

    row = pl.BlockSpec((None, TOKEN_TILE, d), lambda b, i: (b, i, 0))
    t_spec = pl.BlockSpec((None, TOKEN_TILE, d), lambda b, i: (b, jnp.minimum(i, n_lat_tiles - 1), 0))
    return pl.pallas_call(body, grid=(bsz, t // TOKEN_TILE), in_specs=[row, t_spec],
                          out_specs=[row, pl.BlockSpec((SUBLANE, LANE), lambda b, i: (0, 0))],
                          out_shape=[jax.ShapeDtypeStruct(xu.shape, F32), jax.ShapeDtypeStruct((SUBLANE, LANE), F32)],
                          name="loss", compiler_params=_cparams())(xu, target)


def _row_tile(rows, row_bytes, budget=4 << 20):
    if rows * row_bytes <= budget:
        return rows
    best = None
    for t in range(SUBLANE, rows, SUBLANE):
        if rows % t == 0 and t * row_bytes <= budget:
            best = t
    return best if best is not None else rows


def _as3d(x):
    p = x.shape[0]
    c = x.shape[-1]
    return x.reshape(p, -1, c)


def sum_parts(name, x):
    x3 = _as3d(x)
    p, r, c = x3.shape
    tr = _row_tile(r, p * c * 4)

    def body(x_ref, o_ref):
        s = x_ref[0]
        for j in range(1, p):
            s = s + x_ref[j]
        o_ref[...] = s

    out = pl.pallas_call(body, grid=(r // tr,), in_specs=[pl.BlockSpec((p, tr, c), lambda i: (0, i, 0))],
                         out_specs=pl.BlockSpec((tr, c), lambda i: (i, 0)), out_shape=jax.ShapeDtypeStruct((r, c), F32),
                         name=name, compiler_params=_cparams())(x3)
    return out.reshape(x.shape[1:])


def adamw(name, w, g, m, v):
    shape = w.shape
    c = shape[-1]
    to2d = lambda a: a.reshape(-1, c)
    r = to2d(w).shape[0]
    tr = _row_tile(r, 7 * c * 4, budget=6 << 20)

    def body(w_ref, g_ref, m_ref, v_ref, d_ref, nm_ref, nv_ref):
        gg = g_ref[...]
        nm = ADAM_B1 * m_ref[...] + (1.0 - ADAM_B1) * gg
        nv = ADAM_B2 * v_ref[...] + (1.0 - ADAM_B2) * jnp.square(gg)
        m_hat = nm / (1.0 - ADAM_B1 ** ADAM_STEP)
        v_hat = nv / (1.0 - ADAM_B2 ** ADAM_STEP)
        d_ref[...] = -ADAM_LR * (m_hat / (jnp.sqrt(v_hat) + ADAM_EPS) + ADAM_WD * w_ref[...])
        nm_ref[...] = nm
        nv_ref[...] = nv

    spec = pl.BlockSpec((tr, c), lambda i: (i, 0))
    outs = pl.pallas_call(body, grid=(r // tr,), in_specs=[spec] * 4, out_specs=[spec] * 3,
                          out_shape=[jax.ShapeDtypeStruct((r, c), F32)] * 3, name=name, compiler_params=_cparams())(
        to2d(w), to2d(g), to2d(m), to2d(v))
    return tuple(o.reshape(shape) for o in outs)


def exchange(name, srcs, group, same):
    p = 2 ** len(group)
    n = len(srcs)
    out_shape = [jax.ShapeDtypeStruct(((p,) + s.shape) if same else s.shape, s.dtype) for s in srcs]

    def body(*refs):
        src_refs, out_refs = refs[:n], refs[n:2 * n]
        send_sems, recv_sems, local_sems = refs[2 * n:]
        pos = {a: lax.axis_index(a) for a in MESH_AXES}

        def index_of(coords):
            idx = 0
            for a in group:
                idx = idx * 2 + coords[a]
            return idx

        me = index_of(pos)
        peers = []
        for rel in range(1, p):
            coords = dict(pos)
            for bit, a in enumerate(reversed(group)):
                if (rel >> bit) & 1:
                    coords[a] = 1 - coords[a]
            peers.append((coords, index_of(coords)))

        def src_for(a, idx):
            return src_refs[a] if same else src_refs[a].at[idx]

        local, sends, recvs = [], [], []
        for a in range(n):
            cp = pltpu.make_async_copy(src_for(a, me), out_refs[a].at[me], local_sems.at[a])
            cp.start()
            local.append(cp)
            for r, (coords, idx) in enumerate(peers):
                dev = tuple(coords[ax] for ax in MESH_AXES)
                send = pltpu.make_async_remote_copy(src_ref=src_for(a, idx), dst_ref=out_refs[a].at[me],
                                                    send_sem=send_sems.at[a, r], recv_sem=recv_sems.at[a, r],
                                                    device_id=dev, device_id_type=pl.DeviceIdType.MESH)
                send.start()
                sends.append(send)
                recvs.append(pltpu.make_async_remote_copy(src_ref=src_for(a, idx), dst_ref=out_refs[a].at[idx],
                                                          send_sem=send_sems.at[a, r], recv_sem=recv_sems.at[a, r],
                                                          device_id=dev, device_id_type=pl.DeviceIdType.MESH))
        for cp in sends:
            cp.wait_send()
        for cp in recvs:
            cp.wait_recv()
        for cp in local:
            cp.wait()

    any_spec = pl.BlockSpec(memory_space=pl.ANY)
    outs = pl.pallas_call(body, in_specs=[any_spec] * n, out_specs=[any_spec] * n, out_shape=out_shape,
                          scratch_shapes=[pltpu.SemaphoreType.DMA((n, p - 1)), pltpu.SemaphoreType.DMA((n, p - 1)),
                                          pltpu.SemaphoreType.DMA((n,))],
                          name=name)(*srcs)
    return list(outs)


def _rope_tables(n_lat, n_ctx, rot_dim, heads):
    n_freq = rot_dim // 4
    tok = jnp.arange(n_lat, dtype=jnp.int32)
    inv = ROPE_THETA ** (-jnp.arange(n_freq, dtype=F32) / n_freq)
    ang = jnp.concatenate([(tok // GRID_W).astype(F32)[:, None] * inv, (tok % GRID_W).astype(F32)[:, None] * inv], axis=-1)
    cos, sin = jnp.cos(ang), jnp.sin(ang)
    cos = jnp.concatenate([jnp.concatenate([cos, cos], -1), jnp.ones((n_ctx, rot_dim), F32)], 0)
    sin = jnp.concatenate([jnp.concatenate([-sin, sin], -1), jnp.zeros((n_ctx, rot_dim), F32)], 0)
    half = rot_dim // 2
    w = heads * rot_dim
    j = np.arange(w)
    src = (j // rot_dim) * rot_dim + (j % rot_dim + half) % rot_dim
    swap = np.zeros((w, w), np.float32)
    swap[src, j] = 1.0
    return jnp.tile(cos, (1, heads)), jnp.tile(sin, (1, heads)), jnp.asarray(swap)


def _to_heads(a, nh):
    b, t, _ = a.shape
    return a.reshape(b, t, nh, -1).transpose(0, 2, 1, 3)


def _from_heads(a):
    b, nh, t, w = a.shape
    return a.transpose(0, 2, 1, 3).reshape(b, t, nh * w)


def _pad_w_in(w_in):
    parts, off = [], 0
    for size, pad in zip(IN_SIZES, IN_PAD):
        parts.append(w_in[:, off:off + size])
        if pad > size:
            parts.append(jnp.zeros((w_in.shape[0], pad - size), w_in.dtype))
        off += size
    return jnp.concatenate(parts, axis=1)


def _forward(xu, mods, prm, n_lat, n_ctx):
    bsz, t, d = xu.shape
    depth = mods.shape[0]
    n_lat_tiles = n_lat // TOKEN_TILE
    rope_a = _rope_tables(n_lat, n_ctx, HEAD_DIM, 1)
    rope_bq = _rope_tables(n_lat, n_ctx, B_ROPE, B_HEADS)
    rope_bk = _rope_tables(n_lat, n_ctx, B_ROPE, 1)
    clb = prm["c_lower_bounds"].reshape(depth, 2, C_HEADS, C_DK).transpose(2, 0, 1, 3)
    for l in range(depth):
        nm = lambda s: f"l{l}_{s}"
        vec = lambda name: prm[name][l][None, :]
        h = premod(nm("premix"), xu, vec("g_pre_mix"), mods[l], 0, n_lat_tiles)
        z = linear(nm("w_in"), h.reshape(bsz * t, d), _pad_w_in(prm["w_in"][l])).reshape(bsz, t, D_IN_PAD)
        seg = lambda i: z[:, :, IN_OFF[i]:IN_OFF[i] + IN_SIZES[i]]
        aq = a_prep(nm("aq"), _to_heads(seg(0), A_HEADS), vec("a_q_norm"), *rope_a)
        ak = a_prep(nm("ak"), _to_heads(seg(1), A_KV_HEADS), vec("a_k_norm"), *rope_a)
        av = _to_heads(seg(2), A_KV_HEADS)
        ya = attention_a(nm("att_a"), aq.reshape(bsz, A_KV_HEADS, A_GROUP, t, HEAD_DIM), ak, av, n_lat)
        ya = _from_heads(ya.reshape(bsz, A_HEADS, t, HEAD_DIM))
        wq = prm["w_q_up"][l].reshape(B_Q_RANK, B_HEADS, B_NOPE + B_ROPE)
        wkv = prm["w_kv_up"][l].reshape(B_KV_RANK, B_HEADS, B_NOPE + B_V)
        qn, qp, kn, bv, kp = b_prep(
            nm("b_prep"), seg(3), seg(4), seg(5), vec("b_q_norm"),
            wq[:, :, :B_NOPE].reshape(B_Q_RANK, -1), wq[:, :, B_NOPE:].reshape(B_Q_RANK, -1), vec("b_kv_norm"),
            wkv[:, :, :B_NOPE].reshape(B_KV_RANK, -1), wkv[:, :, B_NOPE:].reshape(B_KV_RANK, -1), *rope_bq, *rope_bk)
        yb = attention_b(nm("att_b"), _to_heads(qn, B_HEADS), _to_heads(qp, B_HEADS), _to_heads(kn, B_HEADS), kp,
                         _to_heads(bv, B_HEADS), n_lat)
        yb = _from_heads(yb)
        cq, kf, gf, kb, gb = c_prep(nm("c_prep"), _to_heads(seg(6), C_HEADS), _to_heads(seg(7), C_HEADS),
                                    _to_heads(seg(8), C_HEADS), clb, l)
        cv = _to_heads(seg(9), C_HEADS)
        o_f = gla_scan(nm("scan_f"), cq, kf, cv, gf, False, n_lat // SCAN_CHUNK)
        o_b = gla_scan(nm("scan_b"), cq, kb, cv, gb, True, n_lat // SCAN_CHUNK)
        yc = _from_heads(c_readout(nm("c_out"), o_f, o_b, _to_heads(seg(10), C_HEADS), vec("c_out_norm")))
        y = linear(nm("w_out"), jnp.concatenate([ya, yb, yc], axis=-1).reshape(bsz * t, d), prm["w_out"][l]).reshape(bsz, t, d)
        x1 = resid(nm("res_mix"), xu, y, vec("g_post_mix"), mods[l], 2, n_lat_tiles)
        h2 = premod(nm("preffn"), x1, vec("g_pre_ffn"), mods[l], 3, n_lat_tiles)
        u = linear(nm("ff1"), h2.reshape(bsz * t, d), prm["w_ff1"][l])
        f = linear(nm("ff2"), sq_relu(nm("sqrelu"), u), prm["w_ff2"][l]).reshape(bsz, t, d)
        xu = resid(nm("res_ffn"), x1, f, vec("g_post_ffn"), mods[l], 5, n_lat_tiles)
    return xu


BIG = {"w_in": 2, "w_q_up": 2, "w_kv_up": 2, "w_out": 1, "w_ff1": 2, "w_ff2": 1}
SMALL = ("g_pre_mix", "g_post_mix", "g_pre_ffn", "g_post_ffn", "a_q_norm", "a_k_norm", "b_q_norm", "b_kv_norm",
         "c_lower_bounds", "c_out_norm")
WEIGHTS = ("c_ctx", "w_ada", "b_ada", "g_pre_mix", "g_post_mix", "g_pre_ffn", "g_post_ffn", "w_in", "a_q_norm", "a_k_norm",
           "b_q_norm", "w_q_up", "b_kv_norm", "w_kv_up", "c_lower_bounds", "c_out_norm", "w_out", "w_ff1", "w_ff2")


def _unshard(g, axis):
    if axis == 1:
        return g.transpose(1, 0, 2, 3).reshape(g.shape[1], N_CHIP * g.shape[2], g.shape[3])
    return g.transpose(1, 2, 0, 3).reshape(g.shape[1], g.shape[2], N_CHIP * g.shape[3])


def _shard_major(g, axis):
    depth, r, c = g.shape
    if axis == 1:
        return g.reshape(depth, N_CHIP, r // N_CHIP, c)
    return g.reshape(depth, r, N_CHIP, c // N_CHIP).transpose(0, 2, 1, 3)


def kernel(x, c, ctx, c_ctx, w_ada, b_ada, g_pre_mix, g_post_mix, g_pre_ffn, g_post_ffn, w_in, a_q_norm, a_k_norm, b_q_norm, w_q_up, b_kv_norm, w_kv_up, c_lower_bounds, c_out_norm, w_out, w_ff1, w_ff2, loss_target, m_c_ctx, m_w_ada, m_b_ada, m_g_pre_mix, m_g_post_mix, m_g_pre_ffn, m_g_post_ffn, m_w_in, m_a_q_norm, m_a_k_norm, m_b_q_norm, m_w_q_up, m_b_kv_norm, m_w_kv_up, m_c_lower_bounds, m_c_out_norm, m_w_out, m_w_ff1, m_w_ff2, v_c_ctx, v_w_ada, v_b_ada, v_g_pre_mix, v_g_post_mix, v_g_pre_ffn, v_g_post_ffn, v_w_in, v_a_q_norm, v_a_k_norm, v_b_q_norm, v_w_q_up, v_b_kv_norm, v_w_kv_up, v_c_lower_bounds, v_c_out_norm, v_w_out, v_w_ff1, v_w_ff2):
    local = dict(c_ctx=c_ctx, w_ada=w_ada, b_ada=b_ada, g_pre_mix=g_pre_mix, g_post_mix=g_post_mix, g_pre_ffn=g_pre_ffn,
                 g_post_ffn=g_post_ffn, w_in=w_in, a_q_norm=a_q_norm, a_k_norm=a_k_norm, b_q_norm=b_q_norm, w_q_up=w_q_up,
                 b_kv_norm=b_kv_norm, w_kv_up=w_kv_up, c_lower_bounds=c_lower_bounds, c_out_norm=c_out_norm, w_out=w_out,
                 w_ff1=w_ff1, w_ff2=w_ff2)
    mom = dict(c_ctx=m_c_ctx, w_ada=m_w_ada, b_ada=m_b_ada, g_pre_mix=m_g_pre_mix, g_post_mix=m_g_post_mix,
               g_pre_ffn=m_g_pre_ffn, g_post_ffn=m_g_post_ffn, w_in=m_w_in, a_q_norm=m_a_q_norm, a_k_norm=m_a_k_norm,
               b_q_norm=m_b_q_norm, w_q_up=m_w_q_up, b_kv_norm=m_b_kv_norm, w_kv_up=m_w_kv_up,
               c_lower_bounds=m_c_lower_bounds, c_out_norm=m_c_out_norm, w_out=m_w_out, w_ff1=m_w_ff1, w_ff2=m_w_ff2)
    var = dict(c_ctx=v_c_ctx, w_ada=v_w_ada, b_ada=v_b_ada, g_pre_mix=v_g_pre_mix, g_post_mix=v_g_post_mix,
               g_pre_ffn=v_g_pre_ffn, g_post_ffn=v_g_post_ffn, w_in=v_w_in, a_q_norm=v_a_q_norm, a_k_norm=v_a_k_norm,
               b_q_norm=v_b_q_norm, w_q_up=v_w_q_up, b_kv_norm=v_b_kv_norm, w_kv_up=v_w_kv_up,
               c_lower_bounds=v_c_lower_bounds, c_out_norm=v_c_out_norm, w_out=v_w_out, w_ff1=v_w_ff1, w_ff2=v_w_ff2)

    bsz, n_lat, d = x.shape
    n_ctx = ctx.shape[1]
    depth = w_ada.shape[0]
    assert depth == 2 and n_lat % TOKEN_TILE == 0 and n_ctx % TOKEN_TILE == 0 and bsz * N_DEV + 1 <= ADA_ROWS
    ax, ay, ac = (lax.axis_index(a) for a in MESH_AXES)
    chip = 2 * ax + ay
    dev = 2 * chip + ac

    c_all = exchange("gather_c", [c], MESH_AXES, True)[0].reshape(N_DEV * bsz, d)
    big_names = list(BIG)
    gathered = exchange("gather_w", [local[n].astype(BF16) for n in big_names] + [c_lower_bounds], ("x", "y"), True)
    prm = {n: _unshard(g.astype(F32), BIG[n]) for n, g in zip(big_names, gathered[:-1])}
    prm["c_lower_bounds"] = gathered[-1].transpose(1, 2, 0, 3).reshape(depth, 2, -1)
    for n in SMALL:
        if n != "c_lower_bounds":
            prm[n] = local[n]

    n_ada = w_ada.shape[2]
    cc = jnp.concatenate([c_all, c_ctx[None, :], jnp.zeros((ADA_ROWS - N_DEV * bsz - 1, d), F32)], axis=0)
    b_blk = lax.dynamic_slice_in_dim(b_ada, chip * n_ada, n_ada, axis=1)[:, None, :]
    mod_part, ada_vjp = jax.vjp(ada_op, cc, w_ada, b_blk)
    mod_full = exchange("gather_mod", [mod_part], ("x", "y"), True)[0].transpose(1, 2, 0, 3).reshape(depth, ADA_ROWS, 6 * d)
    mod_lat = lax.dynamic_slice_in_dim(mod_full, dev * bsz, bsz, axis=1)
    mod_ctx = jnp.broadcast_to(mod_full[:, N_DEV * bsz:N_DEV * bsz + 1], mod_lat.shape)
    mods = jnp.stack([mod_lat, mod_ctx], axis=2).reshape(depth, bsz, 2, 6, d)

    xu = jnp.concatenate([x, ctx], axis=1)
    x_out, fwd_vjp = jax.vjp(lambda xu_, mods_, prm_: _forward(xu_, mods_, prm_, n_lat, n_ctx), xu, mods, prm)
    dx_out, loss_blk = loss_op(x_out, loss_target, n_lat // TOKEN_TILE)
    d_xu, d_mods, d_prm = fwd_vjp(dx_out)
    grad_x = d_xu[:, :n_lat]

    d_mods = d_mods.reshape(depth, bsz, 2, 6 * d)
    pieces = [d_mods] + [d_prm[n] for n in SMALL] + [loss_blk[0:1, 0:1]]
    sizes = [int(np.prod(p.shape)) for p in pieces]
    flat = jnp.concatenate([p.reshape(-1) for p in pieces])
    n_flat = -(-flat.shape[0] // (SUBLANE * LANE)) * SUBLANE * LANE
    flat = jnp.concatenate([flat, jnp.zeros((n_flat - flat.shape[0],), F32)]).reshape(-1, LANE)
    small_all = exchange("gather_small", [flat], MESH_AXES, True)[0]
    small_sum = sum_parts("sum_small", small_all).reshape(-1)
    offs = np.cumsum([0] + sizes)
    summed = {n: small_sum[offs[i + 1]:offs[i + 2]].reshape(d_prm[n].shape) for i, n in enumerate(SMALL)}
    loss = small_sum[offs[-2]]
    dm_all = small_all.reshape(N_DEV, -1)[:, :sizes[0]].reshape(N_DEV, depth, bsz, 2, 6 * d)
    dm_rows = dm_all.transpose(3, 0, 2, 1, 4).reshape(2, N_DEV * bsz, depth, 6 * d)
    d_ctx_row = sum_parts("sum_dmod_ctx", dm_rows[1])
    grad_b_ada = sum_parts("sum_b_ada", dm_rows.reshape(2 * N_DEV * bsz, depth, 6 * d))
    d_rows = jnp.concatenate([dm_rows[0].transpose(1, 0, 2), d_ctx_row[:, None, :],
                              jnp.zeros((depth, ADA_ROWS - N_DEV * bsz - 1, 6 * d), F32)], axis=1)
    d_cc, grad_w_ada, _ = ada_vjp(lax.dynamic_slice_in_dim(d_rows, chip * n_ada, n_ada, axis=2))
    d_cctx_all = exchange("gather_dcctx", [d_cc[N_DEV * bsz:N_DEV * bsz + 1]], MESH_AXES, True)[0]
    grad_c_ctx = sum_parts("sum_dcctx", d_cctx_all[0::2]).reshape(d)

    shard_major = [_shard_major(d_prm[n], BIG[n]) for n in big_names]
    pair = exchange("rs_pair", shard_major, ("c",), False)
    chip_sum = [sum_parts(f"rs_sum1_{n}", p) for n, p in zip(big_names, pair)]
    quad = exchange("rs_quad", chip_sum, ("x", "y"), False)
    mine = [sum_parts(f"rs_sum2_{n}", q) for n, q in zip(big_names, quad)]
    both = exchange("rs_share", mine, ("c",), True)

    grads = dict(summed)
    grads["c_lower_bounds"] = lax.dynamic_slice_in_dim(summed["c_lower_bounds"], chip * c_lower_bounds.shape[2],
                                                       c_lower_bounds.shape[2], axis=2)
    grads.update(c_ctx=grad_c_ctx, w_ada=grad_w_ada, b_ada=grad_b_ada)
    grads.update({n: g for n, g in zip(big_names, both)})

    deltas, new_m, new_v = {}, {}, {}
    for n in WEIGHTS:
        as2d = (lambda a: a[None, :]) if local[n].ndim == 1 else (lambda a: a)
        dl, nm_, nv_ = adamw("adamw_" + n, as2d(local[n]), as2d(grads[n]), as2d(mom[n]), as2d(var[n]))
        deltas[n], new_m[n], new_v[n] = (a.reshape(local[n].shape) for a in (dl, nm_, nv_))
    return (loss, grad_x, *[grads[n] for n in WEIGHTS], *[deltas[n] for n in WEIGHTS],
            *[new_m[n] for n in WEIGHTS], *[new_v[n] for n in WEIGHTS])
```

```python
import functools
from typing import Any, Callable, NamedTuple

import numpy as np
import jax
import jax.numpy as jnp
from jax import lax
from jax.experimental import pallas as pl
from jax.experimental.pallas import tpu as pltpu

F32 = jnp.float32
BF16 = jnp.bfloat16
HIGHEST = lax.Precision.HIGHEST

GRID_W = 64
HEAD_DIM = 64
A_HEADS, A_KV_HEADS = 8, 2
A_GROUP = A_HEADS // A_KV_HEADS
B_HEADS, B_Q_RANK, B_KV_RANK, B_NOPE, B_ROPE, B_V = 4, 192, 128, 64, 32, 64
C_HEADS, C_DK, C_DV = 4, 64, 64
SCAN_CHUNK = 64
ROPE_THETA = 10000.0
EPS = 1e-6
F_TINY = 1e-30
ADAM_LR, ADAM_B1, ADAM_B2, ADAM_EPS, ADAM_WD, ADAM_STEP = 0.001, 0.9, 0.999, 1e-08, 0.01, 10

IN_SIZES = (512, 128, 128, 192, 128, 32, 256, 256, 256, 256, 256)
IN_PAD = (512, 128, 128, 256, 128, 128, 256, 256, 256, 256, 256)
IN_OFF = tuple(int(v) for v in np.cumsum((0,) + IN_PAD)[:-1])
D_IN_PAD = int(sum(IN_PAD))

LANE = 128
SUBLANE = 8
TOKEN_TILE = 256
ATT_A_TQ = 64
ATT_B_TQ = 256
MM_TILE = 512
MM_TK = 2048
MM_TK_TOKENS = 512
VMEM_LIMIT = 56 * 1024 * 1024
MESH_AXES = ("x", "y", "c")
N_DEV = 8
N_CHIP = 4
ADA_ROWS = 24


class Arg(NamedTuple):
    arr: Any
    block: tuple
    imap: Callable
    kind: str
    first: Callable = None


class Out(NamedTuple):
    shape: tuple
    block: tuple
    imap: Callable


def _cparams():
    return pltpu.CompilerParams(vmem_limit_bytes=VMEM_LIMIT)


def tile_op(name, fn, grid, args, outs):
    n_in, n_out = len(args), len(outs)
    in_specs = [pl.BlockSpec(a.block, a.imap) for a in args]
    out_specs = [pl.BlockSpec(o.block, o.imap) for o in outs]
    out_shape = [jax.ShapeDtypeStruct(o.shape, F32) for o in outs]
    diff = [i for i, a in enumerate(args) if a.kind != "const"]

    def fwd_call(*arrays):
        def body(*refs):
            ids = tuple(pl.program_id(i) for i in range(len(grid)))
            res = fn(ids, *[r[...] for r in refs[:n_in]])
            for r, o in zip(refs[n_in:], res):
                r[...] = o

        return pl.pallas_call(body, grid=grid, in_specs=in_specs, out_specs=out_specs, out_shape=out_shape,
                              name=name + "_fwd", compiler_params=_cparams())(*arrays)

    def bwd_call(arrays, cts):
        def body(*refs):
            ids = tuple(pl.program_id(i) for i in range(len(grid)))
            vals = [r[...] for r in refs[:n_in]]
            ct = tuple(r[...] for r in refs[n_in:n_in + n_out])
            drefs = refs[n_in + n_out:]

            def g(*dv):
                full = list(vals)
                for i, v in zip(diff, dv):
                    full[i] = v
                return tuple(fn(ids, *full))

            _, vjp = jax.vjp(g, *[vals[i] for i in diff])
            ds = vjp(ct)
            for i, d, r in zip(diff, ds, drefs):
                if args[i].kind == "tile":
                    r[...] = d
                else:
                    is_first = args[i].first(ids)

                    @pl.when(is_first)
                    def _(r=r, d=d):
                        r[...] = d

                    @pl.when(jnp.logical_not(is_first))
                    def _(r=r, d=d):
                        r[...] += d

        d_specs = [in_specs[i] for i in diff]
        d_shape = [jax.ShapeDtypeStruct(arrays[i].shape, F32) for i in diff]
        return pl.pallas_call(body, grid=grid, in_specs=in_specs + out_specs, out_specs=d_specs, out_shape=d_shape,
                              name=name + "_bwd", compiler_params=_cparams())(*arrays, *cts)

    @jax.custom_vjp
    def op(*arrays):
        return tuple(fwd_call(*arrays))

    def op_fwd(*arrays):
        return tuple(fwd_call(*arrays)), arrays

    def op_bwd(arrays, cts):
        ds = bwd_call(arrays, cts)
        res, k = [], 0
        for i, a in enumerate(args):
            if a.kind == "const":
                res.append(jnp.zeros_like(arrays[i]))
            else:
                res.append(ds[k])
                k += 1
        return tuple(res)

    op.defvjp(op_fwd, op_bwd)
    return op(*[a.arr for a in args])


def _pick(n, cap):
    if n <= cap:
        return n
    best = None
    for t in range(LANE, cap + 1, LANE):
        if n % t == 0:
            best = t
    assert best is not None, (n, cap)
    return best


_MM_DIMS = {"nn": ((1,), (0,)), "nt": ((1,), (1,)), "tn": ((0,), (0,))}


def _mm(name, a, b, mode):
    if mode == "nn":
        (m, k), n = a.shape, b.shape[1]
    elif mode == "nt":
        (m, k), n = a.shape, b.shape[0]
    else:
        (k, m), n = a.shape, b.shape[1]
    tm, tn = _pick(m, MM_TILE), _pick(n, MM_TILE)
    tk = _pick(k, MM_TK_TOKENS if mode == "tn" else MM_TK)
    grid = (n // tn, m // tm, k // tk)
    if mode == "tn":
        a_spec = pl.BlockSpec((tk, tm), lambda j, i, kk: (kk, i))
    else:
        a_spec = pl.BlockSpec((tm, tk), lambda j, i, kk: (i, kk))
    if mode == "nt":
        b_spec = pl.BlockSpec((tn, tk), lambda j, i, kk: (j, kk))
    else:
        b_spec = pl.BlockSpec((tk, tn), lambda j, i, kk: (kk, j))
    dims = (_MM_DIMS[mode], ((), ()))

    def body(a_ref, b_ref, o_ref):
        p = lax.dot_general(a_ref[...].astype(BF16), b_ref[...].astype(BF16), dims, preferred_element_type=F32)
        kk = pl.program_id(2)

        @pl.when(kk == 0)
        def _():
            o_ref[...] = p

        @pl.when(kk != 0)
        def _():
            o_ref[...] += p

    return pl.pallas_call(body, grid=grid, in_specs=[a_spec, b_spec],
                          out_specs=pl.BlockSpec((tm, tn), lambda j, i, kk: (i, j)),
                          out_shape=jax.ShapeDtypeStruct((m, n), F32), name=name, compiler_params=_cparams())(a, b)


def linear(name, a, w):
    @jax.custom_vjp
    def op(a, w):
        return _mm(name + "_fwd", a, w, "nn")

    def op_fwd(a, w):
        return _mm(name + "_fwd", a, w, "nn"), (a, w)

    def op_bwd(res, g):
        a, w = res
        return _mm(name + "_da", g, w, "nt"), _mm(name + "_dw", a, g, "tn")

    op.defvjp(op_fwd, op_bwd)
    return op(a, w)


def _rms(x, g):
    return x * lax.rsqrt(jnp.mean(x * x, axis=-1, keepdims=True) + EPS) * g


def _sigmoid(z):
    return 1.0 / (1.0 + jnp.exp(-z))


def _silu(z):
    return z * _sigmoid(z)


def _rope(y, cos, sin_signed, swap):
    return y * cos + jnp.dot(y, swap, precision=HIGHEST, preferred_element_type=F32) * sin_signed


def _softmax_rows(s):
    m = jnp.max(s, axis=-1, keepdims=True)
    e = jnp.exp(s - m)
    return e * (1.0 / jnp.sum(e, axis=-1, keepdims=True))


def _dot_bf16(a, b, dims=((1,), (0,))):
    return lax.dot_general(a.astype(BF16), b.astype(BF16), (dims, ((), ())), preferred_element_type=F32)


def _gla_step(state, q, k, v, g, reverse):
    c, d = q.shape
    row = lax.broadcasted_iota(jnp.int32, (c, c), 0)
    col = lax.broadcasted_iota(jnp.int32, (c, c), 1)
    tri = (row <= col) if reverse else (row >= col)
    b = jnp.dot(tri.astype(F32), g, precision=HIGHEST, preferred_element_type=F32)
    r3 = lax.broadcasted_iota(jnp.int32, (c, c, d), 0)
    c3 = lax.broadcasted_iota(jnp.int32, (c, c, d), 1)
    tri3 = (r3 <= c3) if reverse else (r3 >= c3)
    diff = b[:, None, :] - b[None, :, :]
    decay = jnp.where(tri3, jnp.exp(jnp.where(tri3, diff, 0.0)), 0.0)
    scores = jnp.sum(q[:, None, :] * k[None, :, :] * decay, axis=-1)
    o = jnp.dot(scores, v, preferred_element_type=F32) + jnp.dot(q * jnp.exp(b), state, preferred_element_type=F32)
    b_end = b[0:1, :] if reverse else b[c - 1:c, :]
    kd = k * jnp.exp(b_end - b)
    new_state = state * jnp.exp(b_end).reshape(d, 1) + lax.dot_general(kd, v, (((0,), (0,)), ((), ())), preferred_element_type=F32)
    return new_state, o


def gla_scan(name, q, k, v, g, reverse, n_lat_chunks):
    bsz, nh, t, d = q.shape
    c = SCAN_CHUNK
    n = t // c

    def chunk_of(j):
        return (n - 1 - j) if reverse else lax.rem(j + n_lat_chunks, n)

    blk = (None, nh, c, d)
    st_blk = (None, nh, None, d, d)

    def fwd_call(q, k, v, g):
        def body(q_ref, k_ref, v_ref, g_ref, o_ref, states_ref, st):
            @pl.when(pl.program_id(1) == 0)
            def _():
                st[...] = jnp.zeros_like(st)
            for h in range(nh):
                s = st[h]
                states_ref[h] = s
                ns, o = _gla_step(s, q_ref[h], k_ref[h], v_ref[h], g_ref[h], reverse)
                st[h] = ns
                o_ref[h] = o

        spec = pl.BlockSpec(blk, lambda b, j: (b, 0, chunk_of(j), 0))
        return pl.pallas_call(
            body, grid=(bsz, n), in_specs=[spec] * 4,
            out_specs=[spec, pl.BlockSpec(st_blk, lambda b, j: (b, 0, j, 0, 0))],
            out_shape=[jax.ShapeDtypeStruct(q.shape, F32), jax.ShapeDtypeStruct((bsz, nh, n, d, d), F32)],
            scratch_shapes=[pltpu.VMEM((nh, d, d), F32)], name=name + "_fwd", compiler_params=_cparams())(q, k, v, g)

    def bwd_call(q, k, v, g, states, do):
        def body(q_ref, k_ref, v_ref, g_ref, s_ref, do_ref, dq_ref, dk_ref, dv_ref, dg_ref, dst):
            @pl.when(pl.program_id(1) == 0)
            def _():
                dst[...] = jnp.zeros_like(dst)
            step = functools.partial(_gla_step, reverse=reverse)
            for h in range(nh):
                _, vjp = jax.vjp(step, s_ref[h], q_ref[h], k_ref[h], v_ref[h], g_ref[h])
                ds, dq, dk, dv, dg = vjp((dst[h], do_ref[h]))
                dst[h] = ds
                dq_ref[h] = dq
                dk_ref[h] = dk
                dv_ref[h] = dv
                dg_ref[h] = dg

        spec = pl.BlockSpec(blk, lambda b, jj: (b, 0, chunk_of(n - 1 - jj), 0))
        s_spec = pl.BlockSpec(st_blk, lambda b, jj: (b, 0, n - 1 - jj, 0, 0))
        return pl.pallas_call(
            body, grid=(bsz, n), in_specs=[spec] * 4 + [s_spec, spec], out_specs=[spec] * 4,
            out_shape=[jax.ShapeDtypeStruct(q.shape, F32)] * 4,
            scratch_shapes=[pltpu.VMEM((nh, d, d), F32)], name=name + "_bwd", compiler_params=_cparams())(q, k, v, g, states, do)

    @jax.custom_vjp
    def op(q, k, v, g):
        return fwd_call(q, k, v, g)[0]

    def op_fwd(q, k, v, g):
        o, states = fwd_call(q, k, v, g)
        return o, (q, k, v, g, states)

    def op_bwd(res, do):
        return tuple(bwd_call(*res, do))

    op.defvjp(op_fwd, op_bwd)
    return op(q, k, v, g)


def _zero_ids(ids):
    z = ids[0] == 0
    for i in ids[1:]:
        z = jnp.logical_and(z, i == 0)
    return z


def _token_grid(x, n_lat_tiles):
    bsz, t, d = x.shape
    nt = t // TOKEN_TILE
    row = lambda w: ((None, TOKEN_TILE, w), lambda b, i: (b, i, 0))
    mod_block = (None, None, 6, d)
    mod_imap = lambda b, i: (b, (i >= n_lat_tiles).astype(jnp.int32), 0, 0)
    mod_first = lambda ids: jnp.logical_or(ids[1] == 0, ids[1] == n_lat_tiles)
    return bsz, t, d, nt, row, (mod_block, mod_imap, mod_first)


def premod(name, x, gain, mods, r0, n_lat_tiles):
    bsz, t, d, nt, row, (mb, mi, mf) = _token_grid(x, n_lat_tiles)

    def fn(ids, xb, gb, mod):
        return (_rms(xb, gb) * (1.0 + mod[r0 + 1:r0 + 2]) + mod[r0:r0 + 1],)

    args = [Arg(x, *row(d), "tile"), Arg(gain, (1, d), lambda b, i: (0, 0), "acc", _zero_ids), Arg(mods, mb, mi, "acc", mf)]
    return tile_op(name, fn, (bsz, nt), args, [Out(x.shape, *row(d))])[0]


def resid(name, x, y, gain, mods, r, n_lat_tiles):
    bsz, t, d, nt, row, (mb, mi, mf) = _token_grid(x, n_lat_tiles)

    def fn(ids, xb, yb, gb, mod):
        return (xb + mod[r:r + 1] * _rms(yb, gb),)

    args = [Arg(x, *row(d), "tile"), Arg(y, *row(d), "tile"), Arg(gain, (1, d), lambda b, i: (0, 0), "acc", _zero_ids),
            Arg(mods, mb, mi, "acc", mf)]
    return tile_op(name, fn, (bsz, nt), args, [Out(x.shape, *row(d))])[0]


def sq_relu(name, u):
    m, n = u.shape
    spec = ((TOKEN_TILE, n), lambda i: (i, 0))
    fn = lambda ids, ub: (jnp.square(jnp.maximum(ub, 0.0)),)
    return tile_op(name, fn, (m // TOKEN_TILE,), [Arg(u, *spec, "tile")], [Out(u.shape, *spec)])[0]


def _head_spec(w):
    return (None, None, TOKEN_TILE, w), lambda b, h, i: (b, h, i, 0)


def a_prep(name, x, gain, cos, sin, swap):
    bsz, nh, t, d = x.shape
    tab = ((TOKEN_TILE, d), lambda b, h, i: (i, 0))

    def fn(ids, xb, gb, cb, sb, pb):
        return (_rope(_rms(xb, gb), cb, sb, pb),)

    args = [Arg(x, *_head_spec(d), "tile"), Arg(gain, (1, d), lambda b, h, i: (0, 0), "acc", _zero_ids),
            Arg(cos, *tab, "const"), Arg(sin, *tab, "const"), Arg(swap, (d, d), lambda b, h, i: (0, 0), "const")]
    return tile_op(name, fn, (bsz, nh, t // TOKEN_TILE), args, [Out(x.shape, *_head_spec(d))])[0]


def b_prep(name, bqd, bkvd, bkr, bqn, wq_nope, wq_pe, bkvn, wkv_nope, wkv_v, cos_q, sin_q, swap_q, cos_k, sin_k, swap_k):
    bsz, t, _ = bqd.shape
    row = lambda w: ((None, TOKEN_TILE, w), lambda b, i: (b, i, 0))
    whole = lambda a: (a.shape, lambda b, i: (0,) * a.ndim)
    tab = lambda w: ((TOKEN_TILE, w), lambda b, i: (i, 0))

    def fn(ids, qd, kvd, kr, qn, wqn, wqp, kvn, wkn, wkv, cq, sq, pq, ck, sk, pk):
        hq = _rms(qd, qn)
        hkv = _rms(kvd, kvn)
        return (_dot_bf16(hq, wqn), _rope(_dot_bf16(hq, wqp), cq, sq, pq), _dot_bf16(hkv, wkn), _dot_bf16(hkv, wkv),
                _rope(kr, ck, sk, pk))

    params = [bqn, wq_nope, wq_pe, bkvn, wkv_nope, wkv_v]
    args = [Arg(bqd, *row(B_Q_RANK), "tile"), Arg(bkvd, *row(B_KV_RANK), "tile"), Arg(bkr, *row(B_ROPE), "tile")]
    args += [Arg(p, *whole(p), "acc", _zero_ids) for p in params]
    args += [Arg(cos_q, *tab(cos_q.shape[1]), "const"), Arg(sin_q, *tab(cos_q.shape[1]), "const"), Arg(swap_q, *whole(swap_q), "const"),
             Arg(cos_k, *tab(B_ROPE), "const"), Arg(sin_k, *tab(B_ROPE), "const"), Arg(swap_k, *whole(swap_k), "const")]
    widths = (B_HEADS * B_NOPE, B_HEADS * B_ROPE, B_HEADS * B_NOPE, B_HEADS * B_V, B_ROPE)
    outs = [Out((bsz, t, w), *row(w)) for w in widths]
    return tile_op(name, fn, (bsz, t // TOKEN_TILE), args, outs)


def c_prep(name, cq, cff, cfb, clb, layer):
    bsz, nh, t, d = cq.shape
    depth = clb.shape[1]
    spec = ((None, None, TOKEN_TILE, d), lambda h, b, i: (b, h, i, 0))

    def fn(ids, q, zf, zb, lbs):
        m = lbs[0]
        for j in range(1, depth):
            m = jnp.maximum(m, lbs[j])
        e = [jnp.exp(lbs[j] - m) for j in range(depth)]
        tot = e[0]
        for j in range(1, depth):
            tot = tot + e[j]
        p = [ej / tot for ej in e]
        cum = p[0]
        for j in range(1, layer + 1):
            cum = cum + p[j]
        lower = cum - p[0]

        def gate(z, lb):
            f = lb + (1.0 - lb) * _sigmoid(z)
            return jnp.log(jnp.maximum(f, F_TINY)), (1.0 - lb) * _sigmoid(-z)

        gf, kf = gate(zf, lower[0:1])
        gb, kb = gate(zb, lower[1:2])
        return _silu(q), kf, gf, kb, gb

    first = lambda ids: jnp.logical_and(ids[1] == 0, ids[2] == 0)
    args = [Arg(cq, *spec, "tile"), Arg(cff, *spec, "tile"), Arg(cfb, *spec, "tile"),
            Arg(clb, (None, depth, 2, d), lambda h, b, i: (h, 0, 0, 0), "acc", first)]
    return tile_op(name, fn, (nh, bsz, t // TOKEN_TILE), args, [Out(cq.shape, *spec)] * 5)


def c_readout(name, o_f, o_b, gate, gain):
    bsz, nh, t, d = o_f.shape

    def fn(ids, of, ob, gt, gn):
        return (_rms(of + ob, gn) * _silu(gt),)

    args = [Arg(o_f, *_head_spec(d), "tile"), Arg(o_b, *_head_spec(d), "tile"), Arg(gate, *_head_spec(d), "tile"),
            Arg(gain, (1, d), lambda b, h, i: (0, 0), "acc", _zero_ids)]
    return tile_op(name, fn, (bsz, nh, t // TOKEN_TILE), args, [Out(o_f.shape, *_head_spec(d))])[0]


def _key_bias(t, tq, i, n_lat):
    key = lax.broadcasted_iota(jnp.int32, (1, t), 1)
    return jnp.where(jnp.logical_and(i * tq >= n_lat, key < n_lat), -1e30, 0.0)


def attention_a(name, q, k, v, n_lat):
    bsz, hk, grp, t, d = q.shape
    tq = ATT_A_TQ
    scale = HEAD_DIM ** -0.5
    q_spec = ((None, None, grp, tq, d), lambda b, h, i: (b, h, 0, i, 0))
    kv_spec = ((None, None, d, t), lambda b, h, i: (b, h, 0, 0))

    def fn(ids, qb, kb, vb):
        qq = qb.reshape(grp * tq, d)
        s = _dot_bf16(qq, kb) * scale + _key_bias(t, tq, ids[2], n_lat)
        return (_dot_bf16(_softmax_rows(s), vb, ((1,), (1,))).reshape(grp, tq, d),)

    first = lambda ids: ids[2] == 0
    args = [Arg(q, *q_spec, "tile"), Arg(k, *kv_spec, "acc", first), Arg(v, *kv_spec, "acc", first)]
    return tile_op(name, fn, (bsz, hk, t // tq), args, [Out(q.shape, *q_spec)])[0]


def attention_b(name, qn, qp, kn, kp, v, n_lat):
    bsz, nh, t, _ = qn.shape
    tq = ATT_B_TQ
    scale = (B_NOPE + B_ROPE) ** -0.5
    q_spec = lambda w: ((None, None, tq, w), lambda b, h, i: (b, h, i, 0))
    kv_spec = lambda w: ((None, None, w, t), lambda b, h, i: (b, h, 0, 0))

    def fn(ids, qnb, qpb, knb, kpb, vb):
        s = (_dot_bf16(qnb, knb) + _dot_bf16(qpb, kpb)) * scale + _key_bias(t, tq, ids[2], n_lat)
        return (_dot_bf16(_softmax_rows(s), vb, ((1,), (1,))),)

    first = lambda ids: ids[2] == 0
    first_b = lambda ids: jnp.logical_and(ids[1] == 0, ids[2] == 0)
    args = [Arg(qn, *q_spec(B_NOPE), "tile"), Arg(qp, *q_spec(B_ROPE), "tile"), Arg(kn, *kv_spec(B_NOPE), "acc", first),
            Arg(kp, (None, B_ROPE, t), lambda b, h, i: (b, 0, 0), "acc", first_b), Arg(v, *kv_spec(B_V), "acc", first)]
    return tile_op(name, fn, (bsz, nh, t // tq), args, [Out((bsz, nh, t, B_V), *q_spec(B_V))])[0]


def ada_op(cc, w, b):
    depth, d, n = w.shape

    def fn(ids, ccb, wb, bb):
        return (_dot_bf16(_silu(ccb), wb) + bb,)

    args = [Arg(cc, cc.shape, lambda l: (0, 0), "acc", lambda ids: ids[0] == 0),
            Arg(w, (None, d, n), lambda l: (l, 0, 0), "tile"), Arg(b, (None, 1, n), lambda l: (l, 0, 0), "tile")]
    return tile_op("ada", fn, (depth,), args, [Out((depth, cc.shape[0], n), (None, cc.shape[0], n), lambda l: (l, 0, 0))])[0]


def loss_op(xu, target, n_lat_tiles):
    bsz, t, d = xu.shape

    def body(x_ref, t_ref, dx_ref, l_ref):
        b, i = pl.program_id(0), pl.program_id(1)

        @pl.when(jnp.logical_and(b == 0, i == 0))
        def _():
            l_ref[...] = jnp.zeros_like(l_ref)

        @pl.when(i < n_lat_tiles)
        def _():
            err = x_ref[...] - t_ref[...]
            dx_ref[...] = err * (1.0 / d)
            l_ref[...] += 0.5 * jnp.sum(jnp.mean(err * err, axis=-1))

        @pl.when(i >= n_lat_tiles)
        def _():
            dx_ref[...] = jnp.zeros_like(dx_ref)

    row = pl.BlockSpec((None, TOKEN_TILE, d), lambda b, i: (b, i, 0))
    t_spec = pl.BlockSpec((None, TOKEN_TILE, d), lambda b, i: (b, jnp.minimum(i, n_lat_tiles - 1), 0))
    return pl.pallas_call(body, grid=(bsz, t // TOKEN_TILE), in_specs=[row, t_spec],
                          out_specs=[row, pl.BlockSpec((SUBLANE, LANE), lambda b, i: (0, 0))],
                          out_shape=[jax.ShapeDtypeStruct(xu.shape, F32), jax.ShapeDtypeStruct((SUBLANE, LANE), F32)],
                          name="loss", compiler_params=_cparams())(xu, target)


def _row_tile(rows, row_bytes, budget=4 << 20):
    if rows * row_bytes <= budget:
        return rows
    best = None
    for t in range(SUBLANE, rows, SUBLANE):
        if rows % t == 0 and t * row_bytes <= budget:
            best = t
    return best if best is not None else rows


def _as3d(x):
    p = x.shape[0]
    c = x.shape[-1]
    return x.reshape(p, -1, c)


def sum_parts(name, x):
    x3 = _as3d(x)
    p, r, c = x3.shape
    tr = _row_tile(r, p * c * 4)

    def body(x_ref, o_ref):
        s = x_ref[0]
        for j in range(1, p):
            s = s + x_ref[j]
        o_ref[...] = s

    out = pl.pallas_call(body, grid=(r // tr,), in_specs=[pl.BlockSpec((p, tr, c), lambda i: (0, i, 0))],
                         out_specs=pl.BlockSpec((tr, c), lambda i: (i, 0)), out_shape=jax.ShapeDtypeStruct((r, c), F32),
                         name=name, compiler_params=_cparams())(x3)
    return out.reshape(x.shape[1:])


def adamw(name, w, g, m, v):
    shape = w.shape
    c = shape[-1]
    to2d = lambda a: a.reshape(-1, c)
    r = to2d(w).shape[0]
    tr = _row_tile(r, 7 * c * 4, budget=6 << 20)

    def body(w_ref, g_ref, m_ref, v_ref, d_ref, nm_ref, nv_ref):
        gg = g_ref[...]
        nm = ADAM_B1 * m_ref[...] + (1.0 - ADAM_B1) * gg
        nv = ADAM_B2 * v_ref[...] + (1.0 - ADAM_B2) * jnp.square(gg)
        m_hat = nm / (1.0 - ADAM_B1 ** ADAM_STEP)
        v_hat = nv / (1.0 - ADAM_B2 ** ADAM_STEP)
        d_ref[...] = -ADAM_LR * (m_hat / (jnp.sqrt(v_hat) + ADAM_EPS) + ADAM_WD * w_ref[...])
        nm_ref[...] = nm
        nv_ref[...] = nv

    spec = pl.BlockSpec((tr, c), lambda i: (i, 0))
    outs = pl.pallas_call(body, grid=(r // tr,), in_specs=[spec] * 4, out_specs=[spec] * 3,
                          out_shape=[jax.ShapeDtypeStruct((r, c), F32)] * 3, name=name, compiler_params=_cparams())(
        to2d(w), to2d(g), to2d(m), to2d(v))
    return tuple(o.reshape(shape) for o in outs)


def _n_pieces(shape, dtype, limit):
    if limit <= 1 or len(shape) < 2:
        return 1
    rows_per_tile = 1 if len(shape) >= 3 else SUBLANE * (4 // jnp.dtype(dtype).itemsize)
    for k in range(min(limit, shape[0]), 0, -1):
        if shape[0] % k == 0 and (shape[0] // k) % rows_per_tile == 0:
            return k
    return 1


def exchange(name, srcs, group, same, pieces=1):
    p = 2 ** len(group)
    n = len(srcs)
    out_shape = [jax.ShapeDtypeStruct(((p,) + s.shape) if same else s.shape, s.dtype) for s in srcs]
    n_pc = [_n_pieces(s.shape if same else s.shape[1:], s.dtype, pieces) for s in srcs]
    max_pc = max(n_pc)

    def body(*refs):
        src_refs, out_refs = refs[:n], refs[n:2 * n]
        send_sems, recv_sems, local_sems = refs[2 * n:]
        pos = {a: lax.axis_index(a) for a in MESH_AXES}

        def index_of(coords):
            idx = 0
            for a in group:
                idx = idx * 2 + coords[a]
            return idx

        me = index_of(pos)
        peers = []
        for rel in range(1, p):
            coords = dict(pos)
            for bit, a in enumerate(reversed(group)):
                if (rel >> bit) & 1:
                    coords[a] = 1 - coords[a]
            peers.append((coords, index_of(coords)))

        def src_for(a, idx):
            return src_refs[a] if same else src_refs[a].at[idx]

        def piece(ref, a, k):
            if n_pc[a] == 1:
                return ref
            rows = ref.shape[0] // n_pc[a]
            return ref.at[pl.ds(k * rows, rows)]

        local, sends, recvs = [], [], []
        for a in range(n):
            for k in range(n_pc[a]):
                cp = pltpu.make_async_copy(piece(src_for(a, me), a, k), piece(out_refs[a].at[me], a, k), local_sems.at[a, k])
                cp.start()
                local.append(cp)
                for r, (coords, idx) in enumerate(peers):
                    dev = tuple(coords[ax] for ax in MESH_AXES)
                    src = piece(src_for(a, idx), a, k)
                    send = pltpu.make_async_remote_copy(src_ref=src, dst_ref=piece(out_refs[a].at[me], a, k),
                                                        send_sem=send_sems.at[a, r, k], recv_sem=recv_sems.at[a, r, k],
                                                        device_id=dev, device_id_type=pl.DeviceIdType.MESH)
                    send.start()
                    sends.append(send)
                    recvs.append(pltpu.make_async_remote_copy(src_ref=src, dst_ref=piece(out_refs[a].at[idx], a, k),
                                                              send_sem=send_sems.at[a, r, k], recv_sem=recv_sems.at[a, r, k],
                                                              device_id=dev, device_id_type=pl.DeviceIdType.MESH))
        for cp in sends:
            cp.wait_send()
        for cp in recvs:
            cp.wait_recv()
        for cp in local:
            cp.wait()

    any_spec = pl.BlockSpec(memory_space=pl.ANY)
    outs = pl.pallas_call(body, in_specs=[any_spec] * n, out_specs=[any_spec] * n, out_shape=out_shape,
                          scratch_shapes=[pltpu.SemaphoreType.DMA((n, p - 1, max_pc)), pltpu.SemaphoreType.DMA((n, p - 1, max_pc)),
                                          pltpu.SemaphoreType.DMA((n, max_pc))],
                          name=name)(*srcs)
    return list(outs)


def _rope_tables(n_lat, n_ctx, rot_dim, heads):
    n_freq = rot_dim // 4
    tok = jnp.arange(n_lat, dtype=jnp.int32)
    inv = ROPE_THETA ** (-jnp.arange(n_freq, dtype=F32) / n_freq)
    ang = jnp.concatenate([(tok // GRID_W).astype(F32)[:, None] * inv, (tok % GRID_W).astype(F32)[:, None] * inv], axis=-1)
    cos, sin = jnp.cos(ang), jnp.sin(ang)
    cos = jnp.concatenate([jnp.concatenate([cos, cos], -1), jnp.ones((n_ctx, rot_dim), F32)], 0)
    sin = jnp.concatenate([jnp.concatenate([-sin, sin], -1), jnp.zeros((n_ctx, rot_dim), F32)], 0)
    half = rot_dim // 2
    w = heads * rot_dim
    j = np.arange(w)
    src = (j // rot_dim) * rot_dim + (j % rot_dim + half) % rot_dim
    swap = np.zeros((w, w), np.float32)
    swap[src, j] = 1.0
    return jnp.tile(cos, (1, heads)), jnp.tile(sin, (1, heads)), jnp.asarray(swap)


def _to_heads(a, nh):
    b, t, _ = a.shape
    return a.reshape(b, t, nh, -1).transpose(0, 2, 1, 3)


def _to_heads_t(a, nh):
    b, t, _ = a.shape
    return a.reshape(b, t, nh, -1).transpose(0, 2, 3, 1)


def _from_heads(a):
    b, nh, t, w = a.shape
    return a.transpose(0, 2, 1, 3).reshape(b, t, nh * w)


def _pad_w_in(w_in):
    parts, off = [], 0
    for size, pad in zip(IN_SIZES, IN_PAD):
        parts.append(w_in[:, off:off + size])
        if pad > size:
            parts.append(jnp.zeros((w_in.shape[0], pad - size), w_in.dtype))
        off += size
    return jnp.concatenate(parts, axis=1)


def _forward(xu, mods, prm, n_lat, n_ctx):
    bsz, t, d = xu.shape
    depth = mods.shape[0]
    n_lat_tiles = n_lat // TOKEN_TILE
    rope_a = _rope_tables(n_lat, n_ctx, HEAD_DIM, 1)
    rope_bq = _rope_tables(n_lat, n_ctx, B_ROPE, B_HEADS)
    rope_bk = _rope_tables(n_lat, n_ctx, B_ROPE, 1)
    clb = prm["c_lower_bounds"].reshape(depth, 2, C_HEADS, C_DK).transpose(2, 0, 1, 3)
    for l in range(depth):
        nm = lambda s: f"l{l}_{s}"
        vec = lambda name: prm[name][l][None, :]
        h = premod(nm("premix"), xu, vec("g_pre_mix"), mods[l], 0, n_lat_tiles)
        z = linear(nm("w_in"), h.reshape(bsz * t, d), _pad_w_in(prm["w_in"][l])).reshape(bsz, t, D_IN_PAD)
        seg = lambda i: z[:, :, IN_OFF[i]:IN_OFF[i] + IN_SIZES[i]]
        aq = a_prep(nm("aq"), _to_heads(seg(0), A_HEADS), vec("a_q_norm"), *rope_a)
        ak = a_prep(nm("ak"), _to_heads(seg(1), A_KV_HEADS), vec("a_k_norm"), *rope_a)
        av = _to_heads(seg(2), A_KV_HEADS)
        ya = attention_a(nm("att_a"), aq.reshape(bsz, A_KV_HEADS, A_GROUP, t, HEAD_DIM), ak.transpose(0, 1, 3, 2),
                         av.transpose(0, 1, 3, 2), n_lat)
        ya = _from_heads(ya.reshape(bsz, A_HEADS, t, HEAD_DIM))
        wq = prm["w_q_up"][l].reshape(B_Q_RANK, B_HEADS, B_NOPE + B_ROPE)
        wkv = prm["w_kv_up"][l].reshape(B_KV_RANK, B_HEADS, B_NOPE + B_V)
        qn, qp, kn, bv, kp = b_prep(
            nm("b_prep"), seg(3), seg(4), seg(5), vec("b_q_norm"),
            wq[:, :, :B_NOPE].reshape(B_Q_RANK, -1), wq[:, :, B_NOPE:].reshape(B_Q_RANK, -1), vec("b_kv_norm"),
            wkv[:, :, :B_NOPE].reshape(B_KV_RANK, -1), wkv[:, :, B_NOPE:].reshape(B_KV_RANK, -1), *rope_bq, *rope_bk)
        yb = attention_b(nm("att_b"), _to_heads(qn, B_HEADS), _to_heads(qp, B_HEADS), _to_heads_t(kn, B_HEADS),
                         kp.transpose(0, 2, 1), _to_heads_t(bv, B_HEADS), n_lat)
        yb = _from_heads(yb)
        cq, kf, gf, kb, gb = c_prep(nm("c_prep"), _to_heads(seg(6), C_HEADS), _to_heads(seg(7), C_HEADS),
                                    _to_heads(seg(8), C_HEADS), clb, l)
        cv = _to_heads(seg(9), C_HEADS)
        o_f = gla_scan(nm("scan_f"), cq, kf, cv, gf, False, n_lat // SCAN_CHUNK)
        o_b = gla_scan(nm("scan_b"), cq, kb, cv, gb, True, n_lat // SCAN_CHUNK)
        yc = _from_heads(c_readout(nm("c_out"), o_f, o_b, _to_heads(seg(10), C_HEADS), vec("c_out_norm")))
        y = linear(nm("w_out"), jnp.concatenate([ya, yb, yc], axis=-1).reshape(bsz * t, d), prm["w_out"][l]).reshape(bsz, t, d)
        x1 = resid(nm("res_mix"), xu, y, vec("g_post_mix"), mods[l], 2, n_lat_tiles)
        h2 = premod(nm("preffn"), x1, vec("g_pre_ffn"), mods[l], 3, n_lat_tiles)
        u = linear(nm("ff1"), h2.reshape(bsz * t, d), prm["w_ff1"][l])
        f = linear(nm("ff2"), sq_relu(nm("sqrelu"), u), prm["w_ff2"][l]).reshape(bsz, t, d)
        xu = resid(nm("res_ffn"), x1, f, vec("g_post_ffn"), mods[l], 5, n_lat_tiles)
    return xu


BIG = {"w_in": 2, "w_q_up": 2, "w_kv_up": 2, "w_out": 1, "w_ff1": 2, "w_ff2": 1}
SMALL = ("g_pre_mix", "g_post_mix", "g_pre_ffn", "g_post_ffn", "a_q_norm", "a_k_norm", "b_q_norm", "b_kv_norm",
         "c_lower_bounds", "c_out_norm")
WEIGHTS = ("c_ctx", "w_ada", "b_ada", "g_pre_mix", "g_post_mix", "g_pre_ffn", "g_post_ffn", "w_in", "a_q_norm", "a_k_norm",
           "b_q_norm", "w_q_up", "b_kv_norm", "w_kv_up", "c_lower_bounds", "c_out_norm", "w_out", "w_ff1", "w_ff2")


def _unshard(g, axis):
    depth, _, r, c = g.shape
    if axis == 1:
        return g.reshape(depth, N_CHIP * r, c)
    return g.transpose(0, 2, 1, 3).reshape(depth, r, N_CHIP * c)


def _shard_major(g, axis):
    depth, r, c = g.shape
    if axis == 1:
        return g.reshape(depth, N_CHIP, r // N_CHIP, c)
    return g.reshape(depth, r, N_CHIP, c // N_CHIP).transpose(0, 2, 1, 3)


def kernel(x, c, ctx, c_ctx, w_ada, b_ada, g_pre_mix, g_post_mix, g_pre_ffn, g_post_ffn, w_in, a_q_norm, a_k_norm, b_q_norm, w_q_up, b_kv_norm, w_kv_up, c_lower_bounds, c_out_norm, w_out, w_ff1, w_ff2, loss_target, m_c_ctx, m_w_ada, m_b_ada, m_g_pre_mix, m_g_post_mix, m_g_pre_ffn, m_g_post_ffn, m_w_in, m_a_q_norm, m_a_k_norm, m_b_q_norm, m_w_q_up, m_b_kv_norm, m_w_kv_up, m_c_lower_bounds, m_c_out_norm, m_w_out, m_w_ff1, m_w_ff2, v_c_ctx, v_w_ada, v_b_ada, v_g_pre_mix, v_g_post_mix, v_g_pre_ffn, v_g_post_ffn, v_w_in, v_a_q_norm, v_a_k_norm, v_b_q_norm, v_w_q_up, v_b_kv_norm, v_w_kv_up, v_c_lower_bounds, v_c_out_norm, v_w_out, v_w_ff1, v_w_ff2):
    local = dict(c_ctx=c_ctx, w_ada=w_ada, b_ada=b_ada, g_pre_mix=g_pre_mix, g_post_mix=g_post_mix, g_pre_ffn=g_pre_ffn,
                 g_post_ffn=g_post_ffn, w_in=w_in, a_q_norm=a_q_norm, a_k_norm=a_k_norm, b_q_norm=b_q_norm, w_q_up=w_q_up,
                 b_kv_norm=b_kv_norm, w_kv_up=w_kv_up, c_lower_bounds=c_lower_bounds, c_out_norm=c_out_norm, w_out=w_out,
                 w_ff1=w_ff1, w_ff2=w_ff2)
    mom = dict(c_ctx=m_c_ctx, w_ada=m_w_ada, b_ada=m_b_ada, g_pre_mix=m_g_pre_mix, g_post_mix=m_g_post_mix,
               g_pre_ffn=m_g_pre_ffn, g_post_ffn=m_g_post_ffn, w_in=m_w_in, a_q_norm=m_a_q_norm, a_k_norm=m_a_k_norm,
               b_q_norm=m_b_q_norm, w_q_up=m_w_q_up, b_kv_norm=m_b_kv_norm, w_kv_up=m_w_kv_up,
               c_lower_bounds=m_c_lower_bounds, c_out_norm=m_c_out_norm, w_out=m_w_out, w_ff1=m_w_ff1, w_ff2=m_w_ff2)
    var = dict(c_ctx=v_c_ctx, w_ada=v_w_ada, b_ada=v_b_ada, g_pre_mix=v_g_pre_mix, g_post_mix=v_g_post_mix,
               g_pre_ffn=v_g_pre_ffn, g_post_ffn=v_g_post_ffn, w_in=v_w_in, a_q_norm=v_a_q_norm, a_k_norm=v_a_k_norm,
               b_q_norm=v_b_q_norm, w_q_up=v_w_q_up, b_kv_norm=v_b_kv_norm, w_kv_up=v_w_kv_up,
               c_lower_bounds=v_c_lower_bounds, c_out_norm=v_c_out_norm, w_out=v_w_out, w_ff1=v_w_ff1, w_ff2=v_w_ff2)

    bsz, n_lat, d = x.shape
    n_ctx = ctx.shape[1]
    depth = w_ada.shape[0]
    assert depth == 2 and n_lat % TOKEN_TILE == 0 and n_ctx % TOKEN_TILE == 0 and bsz * N_DEV + 1 <= ADA_ROWS
    ax, ay, ac = (lax.axis_index(a) for a in MESH_AXES)
    chip = 2 * ax + ay
    dev = 2 * chip + ac

    c_all = exchange("gather_c", [c], MESH_AXES, True)[0].reshape(N_DEV * bsz, d)
    big_names = list(BIG)
    mine = [lax.dynamic_index_in_dim(local[n].astype(BF16), ac, axis=0, keepdims=False) for n in big_names]
    over_chips = exchange("gather_w_quad", mine + [c_lower_bounds], ("x", "y"), True)
    both_layers = exchange("gather_w_pair", over_chips[:-1], ("c",), True, pieces=8)
    prm = {n: _unshard(g.astype(F32), BIG[n]) for n, g in zip(big_names, both_layers)}
    prm["c_lower_bounds"] = over_chips[-1].transpose(1, 2, 0, 3).reshape(depth, 2, -1)
    for n in SMALL:
        if n != "c_lower_bounds":
            prm[n] = local[n]

    n_ada = w_ada.shape[2]
    cc = jnp.concatenate([c_all, c_ctx[None, :], jnp.zeros((ADA_ROWS - N_DEV * bsz - 1, d), F32)], axis=0)
    b_blk = lax.dynamic_slice_in_dim(b_ada, chip * n_ada, n_ada, axis=1)[:, None, :]
    mod_part, ada_vjp = jax.vjp(ada_op, cc, w_ada, b_blk)
    mod_full = exchange("gather_mod", [mod_part], ("x", "y"), True)[0].transpose(1, 2, 0, 3).reshape(depth, ADA_ROWS, 6 * d)
    mod_lat = lax.dynamic_slice_in_dim(mod_full, dev * bsz, bsz, axis=1)
    mod_ctx = jnp.broadcast_to(mod_full[:, N_DEV * bsz:N_DEV * bsz + 1], mod_lat.shape)
    mods = jnp.stack([mod_lat, mod_ctx], axis=2).reshape(depth, bsz, 2, 6, d)

    xu = jnp.concatenate([x, ctx], axis=1)
    x_out, fwd_vjp = jax.vjp(lambda xu_, mods_, prm_: _forward(xu_, mods_, prm_, n_lat, n_ctx), xu, mods, prm)
    dx_out, loss_blk = loss_op(x_out, loss_target, n_lat // TOKEN_TILE)
    d_xu, d_mods, d_prm = fwd_vjp(dx_out)
    grad_x = d_xu[:, :n_lat]

    d_mods = d_mods.reshape(depth, bsz, 2, 6 * d)
    pieces = [d_mods] + [d_prm[n] for n in SMALL] + [loss_blk[0:1, 0:1]]
    sizes = [int(np.prod(p.shape)) for p in pieces]
    flat = jnp.concatenate([p.reshape(-1) for p in pieces])
    n_flat = -(-flat.shape[0] // (SUBLANE * LANE)) * SUBLANE * LANE
    flat = jnp.concatenate([flat, jnp.zeros((n_flat - flat.shape[0],), F32)]).reshape(-1, LANE)
    small_all = exchange("gather_small", [flat], MESH_AXES, True)[0]
    small_sum = sum_parts("sum_small", small_all).reshape(-1)
    offs = np.cumsum([0] + sizes)
    summed = {n: small_sum[offs[i + 1]:offs[i + 2]].reshape(d_prm[n].shape) for i, n in enumerate(SMALL)}
    loss = small_sum[offs[-2]]
    dm_all = small_all.reshape(N_DEV, -1)[:, :sizes[0]].reshape(N_DEV, depth, bsz, 2, 6 * d)
    dm_rows = dm_all.transpose(3, 0, 2, 1, 4).reshape(2, N_DEV * bsz, depth, 6 * d)
    d_ctx_row = sum_parts("sum_dmod_ctx", dm_rows[1])
    grad_b_ada = sum_parts("sum_b_ada", dm_rows.reshape(2 * N_DEV * bsz, depth, 6 * d))
    d_rows = jnp.concatenate([dm_rows[0].transpose(1, 0, 2), d_ctx_row[:, None, :],
                              jnp.zeros((depth, ADA_ROWS - N_DEV * bsz - 1, 6 * d), F32)], axis=1)
    d_cc, grad_w_ada, _ = ada_vjp(lax.dynamic_slice_in_dim(d_rows, chip * n_ada, n_ada, axis=2))
    d_cctx_all = exchange("gather_dcctx", [d_cc[N_DEV * bsz:N_DEV * bsz + 1]], MESH_AXES, True)[0]
    grad_c_ctx = sum_parts("sum_dcctx", d_cctx_all[0::2]).reshape(d)

    shard_major = [_shard_major(d_prm[n], BIG[n]) for n in big_names]
    pair = exchange("rs_pair", shard_major, ("c",), False, pieces=4)
    chip_sum = [sum_parts(f"rs_sum1_{n}", p) for n, p in zip(big_names, pair)]
    quad = exchange("rs_quad", chip_sum, ("x", "y"), False)
    total = [sum_parts(f"rs_sum2_{n}", q) for n, q in zip(big_names, quad)]
    both = exchange("rs_share", total, ("c",), True, pieces=8)

    grads = dict(summed)
    grads["c_lower_bounds"] = lax.dynamic_slice_in_dim(summed["c_lower_bounds"], chip * c_lower_bounds.shape[2],
                                                       c_lower_bounds.shape[2], axis=2)
    grads.update(c_ctx=grad_c_ctx, w_ada=grad_w_ada, b_ada=grad_b_ada)
    grads.update({n: g for n, g in zip(big_names, both)})

    deltas, new_m, new_v = {}, {}, {}
    for n in WEIGHTS:
        as2d = (lambda a: a[None, :]) if local[n].ndim == 1 else (lambda a: a)
        dl, nm_, nv_ = adamw("adamw_" + n, as2d(local[n]), as2d(grads[n]), as2d(mom[n]), as2d(var[n]))
        deltas[n], new_m[n], new_v[n] = (a.reshape(local[n].shape) for a in (dl, nm_, nv_))
    return (loss, grad_x, *[grads[n] for n in WEIGHTS], *[deltas[n] for n in WEIGHTS],
            *[new_m[n] for n in WEIGHTS], *[new_v[n] for n in WEIGHTS])
```

```python
import functools
from typing import Any, Callable, NamedTuple

import numpy as np
import jax
import jax.numpy as jnp
from jax import lax
from jax.experimental import pallas as pl
from jax.experimental.pallas import tpu as pltpu

F32 = jnp.float32
BF16 = jnp.bfloat16
HIGHEST = lax.Precision.HIGHEST

GRID_W = 64
HEAD_DIM = 64
A_HEADS, A_KV_HEADS = 8, 2
A_GROUP = A_HEADS // A_KV_HEADS
B_HEADS, B_Q_RANK, B_KV_RANK, B_NOPE, B_ROPE, B_V = 4, 192, 128, 64, 32, 64
C_HEADS, C_DK, C_DV = 4, 64, 64
SCAN_CHUNK = 64
SCAN_SUB = 16
ROPE_THETA = 10000.0
EPS = 1e-6
F_TINY = 1e-30
ADAM_LR, ADAM_B1, ADAM_B2, ADAM_EPS, ADAM_WD, ADAM_STEP = 0.001, 0.9, 0.999, 1e-08, 0.01, 10

IN_SIZES = (512, 128, 128, 192, 128, 32, 256, 256, 256, 256, 256)
IN_PAD = (512, 128, 128, 256, 128, 128, 256, 256, 256, 256, 256)
IN_OFF = tuple(int(v) for v in np.cumsum((0,) + IN_PAD)[:-1])
D_IN_PAD = int(sum(IN_PAD))

LANE = 128
SUBLANE = 8
TOKEN_TILE = 256
ATT_A_TQ = 64
ATT_B_TQ = 256
MM_ROWS = 256
MM_TK_TOKENS = 512
MM_TN_OUT = 2560 * 1024
VMEM_LIMIT = 56 * 1024 * 1024
MESH_AXES = ("x", "y", "c")
N_DEV = 8
N_CHIP = 4
ADA_ROWS = 24


class Arg(NamedTuple):
    arr: Any
    block: tuple
    imap: Callable
    kind: str
    first: Callable = None


class Out(NamedTuple):
    shape: tuple
    block: tuple
    imap: Callable


def _cparams():
    return pltpu.CompilerParams(vmem_limit_bytes=VMEM_LIMIT)


def tile_op(name, fn, grid, args, outs):
    n_in, n_out = len(args), len(outs)
    in_specs = [pl.BlockSpec(a.block, a.imap) for a in args]
    out_specs = [pl.BlockSpec(o.block, o.imap) for o in outs]
    out_shape = [jax.ShapeDtypeStruct(o.shape, F32) for o in outs]
    diff = [i for i, a in enumerate(args) if a.kind != "const"]

    def fwd_call(*arrays):
        def body(*refs):
            ids = tuple(pl.program_id(i) for i in range(len(grid)))
            res = fn(ids, *[r[...] for r in refs[:n_in]])
            for r, o in zip(refs[n_in:], res):
                r[...] = o

        return pl.pallas_call(body, grid=grid, in_specs=in_specs, out_specs=out_specs, out_shape=out_shape,
                              name=name + "_fwd", compiler_params=_cparams())(*arrays)

    def bwd_call(arrays, cts):
        def body(*refs):
            ids = tuple(pl.program_id(i) for i in range(len(grid)))
            vals = [r[...] for r in refs[:n_in]]
            ct = tuple(r[...] for r in refs[n_in:n_in + n_out])
            drefs = refs[n_in + n_out:]

            def g(*dv):
                full = list(vals)
                for i, v in zip(diff, dv):
                    full[i] = v
                return tuple(fn(ids, *full))

            _, vjp = jax.vjp(g, *[vals[i] for i in diff])
            ds = vjp(ct)
            for i, d, r in zip(diff, ds, drefs):
                if args[i].kind == "tile":
                    r[...] = d
                else:
                    is_first = args[i].first(ids)

                    @pl.when(is_first)
                    def _(r=r, d=d):
                        r[...] = d

                    @pl.when(jnp.logical_not(is_first))
                    def _(r=r, d=d):
                        r[...] += d

        d_specs = [in_specs[i] for i in diff]
        d_shape = [jax.ShapeDtypeStruct(arrays[i].shape, F32) for i in diff]
        return pl.pallas_call(body, grid=grid, in_specs=in_specs + out_specs, out_specs=d_specs, out_shape=d_shape,
                              name=name + "_bwd", compiler_params=_cparams())(*arrays, *cts)

    @jax.custom_vjp
    def op(*arrays):
        return tuple(fwd_call(*arrays))

    def op_fwd(*arrays):
        return tuple(fwd_call(*arrays)), arrays

    def op_bwd(arrays, cts):
        ds = bwd_call(arrays, cts)
        res, k = [], 0
        for i, a in enumerate(args):
            if a.kind == "const":
                res.append(jnp.zeros_like(arrays[i]))
            else:
                res.append(ds[k])
                k += 1
        return tuple(res)

    op.defvjp(op_fwd, op_bwd)
    return op(*[a.arr for a in args])


def _pick(n, cap):
    if n <= cap:
        return n
    best = None
    for t in range(LANE, cap + 1, LANE):
        if n % t == 0:
            best = t
    assert best is not None, (n, cap)
    return best


_NN = (((1,), (0,)), ((), ()))
_NT = (((1,), (1,)), ((), ()))
_TN = (((0,), (0,)), ((), ()))


def _resident(shape):
    return pl.BlockSpec(shape, lambda *ids: (0,) * len(shape), pipeline_mode=pl.Buffered(1))


def _mm_rows(name, a, w, transposed):
    m, k = a.shape
    n = w.shape[0] if transposed else w.shape[1]
    tm = MM_ROWS
    dims = _NT if transposed else _NN

    def body(a_ref, w_ref, o_ref):
        o_ref[...] = lax.dot_general(a_ref[...].astype(BF16), w_ref[...], dims, preferred_element_type=F32)

    return pl.pallas_call(body, grid=(m // tm,), in_specs=[pl.BlockSpec((tm, k), lambda i: (i, 0)), _resident(w.shape)],
                          out_specs=pl.BlockSpec((tm, n), lambda i: (i, 0)), out_shape=jax.ShapeDtypeStruct((m, n), F32),
                          name=name, compiler_params=_cparams())(a, w)


def _mm_tn(name, a, g):
    t, k = a.shape
    n = g.shape[1]
    tko, tno = k, n
    while tko * tno > MM_TN_OUT:
        if tko >= tno:
            tko //= 2
        else:
            tno //= 2
    assert k % tko == 0 and n % tno == 0 and tko % LANE == 0 and tno % LANE == 0
    tt = _pick(t, MM_TK_TOKENS)

    def body(a_ref, g_ref, o_ref):
        p = lax.dot_general(a_ref[...].astype(BF16), g_ref[...].astype(BF16), _TN, preferred_element_type=F32)
        kk = pl.program_id(2)

        @pl.when(kk == 0)
        def _():
            o_ref[...] = p

        @pl.when(kk != 0)
        def _():
            o_ref[...] += p

    return pl.pallas_call(body, grid=(k // tko, n // tno, t // tt),
                          in_specs=[pl.BlockSpec((tt, tko), lambda i, j, kk: (kk, i)), pl.BlockSpec((tt, tno), lambda i, j, kk: (kk, j))],
                          out_specs=pl.BlockSpec((tko, tno), lambda i, j, kk: (i, j)),
                          out_shape=jax.ShapeDtypeStruct((k, n), F32), name=name, compiler_params=_cparams())(a, g)


def linear(name, a, w, w_lo):
    @jax.custom_vjp
    def op(a, w, w_lo):
        return _mm_rows(name + "_fwd", a, w_lo, False)

    def op_fwd(a, w, w_lo):
        return _mm_rows(name + "_fwd", a, w_lo, False), (a, w_lo)

    def op_bwd(res, g):
        a, w_lo = res
        return _mm_rows(name + "_da", g, w_lo, True), _mm_tn(name + "_dw", a, g), jnp.zeros_like(w_lo)

    op.defvjp(op_fwd, op_bwd)
    return op(a, w, w_lo)


def mlp(name, h, w1, w2, w1_lo, w2_lo):
    m, d = h.shape
    f = w1_lo.shape[1]
    tm = MM_ROWS
    row = pl.BlockSpec((tm, d), lambda i: (i, 0))
    wide = pl.BlockSpec((tm, f), lambda i: (i, 0))

    def fwd_call(h, w1_lo, w2_lo):
        def body(h_ref, w1_ref, w2_ref, o_ref):
            u = lax.dot_general(h_ref[...].astype(BF16), w1_ref[...], _NN, preferred_element_type=F32)
            act = jnp.square(jnp.maximum(u, 0.0))
            o_ref[...] = lax.dot_general(act.astype(BF16), w2_ref[...], _NN, preferred_element_type=F32)

        return pl.pallas_call(body, grid=(m // tm,), in_specs=[row, _resident(w1_lo.shape), _resident(w2_lo.shape)],
                              out_specs=row, out_shape=jax.ShapeDtypeStruct((m, d), F32), name=name + "_fwd",
                              compiler_params=_cparams())(h, w1_lo, w2_lo)

    def bwd_call(h, w1_lo, w2_lo, dy):
        def body(h_ref, dy_ref, w1_ref, w2_ref, dh_ref, act_ref, du_ref):
            u = lax.dot_general(h_ref[...].astype(BF16), w1_ref[...], _NN, preferred_element_type=F32)
            r = jnp.maximum(u, 0.0)
            act_ref[...] = (r * r).astype(BF16)
            dact = lax.dot_general(dy_ref[...].astype(BF16), w2_ref[...], _NT, preferred_element_type=F32)
            du = (dact * (2.0 * r)).astype(BF16)
            du_ref[...] = du
            dh_ref[...] = lax.dot_general(du, w1_ref[...], _NT, preferred_element_type=F32)

        return pl.pallas_call(body, grid=(m // tm,), in_specs=[row, row, _resident(w1_lo.shape), _resident(w2_lo.shape)],
                              out_specs=[row, wide, wide],
                              out_shape=[jax.ShapeDtypeStruct((m, d), F32), jax.ShapeDtypeStruct((m, f), BF16),
                                         jax.ShapeDtypeStruct((m, f), BF16)],
                              name=name + "_bwd", compiler_params=_cparams())(h, dy, w1_lo, w2_lo)

    @jax.custom_vjp
    def op(h, w1, w2, w1_lo, w2_lo):
        return fwd_call(h, w1_lo, w2_lo)

    def op_fwd(h, w1, w2, w1_lo, w2_lo):
        return fwd_call(h, w1_lo, w2_lo), (h, w1_lo, w2_lo)

    def op_bwd(res, dy):
        h, w1_lo, w2_lo = res
        dh, act, du = bwd_call(h, w1_lo, w2_lo, dy)
        return (dh, _mm_tn(name + "_dw1", h, du), _mm_tn(name + "_dw2", act, dy), jnp.zeros_like(w1_lo), jnp.zeros_like(w2_lo))

    op.defvjp(op_fwd, op_bwd)
    return op(h, w1, w2, w1_lo, w2_lo)


def _rms(x, g):
    return x * lax.rsqrt(jnp.mean(x * x, axis=-1, keepdims=True) + EPS) * g


def _sigmoid(z):
    return 1.0 / (1.0 + jnp.exp(-z))


def _silu(z):
    return z * _sigmoid(z)


def _rope(y, cos, sin_signed, swap):
    return y * cos + jnp.dot(y, swap, precision=HIGHEST, preferred_element_type=F32) * sin_signed


def _dot_bf16(a, b, dims=((1,), (0,))):
    return lax.dot_general(a.astype(BF16), b.astype(BF16), (dims, ((), ())), preferred_element_type=F32)


def _gla_step(state, q, k, v, g, reverse):
    c, d = q.shape
    sub = SCAN_SUB
    nb = c // sub
    row = lax.broadcasted_iota(jnp.int32, (c, c), 0)
    col = lax.broadcasted_iota(jnp.int32, (c, c), 1)
    tri = (row <= col) if reverse else (row >= col)
    b = jnp.dot(tri.astype(F32), g, precision=HIGHEST, preferred_element_type=F32)
    o = jnp.dot(q * jnp.exp(b), state, preferred_element_type=F32)
    rs = lax.broadcasted_iota(jnp.int32, (sub, sub), 0)
    cs = lax.broadcasted_iota(jnp.int32, (sub, sub), 1)
    tri_s = ((rs <= cs) if reverse else (rs >= cs)).astype(F32)
    rowc = lax.broadcasted_iota(jnp.int32, (c, 1), 0)
    nonpos = lambda x: jnp.where(x > 0.0, 0.0, x)
    diag = []
    for j in range(nb):
        sl = slice(j * sub, (j + 1) * sub)
        bj, kj, vj, qj = b[sl], k[sl], v[sl], q[sl]
        dec = jnp.exp(nonpos(bj[:, None, :] - bj[None, :, :]))
        sc = jnp.sum(qj[:, None, :] * kj[None, :, :] * dec, axis=-1) * tri_s
        diag.append(jnp.dot(sc, vj, preferred_element_type=F32))
        if (j > 0) if reverse else (j < nb - 1):
            ref = bj[0:1] if reverse else bj[sub - 1:sub]
            qa = q * jnp.exp(nonpos(b - ref))
            ks = kj * jnp.exp(ref - bj)
            scj = lax.dot_general(qa, ks, (((1,), (1,)), ((), ())), precision=HIGHEST, preferred_element_type=F32)
            later = (rowc < j * sub) if reverse else (rowc >= (j + 1) * sub)
            o = o + jnp.dot(jnp.where(later, scj, 0.0), vj, preferred_element_type=F32)
    o = o + jnp.concatenate(diag, axis=0)
    b_end = b[0:1, :] if reverse else b[c - 1:c, :]
    kd = k * jnp.exp(b_end - b)
    new_state = state * jnp.exp(b_end).reshape(d, 1) + lax.dot_general(kd, v, (((0,), (0,)), ((), ())), preferred_element_type=F32)
    return new_state, o


def gla_scan(name, q, k, v, g, reverse, n_lat_chunks):
    bsz, nh, t, d = q.shape
    c = SCAN_CHUNK
    n = t // c

    def chunk_of(j):
        return (n - 1 - j) if reverse else lax.rem(j + n_lat_chunks, n)

    blk = (None, nh, c, d)
    st_blk = (None, nh, None, d, d)

    def fwd_call(q, k, v, g):
        def body(q_ref, k_ref, v_ref, g_ref, o_ref, states_ref, st):
            @pl.when(pl.program_id(1) == 0)
            def _():
                st[...] = jnp.zeros_like(st)
            for h in range(nh):
                s = st[h]
                states_ref[h] = s
                ns, o = _gla_step(s, q_ref[h], k_ref[h], v_ref[h], g_ref[h], reverse)
                st[h] = ns
                o_ref[h] = o

        spec = pl.BlockSpec(blk, lambda b, j: (b, 0, chunk_of(j), 0))
        return pl.pallas_call(
            body, grid=(bsz, n), in_specs=[spec] * 4,
            out_specs=[spec, pl.BlockSpec(st_blk, lambda b, j: (b, 0, j, 0, 0))],
            out_shape=[jax.ShapeDtypeStruct(q.shape, F32), jax.ShapeDtypeStruct((bsz, nh, n, d, d), F32)],
            scratch_shapes=[pltpu.VMEM((nh, d, d), F32)], name=name + "_fwd", compiler_params=_cparams())(q, k, v, g)

    def bwd_call(q, k, v, g, states, do):
        def body(q_ref, k_ref, v_ref, g_ref, s_ref, do_ref, dq_ref, dk_ref, dv_ref, dg_ref, dst):
            @pl.when(pl.program_id(1) == 0)
            def _():
                dst[...] = jnp.zeros_like(dst)
            step = functools.partial(_gla_step, reverse=reverse)
            for h in range(nh):
                _, vjp = jax.vjp(step, s_ref[h], q_ref[h], k_ref[h], v_ref[h], g_ref[h])
                ds, dq, dk, dv, dg = vjp((dst[h], do_ref[h]))
                dst[h] = ds
                dq_ref[h] = dq
                dk_ref[h] = dk
                dv_ref[h] = dv
                dg_ref[h] = dg

        spec = pl.BlockSpec(blk, lambda b, jj: (b, 0, chunk_of(n - 1 - jj), 0))
        s_spec = pl.BlockSpec(st_blk, lambda b, jj: (b, 0, n - 1 - jj, 0, 0))
        return pl.pallas_call(
            body, grid=(bsz, n), in_specs=[spec] * 4 + [s_spec, spec], out_specs=[spec] * 4,
            out_shape=[jax.ShapeDtypeStruct(q.shape, F32)] * 4,
            scratch_shapes=[pltpu.VMEM((nh, d, d), F32)], name=name + "_bwd", compiler_params=_cparams())(q, k, v, g, states, do)

    @jax.custom_vjp
    def op(q, k, v, g):
        return fwd_call(q, k, v, g)[0]

    def op_fwd(q, k, v, g):
        o, states = fwd_call(q, k, v, g)
        return o, (q, k, v, g, states)

    def op_bwd(res, do):
        return tuple(bwd_call(*res, do))

    op.defvjp(op_fwd, op_bwd)
    return op(q, k, v, g)


def _zero_ids(ids):
    z = ids[0] == 0
    for i in ids[1:]:
        z = jnp.logical_and(z, i == 0)
    return z


def _token_grid(x, n_lat_tiles):
    bsz, t, d = x.shape
    nt = t // TOKEN_TILE
    row = lambda w: ((None, TOKEN_TILE, w), lambda b, i: (b, i, 0))
    mod_block = (None, None, 6, d)
    mod_imap = lambda b, i: (b, (i >= n_lat_tiles).astype(jnp.int32), 0, 0)
    mod_first = lambda ids: jnp.logical_or(ids[1] == 0, ids[1] == n_lat_tiles)
    return bsz, t, d, nt, row, (mod_block, mod_imap, mod_first)


def premod(name, x, gain, mods, r0, n_lat_tiles):
    bsz, t, d, nt, row, (mb, mi, mf) = _token_grid(x, n_lat_tiles)

    def fn(ids, xb, gb, mod):
        return (_rms(xb, gb) * (1.0 + mod[r0 + 1:r0 + 2]) + mod[r0:r0 + 1],)

    args = [Arg(x, *row(d), "tile"), Arg(gain, (1, d), lambda b, i: (0, 0), "acc", _zero_ids), Arg(mods, mb, mi, "acc", mf)]
    return tile_op(name, fn, (bsz, nt), args, [Out(x.shape, *row(d))])[0]


def resid(name, x, y, gain, mods, r, n_lat_tiles):
    bsz, t, d, nt, row, (mb, mi, mf) = _token_grid(x, n_lat_tiles)

    def fn(ids, xb, yb, gb, mod):
        return (xb + mod[r:r + 1] * _rms(yb, gb),)

    args = [Arg(x, *row(d), "tile"), Arg(y, *row(d), "tile"), Arg(gain, (1, d), lambda b, i: (0, 0), "acc", _zero_ids),
            Arg(mods, mb, mi, "acc", mf)]
    return tile_op(name, fn, (bsz, nt), args, [Out(x.shape, *row(d))])[0]


def _head_spec(w):
    return (None, None, TOKEN_TILE, w), lambda b, h, i: (b, h, i, 0)


def a_prep(name, x, gain, cos, sin, swap):
    bsz, nh, t, d = x.shape
    tab = ((TOKEN_TILE, d), lambda b, h, i: (i, 0))

    def fn(ids, xb, gb, cb, sb, pb):
        return (_rope(_rms(xb, gb), cb, sb, pb),)

    args = [Arg(x, *_head_spec(d), "tile"), Arg(gain, (1, d), lambda b, h, i: (0, 0), "acc", _zero_ids),
            Arg(cos, *tab, "const"), Arg(sin, *tab, "const"), Arg(swap, (d, d), lambda b, h, i: (0, 0), "const")]
    return tile_op(name, fn, (bsz, nh, t // TOKEN_TILE), args, [Out(x.shape, *_head_spec(d))])[0]


def b_prep(name, bqd, bkvd, bkr, bqn, wq_nope, wq_pe, bkvn, wkv_nope, wkv_v, cos_q, sin_q, swap_q, cos_k, sin_k, swap_k):
    bsz, t, _ = bqd.shape
    row = lambda w: ((None, TOKEN_TILE, w), lambda b, i: (b, i, 0))
    whole = lambda a: (a.shape, lambda b, i: (0,) * a.ndim)
    tab = lambda w: ((TOKEN_TILE, w), lambda b, i: (i, 0))

    def fn(ids, qd, kvd, kr, qn, wqn, wqp, kvn, wkn, wkv, cq, sq, pq, ck, sk, pk):
        hq = _rms(qd, qn)
        hkv = _rms(kvd, kvn)
        return (_dot_bf16(hq, wqn), _rope(_dot_bf16(hq, wqp), cq, sq, pq), _dot_bf16(hkv, wkn), _dot_bf16(hkv, wkv),
                _rope(kr, ck, sk, pk))

    params = [bqn, wq_nope, wq_pe, bkvn, wkv_nope, wkv_v]
    args = [Arg(bqd, *row(B_Q_RANK), "tile"), Arg(bkvd, *row(B_KV_RANK), "tile"), Arg(bkr, *row(B_ROPE), "tile")]
    args += [Arg(p, *whole(p), "acc", _zero_ids) for p in params]
    args += [Arg(cos_q, *tab(cos_q.shape[1]), "const"), Arg(sin_q, *tab(cos_q.shape[1]), "const"), Arg(swap_q, *whole(swap_q), "const"),
             Arg(cos_k, *tab(B_ROPE), "const"), Arg(sin_k, *tab(B_ROPE), "const"), Arg(swap_k, *whole(swap_k), "const")]
    widths = (B_HEADS * B_NOPE, B_HEADS * B_ROPE, B_HEADS * B_NOPE, B_HEADS * B_V, B_ROPE)
    outs = [Out((bsz, t, w), *row(w)) for w in widths]
    return tile_op(name, fn, (bsz, t // TOKEN_TILE), args, outs)


def c_prep(name, cq, cff, cfb, clb, layer):
    bsz, nh, t, d = cq.shape
    depth = clb.shape[1]
    spec = ((None, None, TOKEN_TILE, d), lambda h, b, i: (b, h, i, 0))

    def fn(ids, q, zf, zb, lbs):
        m = lbs[0]
        for j in range(1, depth):
            m = jnp.maximum(m, lbs[j])
        e = [jnp.exp(lbs[j] - m) for j in range(depth)]
        tot = e[0]
        for j in range(1, depth):
            tot = tot + e[j]
        p = [ej / tot for ej in e]
        cum = p[0]
        for j in range(1, layer + 1):
            cum = cum + p[j]
        lower = cum - p[0]

        def gate(z, lb):
            f = lb + (1.0 - lb) * _sigmoid(z)
            return jnp.log(jnp.maximum(f, F_TINY)), (1.0 - lb) * _sigmoid(-z)

        gf, kf = gate(zf, lower[0:1])
        gb, kb = gate(zb, lower[1:2])
        return _silu(q), kf, gf, kb, gb

    first = lambda ids: jnp.logical_and(ids[1] == 0, ids[2] == 0)
    args = [Arg(cq, *spec, "tile"), Arg(cff, *spec, "tile"), Arg(cfb, *spec, "tile"),
            Arg(clb, (None, depth, 2, d), lambda h, b, i: (h, 0, 0, 0), "acc", first)]
    return tile_op(name, fn, (nh, bsz, t // TOKEN_TILE), args, [Out(cq.shape, *spec)] * 5)


def c_readout(name, o_f, o_b, gate, gain):
    bsz, nh, t, d = o_f.shape

    def fn(ids, of, ob, gt, gn):
        return (_rms(of + ob, gn) * _silu(gt),)

    args = [Arg(o_f, *_head_spec(d), "tile"), Arg(o_b, *_head_spec(d), "tile"), Arg(gate, *_head_spec(d), "tile"),
            Arg(gain, (1, d), lambda b, h, i: (0, 0), "acc", _zero_ids)]
    return tile_op(name, fn, (bsz, nh, t // TOKEN_TILE), args, [Out(o_f.shape, *_head_spec(d))])[0]


def _key_bias(t, tq, i, n_lat):
    key = lax.broadcasted_iota(jnp.int32, (1, t), 1)
    return jnp.where(jnp.logical_and(i * tq >= n_lat, key < n_lat), -1e30, 0.0)


def attention(name, qs, kts, vt, scale, fold_scale, tq, n_lat):
    bsz, hk, grp, t, _ = qs[0].shape
    dv = vt.shape[2]
    rows = grp * tq
    n_parts = len(qs)
    q_specs = [pl.BlockSpec((None, None, grp, tq, q.shape[-1]), lambda b, h, i: (b, h, 0, i, 0)) for q in qs]
    shared = [kt.shape[1] == 1 for kt in kts]
    kt_specs = [pl.BlockSpec((None, None, kt.shape[2], t), (lambda b, h, i: (b, 0, 0, 0)) if sh else (lambda b, h, i: (b, h, 0, 0)))
                for kt, sh in zip(kts, shared)]
    vt_spec = pl.BlockSpec((None, None, dv, t), lambda b, h, i: (b, h, 0, 0))
    o_spec = pl.BlockSpec((None, None, grp, tq, dv), lambda b, h, i: (b, h, 0, i, 0))
    lse_spec = pl.BlockSpec((None, None, grp, tq, LANE), lambda b, h, i: (b, h, 0, i, 0))
    o_shape = jax.ShapeDtypeStruct((bsz, hk, grp, t, dv), F32)
    lse_shape = jax.ShapeDtypeStruct((bsz, hk, grp, t, LANE), F32)
    grid = (bsz, hk, t // tq)

    def scores(q_refs, kt_refs):
        qq, s = [], None
        for q_ref, kt_ref in zip(q_refs, kt_refs):
            q2 = q_ref[...].reshape(rows, q_ref.shape[-1])
            q2 = (q2 * scale if fold_scale else q2).astype(BF16)
            part = lax.dot_general(q2, kt_ref[...].astype(BF16), _NN, preferred_element_type=F32)
            s = part if s is None else s + part
            qq.append(q2)
        if not fold_scale:
            s = s * scale
        return qq, s + _key_bias(t, tq, pl.program_id(2), n_lat)

    def fwd_call(*arrays):
        def body(*refs):
            q_refs, kt_refs, vt_ref = refs[:n_parts], refs[n_parts:2 * n_parts], refs[2 * n_parts]
            o_ref, lse_ref = refs[2 * n_parts + 1:]
            _, s = scores(q_refs, kt_refs)
            m = jnp.max(s, axis=-1, keepdims=True)
            e = jnp.exp(s - m)
            l = jnp.sum(e, axis=-1, keepdims=True)
            o = lax.dot_general(e.astype(BF16), vt_ref[...].astype(BF16), _NT, preferred_element_type=F32) * (1.0 / l)
            o_ref[...] = o.reshape(grp, tq, dv)
            lse_ref[...] = jnp.broadcast_to(m + jnp.log(l), (rows, LANE)).reshape(grp, tq, LANE)

        return pl.pallas_call(body, grid=grid, in_specs=q_specs + kt_specs + [vt_spec], out_specs=[o_spec, lse_spec],
                              out_shape=[o_shape, lse_shape], name=name + "_fwd", compiler_params=_cparams())(*arrays)

    def bwd_call(arrays, o, lse, do):
        def body(*refs):
            q_refs, kt_refs, vt_ref = refs[:n_parts], refs[n_parts:2 * n_parts], refs[2 * n_parts]
            o_ref, lse_ref, do_ref = refs[2 * n_parts + 1:2 * n_parts + 4]
            d_refs = refs[2 * n_parts + 4:]
            dq_refs, dkt_refs, dvt_ref = d_refs[:n_parts], d_refs[n_parts:2 * n_parts], d_refs[2 * n_parts]
            h, i = pl.program_id(1), pl.program_id(2)
            qq, s = scores(q_refs, kt_refs)
            p = jnp.exp(s - lse_ref[...].reshape(rows, LANE)[:, 0:1])
            do2 = do_ref[...].reshape(rows, dv)
            delta = jnp.sum(do2 * o_ref[...].reshape(rows, dv), axis=-1, keepdims=True)
            do_lo = do2.astype(BF16)
            dp = lax.dot_general(do_lo, vt_ref[...].astype(BF16), _NN, preferred_element_type=F32)
            ds = p * (dp - delta)
            if not fold_scale:
                ds = ds * scale
            ds_lo = ds.astype(BF16)

            def accumulate(ref, val, first):
                @pl.when(first)
                def _():
                    ref[...] = val

                @pl.when(jnp.logical_not(first))
                def _():
                    ref[...] += val

            accumulate(dvt_ref, lax.dot_general(do_lo, p.astype(BF16), _TN, preferred_element_type=F32), i == 0)
            for q2, kt_ref, dq_ref, dkt_ref, sh in zip(qq, kt_refs, dq_refs, dkt_refs, shared):
                dq = lax.dot_general(ds_lo, kt_ref[...].astype(BF16), _NT, preferred_element_type=F32)
                dq_ref[...] = (dq * scale if fold_scale else dq).reshape(dq_ref.shape)
                first = jnp.logical_and(h == 0, i == 0) if sh else i == 0
                accumulate(dkt_ref, lax.dot_general(q2, ds_lo, _TN, preferred_element_type=F32), first)

        in_specs = q_specs + kt_specs + [vt_spec, o_spec, lse_spec, o_spec]
        d_shape = [jax.ShapeDtypeStruct(a.shape, F32) for a in arrays]
        return pl.pallas_call(body, grid=grid, in_specs=in_specs, out_specs=q_specs + kt_specs + [vt_spec], out_shape=d_shape,
                              name=name + "_bwd", compiler_params=_cparams())(*arrays, o, lse, do)

    @jax.custom_vjp
    def op(*arrays):
        return fwd_call(*arrays)[0]

    def op_fwd(*arrays):
        o, lse = fwd_call(*arrays)
        return o, (arrays, o, lse)

    def op_bwd(res, do):
        arrays, o, lse = res
        return tuple(bwd_call(arrays, o, lse, do))

    op.defvjp(op_fwd, op_bwd)
    return op(*qs, *kts, vt)


def ada_op(cc, w, b):
    depth, d, n = w.shape

    def fn(ids, ccb, wb, bb):
        return (_dot_bf16(_silu(ccb), wb) + bb,)

    args = [Arg(cc, cc.shape, lambda l: (0, 0), "acc", lambda ids: ids[0] == 0),
            Arg(w, (None, d, n), lambda l: (l, 0, 0), "tile"), Arg(b, (None, 1, n), lambda l: (l, 0, 0), "tile")]
    return tile_op("ada", fn, (depth,), args, [Out((depth, cc.shape[0], n), (None, cc.shape[0], n), lambda l: (l, 0, 0))])[0]


def loss_op(xu, target, n_lat_tiles):
    bsz, t, d = xu.shape

    def body(x_ref, t_ref, dx_ref, l_ref):
        b, i = pl.program_id(0), pl.program_id(1)

        @pl.when(jnp.logical_and(b == 0, i == 0))
        def _():
            l_ref[...] = jnp.zeros_like(l_ref)

        @pl.when(i < n_lat_tiles)
        def _():
            err = x_ref[...] - t_ref[...]
            dx_ref[...] = err * (1.0 / d)
            l_ref[...] += 0.5 * jnp.sum(jnp.mean(err * err, axis=-1))

        @pl.when(i >= n_lat_tiles)
        def _():
            dx_ref[...] = jnp.zeros_like(dx_ref)

    row = pl.BlockSpec((None, TOKEN_TILE, d), lambda b, i: (b, i, 0))
    t_spec = pl.BlockSpec((None, TOKEN_TILE, d), lambda b, i: (b, jnp.minimum(i, n_lat_tiles - 1), 0))
    return pl.pallas_call(body, grid=(bsz, t // TOKEN_TILE), in_specs=[row, t_spec],
                          out_specs=[row, pl.BlockSpec((SUBLANE, LANE), lambda b, i: (0, 0))],
                          out_shape=[jax.ShapeDtypeStruct(xu.shape, F32), jax.ShapeDtypeStruct((SUBLANE, LANE), F32)],
                          name="loss", compiler_params=_cparams())(xu, target)


def _row_tile(rows, row_bytes, budget=4 << 20):
    if rows * row_bytes <= budget:
        return rows
    best = None
    for t in range(SUBLANE, rows, SUBLANE):
        if rows % t == 0 and t * row_bytes <= budget:
            best = t
    return best if best is not None else rows


def _as3d(x):
    p = x.shape[0]
    c = x.shape[-1]
    return x.reshape(p, -1, c)


def sum_parts(name, x):
    x3 = _as3d(x)
    p, r, c = x3.shape
    tr = _row_tile(r, p * c * 4)

    def body(x_ref, o_ref):
        s = x_ref[0]
        for j in range(1, p):
            s = s + x_ref[j]
        o_ref[...] = s

    out = pl.pallas_call(body, grid=(r // tr,), in_specs=[pl.BlockSpec((p, tr, c), lambda i: (0, i, 0))],
                         out_specs=pl.BlockSpec((tr, c), lambda i: (i, 0)), out_shape=jax.ShapeDtypeStruct((r, c), F32),
                         name=name, compiler_params=_cparams())(x3)
    return out.reshape(x.shape[1:])


def adamw(name, w, g, m, v):
    shape = w.shape
    c = shape[-1]
    to2d = lambda a: a.reshape(-1, c)
    r = to2d(w).shape[0]
    tr = _row_tile(r, 7 * c * 4, budget=6 << 20)

    def body(w_ref, g_ref, m_ref, v_ref, d_ref, nm_ref, nv_ref):
        gg = g_ref[...]
        nm = ADAM_B1 * m_ref[...] + (1.0 - ADAM_B1) * gg
        nv = ADAM_B2 * v_ref[...] + (1.0 - ADAM_B2) * jnp.square(gg)
        m_hat = nm / (1.0 - ADAM_B1 ** ADAM_STEP)
        v_hat = nv / (1.0 - ADAM_B2 ** ADAM_STEP)
        d_ref[...] = -ADAM_LR * (m_hat / (jnp.sqrt(v_hat) + ADAM_EPS) + ADAM_WD * w_ref[...])
        nm_ref[...] = nm
        nv_ref[...] = nv

    spec = pl.BlockSpec((tr, c), lambda i: (i, 0))
    outs = pl.pallas_call(body, grid=(r // tr,), in_specs=[spec] * 4, out_specs=[spec] * 3,
                          out_shape=[jax.ShapeDtypeStruct((r, c), F32)] * 3, name=name, compiler_params=_cparams())(
        to2d(w), to2d(g), to2d(m), to2d(v))
    return tuple(o.reshape(shape) for o in outs)


def _n_pieces(shape, dtype, limit):
    if limit <= 1 or len(shape) < 2:
        return 1
    rows_per_tile = 1 if len(shape) >= 3 else SUBLANE * (4 // jnp.dtype(dtype).itemsize)
    for k in range(min(limit, shape[0]), 0, -1):
        if shape[0] % k == 0 and (shape[0] // k) % rows_per_tile == 0:
            return k
    return 1


def exchange(name, srcs, group, same, pieces=1):
    p = 2 ** len(group)
    n = len(srcs)
    out_shape = [jax.ShapeDtypeStruct(((p,) + s.shape) if same else s.shape, s.dtype) for s in srcs]
    n_pc = [_n_pieces(s.shape if same else s.shape[1:], s.dtype, pieces) for s in srcs]
    max_pc = max(n_pc)

    def body(*refs):
        src_refs, out_refs = refs[:n], refs[n:2 * n]
        send_sems, recv_sems, local_sems = refs[2 * n:]
        pos = {a: lax.axis_index(a) for a in MESH_AXES}

        def index_of(coords):
            idx = 0
            for a in group:
                idx = idx * 2 + coords[a]
            return idx

        me = index_of(pos)
        peers = []
        for rel in range(1, p):
            coords = dict(pos)
            for bit, a in enumerate(reversed(group)):
                if (rel >> bit) & 1:
                    coords[a] = 1 - coords[a]
            peers.append((coords, index_of(coords)))

        def src_for(a, idx):
            return src_refs[a] if same else src_refs[a].at[idx]

        def piece(ref, a, k):
            if n_pc[a] == 1:
                return ref
            rows = ref.shape[0] // n_pc[a]
            return ref.at[pl.ds(k * rows, rows)]

        local, sends, recvs = [], [], []
        for a in range(n):
            for k in range(n_pc[a]):
                cp = pltpu.make_async_copy(piece(src_for(a, me), a, k), piece(out_refs[a].at[me], a, k), local_sems.at[a, k])
                cp.start()
                local.append(cp)
                for r, (coords, idx) in enumerate(peers):
                    dev = tuple(coords[ax] for ax in MESH_AXES)
                    src = piece(src_for(a, idx), a, k)
                    send = pltpu.make_async_remote_copy(src_ref=src, dst_ref=piece(out_refs[a].at[me], a, k),
                                                        send_sem=send_sems.at[a, r, k], recv_sem=recv_sems.at[a, r, k],
                                                        device_id=dev, device_id_type=pl.DeviceIdType.MESH)
                    send.start()
                    sends.append(send)
                    recvs.append(pltpu.make_async_remote_copy(src_ref=src, dst_ref=piece(out_refs[a].at[idx], a, k),
                                                              send_sem=send_sems.at[a, r, k], recv_sem=recv_sems.at[a, r, k],
                                                              device_id=dev, device_id_type=pl.DeviceIdType.MESH))
        for cp in sends:
            cp.wait_send()
        for cp in recvs:
            cp.wait_recv()
        for cp in local:
            cp.wait()

    any_spec = pl.BlockSpec(memory_space=pl.ANY)
    outs = pl.pallas_call(body, in_specs=[any_spec] * n, out_specs=[any_spec] * n, out_shape=out_shape,
                          scratch_shapes=[pltpu.SemaphoreType.DMA((n, p - 1, max_pc)), pltpu.SemaphoreType.DMA((n, p - 1, max_pc)),
                                          pltpu.SemaphoreType.DMA((n, max_pc))],
                          name=name)(*srcs)
    return list(outs)


def _rope_tables(n_lat, n_ctx, rot_dim, heads):
    n_freq = rot_dim // 4
    tok = jnp.arange(n_lat, dtype=jnp.int32)
    inv = ROPE_THETA ** (-jnp.arange(n_freq, dtype=F32) / n_freq)
    ang = jnp.concatenate([(tok // GRID_W).astype(F32)[:, None] * inv, (tok % GRID_W).astype(F32)[:, None] * inv], axis=-1)
    cos, sin = jnp.cos(ang), jnp.sin(ang)
    cos = jnp.concatenate([jnp.concatenate([cos, cos], -1), jnp.ones((n_ctx, rot_dim), F32)], 0)
    sin = jnp.concatenate([jnp.concatenate([-sin, sin], -1), jnp.zeros((n_ctx, rot_dim), F32)], 0)
    half = rot_dim // 2
    w = heads * rot_dim
    j = np.arange(w)
    src = (j // rot_dim) * rot_dim + (j % rot_dim + half) % rot_dim
    swap = np.zeros((w, w), np.float32)
    swap[src, j] = 1.0
    return jnp.tile(cos, (1, heads)), jnp.tile(sin, (1, heads)), jnp.asarray(swap)


def _to_heads(a, nh):
    b, t, _ = a.shape
    return a.reshape(b, t, nh, -1).transpose(0, 2, 1, 3)


def _to_heads_t(a, nh):
    b, t, _ = a.shape
    return a.reshape(b, t, nh, -1).transpose(0, 2, 3, 1)


def _from_heads(a):
    b, nh, t, w = a.shape
    return a.transpose(0, 2, 1, 3).reshape(b, t, nh * w)


def _pad_w_in(w_in):
    parts, off = [], 0
    for size, pad in zip(IN_SIZES, IN_PAD):
        parts.append(w_in[:, off:off + size])
        if pad > size:
            parts.append(jnp.zeros((w_in.shape[0], pad - size), w_in.dtype))
        off += size
    return jnp.concatenate(parts, axis=1)


def _forward(xu, mods, prm, prm_lo, n_lat, n_ctx):
    bsz, t, d = xu.shape
    depth = mods.shape[0]
    n_lat_tiles = n_lat // TOKEN_TILE
    rope_a = _rope_tables(n_lat, n_ctx, HEAD_DIM, 1)
    rope_bq = _rope_tables(n_lat, n_ctx, B_ROPE, B_HEADS)
    rope_bk = _rope_tables(n_lat, n_ctx, B_ROPE, 1)
    clb = prm["c_lower_bounds"].reshape(depth, 2, C_HEADS, C_DK).transpose(2, 0, 1, 3)
    for l in range(depth):
        nm = lambda s: f"l{l}_{s}"
        vec = lambda name: prm[name][l][None, :]
        h = premod(nm("premix"), xu, vec("g_pre_mix"), mods[l], 0, n_lat_tiles)
        z = linear(nm("w_in"), h.reshape(bsz * t, d), _pad_w_in(prm["w_in"][l]), _pad_w_in(prm_lo["w_in"][l])).reshape(bsz, t, D_IN_PAD)
        seg = lambda i: z[:, :, IN_OFF[i]:IN_OFF[i] + IN_SIZES[i]]
        aq = a_prep(nm("aq"), _to_heads(seg(0), A_HEADS), vec("a_q_norm"), *rope_a)
        ak = a_prep(nm("ak"), _to_heads(seg(1), A_KV_HEADS), vec("a_k_norm"), *rope_a)
        av = _to_heads(seg(2), A_KV_HEADS)
        ya = attention(nm("att_a"), [aq.reshape(bsz, A_KV_HEADS, A_GROUP, t, HEAD_DIM)], [ak.transpose(0, 1, 3, 2)],
                       av.transpose(0, 1, 3, 2), HEAD_DIM ** -0.5, True, ATT_A_TQ, n_lat)
        ya = _from_heads(ya.reshape(bsz, A_HEADS, t, HEAD_DIM))
        wq = prm["w_q_up"][l].reshape(B_Q_RANK, B_HEADS, B_NOPE + B_ROPE)
        wkv = prm["w_kv_up"][l].reshape(B_KV_RANK, B_HEADS, B_NOPE + B_V)
        qn, qp, kn, bv, kp = b_prep(
            nm("b_prep"), seg(3), seg(4), seg(5), vec("b_q_norm"),
            wq[:, :, :B_NOPE].reshape(B_Q_RANK, -1), wq[:, :, B_NOPE:].reshape(B_Q_RANK, -1), vec("b_kv_norm"),
            wkv[:, :, :B_NOPE].reshape(B_KV_RANK, -1), wkv[:, :, B_NOPE:].reshape(B_KV_RANK, -1), *rope_bq, *rope_bk)
        yb = attention(nm("att_b"), [_to_heads(qn, B_HEADS)[:, :, None], _to_heads(qp, B_HEADS)[:, :, None]],
                       [_to_heads_t(kn, B_HEADS), kp.transpose(0, 2, 1)[:, None]], _to_heads_t(bv, B_HEADS),
                       (B_NOPE + B_ROPE) ** -0.5, False, ATT_B_TQ, n_lat)
        yb = _from_heads(yb[:, :, 0])
        cq, kf, gf, kb, gb = c_prep(nm("c_prep"), _to_heads(seg(6), C_HEADS), _to_heads(seg(7), C_HEADS),
                                    _to_heads(seg(8), C_HEADS), clb, l)
        cv = _to_heads(seg(9), C_HEADS)
        o_f = gla_scan(nm("scan_f"), cq, kf, cv, gf, False, n_lat // SCAN_CHUNK)
        o_b = gla_scan(nm("scan_b"), cq, kb, cv, gb, True, n_lat // SCAN_CHUNK)
        yc = _from_heads(c_readout(nm("c_out"), o_f, o_b, _to_heads(seg(10), C_HEADS), vec("c_out_norm")))
        y = linear(nm("w_out"), jnp.concatenate([ya, yb, yc], axis=-1).reshape(bsz * t, d), prm["w_out"][l],
                   prm_lo["w_out"][l]).reshape(bsz, t, d)
        x1 = resid(nm("res_mix"), xu, y, vec("g_post_mix"), mods[l], 2, n_lat_tiles)
        h2 = premod(nm("preffn"), x1, vec("g_pre_ffn"), mods[l], 3, n_lat_tiles)
        f = mlp(nm("mlp"), h2.reshape(bsz * t, d), prm["w_ff1"][l], prm["w_ff2"][l], prm_lo["w_ff1"][l],
                prm_lo["w_ff2"][l]).reshape(bsz, t, d)
        xu = resid(nm("res_ffn"), x1, f, vec("g_post_ffn"), mods[l], 5, n_lat_tiles)
    return xu


BIG = {"w_in": 2, "w_q_up": 2, "w_kv_up": 2, "w_out": 1, "w_ff1": 2, "w_ff2": 1}
SMALL = ("g_pre_mix", "g_post_mix", "g_pre_ffn", "g_post_ffn", "a_q_norm", "a_k_norm", "b_q_norm", "b_kv_norm",
         "c_lower_bounds", "c_out_norm")
WEIGHTS = ("c_ctx", "w_ada", "b_ada", "g_pre_mix", "g_post_mix", "g_pre_ffn", "g_post_ffn", "w_in", "a_q_norm", "a_k_norm",
           "b_q_norm", "w_q_up", "b_kv_norm", "w_kv_up", "c_lower_bounds", "c_out_norm", "w_out", "w_ff1", "w_ff2")


def _unshard(g, axis):
    depth, _, r, c = g.shape
    if axis == 1:
        return g.reshape(depth, N_CHIP * r, c)
    return g.transpose(0, 2, 1, 3).reshape(depth, r, N_CHIP * c)


def _shard_major(g, axis):
    depth, r, c = g.shape
    if axis == 1:
        return g.reshape(depth, N_CHIP, r // N_CHIP, c)
    return g.reshape(depth, r, N_CHIP, c // N_CHIP).transpose(0, 2, 1, 3)


def kernel(x, c, ctx, c_ctx, w_ada, b_ada, g_pre_mix, g_post_mix, g_pre_ffn, g_post_ffn, w_in, a_q_norm, a_k_norm, b_q_norm, w_q_up, b_kv_norm, w_kv_up, c_lower_bounds, c_out_norm, w_out, w_ff1, w_ff2, loss_target, m_c_ctx, m_w_ada, m_b_ada, m_g_pre_mix, m_g_post_mix, m_g_pre_ffn, m_g_post_ffn, m_w_in, m_a_q_norm, m_a_k_norm, m_b_q_norm, m_w_q_up, m_b_kv_norm, m_w_kv_up, m_c_lower_bounds, m_c_out_norm, m_w_out, m_w_ff1, m_w_ff2, v_c_ctx, v_w_ada, v_b_ada, v_g_pre_mix, v_g_post_mix, v_g_pre_ffn, v_g_post_ffn, v_w_in, v_a_q_norm, v_a_k_norm, v_b_q_norm, v_w_q_up, v_b_kv_norm, v_w_kv_up, v_c_lower_bounds, v_c_out_norm, v_w_out, v_w_ff1, v_w_ff2):
    local = dict(c_ctx=c_ctx, w_ada=w_ada, b_ada=b_ada, g_pre_mix=g_pre_mix, g_post_mix=g_post_mix, g_pre_ffn=g_pre_ffn,
                 g_post_ffn=g_post_ffn, w_in=w_in, a_q_norm=a_q_norm, a_k_norm=a_k_norm, b_q_norm=b_q_norm, w_q_up=w_q_up,
                 b_kv_norm=b_kv_norm, w_kv_up=w_kv_up, c_lower_bounds=c_lower_bounds, c_out_norm=c_out_norm, w_out=w_out,
                 w_ff1=w_ff1, w_ff2=w_ff2)
    mom = dict(c_ctx=m_c_ctx, w_ada=m_w_ada, b_ada=m_b_ada, g_pre_mix=m_g_pre_mix, g_post_mix=m_g_post_mix,
               g_pre_ffn=m_g_pre_ffn, g_post_ffn=m_g_post_ffn, w_in=m_w_in, a_q_norm=m_a_q_norm, a_k_norm=m_a_k_norm,
               b_q_norm=m_b_q_norm, w_q_up=m_w_q_up, b_kv_norm=m_b_kv_norm, w_kv_up=m_w_kv_up,
               c_lower_bounds=m_c_lower_bounds, c_out_norm=m_c_out_norm, w_out=m_w_out, w_ff1=m_w_ff1, w_ff2=m_w_ff2)
    var = dict(c_ctx=v_c_ctx, w_ada=v_w_ada, b_ada=v_b_ada, g_pre_mix=v_g_pre_mix, g_post_mix=v_g_post_mix,
               g_pre_ffn=v_g_pre_ffn, g_post_ffn=v_g_post_ffn, w_in=v_w_in, a_q_norm=v_a_q_norm, a_k_norm=v_a_k_norm,
               b_q_norm=v_b_q_norm, w_q_up=v_w_q_up, b_kv_norm=v_b_kv_norm, w_kv_up=v_w_kv_up,
               c_lower_bounds=v_c_lower_bounds, c_out_norm=v_c_out_norm, w_out=v_w_out, w_ff1=v_w_ff1, w_ff2=v_w_ff2)

    bsz, n_lat, d = x.shape
    n_ctx = ctx.shape[1]
    depth = w_ada.shape[0]
    assert depth == 2 and n_lat % TOKEN_TILE == 0 and n_ctx % TOKEN_TILE == 0 and bsz * N_DEV + 1 <= ADA_ROWS
    ax, ay, ac = (lax.axis_index(a) for a in MESH_AXES)
    chip = 2 * ax + ay
    dev = 2 * chip + ac

    c_all = exchange("gather_c", [c], MESH_AXES, True)[0].reshape(N_DEV * bsz, d)
    big_names = list(BIG)
    over_chips = exchange("gather_w", [local[n].astype(BF16) for n in big_names] + [c_lower_bounds], ("x", "y"), True)
    prm_lo = {n: _unshard(g.transpose(1, 0, 2, 3), BIG[n]) for n, g in zip(big_names, over_chips[:-1])}
    prm = {n: w.astype(F32) for n, w in prm_lo.items()}
    prm["c_lower_bounds"] = over_chips[-1].transpose(1, 2, 0, 3).reshape(depth, 2, -1)
    for n in SMALL:
        if n != "c_lower_bounds":
            prm[n] = local[n]

    n_ada = w_ada.shape[2]
    cc = jnp.concatenate([c_all, c_ctx[None, :], jnp.zeros((ADA_ROWS - N_DEV * bsz - 1, d), F32)], axis=0)
    b_blk = lax.dynamic_slice_in_dim(b_ada, chip * n_ada, n_ada, axis=1)[:, None, :]
    mod_part, ada_vjp = jax.vjp(ada_op, cc, w_ada, b_blk)
    mod_full = exchange("gather_mod", [mod_part], ("x", "y"), True)[0].transpose(1, 2, 0, 3).reshape(depth, ADA_ROWS, 6 * d)
    mod_lat = lax.dynamic_slice_in_dim(mod_full, dev * bsz, bsz, axis=1)
    mod_ctx = jnp.broadcast_to(mod_full[:, N_DEV * bsz:N_DEV * bsz + 1], mod_lat.shape)
    mods = jnp.stack([mod_lat, mod_ctx], axis=2).reshape(depth, bsz, 2, 6, d)

    xu = jnp.concatenate([x, ctx], axis=1)
    x_out, fwd_vjp = jax.vjp(lambda xu_, mods_, prm_: _forward(xu_, mods_, prm_, prm_lo, n_lat, n_ctx), xu, mods, prm)
    dx_out, loss_blk = loss_op(x_out, loss_target, n_lat // TOKEN_TILE)
    d_xu, d_mods, d_prm = fwd_vjp(dx_out)
    grad_x = d_xu[:, :n_lat]

    d_mods = d_mods.reshape(depth, bsz, 2, 6 * d)
    pieces = [d_mods] + [d_prm[n] for n in SMALL] + [loss_blk[0:1, 0:1]]
    sizes = [int(np.prod(p.shape)) for p in pieces]
    flat = jnp.concatenate([p.reshape(-1) for p in pieces])
    n_flat = -(-flat.shape[0] // (SUBLANE * LANE)) * SUBLANE * LANE
    flat = jnp.concatenate([flat, jnp.zeros((n_flat - flat.shape[0],), F32)]).reshape(-1, LANE)
    small_all = exchange("gather_small", [flat], MESH_AXES, True)[0]
    small_sum = sum_parts("sum_small", small_all).reshape(-1)
    offs = np.cumsum([0] + sizes)
    summed = {n: small_sum[offs[i + 1]:offs[i + 2]].reshape(d_prm[n].shape) for i, n in enumerate(SMALL)}
    loss = small_sum[offs[-2]]
    dm_all = small_all.reshape(N_DEV, -1)[:, :sizes[0]].reshape(N_DEV, depth, bsz, 2, 6 * d)
    dm_rows = dm_all.transpose(3, 0, 2, 1, 4).reshape(2, N_DEV * bsz, depth, 6 * d)
    d_ctx_row = sum_parts("sum_dmod_ctx", dm_rows[1])
    grad_b_ada = sum_parts("sum_b_ada", dm_rows.reshape(2 * N_DEV * bsz, depth, 6 * d))
    d_rows = jnp.concatenate([dm_rows[0].transpose(1, 0, 2), d_ctx_row[:, None, :],
                              jnp.zeros((depth, ADA_ROWS - N_DEV * bsz - 1, 6 * d), F32)], axis=1)
    d_cc, grad_w_ada, _ = ada_vjp(lax.dynamic_slice_in_dim(d_rows, chip * n_ada, n_ada, axis=2))
    d_cctx_all = exchange("gather_dcctx", [d_cc[N_DEV * bsz:N_DEV * bsz + 1]], MESH_AXES, True)[0]
    grad_c_ctx = sum_parts("sum_dcctx", d_cctx_all[0::2]).reshape(d)

    shard_major = [_shard_major(d_prm[n], BIG[n]) for n in big_names]
    pair = exchange("rs_pair", shard_major, ("c",), False, pieces=4)
    chip_sum = [sum_parts(f"rs_sum1_{n}", p) for n, p in zip(big_names, pair)]
    quad = exchange("rs_quad", chip_sum, ("x", "y"), False)
    total = [sum_parts(f"rs_sum2_{n}", q) for n, q in zip(big_names, quad)]
    both = exchange("rs_share", total, ("c",), True, pieces=8)

    grads = dict(summed)
    grads["c_lower_bounds"] = lax.dynamic_slice_in_dim(summed["c_lower_bounds"], chip * c_lower_bounds.shape[2],
                                                       c_lower_bounds.shape[2], axis=2)
    grads.update(c_ctx=grad_c_ctx, w_ada=grad_w_ada, b_ada=grad_b_ada)
    grads.update({n: g for n, g in zip(big_names, both)})

    deltas, new_m, new_v = {}, {}, {}
    for n in WEIGHTS:
        as2d = (lambda a: a[None, :]) if local[n].ndim == 1 else (lambda a: a)
        dl, nm_, nv_ = adamw("adamw_" + n, as2d(local[n]), as2d(grads[n]), as2d(mom[n]), as2d(var[n]))
        deltas[n], new_m[n], new_v[n] = (a.reshape(local[n].shape) for a in (dl, nm_, nv_))
    return (loss, grad_x, *[grads[n] for n in WEIGHTS], *[deltas[n] for n in WEIGHTS],
            *[new_m[n] for n in WEIGHTS], *[new_v[n] for n in WEIGHTS])
```

```python
import functools
from typing import Any, Callable, NamedTuple

import numpy as np
import jax
import jax.numpy as jnp
from jax import lax
from jax.experimental import pallas as pl
from jax.experimental.pallas import tpu as pltpu

F32 = jnp.float32
BF16 = jnp.bfloat16
HIGHEST = lax.Precision.HIGHEST

GRID_W = 64
HEAD_DIM = 64
A_HEADS, A_KV_HEADS = 8, 2
A_GROUP = A_HEADS // A_KV_HEADS
B_HEADS, B_Q_RANK, B_KV_RANK, B_NOPE, B_ROPE, B_V = 4, 192, 128, 64, 32, 64
C_HEADS, C_DK, C_DV = 4, 64, 64
SCAN_CHUNK = 64
SCAN_SUB = 16
ROPE_THETA = 10000.0
EPS = 1e-6
F_TINY = 1e-30
ADAM_LR, ADAM_B1, ADAM_B2, ADAM_EPS, ADAM_WD, ADAM_STEP = 0.001, 0.9, 0.999, 1e-08, 0.01, 10

IN_SIZES = (512, 128, 128, 192, 128, 32, 256, 256, 256, 256, 256)
IN_PAD = (512, 128, 128, 256, 128, 128, 256, 256, 256, 256, 256)
IN_OFF = tuple(int(v) for v in np.cumsum((0,) + IN_PAD)[:-1])
D_IN_PAD = int(sum(IN_PAD))

LANE = 128
SUBLANE = 8
TOKEN_TILE = 256
ATT_A_TQ = 64
ATT_B_TQ = 256
ATT_FWD_ROW_GROUPS = 2
MM_ROWS = 256
MM_TK_TOKENS = 512
MM_TN_OUT = 2560 * 1024
VMEM_LIMIT = 56 * 1024 * 1024
MESH_AXES = ("x", "y", "c")
N_DEV = 8
N_CHIP = 4
ADA_ROWS = 24


class Arg(NamedTuple):
    arr: Any
    block: tuple
    imap: Callable
    kind: str
    first: Callable = None


class Out(NamedTuple):
    shape: tuple
    block: tuple
    imap: Callable


def _cparams():
    return pltpu.CompilerParams(vmem_limit_bytes=VMEM_LIMIT)


def tile_op(name, fn, grid, args, outs):
    n_in, n_out = len(args), len(outs)
    in_specs = [pl.BlockSpec(a.block, a.imap) for a in args]
    out_specs = [pl.BlockSpec(o.block, o.imap) for o in outs]
    out_shape = [jax.ShapeDtypeStruct(o.shape, F32) for o in outs]
    diff = [i for i, a in enumerate(args) if a.kind != "const"]

    def fwd_call(*arrays):
        def body(*refs):
            ids = tuple(pl.program_id(i) for i in range(len(grid)))
            res = fn(ids, *[r[...] for r in refs[:n_in]])
            for r, o in zip(refs[n_in:], res):
                r[...] = o

        return pl.pallas_call(body, grid=grid, in_specs=in_specs, out_specs=out_specs, out_shape=out_shape,
                              name=name + "_fwd", compiler_params=_cparams())(*arrays)

    def bwd_call(arrays, cts):
        def body(*refs):
            ids = tuple(pl.program_id(i) for i in range(len(grid)))
            vals = [r[...] for r in refs[:n_in]]
            ct = tuple(r[...] for r in refs[n_in:n_in + n_out])
            drefs = refs[n_in + n_out:]

            def g(*dv):
                full = list(vals)
                for i, v in zip(diff, dv):
                    full[i] = v
                return tuple(fn(ids, *full))

            _, vjp = jax.vjp(g, *[vals[i] for i in diff])
            ds = vjp(ct)
            for i, d, r in zip(diff, ds, drefs):
                if args[i].kind == "tile":
                    r[...] = d
                else:
                    is_first = args[i].first(ids)

                    @pl.when(is_first)
                    def _(r=r, d=d):
                        r[...] = d

                    @pl.when(jnp.logical_not(is_first))
                    def _(r=r, d=d):
                        r[...] += d

        d_specs = [in_specs[i] for i in diff]
        d_shape = [jax.ShapeDtypeStruct(arrays[i].shape, F32) for i in diff]
        return pl.pallas_call(body, grid=grid, in_specs=in_specs + out_specs, out_specs=d_specs, out_shape=d_shape,
                              name=name + "_bwd", compiler_params=_cparams())(*arrays, *cts)

    @jax.custom_vjp
    def op(*arrays):
        return tuple(fwd_call(*arrays))

    def op_fwd(*arrays):
        return tuple(fwd_call(*arrays)), arrays

    def op_bwd(arrays, cts):
        ds = bwd_call(arrays, cts)
        res, k = [], 0
        for i, a in enumerate(args):
            if a.kind == "const":
                res.append(jnp.zeros_like(arrays[i]))
            else:
                res.append(ds[k])
                k += 1
        return tuple(res)

    op.defvjp(op_fwd, op_bwd)
    return op(*[a.arr for a in args])


def _pick(n, cap):
    if n <= cap:
        return n
    best = None
    for t in range(LANE, cap + 1, LANE):
        if n % t == 0:
            best = t
    assert best is not None, (n, cap)
    return best


_NN = (((1,), (0,)), ((), ()))
_NT = (((1,), (1,)), ((), ()))
_TN = (((0,), (0,)), ((), ()))


def _resident(shape):
    return pl.BlockSpec(shape, lambda *ids: (0,) * len(shape), pipeline_mode=pl.Buffered(1))


def _mm_rows(name, a, w, transposed):
    m, k = a.shape
    n = w.shape[0] if transposed else w.shape[1]
    tm = MM_ROWS
    dims = _NT if transposed else _NN

    def body(a_ref, w_ref, o_ref):
        o_ref[...] = lax.dot_general(a_ref[...].astype(BF16), w_ref[...], dims, preferred_element_type=F32)

    return pl.pallas_call(body, grid=(m // tm,), in_specs=[pl.BlockSpec((tm, k), lambda i: (i, 0)), _resident(w.shape)],
                          out_specs=pl.BlockSpec((tm, n), lambda i: (i, 0)), out_shape=jax.ShapeDtypeStruct((m, n), F32),
                          name=name, compiler_params=_cparams())(a, w)


def _mm_tn(name, a, g):
    t, k = a.shape
    n = g.shape[1]
    tko, tno = k, n
    while tko * tno > MM_TN_OUT:
        if tko >= tno:
            tko //= 2
        else:
            tno //= 2
    assert k % tko == 0 and n % tno == 0 and tko % LANE == 0 and tno % LANE == 0
    tt = _pick(t, MM_TK_TOKENS)

    def body(a_ref, g_ref, o_ref):
        p = lax.dot_general(a_ref[...].astype(BF16), g_ref[...].astype(BF16), _TN, preferred_element_type=F32)
        kk = pl.program_id(2)

        @pl.when(kk == 0)
        def _():
            o_ref[...] = p

        @pl.when(kk != 0)
        def _():
            o_ref[...] += p

    return pl.pallas_call(body, grid=(k // tko, n // tno, t // tt),
                          in_specs=[pl.BlockSpec((tt, tko), lambda i, j, kk: (kk, i)), pl.BlockSpec((tt, tno), lambda i, j, kk: (kk, j))],
                          out_specs=pl.BlockSpec((tko, tno), lambda i, j, kk: (i, j)),
                          out_shape=jax.ShapeDtypeStruct((k, n), F32), name=name, compiler_params=_cparams())(a, g)


def linear(name, a, w, w_lo):
    @jax.custom_vjp
    def op(a, w, w_lo):
        return _mm_rows(name + "_fwd", a, w_lo, False)

    def op_fwd(a, w, w_lo):
        return _mm_rows(name + "_fwd", a, w_lo, False), (a, w_lo)

    def op_bwd(res, g):
        a, w_lo = res
        return _mm_rows(name + "_da", g, w_lo, True), _mm_tn(name + "_dw", a, g), jnp.zeros_like(w_lo)

    op.defvjp(op_fwd, op_bwd)
    return op(a, w, w_lo)


def mlp(name, h, w1, w2, w1_lo, w2_lo):
    m, d = h.shape
    f = w1_lo.shape[1]
    tm = MM_ROWS
    row = pl.BlockSpec((tm, d), lambda i: (i, 0))
    wide = pl.BlockSpec((tm, f), lambda i: (i, 0))

    def fwd_call(h, w1_lo, w2_lo):
        def body(h_ref, w1_ref, w2_ref, o_ref):
            u = lax.dot_general(h_ref[...].astype(BF16), w1_ref[...], _NN, preferred_element_type=F32)
            act = jnp.square(jnp.maximum(u, 0.0))
            o_ref[...] = lax.dot_general(act.astype(BF16), w2_ref[...], _NN, preferred_element_type=F32)

        return pl.pallas_call(body, grid=(m // tm,), in_specs=[row, _resident(w1_lo.shape), _resident(w2_lo.shape)],
                              out_specs=row, out_shape=jax.ShapeDtypeStruct((m, d), F32), name=name + "_fwd",
                              compiler_params=_cparams())(h, w1_lo, w2_lo)

    def bwd_call(h, w1_lo, w2_lo, dy):
        def body(h_ref, dy_ref, w1_ref, w2_ref, dh_ref, act_ref, du_ref):
            u = lax.dot_general(h_ref[...].astype(BF16), w1_ref[...], _NN, preferred_element_type=F32)
            r = jnp.maximum(u, 0.0)
            act_ref[...] = (r * r).astype(BF16)
            dact = lax.dot_general(dy_ref[...].astype(BF16), w2_ref[...], _NT, preferred_element_type=F32)
            du = (dact * (2.0 * r)).astype(BF16)
            du_ref[...] = du
            dh_ref[...] = lax.dot_general(du, w1_ref[...], _NT, preferred_element_type=F32)

        return pl.pallas_call(body, grid=(m // tm,), in_specs=[row, row, _resident(w1_lo.shape), _resident(w2_lo.shape)],
                              out_specs=[row, wide, wide],
                              out_shape=[jax.ShapeDtypeStruct((m, d), F32), jax.ShapeDtypeStruct((m, f), BF16),
                                         jax.ShapeDtypeStruct((m, f), BF16)],
                              name=name + "_bwd", compiler_params=_cparams())(h, dy, w1_lo, w2_lo)

    @jax.custom_vjp
    def op(h, w1, w2, w1_lo, w2_lo):
        return fwd_call(h, w1_lo, w2_lo)

    def op_fwd(h, w1, w2, w1_lo, w2_lo):
        return fwd_call(h, w1_lo, w2_lo), (h, w1_lo, w2_lo)

    def op_bwd(res, dy):
        h, w1_lo, w2_lo = res
        dh, act, du = bwd_call(h, w1_lo, w2_lo, dy)
        return (dh, _mm_tn(name + "_dw1", h, du), _mm_tn(name + "_dw2", act, dy), jnp.zeros_like(w1_lo), jnp.zeros_like(w2_lo))

    op.defvjp(op_fwd, op_bwd)
    return op(h, w1, w2, w1_lo, w2_lo)


def _rms(x, g):
    return x * lax.rsqrt(jnp.mean(x * x, axis=-1, keepdims=True) + EPS) * g


def _sigmoid(z):
    return 1.0 / (1.0 + jnp.exp(-z))


def _silu(z):
    return z * _sigmoid(z)


def _rope(y, cos, sin_signed, swap):
    return y * cos + jnp.dot(y, swap, precision=HIGHEST, preferred_element_type=F32) * sin_signed


def _dot_bf16(a, b, dims=((1,), (0,))):
    return lax.dot_general(a.astype(BF16), b.astype(BF16), (dims, ((), ())), preferred_element_type=F32)


def _gla_step(state, q, k, v, g, reverse):
    c, d = q.shape
    sub = SCAN_SUB
    nb = c // sub
    row = lax.broadcasted_iota(jnp.int32, (c, c), 0)
    col = lax.broadcasted_iota(jnp.int32, (c, c), 1)
    tri = (row <= col) if reverse else (row >= col)
    b = jnp.dot(tri.astype(F32), g, precision=HIGHEST, preferred_element_type=F32)
    o = jnp.dot(q * jnp.exp(b), state, preferred_element_type=F32)
    rs = lax.broadcasted_iota(jnp.int32, (sub, sub), 0)
    cs = lax.broadcasted_iota(jnp.int32, (sub, sub), 1)
    tri_s = ((rs <= cs) if reverse else (rs >= cs)).astype(F32)
    rowc = lax.broadcasted_iota(jnp.int32, (c, 1), 0)
    nonpos = lambda x: jnp.where(x > 0.0, 0.0, x)
    diag = []
    for j in range(nb):
        sl = slice(j * sub, (j + 1) * sub)
        bj, kj, vj, qj = b[sl], k[sl], v[sl], q[sl]
        dec = jnp.exp(nonpos(bj[:, None, :] - bj[None, :, :]))
        sc = jnp.sum(qj[:, None, :] * kj[None, :, :] * dec, axis=-1) * tri_s
        diag.append(jnp.dot(sc, vj, preferred_element_type=F32))
        if (j > 0) if reverse else (j < nb - 1):
            ref = bj[0:1] if reverse else bj[sub - 1:sub]
            qa = q * jnp.exp(nonpos(b - ref))
            ks = kj * jnp.exp(ref - bj)
            scj = lax.dot_general(qa, ks, (((1,), (1,)), ((), ())), precision=HIGHEST, preferred_element_type=F32)
            later = (rowc < j * sub) if reverse else (rowc >= (j + 1) * sub)
            o = o + jnp.dot(jnp.where(later, scj, 0.0), vj, preferred_element_type=F32)
    o = o + jnp.concatenate(diag, axis=0)
    b_end = b[0:1, :] if reverse else b[c - 1:c, :]
    kd = k * jnp.exp(b_end - b)
    new_state = state * jnp.exp(b_end).reshape(d, 1) + lax.dot_general(kd, v, (((0,), (0,)), ((), ())), preferred_element_type=F32)
    return new_state, o


def gla_scan(name, q, k, v, g, reverse, n_lat_chunks):
    bsz, nh, t, d = q.shape
    c = SCAN_CHUNK
    n = t // c

    def chunk_of(j):
        return (n - 1 - j) if reverse else lax.rem(j + n_lat_chunks, n)

    blk = (bsz, nh, c, d)
    st_blk = (bsz, nh, None, d, d)
    chains = [(b, h) for b in range(bsz) for h in range(nh)]

    def fwd_call(q, k, v, g):
        def body(q_ref, k_ref, v_ref, g_ref, o_ref, states_ref, st):
            @pl.when(pl.program_id(0) == 0)
            def _():
                st[...] = jnp.zeros_like(st)
            for bh in chains:
                s = st[bh]
                states_ref[bh] = s
                ns, o = _gla_step(s, q_ref[bh], k_ref[bh], v_ref[bh], g_ref[bh], reverse)
                st[bh] = ns
                o_ref[bh] = o

        spec = pl.BlockSpec(blk, lambda j: (0, 0, chunk_of(j), 0))
        return pl.pallas_call(
            body, grid=(n,), in_specs=[spec] * 4,
            out_specs=[spec, pl.BlockSpec(st_blk, lambda j: (0, 0, j, 0, 0))],
            out_shape=[jax.ShapeDtypeStruct(q.shape, F32), jax.ShapeDtypeStruct((bsz, nh, n, d, d), F32)],
            scratch_shapes=[pltpu.VMEM((bsz, nh, d, d), F32)], name=name + "_fwd", compiler_params=_cparams())(q, k, v, g)

    def bwd_call(q, k, v, g, states, do):
        def body(q_ref, k_ref, v_ref, g_ref, s_ref, do_ref, dq_ref, dk_ref, dv_ref, dg_ref, dst):
            @pl.when(pl.program_id(0) == 0)
            def _():
                dst[...] = jnp.zeros_like(dst)
            step = functools.partial(_gla_step, reverse=reverse)
            for bh in chains:
                _, vjp = jax.vjp(step, s_ref[bh], q_ref[bh], k_ref[bh], v_ref[bh], g_ref[bh])
                ds, dq, dk, dv, dg = vjp((dst[bh], do_ref[bh]))
                dst[bh] = ds
                dq_ref[bh] = dq
                dk_ref[bh] = dk
                dv_ref[bh] = dv
                dg_ref[bh] = dg

        spec = pl.BlockSpec(blk, lambda jj: (0, 0, chunk_of(n - 1 - jj), 0))
        s_spec = pl.BlockSpec(st_blk, lambda jj: (0, 0, n - 1 - jj, 0, 0))
        return pl.pallas_call(
            body, grid=(n,), in_specs=[spec] * 4 + [s_spec, spec], out_specs=[spec] * 4,
            out_shape=[jax.ShapeDtypeStruct(q.shape, F32)] * 4,
            scratch_shapes=[pltpu.VMEM((bsz, nh, d, d), F32)], name=name + "_bwd", compiler_params=_cparams())(q, k, v, g, states, do)

    @jax.custom_vjp
    def op(q, k, v, g):
        return fwd_call(q, k, v, g)[0]

    def op_fwd(q, k, v, g):
        o, states = fwd_call(q, k, v, g)
        return o, (q, k, v, g, states)

    def op_bwd(res, do):
        return tuple(bwd_call(*res, do))

    op.defvjp(op_fwd, op_bwd)
    return op(q, k, v, g)


def _zero_ids(ids):
    z = ids[0] == 0
    for i in ids[1:]:
        z = jnp.logical_and(z, i == 0)
    return z


def _token_grid(x, n_lat_tiles):
    bsz, t, d = x.shape
    nt = t // TOKEN_TILE
    row = lambda w: ((None, TOKEN_TILE, w), lambda b, i: (b, i, 0))
    mod_block = (None, None, 6, d)
    mod_imap = lambda b, i: (b, (i >= n_lat_tiles).astype(jnp.int32), 0, 0)
    mod_first = lambda ids: jnp.logical_or(ids[1] == 0, ids[1] == n_lat_tiles)
    return bsz, t, d, nt, row, (mod_block, mod_imap, mod_first)


def premod(name, x, gain, mods, r0, n_lat_tiles):
    bsz, t, d, nt, row, (mb, mi, mf) = _token_grid(x, n_lat_tiles)

    def fn(ids, xb, gb, mod):
        return (_rms(xb, gb) * (1.0 + mod[r0 + 1:r0 + 2]) + mod[r0:r0 + 1],)

    args = [Arg(x, *row(d), "tile"), Arg(gain, (1, d), lambda b, i: (0, 0), "acc", _zero_ids), Arg(mods, mb, mi, "acc", mf)]
    return tile_op(name, fn, (bsz, nt), args, [Out(x.shape, *row(d))])[0]


def resid(name, x, y, gain, mods, r, n_lat_tiles):
    bsz, t, d, nt, row, (mb, mi, mf) = _token_grid(x, n_lat_tiles)

    def fn(ids, xb, yb, gb, mod):
        return (xb + mod[r:r + 1] * _rms(yb, gb),)

    args = [Arg(x, *row(d), "tile"), Arg(y, *row(d), "tile"), Arg(gain, (1, d), lambda b, i: (0, 0), "acc", _zero_ids),
            Arg(mods, mb, mi, "acc", mf)]
    return tile_op(name, fn, (bsz, nt), args, [Out(x.shape, *row(d))])[0]


def _heads_spec(nh, w):
    return (None, nh, TOKEN_TILE, w), lambda b, i: (b, 0, i, 0)


def a_prep(name, x, gain, cos, sin, swap):
    bsz, nh, t, d = x.shape
    tab = ((TOKEN_TILE, d), lambda b, i: (i, 0))

    def fn(ids, xb, gb, cb, sb, pb):
        y = _rms(xb, gb)
        swapped = jnp.dot(y.reshape(nh * TOKEN_TILE, d), pb, precision=HIGHEST, preferred_element_type=F32)
        return (y * cb + swapped.reshape(nh, TOKEN_TILE, d) * sb,)

    args = [Arg(x, *_heads_spec(nh, d), "tile"), Arg(gain, (1, d), lambda b, i: (0, 0), "acc", _zero_ids),
            Arg(cos, *tab, "const"), Arg(sin, *tab, "const"), Arg(swap, (d, d), lambda b, i: (0, 0), "const")]
    return tile_op(name, fn, (bsz, t // TOKEN_TILE), args, [Out(x.shape, *_heads_spec(nh, d))])[0]


def b_prep(name, bqd, bkvd, bkr, bqn, wq_nope, wq_pe, bkvn, wkv_nope, wkv_v, cos_q, sin_q, swap_q, cos_k, sin_k, swap_k):
    bsz, t, _ = bqd.shape
    row = lambda w: ((None, TOKEN_TILE, w), lambda b, i: (b, i, 0))
    whole = lambda a: (a.shape, lambda b, i: (0,) * a.ndim)
    tab = lambda w: ((TOKEN_TILE, w), lambda b, i: (i, 0))

    def fn(ids, qd, kvd, kr, qn, wqn, wqp, kvn, wkn, wkv, cq, sq, pq, ck, sk, pk):
        hq = _rms(qd, qn)
        hkv = _rms(kvd, kvn)
        return (_dot_bf16(hq, wqn), _rope(_dot_bf16(hq, wqp), cq, sq, pq), _dot_bf16(hkv, wkn), _dot_bf16(hkv, wkv),
                _rope(kr, ck, sk, pk))

    params = [bqn, wq_nope, wq_pe, bkvn, wkv_nope, wkv_v]
    args = [Arg(bqd, *row(B_Q_RANK), "tile"), Arg(bkvd, *row(B_KV_RANK), "tile"), Arg(bkr, *row(B_ROPE), "tile")]
    args += [Arg(p, *whole(p), "acc", _zero_ids) for p in params]
    args += [Arg(cos_q, *tab(cos_q.shape[1]), "const"), Arg(sin_q, *tab(cos_q.shape[1]), "const"), Arg(swap_q, *whole(swap_q), "const"),
             Arg(cos_k, *tab(B_ROPE), "const"), Arg(sin_k, *tab(B_ROPE), "const"), Arg(swap_k, *whole(swap_k), "const")]
    widths = (B_HEADS * B_NOPE, B_HEADS * B_ROPE, B_HEADS * B_NOPE, B_HEADS * B_V, B_ROPE)
    outs = [Out((bsz, t, w), *row(w)) for w in widths]
    return tile_op(name, fn, (bsz, t // TOKEN_TILE), args, outs)


def c_prep(name, cq, cff, cfb, clb, layer):
    bsz, nh, t, d = cq.shape
    depth = clb.shape[1]
    spec = _heads_spec(nh, d)

    def fn(ids, q, zf, zb, lbs):
        lb = [lbs[:, j] for j in range(depth)]
        m = lb[0]
        for j in range(1, depth):
            m = jnp.maximum(m, lb[j])
        e = [jnp.exp(lb[j] - m) for j in range(depth)]
        tot = e[0]
        for j in range(1, depth):
            tot = tot + e[j]
        p = [ej / tot for ej in e]
        cum = p[0]
        for j in range(1, layer + 1):
            cum = cum + p[j]
        lower = cum - p[0]

        def gate(z, lo):
            f = lo + (1.0 - lo) * _sigmoid(z)
            return jnp.log(jnp.maximum(f, F_TINY)), (1.0 - lo) * _sigmoid(-z)

        gf, kf = gate(zf, lower[:, 0:1])
        gb, kb = gate(zb, lower[:, 1:2])
        return _silu(q), kf, gf, kb, gb

    args = [Arg(cq, *spec, "tile"), Arg(cff, *spec, "tile"), Arg(cfb, *spec, "tile"),
            Arg(clb, clb.shape, lambda b, i: (0, 0, 0, 0), "acc", _zero_ids)]
    return tile_op(name, fn, (bsz, t // TOKEN_TILE), args, [Out(cq.shape, *spec)] * 5)


def c_readout(name, o_f, o_b, gate, gain):
    bsz, nh, t, d = o_f.shape
    spec = _heads_spec(nh, d)

    def fn(ids, of, ob, gt, gn):
        return (_rms(of + ob, gn) * _silu(gt),)

    args = [Arg(o_f, *spec, "tile"), Arg(o_b, *spec, "tile"), Arg(gate, *spec, "tile"),
            Arg(gain, (1, d), lambda b, i: (0, 0), "acc", _zero_ids)]
    return tile_op(name, fn, (bsz, t // TOKEN_TILE), args, [Out(o_f.shape, *spec)])[0]


def _key_bias(t, tq, i, n_lat):
    key = lax.broadcasted_iota(jnp.int32, (1, t), 1)
    return jnp.where(jnp.logical_and(i * tq >= n_lat, key < n_lat), -1e30, 0.0)


def attention(name, qs, kts, vt, scale, fold_scale, tq, n_lat):
    bsz, hk, grp, t, _ = qs[0].shape
    dv = vt.shape[2]
    rows = grp * tq
    n_parts = len(qs)
    q_specs = [pl.BlockSpec((None, None, grp, tq, q.shape[-1]), lambda b, h, i: (b, h, 0, i, 0)) for q in qs]
    shared = [kt.shape[1] == 1 for kt in kts]
    kt_specs = [pl.BlockSpec((None, None, kt.shape[2], t), (lambda b, h, i: (b, 0, 0, 0)) if sh else (lambda b, h, i: (b, h, 0, 0)))
                for kt, sh in zip(kts, shared)]
    vt_spec = pl.BlockSpec((None, None, dv, t), lambda b, h, i: (b, h, 0, 0))
    o_spec = pl.BlockSpec((None, None, grp, tq, dv), lambda b, h, i: (b, h, 0, i, 0))
    lse_spec = pl.BlockSpec((None, None, grp, tq, LANE), lambda b, h, i: (b, h, 0, i, 0))
    o_shape = jax.ShapeDtypeStruct((bsz, hk, grp, t, dv), F32)
    lse_shape = jax.ShapeDtypeStruct((bsz, hk, grp, t, LANE), F32)
    grid = (bsz, hk, t // tq)

    def load_q(q_refs):
        qq = []
        for q_ref in q_refs:
            q2 = q_ref[...].reshape(rows, q_ref.shape[-1])
            qq.append((q2 * scale if fold_scale else q2).astype(BF16))
        return qq

    def scores(qq, kt_refs, bias):
        s = None
        for q2, kt_ref in zip(qq, kt_refs):
            part = lax.dot_general(q2, kt_ref[...].astype(BF16), _NN, preferred_element_type=F32)
            s = part if s is None else s + part
        if not fold_scale:
            s = s * scale
        return s + bias

    def fwd_call(*arrays):
        def body(*refs):
            q_refs, kt_refs, vt_ref = refs[:n_parts], refs[n_parts:2 * n_parts], refs[2 * n_parts]
            o_ref, lse_ref = refs[2 * n_parts + 1:]
            qq = load_q(q_refs)
            bias = _key_bias(t, tq, pl.program_id(2), n_lat)
            n_grp = ATT_FWD_ROW_GROUPS
            gr = rows // n_grp
            outs, lses = [], []
            for r in range(n_grp):
                s = scores([q2[r * gr:(r + 1) * gr] for q2 in qq], kt_refs, bias)
                m = jnp.max(s, axis=-1, keepdims=True)
                e = jnp.exp(s - m)
                l = jnp.sum(e, axis=-1, keepdims=True)
                outs.append(lax.dot_general(e.astype(BF16), vt_ref[...].astype(BF16), _NT, preferred_element_type=F32) * (1.0 / l))
                lses.append(jnp.broadcast_to(m + jnp.log(l), (gr, LANE)))
            o_ref[...] = jnp.concatenate(outs, axis=0).reshape(grp, tq, dv)
            lse_ref[...] = jnp.concatenate(lses, axis=0).reshape(grp, tq, LANE)

        return pl.pallas_call(body, grid=grid, in_specs=q_specs + kt_specs + [vt_spec], out_specs=[o_spec, lse_spec],
                              out_shape=[o_shape, lse_shape], name=name + "_fwd", compiler_params=_cparams())(*arrays)

    def bwd_call(arrays, o, lse, do):
        def body(*refs):
            q_refs, kt_refs, vt_ref = refs[:n_parts], refs[n_parts:2 * n_parts], refs[2 * n_parts]
            o_ref, lse_ref, do_ref = refs[2 * n_parts + 1:2 * n_parts + 4]
            d_refs = refs[2 * n_parts + 4:]
            dq_refs, dkt_refs, dvt_ref = d_refs[:n_parts], d_refs[n_parts:2 * n_parts], d_refs[2 * n_parts]
            h, i = pl.program_id(1), pl.program_id(2)
            qq = load_q(q_refs)
            s = scores(qq, kt_refs, _key_bias(t, tq, i, n_lat))
            p = jnp.exp(s - lse_ref[...].reshape(rows, LANE)[:, 0:1])
            do2 = do_ref[...].reshape(rows, dv)
            delta = jnp.sum(do2 * o_ref[...].reshape(rows, dv), axis=-1, keepdims=True)
            do_lo = do2.astype(BF16)
            dp = lax.dot_general(do_lo, vt_ref[...].astype(BF16), _NN, preferred_element_type=F32)
            ds = p * (dp - delta)
            if not fold_scale:
                ds = ds * scale
            ds_lo = ds.astype(BF16)

            def accumulate(ref, val, first):
                @pl.when(first)
                def _():
                    ref[...] = val

                @pl.when(jnp.logical_not(first))
                def _():
                    ref[...] += val

            accumulate(dvt_ref, lax.dot_general(do_lo, p.astype(BF16), _TN, preferred_element_type=F32), i == 0)
            for q2, kt_ref, dq_ref, dkt_ref, sh in zip(qq, kt_refs, dq_refs, dkt_refs, shared):
                dq = lax.dot_general(ds_lo, kt_ref[...].astype(BF16), _NT, preferred_element_type=F32)
                dq_ref[...] = (dq * scale if fold_scale else dq).reshape(dq_ref.shape)
                first = jnp.logical_and(h == 0, i == 0) if sh else i == 0
                accumulate(dkt_ref, lax.dot_general(q2, ds_lo, _TN, preferred_element_type=F32), first)

        in_specs = q_specs + kt_specs + [vt_spec, o_spec, lse_spec, o_spec]
        d_shape = [jax.ShapeDtypeStruct(a.shape, F32) for a in arrays]
        return pl.pallas_call(body, grid=grid, in_specs=in_specs, out_specs=q_specs + kt_specs + [vt_spec], out_shape=d_shape,
                              name=name + "_bwd", compiler_params=_cparams())(*arrays, o, lse, do)

    @jax.custom_vjp
    def op(*arrays):
        return fwd_call(*arrays)[0]

    def op_fwd(*arrays):
        o, lse = fwd_call(*arrays)
        return o, (arrays, o, lse)

    def op_bwd(res, do):
        arrays, o, lse = res
        return tuple(bwd_call(arrays, o, lse, do))

    op.defvjp(op_fwd, op_bwd)
    return op(*qs, *kts, vt)


def ada_op(cc, w, b):
    depth, d, n = w.shape

    def fn(ids, ccb, wb, bb):
        return (_dot_bf16(_silu(ccb), wb) + bb,)

    args = [Arg(cc, cc.shape, lambda l: (0, 0), "acc", lambda ids: ids[0] == 0),
            Arg(w, (None, d, n), lambda l: (l, 0, 0), "tile"), Arg(b, (None, 1, n), lambda l: (l, 0, 0), "tile")]
    return tile_op("ada", fn, (depth,), args, [Out((depth, cc.shape[0], n), (None, cc.shape[0], n), lambda l: (l, 0, 0))])[0]


def loss_op(xu, target, n_lat_tiles):
    bsz, t, d = xu.shape

    def body(x_ref, t_ref, dx_ref, l_ref):
        b, i = pl.program_id(0), pl.program_id(1)

        @pl.when(jnp.logical_and(b == 0, i == 0))
        def _():
            l_ref[...] = jnp.zeros_like(l_ref)

        @pl.when(i < n_lat_tiles)
        def _():
            err = x_ref[...] - t_ref[...]
            dx_ref[...] = err * (1.0 / d)
            l_ref[...] += 0.5 * jnp.sum(jnp.mean(err * err, axis=-1))

        @pl.when(i >= n_lat_tiles)
        def _():
            dx_ref[...] = jnp.zeros_like(dx_ref)

    row = pl.BlockSpec((None, TOKEN_TILE, d), lambda b, i: (b, i, 0))
    t_spec = pl.BlockSpec((None, TOKEN_TILE, d), lambda b, i: (b, jnp.minimum(i, n_lat_tiles - 1), 0))
    return pl.pallas_call(body, grid=(bsz, t // TOKEN_TILE), in_specs=[row, t_spec],
                          out_specs=[row, pl.BlockSpec((SUBLANE, LANE), lambda b, i: (0, 0))],
                          out_shape=[jax.ShapeDtypeStruct(xu.shape, F32), jax.ShapeDtypeStruct((SUBLANE, LANE), F32)],
                          name="loss", compiler_params=_cparams())(xu, target)


def _row_tile(rows, row_bytes, budget=4 << 20, step=SUBLANE):
    if rows * row_bytes <= budget:
        return rows
    best = None
    for t in range(step, rows, step):
        if rows % t == 0 and t * row_bytes <= budget:
            best = t
    return best if best is not None else rows


def _as3d(x):
    p = x.shape[0]
    c = x.shape[-1]
    return x.reshape(p, -1, c)


def sum_parts(name, x):
    x3 = _as3d(x)
    p, r, c = x3.shape
    tr = _row_tile(r, p * c * 4, step=SUBLANE * (4 // x.dtype.itemsize))

    def body(x_ref, o_ref):
        s = x_ref[0].astype(F32)
        for j in range(1, p):
            s = s + x_ref[j].astype(F32)
        o_ref[...] = s

    out = pl.pallas_call(body, grid=(r // tr,), in_specs=[pl.BlockSpec((p, tr, c), lambda i: (0, i, 0))],
                         out_specs=pl.BlockSpec((tr, c), lambda i: (i, 0)), out_shape=jax.ShapeDtypeStruct((r, c), F32),
                         name=name, compiler_params=_cparams())(x3)
    return out.reshape(x.shape[1:])


def adamw(name, w, g, m, v):
    shape = w.shape
    c = shape[-1]
    to2d = lambda a: a.reshape(-1, c)
    r = to2d(w).shape[0]
    tr = _row_tile(r, 7 * c * 4, budget=6 << 20)

    def body(w_ref, g_ref, m_ref, v_ref, d_ref, nm_ref, nv_ref):
        gg = g_ref[...]
        nm = ADAM_B1 * m_ref[...] + (1.0 - ADAM_B1) * gg
        nv = ADAM_B2 * v_ref[...] + (1.0 - ADAM_B2) * jnp.square(gg)
        m_hat = nm / (1.0 - ADAM_B1 ** ADAM_STEP)
        v_hat = nv / (1.0 - ADAM_B2 ** ADAM_STEP)
        d_ref[...] = -ADAM_LR * (m_hat / (jnp.sqrt(v_hat) + ADAM_EPS) + ADAM_WD * w_ref[...])
        nm_ref[...] = nm
        nv_ref[...] = nv

    spec = pl.BlockSpec((tr, c), lambda i: (i, 0))
    outs = pl.pallas_call(body, grid=(r // tr,), in_specs=[spec] * 4, out_specs=[spec] * 3,
                          out_shape=[jax.ShapeDtypeStruct((r, c), F32)] * 3, name=name, compiler_params=_cparams())(
        to2d(w), to2d(g), to2d(m), to2d(v))
    return tuple(o.reshape(shape) for o in outs)


def _n_pieces(shape, dtype, limit):
    if limit <= 1 or len(shape) < 2:
        return 1
    rows_per_tile = 1 if len(shape) >= 3 else SUBLANE * (4 // jnp.dtype(dtype).itemsize)
    for k in range(min(limit, shape[0]), 0, -1):
        if shape[0] % k == 0 and (shape[0] // k) % rows_per_tile == 0:
            return k
    return 1


def exchange(name, srcs, group, same, pieces=1):
    p = 2 ** len(group)
    n = len(srcs)
    out_shape = [jax.ShapeDtypeStruct(((p,) + s.shape) if same else s.shape, s.dtype) for s in srcs]
    n_pc = [_n_pieces(s.shape if same else s.shape[1:], s.dtype, pieces) for s in srcs]
    max_pc = max(n_pc)

    def body(*refs):
        src_refs, out_refs = refs[:n], refs[n:2 * n]
        send_sems, recv_sems, local_sems = refs[2 * n:]
        pos = {a: lax.axis_index(a) for a in MESH_AXES}

        def index_of(coords):
            idx = 0
            for a in group:
                idx = idx * 2 + coords[a]
            return idx

        me = index_of(pos)
        peers = []
        for rel in range(1, p):
            coords = dict(pos)
            for bit, a in enumerate(reversed(group)):
                if (rel >> bit) & 1:
                    coords[a] = 1 - coords[a]
            peers.append((coords, index_of(coords)))

        def src_for(a, idx):
            return src_refs[a] if same else src_refs[a].at[idx]

        def piece(ref, a, k):
            if n_pc[a] == 1:
                return ref
            rows = ref.shape[0] // n_pc[a]
            return ref.at[pl.ds(k * rows, rows)]

        local, sends, recvs = [], [], []
        for a in range(n):
            for k in range(n_pc[a]):
                cp = pltpu.make_async_copy(piece(src_for(a, me), a, k), piece(out_refs[a].at[me], a, k), local_sems.at[a, k])
                cp.start()
                local.append(cp)
                for r, (coords, idx) in enumerate(peers):
                    dev = tuple(coords[ax] for ax in MESH_AXES)
                    src = piece(src_for(a, idx), a, k)
                    send = pltpu.make_async_remote_copy(src_ref=src, dst_ref=piece(out_refs[a].at[me], a, k),
                                                        send_sem=send_sems.at[a, r, k], recv_sem=recv_sems.at[a, r, k],
                                                        device_id=dev, device_id_type=pl.DeviceIdType.MESH)
                    send.start()
                    sends.append(send)
                    recvs.append(pltpu.make_async_remote_copy(src_ref=src, dst_ref=piece(out_refs[a].at[idx], a, k),
                                                              send_sem=send_sems.at[a, r, k], recv_sem=recv_sems.at[a, r, k],
                                                              device_id=dev, device_id_type=pl.DeviceIdType.MESH))
        for cp in sends:
            cp.wait_send()
        for cp in recvs:
            cp.wait_recv()
        for cp in local:
            cp.wait()

    any_spec = pl.BlockSpec(memory_space=pl.ANY)
    outs = pl.pallas_call(body, in_specs=[any_spec] * n, out_specs=[any_spec] * n, out_shape=out_shape,
                          scratch_shapes=[pltpu.SemaphoreType.DMA((n, p - 1, max_pc)), pltpu.SemaphoreType.DMA((n, p - 1, max_pc)),
                                          pltpu.SemaphoreType.DMA((n, max_pc))],
                          name=name)(*srcs)
    return list(outs)


def _rope_tables(n_lat, n_ctx, rot_dim, heads):
    n_freq = rot_dim // 4
    tok = jnp.arange(n_lat, dtype=jnp.int32)
    inv = ROPE_THETA ** (-jnp.arange(n_freq, dtype=F32) / n_freq)
    ang = jnp.concatenate([(tok // GRID_W).astype(F32)[:, None] * inv, (tok % GRID_W).astype(F32)[:, None] * inv], axis=-1)
    cos, sin = jnp.cos(ang), jnp.sin(ang)
    cos = jnp.concatenate([jnp.concatenate([cos, cos], -1), jnp.ones((n_ctx, rot_dim), F32)], 0)
    sin = jnp.concatenate([jnp.concatenate([-sin, sin], -1), jnp.zeros((n_ctx, rot_dim), F32)], 0)
    half = rot_dim // 2
    w = heads * rot_dim
    j = np.arange(w)
    src = (j // rot_dim) * rot_dim + (j % rot_dim + half) % rot_dim
    swap = np.zeros((w, w), np.float32)
    swap[src, j] = 1.0
    return jnp.tile(cos, (1, heads)), jnp.tile(sin, (1, heads)), jnp.asarray(swap)


def _to_heads(a, nh):
    b, t, _ = a.shape
    return a.reshape(b, t, nh, -1).transpose(0, 2, 1, 3)


def _to_heads_t(a, nh):
    b, t, _ = a.shape
    return a.reshape(b, t, nh, -1).transpose(0, 2, 3, 1)


def _from_heads(a):
    b, nh, t, w = a.shape
    return a.transpose(0, 2, 1, 3).reshape(b, t, nh * w)


def _pad_w_in(w_in):
    parts, off = [], 0
    for size, pad in zip(IN_SIZES, IN_PAD):
        parts.append(w_in[:, off:off + size])
        if pad > size:
            parts.append(jnp.zeros((w_in.shape[0], pad - size), w_in.dtype))
        off += size
    return jnp.concatenate(parts, axis=1)


def _forward(xu, mods, prm, prm_lo, n_lat, n_ctx):
    bsz, t, d = xu.shape
    depth = mods.shape[0]
    n_lat_tiles = n_lat // TOKEN_TILE
    rope_a = _rope_tables(n_lat, n_ctx, HEAD_DIM, 1)
    rope_bq = _rope_tables(n_lat, n_ctx, B_ROPE, B_HEADS)
    rope_bk = _rope_tables(n_lat, n_ctx, B_ROPE, 1)
    clb = prm["c_lower_bounds"].reshape(depth, 2, C_HEADS, C_DK).transpose(2, 0, 1, 3)
    for l in range(depth):
        nm = lambda s: f"l{l}_{s}"
        vec = lambda name: prm[name][l][None, :]
        h = premod(nm("premix"), xu, vec("g_pre_mix"), mods[l], 0, n_lat_tiles)
        z = linear(nm("w_in"), h.reshape(bsz * t, d), _pad_w_in(prm["w_in"][l]), _pad_w_in(prm_lo["w_in"][l])).reshape(bsz, t, D_IN_PAD)
        seg = lambda i: z[:, :, IN_OFF[i]:IN_OFF[i] + IN_SIZES[i]]
        aq = a_prep(nm("aq"), _to_heads(seg(0), A_HEADS), vec("a_q_norm"), *rope_a)
        ak = a_prep(nm("ak"), _to_heads(seg(1), A_KV_HEADS), vec("a_k_norm"), *rope_a)
        av = _to_heads(seg(2), A_KV_HEADS)
        ya = attention(nm("att_a"), [aq.reshape(bsz, A_KV_HEADS, A_GROUP, t, HEAD_DIM)], [ak.transpose(0, 1, 3, 2)],
                       av.transpose(0, 1, 3, 2), HEAD_DIM ** -0.5, True, ATT_A_TQ, n_lat)
        ya = _from_heads(ya.reshape(bsz, A_HEADS, t, HEAD_DIM))
        wq = prm["w_q_up"][l].reshape(B_Q_RANK, B_HEADS, B_NOPE + B_ROPE)
        wkv = prm["w_kv_up"][l].reshape(B_KV_RANK, B_HEADS, B_NOPE + B_V)
        qn, qp, kn, bv, kp = b_prep(
            nm("b_prep"), seg(3), seg(4), seg(5), vec("b_q_norm"),
            wq[:, :, :B_NOPE].reshape(B_Q_RANK, -1), wq[:, :, B_NOPE:].reshape(B_Q_RANK, -1), vec("b_kv_norm"),
            wkv[:, :, :B_NOPE].reshape(B_KV_RANK, -1), wkv[:, :, B_NOPE:].reshape(B_KV_RANK, -1), *rope_bq, *rope_bk)
        qb = jnp.concatenate([_to_heads(qn, B_HEADS), _to_heads(qp, B_HEADS)], axis=-1)[:, :, None]
        kbt = jnp.concatenate([_to_heads_t(kn, B_HEADS),
                               jnp.broadcast_to(kp.transpose(0, 2, 1)[:, None], (bsz, B_HEADS, B_ROPE, t))], axis=2)
        yb = attention(nm("att_b"), [qb], [kbt], _to_heads_t(bv, B_HEADS), (B_NOPE + B_ROPE) ** -0.5, False, ATT_B_TQ, n_lat)
        yb = _from_heads(yb[:, :, 0])
        cq, kf, gf, kb, gb = c_prep(nm("c_prep"), _to_heads(seg(6), C_HEADS), _to_heads(seg(7), C_HEADS),
                                    _to_heads(seg(8), C_HEADS), clb, l)
        cv = _to_heads(seg(9), C_HEADS)
        o_f = gla_scan(nm("scan_f"), cq, kf, cv, gf, False, n_lat // SCAN_CHUNK)
        o_b = gla_scan(nm("scan_b"), cq, kb, cv, gb, True, n_lat // SCAN_CHUNK)
        yc = _from_heads(c_readout(nm("c_out"), o_f, o_b, _to_heads(seg(10), C_HEADS), vec("c_out_norm")))
        y = linear(nm("w_out"), jnp.concatenate([ya, yb, yc], axis=-1).reshape(bsz * t, d), prm["w_out"][l],
                   prm_lo["w_out"][l]).reshape(bsz, t, d)
        x1 = resid(nm("res_mix"), xu, y, vec("g_post_mix"), mods[l], 2, n_lat_tiles)
        h2 = premod(nm("preffn"), x1, vec("g_pre_ffn"), mods[l], 3, n_lat_tiles)
        f = mlp(nm("mlp"), h2.reshape(bsz * t, d), prm["w_ff1"][l], prm["w_ff2"][l], prm_lo["w_ff1"][l],
                prm_lo["w_ff2"][l]).reshape(bsz, t, d)
        xu = resid(nm("res_ffn"), x1, f, vec("g_post_ffn"), mods[l], 5, n_lat_tiles)
    return xu


BIG = {"w_in": 2, "w_q_up": 2, "w_kv_up": 2, "w_out": 1, "w_ff1": 2, "w_ff2": 1}
SMALL = ("g_pre_mix", "g_post_mix", "g_pre_ffn", "g_post_ffn", "a_q_norm", "a_k_norm", "b_q_norm", "b_kv_norm",
         "c_lower_bounds", "c_out_norm")
WEIGHTS = ("c_ctx", "w_ada", "b_ada", "g_pre_mix", "g_post_mix", "g_pre_ffn", "g_post_ffn", "w_in", "a_q_norm", "a_k_norm",
           "b_q_norm", "w_q_up", "b_kv_norm", "w_kv_up", "c_lower_bounds", "c_out_norm", "w_out", "w_ff1", "w_ff2")


def _unshard(g, axis):
    depth, _, r, c = g.shape
    if axis == 1:
        return g.reshape(depth, N_CHIP * r, c)
    return g.transpose(0, 2, 1, 3).reshape(depth, r, N_CHIP * c)


def _shard_major(g, axis):
    depth, r, c = g.shape
    if axis == 1:
        return g.reshape(depth, N_CHIP, r // N_CHIP, c)
    return g.reshape(depth, r, N_CHIP, c // N_CHIP).transpose(0, 2, 1, 3)


def kernel(x, c, ctx, c_ctx, w_ada, b_ada, g_pre_mix, g_post_mix, g_pre_ffn, g_post_ffn, w_in, a_q_norm, a_k_norm, b_q_norm, w_q_up, b_kv_norm, w_kv_up, c_lower_bounds, c_out_norm, w_out, w_ff1, w_ff2, loss_target, m_c_ctx, m_w_ada, m_b_ada, m_g_pre_mix, m_g_post_mix, m_g_pre_ffn, m_g_post_ffn, m_w_in, m_a_q_norm, m_a_k_norm, m_b_q_norm, m_w_q_up, m_b_kv_norm, m_w_kv_up, m_c_lower_bounds, m_c_out_norm, m_w_out, m_w_ff1, m_w_ff2, v_c_ctx, v_w_ada, v_b_ada, v_g_pre_mix, v_g_post_mix, v_g_pre_ffn, v_g_post_ffn, v_w_in, v_a_q_norm, v_a_k_norm, v_b_q_norm, v_w_q_up, v_b_kv_norm, v_w_kv_up, v_c_lower_bounds, v_c_out_norm, v_w_out, v_w_ff1, v_w_ff2):
    local = dict(c_ctx=c_ctx, w_ada=w_ada, b_ada=b_ada, g_pre_mix=g_pre_mix, g_post_mix=g_post_mix, g_pre_ffn=g_pre_ffn,
                 g_post_ffn=g_post_ffn, w_in=w_in, a_q_norm=a_q_norm, a_k_norm=a_k_norm, b_q_norm=b_q_norm, w_q_up=w_q_up,
                 b_kv_norm=b_kv_norm, w_kv_up=w_kv_up, c_lower_bounds=c_lower_bounds, c_out_norm=c_out_norm, w_out=w_out,
                 w_ff1=w_ff1, w_ff2=w_ff2)
    mom = dict(c_ctx=m_c_ctx, w_ada=m_w_ada, b_ada=m_b_ada, g_pre_mix=m_g_pre_mix, g_post_mix=m_g_post_mix,
               g_pre_ffn=m_g_pre_ffn, g_post_ffn=m_g_post_ffn, w_in=m_w_in, a_q_norm=m_a_q_norm, a_k_norm=m_a_k_norm,
               b_q_norm=m_b_q_norm, w_q_up=m_w_q_up, b_kv_norm=m_b_kv_norm, w_kv_up=m_w_kv_up,
               c_lower_bounds=m_c_lower_bounds, c_out_norm=m_c_out_norm, w_out=m_w_out, w_ff1=m_w_ff1, w_ff2=m_w_ff2)
    var = dict(c_ctx=v_c_ctx, w_ada=v_w_ada, b_ada=v_b_ada, g_pre_mix=v_g_pre_mix, g_post_mix=v_g_post_mix,
               g_pre_ffn=v_g_pre_ffn, g_post_ffn=v_g_post_ffn, w_in=v_w_in, a_q_norm=v_a_q_norm, a_k_norm=v_a_k_norm,
               b_q_norm=v_b_q_norm, w_q_up=v_w_q_up, b_kv_norm=v_b_kv_norm, w_kv_up=v_w_kv_up,
               c_lower_bounds=v_c_lower_bounds, c_out_norm=v_c_out_norm, w_out=v_w_out, w_ff1=v_w_ff1, w_ff2=v_w_ff2)

    bsz, n_lat, d = x.shape
    n_ctx = ctx.shape[1]
    depth = w_ada.shape[0]
    assert depth == 2 and n_lat % TOKEN_TILE == 0 and n_ctx % TOKEN_TILE == 0 and bsz * N_DEV + 1 <= ADA_ROWS
    ax, ay, ac = (lax.axis_index(a) for a in MESH_AXES)
    chip = 2 * ax + ay
    dev = 2 * chip + ac

    c_all = exchange("gather_c", [c], MESH_AXES, True)[0].reshape(N_DEV * bsz, d)
    big_names = list(BIG)
    over_chips = exchange("gather_w", [local[n].astype(BF16) for n in big_names] + [c_lower_bounds], ("x", "y"), True)
    prm_lo = {n: _unshard(g.transpose(1, 0, 2, 3), BIG[n]) for n, g in zip(big_names, over_chips[:-1])}
    prm = {n: w.astype(F32) for n, w in prm_lo.items()}
    prm["c_lower_bounds"] = over_chips[-1].transpose(1, 2, 0, 3).reshape(depth, 2, -1)
    for n in SMALL:
        if n != "c_lower_bounds":
            prm[n] = local[n]

    n_ada = w_ada.shape[2]
    cc = jnp.concatenate([c_all, c_ctx[None, :], jnp.zeros((ADA_ROWS - N_DEV * bsz - 1, d), F32)], axis=0)
    b_blk = lax.dynamic_slice_in_dim(b_ada, chip * n_ada, n_ada, axis=1)[:, None, :]
    mod_part, ada_vjp = jax.vjp(ada_op, cc, w_ada, b_blk)
    mod_full = exchange("gather_mod", [mod_part], ("x", "y"), True)[0].transpose(1, 2, 0, 3).reshape(depth, ADA_ROWS, 6 * d)
    mod_lat = lax.dynamic_slice_in_dim(mod_full, dev * bsz, bsz, axis=1)
    mod_ctx = jnp.broadcast_to(mod_full[:, N_DEV * bsz:N_DEV * bsz + 1], mod_lat.shape)
    mods = jnp.stack([mod_lat, mod_ctx], axis=2).reshape(depth, bsz, 2, 6, d)

    xu = jnp.concatenate([x, ctx], axis=1)
    x_out, fwd_vjp = jax.vjp(lambda xu_, mods_, prm_: _forward(xu_, mods_, prm_, prm_lo, n_lat, n_ctx), xu, mods, prm)
    dx_out, loss_blk = loss_op(x_out, loss_target, n_lat // TOKEN_TILE)
    d_xu, d_mods, d_prm = fwd_vjp(dx_out)
    grad_x = d_xu[:, :n_lat]

    d_mods = d_mods.reshape(depth, bsz, 2, 6 * d)
    pieces = [d_mods] + [d_prm[n] for n in SMALL] + [loss_blk[0:1, 0:1]]
    sizes = [int(np.prod(p.shape)) for p in pieces]
    flat = jnp.concatenate([p.reshape(-1) for p in pieces])
    n_flat = -(-flat.shape[0] // (SUBLANE * LANE)) * SUBLANE * LANE
    flat = jnp.concatenate([flat, jnp.zeros((n_flat - flat.shape[0],), F32)]).reshape(-1, LANE)
    small_all = exchange("gather_small", [flat], MESH_AXES, True)[0]
    small_sum = sum_parts("sum_small", small_all).reshape(-1)
    offs = np.cumsum([0] + sizes)
    summed = {n: small_sum[offs[i + 1]:offs[i + 2]].reshape(d_prm[n].shape) for i, n in enumerate(SMALL)}
    loss = small_sum[offs[-2]]
    dm_all = small_all.reshape(N_DEV, -1)[:, :sizes[0]].reshape(N_DEV, depth, bsz, 2, 6 * d)
    dm_rows = dm_all.transpose(3, 0, 2, 1, 4).reshape(2, N_DEV * bsz, depth, 6 * d)
    d_ctx_row = sum_parts("sum_dmod_ctx", dm_rows[1])
    grad_b_ada = sum_parts("sum_b_ada", dm_rows.reshape(2 * N_DEV * bsz, depth, 6 * d))
    d_rows = jnp.concatenate([dm_rows[0].transpose(1, 0, 2), d_ctx_row[:, None, :],
                              jnp.zeros((depth, ADA_ROWS - N_DEV * bsz - 1, 6 * d), F32)], axis=1)
    d_cc, grad_w_ada, _ = ada_vjp(lax.dynamic_slice_in_dim(d_rows, chip * n_ada, n_ada, axis=2))
    d_cctx_all = exchange("gather_dcctx", [d_cc[N_DEV * bsz:N_DEV * bsz + 1]], MESH_AXES, True)[0]
    grad_c_ctx = sum_parts("sum_dcctx", d_cctx_all[0::2]).reshape(d)

    shard_major = [_shard_major(d_prm[n], BIG[n]).astype(BF16) for n in big_names]
    pair = exchange("rs_pair", shard_major, ("c",), False)
    chip_sum = [sum_parts(f"rs_sum1_{n}", p) for n, p in zip(big_names, pair)]
    quad = exchange("rs_quad", [s.astype(BF16) for s in chip_sum], ("x", "y"), False)
    total = [sum_parts(f"rs_sum2_{n}", q) for n, q in zip(big_names, quad)]
    both = exchange("rs_share", total, ("c",), True)

    grads = dict(summed)
    grads["c_lower_bounds"] = lax.dynamic_slice_in_dim(summed["c_lower_bounds"], chip * c_lower_bounds.shape[2],
                                                       c_lower_bounds.shape[2], axis=2)
    grads.update(c_ctx=grad_c_ctx, w_ada=grad_w_ada, b_ada=grad_b_ada)
    grads.update({n: g for n, g in zip(big_names, both)})

    deltas, new_m, new_v = {}, {}, {}
    for n in WEIGHTS:
        as2d = (lambda a: a[None, :]) if local[n].ndim == 1 else (lambda a: a)
        dl, nm_, nv_ = adamw("adamw_" + n, as2d(local[n]), as2d(grads[n]), as2d(mom[n]), as2d(var[n]))
        deltas[n], new_m[n], new_v[n] = (a.reshape(local[n].shape) for a in (dl, nm_, nv_))
    return (loss, grad_x, *[grads[n] for n in WEIGHTS], *[deltas[n] for n in WEIGHTS],
            *[new_m[n] for n in WEIGHTS], *[new_v[n] for n in WEIGHTS])
```

```python
import functools
from typing import Any, Callable, NamedTuple

import numpy as np
import jax
import jax.numpy as jnp
from jax import lax
from jax.experimental import pallas as pl
from jax.experimental.pallas import tpu as pltpu

F32 = jnp.float32
BF16 = jnp.bfloat16
HIGHEST = lax.Precision.HIGHEST

GRID_W = 64
HEAD_DIM = 64
A_HEADS, A_KV_HEADS = 8, 2
A_GROUP = A_HEADS // A_KV_HEADS
B_HEADS, B_Q_RANK, B_KV_RANK, B_NOPE, B_ROPE, B_V = 4, 192, 128, 64, 32, 64
C_HEADS, C_DK, C_DV = 4, 64, 64
SCAN_CHUNK = 64
SCAN_SUB = 16
ROPE_THETA = 10000.0
EPS = 1e-6
F_TINY = 1e-30
ADAM_LR, ADAM_B1, ADAM_B2, ADAM_EPS, ADAM_WD, ADAM_STEP = 0.001, 0.9, 0.999, 1e-08, 0.01, 10

IN_SIZES = (512, 128, 128, 192, 128, 32, 256, 256, 256, 256, 256)
IN_PAD = (512, 128, 128, 256, 128, 128, 256, 256, 256, 256, 256)
IN_OFF = tuple(int(v) for v in np.cumsum((0,) + IN_PAD)[:-1])
D_IN_PAD = int(sum(IN_PAD))

LANE = 128
SUBLANE = 8
TOKEN_TILE = 256
ATT_A_TQ = 64
ATT_A_TQ_FWD = 128
ATT_B_TQ = 256
ATT_FWD_ROW_GROUPS = 4
MM_ROWS = 256
MM_TK_TOKENS = 512
MM_TN_OUT = 2560 * 1024
VMEM_LIMIT = 56 * 1024 * 1024
MESH_AXES = ("x", "y", "c")
N_DEV = 8
N_CHIP = 4
ADA_ROWS = 24


class Arg(NamedTuple):
    arr: Any
    block: tuple
    imap: Callable
    kind: str
    first: Callable = None


class Out(NamedTuple):
    shape: tuple
    block: tuple
    imap: Callable


def _cparams():
    return pltpu.CompilerParams(vmem_limit_bytes=VMEM_LIMIT)


def tile_op(name, fn, grid, args, outs, residual=None):
    n_in, n_out = len(args), len(outs)
    in_specs = [pl.BlockSpec(a.block, a.imap) for a in args]
    out_specs = [pl.BlockSpec(o.block, o.imap) for o in outs]
    out_shape = [jax.ShapeDtypeStruct(o.shape, F32) for o in outs]
    diff = [i for i, a in enumerate(args) if a.kind != "const"]

    def fwd_call(*arrays):
        def body(*refs):
            ids = tuple(pl.program_id(i) for i in range(len(grid)))
            res = list(fn(ids, *[r[...] for r in refs[:n_in]]))
            out_refs = refs[n_in:]
            if residual is not None:
                res[0] = res[0] + refs[n_in][...]
                out_refs = refs[n_in + 1:]
            for r, o in zip(out_refs, res):
                r[...] = o

        specs = in_specs + ([out_specs[0]] if residual is not None else [])
        return pl.pallas_call(body, grid=grid, in_specs=specs, out_specs=out_specs, out_shape=out_shape,
                              name=name + "_fwd", compiler_params=_cparams())(*arrays)

    def bwd_call(arrays, cts):
        def body(*refs):
            ids = tuple(pl.program_id(i) for i in range(len(grid)))
            vals = [r[...] for r in refs[:n_in]]
            ct = tuple(r[...] for r in refs[n_in:n_in + n_out])
            drefs = refs[n_in + n_out:]

            def g(*dv):
                full = list(vals)
                for i, v in zip(diff, dv):
                    full[i] = v
                return tuple(fn(ids, *full))

            _, vjp = jax.vjp(g, *[vals[i] for i in diff])
            ds = vjp(ct)
            for i, d, r in zip(diff, ds, drefs):
                if args[i].kind == "tile":
                    r[...] = d
                else:
                    is_first = args[i].first(ids)

                    @pl.when(is_first)
                    def _(r=r, d=d):
                        r[...] = d

                    @pl.when(jnp.logical_not(is_first))
                    def _(r=r, d=d):
                        r[...] += d

        d_specs = [in_specs[i] for i in diff]
        d_shape = [jax.ShapeDtypeStruct(arrays[i].shape, F32) for i in diff]
        return pl.pallas_call(body, grid=grid, in_specs=in_specs + out_specs, out_specs=d_specs, out_shape=d_shape,
                              name=name + "_bwd", compiler_params=_cparams())(*arrays, *cts)

    @jax.custom_vjp
    def op(*arrays):
        return tuple(fwd_call(*arrays))

    def op_fwd(*arrays):
        return tuple(fwd_call(*arrays)), arrays[:n_in]

    def op_bwd(arrays, cts):
        ds = bwd_call(arrays, cts)
        res, k = [], 0
        for i, a in enumerate(args):
            if a.kind == "const":
                res.append(jnp.zeros_like(arrays[i]))
            else:
                res.append(ds[k])
                k += 1
        if residual is not None:
            res.append(cts[0])
        return tuple(res)

    op.defvjp(op_fwd, op_bwd)
    return op(*[a.arr for a in args], *([residual] if residual is not None else []))


def _pick(n, cap):
    if n <= cap:
        return n
    best = None
    for t in range(LANE, cap + 1, LANE):
        if n % t == 0:
            best = t
    assert best is not None, (n, cap)
    return best


_NN = (((1,), (0,)), ((), ()))
_NT = (((1,), (1,)), ((), ()))
_TN = (((0,), (0,)), ((), ()))


def _resident(shape):
    return pl.BlockSpec(shape, lambda *ids: (0,) * len(shape), pipeline_mode=pl.Buffered(1))


def _mm_rows(name, a, w, transposed):
    m, k = a.shape
    n = w.shape[0] if transposed else w.shape[1]
    tm = MM_ROWS
    dims = _NT if transposed else _NN

    def body(a_ref, w_ref, o_ref):
        o_ref[...] = lax.dot_general(a_ref[...].astype(BF16), w_ref[...], dims, preferred_element_type=F32)

    return pl.pallas_call(body, grid=(m // tm,), in_specs=[pl.BlockSpec((tm, k), lambda i: (i, 0)), _resident(w.shape)],
                          out_specs=pl.BlockSpec((tm, n), lambda i: (i, 0)), out_shape=jax.ShapeDtypeStruct((m, n), F32),
                          name=name, compiler_params=_cparams())(a, w)


def _mm_tn(name, a, g):
    t, k = a.shape
    n = g.shape[1]
    tko, tno = k, n
    while tko * tno > MM_TN_OUT:
        if tko >= tno:
            tko //= 2
        else:
            tno //= 2
    assert k % tko == 0 and n % tno == 0 and tko % LANE == 0 and tno % LANE == 0
    tt = _pick(t, MM_TK_TOKENS)

    def body(a_ref, g_ref, o_ref):
        p = lax.dot_general(a_ref[...].astype(BF16), g_ref[...].astype(BF16), _TN, preferred_element_type=F32)
        kk = pl.program_id(2)

        @pl.when(kk == 0)
        def _():
            o_ref[...] = p

        @pl.when(kk != 0)
        def _():
            o_ref[...] += p

    return pl.pallas_call(body, grid=(k // tko, n // tno, t // tt),
                          in_specs=[pl.BlockSpec((tt, tko), lambda i, j, kk: (kk, i)), pl.BlockSpec((tt, tno), lambda i, j, kk: (kk, j))],
                          out_specs=pl.BlockSpec((tko, tno), lambda i, j, kk: (i, j)),
                          out_shape=jax.ShapeDtypeStruct((k, n), F32), name=name, compiler_params=_cparams())(a, g)


def linear(name, a, w, w_lo):
    @jax.custom_vjp
    def op(a, w, w_lo):
        return _mm_rows(name + "_fwd", a, w_lo, False)

    def op_fwd(a, w, w_lo):
        return _mm_rows(name + "_fwd", a, w_lo, False), (a, w_lo)

    def op_bwd(res, g):
        a, w_lo = res
        return _mm_rows(name + "_da", g, w_lo, True), _mm_tn(name + "_dw", a, g), jnp.zeros_like(w_lo)

    op.defvjp(op_fwd, op_bwd)
    return op(a, w, w_lo)


def mlp(name, h, w1, w2, w1_lo, w2_lo):
    m, d = h.shape
    f = w1_lo.shape[1]
    tm = MM_ROWS
    row = pl.BlockSpec((tm, d), lambda i: (i, 0))
    wide = pl.BlockSpec((tm, f), lambda i: (i, 0))

    def fwd_call(h, w1_lo, w2_lo):
        def body(h_ref, w1_ref, w2_ref, o_ref):
            u = lax.dot_general(h_ref[...].astype(BF16), w1_ref[...], _NN, preferred_element_type=F32)
            act = jnp.square(jnp.maximum(u, 0.0))
            o_ref[...] = lax.dot_general(act.astype(BF16), w2_ref[...], _NN, preferred_element_type=F32)

        return pl.pallas_call(body, grid=(m // tm,), in_specs=[row, _resident(w1_lo.shape), _resident(w2_lo.shape)],
                              out_specs=row, out_shape=jax.ShapeDtypeStruct((m, d), F32), name=name + "_fwd",
                              compiler_params=_cparams())(h, w1_lo, w2_lo)

    def bwd_call(h, w1_lo, w2_lo, dy):
        def body(h_ref, dy_ref, w1_ref, w2_ref, dh_ref, act_ref, du_ref):
            u = lax.dot_general(h_ref[...].astype(BF16), w1_ref[...], _NN, preferred_element_type=F32)
            r = jnp.maximum(u, 0.0)
            act_ref[...] = (r * r).astype(BF16)
            dact = lax.dot_general(dy_ref[...].astype(BF16), w2_ref[...], _NT, preferred_element_type=F32)
            du = (dact * (2.0 * r)).astype(BF16)
            du_ref[...] = du
            dh_ref[...] = lax.dot_general(du, w1_ref[...], _NT, preferred_element_type=F32)

        return pl.pallas_call(body, grid=(m // tm,), in_specs=[row, row, _resident(w1_lo.shape), _resident(w2_lo.shape)],
                              out_specs=[row, wide, wide],
                              out_shape=[jax.ShapeDtypeStruct((m, d), F32), jax.ShapeDtypeStruct((m, f), BF16),
                                         jax.ShapeDtypeStruct((m, f), BF16)],
                              name=name + "_bwd", compiler_params=_cparams())(h, dy, w1_lo, w2_lo)

    @jax.custom_vjp
    def op(h, w1, w2, w1_lo, w2_lo):
        return fwd_call(h, w1_lo, w2_lo)

    def op_fwd(h, w1, w2, w1_lo, w2_lo):
        return fwd_call(h, w1_lo, w2_lo), (h, w1_lo, w2_lo)

    def op_bwd(res, dy):
        h, w1_lo, w2_lo = res
        dh, act, du = bwd_call(h, w1_lo, w2_lo, dy)
        return (dh, _mm_tn(name + "_dw1", h, du), _mm_tn(name + "_dw2", act, dy), jnp.zeros_like(w1_lo), jnp.zeros_like(w2_lo))

    op.defvjp(op_fwd, op_bwd)
    return op(h, w1, w2, w1_lo, w2_lo)


def _rms(x, g):
    return x * lax.rsqrt(jnp.mean(x * x, axis=-1, keepdims=True) + EPS) * g


def _sigmoid(z):
    return 1.0 / (1.0 + jnp.exp(-z))


def _silu(z):
    return z * _sigmoid(z)


def _rope(y, cos, sin_signed, swap):
    return y * cos + jnp.dot(y, swap, precision=HIGHEST, preferred_element_type=F32) * sin_signed


def _dot_bf16(a, b, dims=((1,), (0,))):
    return lax.dot_general(a.astype(BF16), b.astype(BF16), (dims, ((), ())), preferred_element_type=F32)


def _gla_step(state, q, k, v, g, reverse):
    c, d = q.shape
    sub = SCAN_SUB
    nb = c // sub
    row = lax.broadcasted_iota(jnp.int32, (c, c), 0)
    col = lax.broadcasted_iota(jnp.int32, (c, c), 1)
    tri = (row <= col) if reverse else (row >= col)
    b = jnp.dot(tri.astype(F32), g, precision=HIGHEST, preferred_element_type=F32)
    o = jnp.dot(q * jnp.exp(b), state, preferred_element_type=F32)
    rs = lax.broadcasted_iota(jnp.int32, (sub, sub), 0)
    cs = lax.broadcasted_iota(jnp.int32, (sub, sub), 1)
    tri_s = ((rs <= cs) if reverse else (rs >= cs)).astype(F32)
    rowc = lax.broadcasted_iota(jnp.int32, (c, 1), 0)
    nonpos = lambda x: jnp.where(x > 0.0, 0.0, x)
    diag = []
    for j in range(nb):
        sl = slice(j * sub, (j + 1) * sub)
        bj, kj, vj, qj = b[sl], k[sl], v[sl], q[sl]
        dec = jnp.exp(nonpos(bj[:, None, :] - bj[None, :, :]))
        sc = jnp.sum(qj[:, None, :] * kj[None, :, :] * dec, axis=-1) * tri_s
        diag.append(jnp.dot(sc, vj, preferred_element_type=F32))
        if (j > 0) if reverse else (j < nb - 1):
            ref = bj[0:1] if reverse else bj[sub - 1:sub]
            qa = q * jnp.exp(nonpos(b - ref))
            ks = kj * jnp.exp(ref - bj)
            scj = lax.dot_general(qa, ks, (((1,), (1,)), ((), ())), precision=HIGHEST, preferred_element_type=F32)
            later = (rowc < j * sub) if reverse else (rowc >= (j + 1) * sub)
            o = o + jnp.dot(jnp.where(later, scj, 0.0), vj, preferred_element_type=F32)
    o = o + jnp.concatenate(diag, axis=0)
    b_end = b[0:1, :] if reverse else b[c - 1:c, :]
    kd = k * jnp.exp(b_end - b)
    new_state = state * jnp.exp(b_end).reshape(d, 1) + lax.dot_general(kd, v, (((0,), (0,)), ((), ())), preferred_element_type=F32)
    return new_state, o


def gla_scan(name, q, k, v, g, reverse, n_lat_chunks):
    bsz, nh, t, d = q.shape
    c = SCAN_CHUNK
    n = t // c

    def chunk_of(j):
        return (n - 1 - j) if reverse else lax.rem(j + n_lat_chunks, n)

    blk = (bsz, nh, c, d)
    st_blk = (bsz, nh, None, d, d)
    chains = [(b, h) for b in range(bsz) for h in range(nh)]

    def fwd_call(q, k, v, g):
        def body(q_ref, k_ref, v_ref, g_ref, o_ref, states_ref, st):
            @pl.when(pl.program_id(0) == 0)
            def _():
                st[...] = jnp.zeros_like(st)
            for bh in chains:
                s = st[bh]
                states_ref[bh] = s
                ns, o = _gla_step(s, q_ref[bh], k_ref[bh], v_ref[bh], g_ref[bh], reverse)
                st[bh] = ns
                o_ref[bh] = o

        spec = pl.BlockSpec(blk, lambda j: (0, 0, chunk_of(j), 0))
        return pl.pallas_call(
            body, grid=(n,), in_specs=[spec] * 4,
            out_specs=[spec, pl.BlockSpec(st_blk, lambda j: (0, 0, j, 0, 0))],
            out_shape=[jax.ShapeDtypeStruct(q.shape, F32), jax.ShapeDtypeStruct((bsz, nh, n, d, d), F32)],
            scratch_shapes=[pltpu.VMEM((bsz, nh, d, d), F32)], name=name + "_fwd", compiler_params=_cparams())(q, k, v, g)

    def bwd_call(q, k, v, g, states, do):
        def body(q_ref, k_ref, v_ref, g_ref, s_ref, do_ref, dq_ref, dk_ref, dv_ref, dg_ref, dst):
            @pl.when(pl.program_id(0) == 0)
            def _():
                dst[...] = jnp.zeros_like(dst)
            step = functools.partial(_gla_step, reverse=reverse)
            for bh in chains:
                _, vjp = jax.vjp(step, s_ref[bh], q_ref[bh], k_ref[bh], v_ref[bh], g_ref[bh])
                ds, dq, dk, dv, dg = vjp((dst[bh], do_ref[bh]))
                dst[bh] = ds
                dq_ref[bh] = dq
                dk_ref[bh] = dk
                dv_ref[bh] = dv
                dg_ref[bh] = dg

        spec = pl.BlockSpec(blk, lambda jj: (0, 0, chunk_of(n - 1 - jj), 0))
        s_spec = pl.BlockSpec(st_blk, lambda jj: (0, 0, n - 1 - jj, 0, 0))
        return pl.pallas_call(
            body, grid=(n,), in_specs=[spec] * 4 + [s_spec, spec], out_specs=[spec] * 4,
            out_shape=[jax.ShapeDtypeStruct(q.shape, F32)] * 4,
            scratch_shapes=[pltpu.VMEM((bsz, nh, d, d), F32)], name=name + "_bwd", compiler_params=_cparams())(q, k, v, g, states, do)

    @jax.custom_vjp
    def op(q, k, v, g):
        return fwd_call(q, k, v, g)[0]

    def op_fwd(q, k, v, g):
        o, states = fwd_call(q, k, v, g)
        return o, (q, k, v, g, states)

    def op_bwd(res, do):
        return tuple(bwd_call(*res, do))

    op.defvjp(op_fwd, op_bwd)
    return op(q, k, v, g)


def _zero_ids(ids):
    z = ids[0] == 0
    for i in ids[1:]:
        z = jnp.logical_and(z, i == 0)
    return z


def _token_grid(x, n_lat_tiles):
    bsz, t, d = x.shape
    nt = t // TOKEN_TILE
    row = lambda w: ((None, TOKEN_TILE, w), lambda b, i: (b, i, 0))
    mod_block = (None, None, 6, d)
    mod_imap = lambda b, i: (b, (i >= n_lat_tiles).astype(jnp.int32), 0, 0)
    mod_first = lambda ids: jnp.logical_or(ids[1] == 0, ids[1] == n_lat_tiles)
    return bsz, t, d, nt, row, (mod_block, mod_imap, mod_first)


def premod(name, x, gain, mods, r0, n_lat_tiles):
    bsz, t, d, nt, row, (mb, mi, mf) = _token_grid(x, n_lat_tiles)

    def fn(ids, xb, gb, mod):
        return (_rms(xb, gb) * (1.0 + mod[r0 + 1:r0 + 2]) + mod[r0:r0 + 1],)

    args = [Arg(x, *row(d), "tile"), Arg(gain, (1, d), lambda b, i: (0, 0), "acc", _zero_ids), Arg(mods, mb, mi, "acc", mf)]
    return tile_op(name, fn, (bsz, nt), args, [Out(x.shape, *row(d))])[0]


def resid(name, x, y, gain, mods, r, n_lat_tiles):
    bsz, t, d, nt, row, (mb, mi, mf) = _token_grid(x, n_lat_tiles)

    def fn(ids, yb, gb, mod):
        return (mod[r:r + 1] * _rms(yb, gb),)

    args = [Arg(y, *row(d), "tile"), Arg(gain, (1, d), lambda b, i: (0, 0), "acc", _zero_ids), Arg(mods, mb, mi, "acc", mf)]
    return tile_op(name, fn, (bsz, nt), args, [Out(x.shape, *row(d))], residual=x)[0]


def _heads_spec(nh, w):
    return (None, nh, TOKEN_TILE, w), lambda b, i: (b, 0, i, 0)


def a_prep(name, x, gain, cos, sin, swap):
    bsz, nh, t, d = x.shape
    tab = ((TOKEN_TILE, d), lambda b, i: (i, 0))

    def fn(ids, xb, gb, cb, sb, pb):
        y = _rms(xb, gb)
        swapped = jnp.dot(y.reshape(nh * TOKEN_TILE, d), pb, precision=HIGHEST, preferred_element_type=F32)
        return (y * cb + swapped.reshape(nh, TOKEN_TILE, d) * sb,)

    args = [Arg(x, *_heads_spec(nh, d), "tile"), Arg(gain, (1, d), lambda b, i: (0, 0), "acc", _zero_ids),
            Arg(cos, *tab, "const"), Arg(sin, *tab, "const"), Arg(swap, (d, d), lambda b, i: (0, 0), "const")]
    return tile_op(name, fn, (bsz, t // TOKEN_TILE), args, [Out(x.shape, *_heads_spec(nh, d))])[0]


def b_prep(name, bqd, bkvd, bkr, bqn, wq_nope, wq_pe, bkvn, wkv_nope, wkv_v, cos_q, sin_q, swap_q, cos_k, sin_k, swap_k):
    bsz, t, _ = bqd.shape
    row = lambda w: ((None, TOKEN_TILE, w), lambda b, i: (b, i, 0))
    whole = lambda a: (a.shape, lambda b, i: (0,) * a.ndim)
    tab = lambda w: ((TOKEN_TILE, w), lambda b, i: (i, 0))

    def fn(ids, qd, kvd, kr, qn, wqn, wqp, kvn, wkn, wkv, cq, sq, pq, ck, sk, pk):
        hq = _rms(qd, qn)
        hkv = _rms(kvd, kvn)
        return (_dot_bf16(hq, wqn), _rope(_dot_bf16(hq, wqp), cq, sq, pq), _dot_bf16(hkv, wkn), _dot_bf16(hkv, wkv),
                _rope(kr, ck, sk, pk))

    params = [bqn, wq_nope, wq_pe, bkvn, wkv_nope, wkv_v]
    args = [Arg(bqd, *row(B_Q_RANK), "tile"), Arg(bkvd, *row(B_KV_RANK), "tile"), Arg(bkr, *row(B_ROPE), "tile")]
    args += [Arg(p, *whole(p), "acc", _zero_ids) for p in params]
    args += [Arg(cos_q, *tab(cos_q.shape[1]), "const"), Arg(sin_q, *tab(cos_q.shape[1]), "const"), Arg(swap_q, *whole(swap_q), "const"),
             Arg(cos_k, *tab(B_ROPE), "const"), Arg(sin_k, *tab(B_ROPE), "const"), Arg(swap_k, *whole(swap_k), "const")]
    widths = (B_HEADS * B_NOPE, B_HEADS * B_ROPE, B_HEADS * B_NOPE, B_HEADS * B_V, B_ROPE)
    outs = [Out((bsz, t, w), *row(w)) for w in widths]
    return tile_op(name, fn, (bsz, t // TOKEN_TILE), args, outs)


def c_prep(name, cq, cff, cfb, clb, layer):
    bsz, nh, t, d = cq.shape
    depth = clb.shape[1]
    spec = _heads_spec(nh, d)

    def fn(ids, q, zf, zb, lbs):
        lb = [lbs[:, j] for j in range(depth)]
        m = lb[0]
        for j in range(1, depth):
            m = jnp.maximum(m, lb[j])
        e = [jnp.exp(lb[j] - m) for j in range(depth)]
        tot = e[0]
        for j in range(1, depth):
            tot = tot + e[j]
        p = [ej / tot for ej in e]
        cum = p[0]
        for j in range(1, layer + 1):
            cum = cum + p[j]
        lower = cum - p[0]

        def gate(z, lo):
            f = lo + (1.0 - lo) * _sigmoid(z)
            return jnp.log(jnp.maximum(f, F_TINY)), (1.0 - lo) * _sigmoid(-z)

        gf, kf = gate(zf, lower[:, 0:1])
        gb, kb = gate(zb, lower[:, 1:2])
        return _silu(q), kf, gf, kb, gb

    args = [Arg(cq, *spec, "tile"), Arg(cff, *spec, "tile"), Arg(cfb, *spec, "tile"),
            Arg(clb, clb.shape, lambda b, i: (0, 0, 0, 0), "acc", _zero_ids)]
    return tile_op(name, fn, (bsz, t // TOKEN_TILE), args, [Out(cq.shape, *spec)] * 5)


def c_readout(name, o_f, o_b, gate, gain):
    bsz, nh, t, d = o_f.shape
    spec = _heads_spec(nh, d)

    def fn(ids, of, ob, gt, gn):
        return (_rms(of + ob, gn) * _silu(gt),)

    args = [Arg(o_f, *spec, "tile"), Arg(o_b, *spec, "tile"), Arg(gate, *spec, "tile"),
            Arg(gain, (1, d), lambda b, i: (0, 0), "acc", _zero_ids)]
    return tile_op(name, fn, (bsz, t // TOKEN_TILE), args, [Out(o_f.shape, *spec)])[0]


def _key_bias(t, tq, i, n_lat):
    key = lax.broadcasted_iota(jnp.int32, (1, t), 1)
    return jnp.where(jnp.logical_and(i * tq >= n_lat, key < n_lat), -1e30, 0.0)


def attention(name, qs, kts, vt, scale, fold_scale, tq_fwd, tq_bwd, n_lat):
    bsz, hk, grp, t, _ = qs[0].shape
    dv = vt.shape[2]
    n_parts = len(qs)
    shared = [kt.shape[1] == 1 for kt in kts]
    kt_specs = [pl.BlockSpec((None, None, kt.shape[2], t), (lambda b, h, i: (b, 0, 0, 0)) if sh else (lambda b, h, i: (b, h, 0, 0)))
                for kt, sh in zip(kts, shared)]
    vt_spec = pl.BlockSpec((None, None, dv, t), lambda b, h, i: (b, h, 0, 0))
    o_shape = jax.ShapeDtypeStruct((bsz, hk, grp, t, dv), F32)
    lse_shape = jax.ShapeDtypeStruct((bsz, hk, grp, t, LANE), F32)

    def plan(tq_):
        per_head = lambda w: pl.BlockSpec((None, None, grp, tq_, w), lambda b, h, i: (b, h, 0, i, 0))
        return grp * tq_, [per_head(q.shape[-1]) for q in qs], per_head(dv), per_head(LANE), (bsz, hk, t // tq_)

    def load_q(q_refs, rows):
        qq = []
        for q_ref in q_refs:
            q2 = q_ref[...].reshape(rows, q_ref.shape[-1])
            qq.append((q2 * scale if fold_scale else q2).astype(BF16))
        return qq

    def scores(qq, kt_refs, bias):
        s = None
        for q2, kt_ref in zip(qq, kt_refs):
            part = lax.dot_general(q2, kt_ref[...].astype(BF16), _NN, preferred_element_type=F32)
            s = part if s is None else s + part
        if not fold_scale:
            s = s * scale
        return s + bias

    def fwd_call(*arrays):
        tq = tq_fwd
        rows, q_specs, o_spec, lse_spec, grid = plan(tq)

        def body(*refs):
            q_refs, kt_refs, vt_ref = refs[:n_parts], refs[n_parts:2 * n_parts], refs[2 * n_parts]
            o_ref, lse_ref = refs[2 * n_parts + 1:]
            qq = load_q(q_refs, rows)
            bias = _key_bias(t, tq, pl.program_id(2), n_lat)
            n_grp = ATT_FWD_ROW_GROUPS
            gr = rows // n_grp
            outs, lses = [], []
            for r in range(n_grp):
                s = scores([q2[r * gr:(r + 1) * gr] for q2 in qq], kt_refs, bias)
                m = jnp.max(s, axis=-1, keepdims=True)
                e = jnp.exp(s - m)
                l = jnp.sum(e, axis=-1, keepdims=True)
                outs.append(lax.dot_general(e.astype(BF16), vt_ref[...].astype(BF16), _NT, preferred_element_type=F32) * (1.0 / l))
                lses.append(jnp.broadcast_to(m + jnp.log(l), (gr, LANE)))
            o_ref[...] = jnp.concatenate(outs, axis=0).reshape(grp, tq, dv)
            lse_ref[...] = jnp.concatenate(lses, axis=0).reshape(grp, tq, LANE)

        return pl.pallas_call(body, grid=grid, in_specs=q_specs + kt_specs + [vt_spec], out_specs=[o_spec, lse_spec],
                              out_shape=[o_shape, lse_shape], name=name + "_fwd", compiler_params=_cparams())(*arrays)

    def bwd_call(arrays, o, lse, do):
        tq = tq_bwd
        rows, q_specs, o_spec, lse_spec, grid = plan(tq)

        def body(*refs):
            q_refs, kt_refs, vt_ref = refs[:n_parts], refs[n_parts:2 * n_parts], refs[2 * n_parts]
            o_ref, lse_ref, do_ref = refs[2 * n_parts + 1:2 * n_parts + 4]
            d_refs = refs[2 * n_parts + 4:]
            dq_refs, dkt_refs, dvt_ref = d_refs[:n_parts], d_refs[n_parts:2 * n_parts], d_refs[2 * n_parts]
            h, i = pl.program_id(1), pl.program_id(2)
            qq = load_q(q_refs, rows)
            s = scores(qq, kt_refs, _key_bias(t, tq, i, n_lat))
            p = jnp.exp(s - lse_ref[...].reshape(rows, LANE)[:, 0:1])
            do2 = do_ref[...].reshape(rows, dv)
            delta = jnp.sum(do2 * o_ref[...].reshape(rows, dv), axis=-1, keepdims=True)
            do_lo = do2.astype(BF16)
            dp = lax.dot_general(do_lo, vt_ref[...].astype(BF16), _NN, preferred_element_type=F32)
            ds = p * (dp - delta)
            if not fold_scale:
                ds = ds * scale
            ds_lo = ds.astype(BF16)

            def accumulate(ref, val, first):
                @pl.when(first)
                def _():
                    ref[...] = val

                @pl.when(jnp.logical_not(first))
                def _():
                    ref[...] += val

            accumulate(dvt_ref, lax.dot_general(do_lo, p.astype(BF16), _TN, preferred_element_type=F32), i == 0)
            for q2, kt_ref, dq_ref, dkt_ref, sh in zip(qq, kt_refs, dq_refs, dkt_refs, shared):
                dq = lax.dot_general(ds_lo, kt_ref[...].astype(BF16), _NT, preferred_element_type=F32)
                dq_ref[...] = (dq * scale if fold_scale else dq).reshape(dq_ref.shape)
                first = jnp.logical_and(h == 0, i == 0) if sh else i == 0
                accumulate(dkt_ref, lax.dot_general(q2, ds_lo, _TN, preferred_element_type=F32), first)

        in_specs = q_specs + kt_specs + [vt_spec, o_spec, lse_spec, o_spec]
        d_shape = [jax.ShapeDtypeStruct(a.shape, F32) for a in arrays]
        return pl.pallas_call(body, grid=grid, in_specs=in_specs, out_specs=q_specs + kt_specs + [vt_spec], out_shape=d_shape,
                              name=name + "_bwd", compiler_params=_cparams())(*arrays, o, lse, do)

    def lo(arrays):
        return arrays[:n_parts] + tuple(a.astype(BF16) for a in arrays[n_parts:])

    @jax.custom_vjp
    def op(*arrays):
        return fwd_call(*lo(arrays))[0]

    def op_fwd(*arrays):
        low = lo(arrays)
        o, lse = fwd_call(*low)
        return o, (low, o, lse)

    def op_bwd(res, do):
        low, o, lse = res
        return tuple(bwd_call(low, o, lse, do))

    op.defvjp(op_fwd, op_bwd)
    return op(*qs, *kts, vt)


def ada_op(cc, w, b):
    depth, d, n = w.shape

    def fn(ids, ccb, wb, bb):
        return (_dot_bf16(_silu(ccb), wb) + bb,)

    args = [Arg(cc, cc.shape, lambda l: (0, 0), "acc", lambda ids: ids[0] == 0),
            Arg(w, (None, d, n), lambda l: (l, 0, 0), "tile"), Arg(b, (None, 1, n), lambda l: (l, 0, 0), "tile")]
    return tile_op("ada", fn, (depth,), args, [Out((depth, cc.shape[0], n), (None, cc.shape[0], n), lambda l: (l, 0, 0))])[0]


def loss_op(xu, target, n_lat_tiles):
    bsz, t, d = xu.shape

    def body(x_ref, t_ref, dx_ref, l_ref):
        b, i = pl.program_id(0), pl.program_id(1)

        @pl.when(jnp.logical_and(b == 0, i == 0))
        def _():
            l_ref[...] = jnp.zeros_like(l_ref)

        @pl.when(i < n_lat_tiles)
        def _():
            err = x_ref[...] - t_ref[...]
            dx_ref[...] = err * (1.0 / d)
            l_ref[...] += 0.5 * jnp.sum(jnp.mean(err * err, axis=-1))

        @pl.when(i >= n_lat_tiles)
        def _():
            dx_ref[...] = jnp.zeros_like(dx_ref)

    row = pl.BlockSpec((None, TOKEN_TILE, d), lambda b, i: (b, i, 0))
    t_spec = pl.BlockSpec((None, TOKEN_TILE, d), lambda b, i: (b, jnp.minimum(i, n_lat_tiles - 1), 0))
    return pl.pallas_call(body, grid=(bsz, t // TOKEN_TILE), in_specs=[row, t_spec],
                          out_specs=[row, pl.BlockSpec((SUBLANE, LANE), lambda b, i: (0, 0))],
                          out_shape=[jax.ShapeDtypeStruct(xu.shape, F32), jax.ShapeDtypeStruct((SUBLANE, LANE), F32)],
                          name="loss", compiler_params=_cparams())(xu, target)


def _row_tile(rows, row_bytes, budget=4 << 20, step=SUBLANE):
    if rows * row_bytes <= budget:
        return rows
    best = None
    for t in range(step, rows, step):
        if rows % t == 0 and t * row_bytes <= budget:
            best = t
    return best if best is not None else rows


def _as3d(x):
    p = x.shape[0]
    c = x.shape[-1]
    return x.reshape(p, -1, c)


def sum_parts(name, x):
    x3 = _as3d(x)
    p, r, c = x3.shape
    tr = _row_tile(r, p * c * 4, step=SUBLANE * (4 // x.dtype.itemsize))

    def body(x_ref, o_ref):
        s = x_ref[0].astype(F32)
        for j in range(1, p):
            s = s + x_ref[j].astype(F32)
        o_ref[...] = s

    out = pl.pallas_call(body, grid=(r // tr,), in_specs=[pl.BlockSpec((p, tr, c), lambda i: (0, i, 0))],
                         out_specs=pl.BlockSpec((tr, c), lambda i: (i, 0)), out_shape=jax.ShapeDtypeStruct((r, c), F32),
                         name=name, compiler_params=_cparams())(x3)
    return out.reshape(x.shape[1:])


def adamw(name, w, g, m, v):
    shape = w.shape
    c = shape[-1]
    to2d = lambda a: a.reshape(-1, c)
    r = to2d(w).shape[0]
    tr = _row_tile(r, 7 * c * 4, budget=6 << 20)

    def body(w_ref, g_ref, m_ref, v_ref, d_ref, nm_ref, nv_ref):
        gg = g_ref[...]
        nm = ADAM_B1 * m_ref[...] + (1.0 - ADAM_B1) * gg
        nv = ADAM_B2 * v_ref[...] + (1.0 - ADAM_B2) * jnp.square(gg)
        m_hat = nm / (1.0 - ADAM_B1 ** ADAM_STEP)
        v_hat = nv / (1.0 - ADAM_B2 ** ADAM_STEP)
        d_ref[...] = -ADAM_LR * (m_hat / (jnp.sqrt(v_hat) + ADAM_EPS) + ADAM_WD * w_ref[...])
        nm_ref[...] = nm
        nv_ref[...] = nv

    spec = pl.BlockSpec((tr, c), lambda i: (i, 0))
    outs = pl.pallas_call(body, grid=(r // tr,), in_specs=[spec] * 4, out_specs=[spec] * 3,
                          out_shape=[jax.ShapeDtypeStruct((r, c), F32)] * 3, name=name, compiler_params=_cparams())(
        to2d(w), to2d(g), to2d(m), to2d(v))
    return tuple(o.reshape(shape) for o in outs)


def exchange(name, srcs, group, same):
    p = 2 ** len(group)
    n = len(srcs)
    out_shape = [jax.ShapeDtypeStruct(((p,) + s.shape) if same else s.shape, s.dtype) for s in srcs]

    def index_of(coords):
        idx = 0
        for a in group:
            idx = idx * 2 + coords[a]
        return idx

    def body(*refs):
        src_refs, out_refs = refs[:n], refs[n:2 * n]
        send_sems, recv_sems = refs[2 * n:]
        pos = {a: lax.axis_index(a) for a in MESH_AXES}
        me = index_of(pos)
        peers = []
        for rel in range(1, p):
            coords = dict(pos)
            for bit, a in enumerate(reversed(group)):
                if (rel >> bit) & 1:
                    coords[a] = 1 - coords[a]
            peers.append((coords, index_of(coords)))

        def src_for(a, idx):
            return src_refs[a] if same else src_refs[a].at[idx]

        sends, recvs = [], []
        for a in range(n):
            for r, (coords, idx) in enumerate(peers):
                dev = tuple(coords[ax] for ax in MESH_AXES)
                send = pltpu.make_async_remote_copy(src_ref=src_for(a, idx), dst_ref=out_refs[a].at[me],
                                                    send_sem=send_sems.at[a, r], recv_sem=recv_sems.at[a, r],
                                                    device_id=dev, device_id_type=pl.DeviceIdType.MESH)
                send.start()
                sends.append(send)
                recvs.append(pltpu.make_async_remote_copy(src_ref=src_for(a, idx), dst_ref=out_refs[a].at[idx],
                                                          send_sem=send_sems.at[a, r], recv_sem=recv_sems.at[a, r],
                                                          device_id=dev, device_id_type=pl.DeviceIdType.MESH))
        for cp in sends:
            cp.wait_send()
        for cp in recvs:
            cp.wait_recv()

    any_spec = pl.BlockSpec(memory_space=pl.ANY)
    outs = pl.pallas_call(body, in_specs=[any_spec] * n, out_specs=[any_spec] * n, out_shape=out_shape,
                          scratch_shapes=[pltpu.SemaphoreType.DMA((n, p - 1)), pltpu.SemaphoreType.DMA((n, p - 1))],
                          name=name)(*srcs)
    me = index_of({a: lax.axis_index(a) for a in MESH_AXES})
    own = [s if same else lax.dynamic_index_in_dim(s, me, axis=0, keepdims=False) for s in srcs]
    return [lax.dynamic_update_index_in_dim(o, w, me, axis=0) for o, w in zip(outs, own)]


def _rope_tables(n_lat, n_ctx, rot_dim, heads):
    n_freq = rot_dim // 4
    tok = jnp.arange(n_lat, dtype=jnp.int32)
    inv = ROPE_THETA ** (-jnp.arange(n_freq, dtype=F32) / n_freq)
    ang = jnp.concatenate([(tok // GRID_W).astype(F32)[:, None] * inv, (tok % GRID_W).astype(F32)[:, None] * inv], axis=-1)
    cos, sin = jnp.cos(ang), jnp.sin(ang)
    cos = jnp.concatenate([jnp.concatenate([cos, cos], -1), jnp.ones((n_ctx, rot_dim), F32)], 0)
    sin = jnp.concatenate([jnp.concatenate([-sin, sin], -1), jnp.zeros((n_ctx, rot_dim), F32)], 0)
    half = rot_dim // 2
    w = heads * rot_dim
    j = np.arange(w)
    src = (j // rot_dim) * rot_dim + (j % rot_dim + half) % rot_dim
    swap = np.zeros((w, w), np.float32)
    swap[src, j] = 1.0
    return jnp.tile(cos, (1, heads)), jnp.tile(sin, (1, heads)), jnp.asarray(swap)


def _to_heads(a, nh):
    b, t, _ = a.shape
    return a.reshape(b, t, nh, -1).transpose(0, 2, 1, 3)


def _to_heads_t(a, nh):
    b, t, _ = a.shape
    return a.reshape(b, t, nh, -1).transpose(0, 2, 3, 1)


def _from_heads(a):
    b, nh, t, w = a.shape
    return a.transpose(0, 2, 1, 3).reshape(b, t, nh * w)


def _pad_w_in(w_in):
    parts, off = [], 0
    for size, pad in zip(IN_SIZES, IN_PAD):
        parts.append(w_in[:, off:off + size])
        if pad > size:
            parts.append(jnp.zeros((w_in.shape[0], pad - size), w_in.dtype))
        off += size
    return jnp.concatenate(parts, axis=1)


def _forward(xu, mods, prm, prm_lo, n_lat, n_ctx):
    bsz, t, d = xu.shape
    depth = mods.shape[0]
    n_lat_tiles = n_lat // TOKEN_TILE
    rope_a = _rope_tables(n_lat, n_ctx, HEAD_DIM, 1)
    rope_bq = _rope_tables(n_lat, n_ctx, B_ROPE, B_HEADS)
    rope_bk = _rope_tables(n_lat, n_ctx, B_ROPE, 1)
    clb = prm["c_lower_bounds"].reshape(depth, 2, C_HEADS, C_DK).transpose(2, 0, 1, 3)
    for l in range(depth):
        nm = lambda s: f"l{l}_{s}"
        vec = lambda name: prm[name][l][None, :]
        h = premod(nm("premix"), xu, vec("g_pre_mix"), mods[l], 0, n_lat_tiles)
        z = linear(nm("w_in"), h.reshape(bsz * t, d), _pad_w_in(prm["w_in"][l]), _pad_w_in(prm_lo["w_in"][l])).reshape(bsz, t, D_IN_PAD)
        seg = lambda i: z[:, :, IN_OFF[i]:IN_OFF[i] + IN_SIZES[i]]
        aq = a_prep(nm("aq"), _to_heads(seg(0), A_HEADS), vec("a_q_norm"), *rope_a)
        ak = a_prep(nm("ak"), _to_heads(seg(1), A_KV_HEADS), vec("a_k_norm"), *rope_a)
        av = _to_heads(seg(2), A_KV_HEADS)
        ya = attention(nm("att_a"), [aq.reshape(bsz, A_KV_HEADS, A_GROUP, t, HEAD_DIM)], [ak.transpose(0, 1, 3, 2)],
                       av.transpose(0, 1, 3, 2), HEAD_DIM ** -0.5, True, ATT_A_TQ_FWD, ATT_A_TQ, n_lat)
        ya = _from_heads(ya.reshape(bsz, A_HEADS, t, HEAD_DIM))
        wq = prm["w_q_up"][l].reshape(B_Q_RANK, B_HEADS, B_NOPE + B_ROPE)
        wkv = prm["w_kv_up"][l].reshape(B_KV_RANK, B_HEADS, B_NOPE + B_V)
        qn, qp, kn, bv, kp = b_prep(
            nm("b_prep"), seg(3), seg(4), seg(5), vec("b_q_norm"),
            wq[:, :, :B_NOPE].reshape(B_Q_RANK, -1), wq[:, :, B_NOPE:].reshape(B_Q_RANK, -1), vec("b_kv_norm"),
            wkv[:, :, :B_NOPE].reshape(B_KV_RANK, -1), wkv[:, :, B_NOPE:].reshape(B_KV_RANK, -1), *rope_bq, *rope_bk)
        qb = jnp.concatenate([_to_heads(qn, B_HEADS), _to_heads(qp, B_HEADS)], axis=-1)[:, :, None]
        kbt = jnp.concatenate([_to_heads_t(kn, B_HEADS),
                               jnp.broadcast_to(kp.transpose(0, 2, 1)[:, None], (bsz, B_HEADS, B_ROPE, t))], axis=2)
        yb = attention(nm("att_b"), [qb], [kbt], _to_heads_t(bv, B_HEADS), (B_NOPE + B_ROPE) ** -0.5, False, ATT_B_TQ,
                       ATT_B_TQ, n_lat)
        yb = _from_heads(yb[:, :, 0])
        cq, kf, gf, kb, gb = c_prep(nm("c_prep"), _to_heads(seg(6), C_HEADS), _to_heads(seg(7), C_HEADS),
                                    _to_heads(seg(8), C_HEADS), clb, l)
        cv = _to_heads(seg(9), C_HEADS)
        o_f = gla_scan(nm("scan_f"), cq, kf, cv, gf, False, n_lat // SCAN_CHUNK)
        o_b = gla_scan(nm("scan_b"), cq, kb, cv, gb, True, n_lat // SCAN_CHUNK)
        yc = _from_heads(c_readout(nm("c_out"), o_f, o_b, _to_heads(seg(10), C_HEADS), vec("c_out_norm")))
        y = linear(nm("w_out"), jnp.concatenate([ya, yb, yc], axis=-1).reshape(bsz * t, d), prm["w_out"][l],
                   prm_lo["w_out"][l]).reshape(bsz, t, d)
        x1 = resid(nm("res_mix"), xu, y, vec("g_post_mix"), mods[l], 2, n_lat_tiles)
        h2 = premod(nm("preffn"), x1, vec("g_pre_ffn"), mods[l], 3, n_lat_tiles)
        f = mlp(nm("mlp"), h2.reshape(bsz * t, d), prm["w_ff1"][l], prm["w_ff2"][l], prm_lo["w_ff1"][l],
                prm_lo["w_ff2"][l]).reshape(bsz, t, d)
        xu = resid(nm("res_ffn"), x1, f, vec("g_post_ffn"), mods[l], 5, n_lat_tiles)
    return xu


BIG = {"w_in": 2, "w_q_up": 2, "w_kv_up": 2, "w_out": 1, "w_ff1": 2, "w_ff2": 1}
SMALL = ("g_pre_mix", "g_post_mix", "g_pre_ffn", "g_post_ffn", "a_q_norm", "a_k_norm", "b_q_norm", "b_kv_norm",
         "c_lower_bounds", "c_out_norm")
WEIGHTS = ("c_ctx", "w_ada", "b_ada", "g_pre_mix", "g_post_mix", "g_pre_ffn", "g_post_ffn", "w_in", "a_q_norm", "a_k_norm",
           "b_q_norm", "w_q_up", "b_kv_norm", "w_kv_up", "c_lower_bounds", "c_out_norm", "w_out", "w_ff1", "w_ff2")


def _unshard(g, axis):
    depth, _, r, c = g.shape
    if axis == 1:
        return g.reshape(depth, N_CHIP * r, c)
    return g.transpose(0, 2, 1, 3).reshape(depth, r, N_CHIP * c)


def _shard_major(g, axis):
    depth, r, c = g.shape
    if axis == 1:
        return g.reshape(depth, N_CHIP, r // N_CHIP, c)
    return g.reshape(depth, r, N_CHIP, c // N_CHIP).transpose(0, 2, 1, 3)


def kernel(x, c, ctx, c_ctx, w_ada, b_ada, g_pre_mix, g_post_mix, g_pre_ffn, g_post_ffn, w_in, a_q_norm, a_k_norm, b_q_norm, w_q_up, b_kv_norm, w_kv_up, c_lower_bounds, c_out_norm, w_out, w_ff1, w_ff2, loss_target, m_c_ctx, m_w_ada, m_b_ada, m_g_pre_mix, m_g_post_mix, m_g_pre_ffn, m_g_post_ffn, m_w_in, m_a_q_norm, m_a_k_norm, m_b_q_norm, m_w_q_up, m_b_kv_norm, m_w_kv_up, m_c_lower_bounds, m_c_out_norm, m_w_out, m_w_ff1, m_w_ff2, v_c_ctx, v_w_ada, v_b_ada, v_g_pre_mix, v_g_post_mix, v_g_pre_ffn, v_g_post_ffn, v_w_in, v_a_q_norm, v_a_k_norm, v_b_q_norm, v_w_q_up, v_b_kv_norm, v_w_kv_up, v_c_lower_bounds, v_c_out_norm, v_w_out, v_w_ff1, v_w_ff2):
    local = dict(c_ctx=c_ctx, w_ada=w_ada, b_ada=b_ada, g_pre_mix=g_pre_mix, g_post_mix=g_post_mix, g_pre_ffn=g_pre_ffn,
                 g_post_ffn=g_post_ffn, w_in=w_in, a_q_norm=a_q_norm, a_k_norm=a_k_norm, b_q_norm=b_q_norm, w_q_up=w_q_up,
                 b_kv_norm=b_kv_norm, w_kv_up=w_kv_up, c_lower_bounds=c_lower_bounds, c_out_norm=c_out_norm, w_out=w_out,
                 w_ff1=w_ff1, w_ff2=w_ff2)
    mom = dict(c_ctx=m_c_ctx, w_ada=m_w_ada, b_ada=m_b_ada, g_pre_mix=m_g_pre_mix, g_post_mix=m_g_post_mix,
               g_pre_ffn=m_g_pre_ffn, g_post_ffn=m_g_post_ffn, w_in=m_w_in, a_q_norm=m_a_q_norm, a_k_norm=m_a_k_norm,
               b_q_norm=m_b_q_norm, w_q_up=m_w_q_up, b_kv_norm=m_b_kv_norm, w_kv_up=m_w_kv_up,
               c_lower_bounds=m_c_lower_bounds, c_out_norm=m_c_out_norm, w_out=m_w_out, w_ff1=m_w_ff1, w_ff2=m_w_ff2)
    var = dict(c_ctx=v_c_ctx, w_ada=v_w_ada, b_ada=v_b_ada, g_pre_mix=v_g_pre_mix, g_post_mix=v_g_post_mix,
               g_pre_ffn=v_g_pre_ffn, g_post_ffn=v_g_post_ffn, w_in=v_w_in, a_q_norm=v_a_q_norm, a_k_norm=v_a_k_norm,
               b_q_norm=v_b_q_norm, w_q_up=v_w_q_up, b_kv_norm=v_b_kv_norm, w_kv_up=v_w_kv_up,
               c_lower_bounds=v_c_lower_bounds, c_out_norm=v_c_out_norm, w_out=v_w_out, w_ff1=v_w_ff1, w_ff2=v_w_ff2)

    bsz, n_lat, d = x.shape
    n_ctx = ctx.shape[1]
    depth = w_ada.shape[0]
    assert depth == 2 and n_lat % TOKEN_TILE == 0 and n_ctx % TOKEN_TILE == 0 and bsz * N_DEV + 1 <= ADA_ROWS
    ax, ay, ac = (lax.axis_index(a) for a in MESH_AXES)
    chip = 2 * ax + ay
    dev = 2 * chip + ac

    c_all = exchange("gather_c", [c], MESH_AXES, True)[0].reshape(N_DEV * bsz, d)
    big_names = list(BIG)
    over_chips = exchange("gather_w", [local[n].astype(BF16) for n in big_names] + [c_lower_bounds], ("x", "y"), True)
    prm_lo = {n: _unshard(g.transpose(1, 0, 2, 3), BIG[n]) for n, g in zip(big_names, over_chips[:-1])}
    prm = {n: w.astype(F32) for n, w in prm_lo.items()}
    prm["c_lower_bounds"] = over_chips[-1].transpose(1, 2, 0, 3).reshape(depth, 2, -1)
    for n in SMALL:
        if n != "c_lower_bounds":
            prm[n] = local[n]

    n_ada = w_ada.shape[2]
    cc = jnp.concatenate([c_all, c_ctx[None, :], jnp.zeros((ADA_ROWS - N_DEV * bsz - 1, d), F32)], axis=0)
    b_blk = lax.dynamic_slice_in_dim(b_ada, chip * n_ada, n_ada, axis=1)[:, None, :]
    mod_part, ada_vjp = jax.vjp(ada_op, cc, w_ada, b_blk)
    mod_full = exchange("gather_mod", [mod_part], ("x", "y"), True)[0].transpose(1, 2, 0, 3).reshape(depth, ADA_ROWS, 6 * d)
    mod_lat = lax.dynamic_slice_in_dim(mod_full, dev * bsz, bsz, axis=1)
    mod_ctx = jnp.broadcast_to(mod_full[:, N_DEV * bsz:N_DEV * bsz + 1], mod_lat.shape)
    mods = jnp.stack([mod_lat, mod_ctx], axis=2).reshape(depth, bsz, 2, 6, d)

    xu = jnp.concatenate([x, ctx], axis=1)
    x_out, fwd_vjp = jax.vjp(lambda xu_, mods_, prm_: _forward(xu_, mods_, prm_, prm_lo, n_lat, n_ctx), xu, mods, prm)
    dx_out, loss_blk = loss_op(x_out, loss_target, n_lat // TOKEN_TILE)
    d_xu, d_mods, d_prm = fwd_vjp(dx_out)
    grad_x = d_xu[:, :n_lat]

    d_mods = d_mods.reshape(depth, bsz, 2, 6 * d)
    pieces = [d_mods] + [d_prm[n] for n in SMALL] + [loss_blk[0:1, 0:1]]
    sizes = [int(np.prod(p.shape)) for p in pieces]
    flat = jnp.concatenate([p.reshape(-1) for p in pieces])
    n_flat = -(-flat.shape[0] // (SUBLANE * LANE)) * SUBLANE * LANE
    flat = jnp.concatenate([flat, jnp.zeros((n_flat - flat.shape[0],), F32)]).reshape(-1, LANE)
    small_all = exchange("gather_small", [flat], MESH_AXES, True)[0]
    small_sum = sum_parts("sum_small", small_all).reshape(-1)
    offs = np.cumsum([0] + sizes)
    summed = {n: small_sum[offs[i + 1]:offs[i + 2]].reshape(d_prm[n].shape) for i, n in enumerate(SMALL)}
    loss = small_sum[offs[-2]]
    dm_all = small_all.reshape(N_DEV, -1)[:, :sizes[0]].reshape(N_DEV, depth, bsz, 2, 6 * d)
    dm_rows = dm_all.transpose(3, 0, 2, 1, 4).reshape(2, N_DEV * bsz, depth, 6 * d)
    d_ctx_row = sum_parts("sum_dmod_ctx", dm_rows[1])
    grad_b_ada = sum_parts("sum_b_ada", dm_rows.reshape(2 * N_DEV * bsz, depth, 6 * d))
    d_rows = jnp.concatenate([dm_rows[0].transpose(1, 0, 2), d_ctx_row[:, None, :],
                              jnp.zeros((depth, ADA_ROWS - N_DEV * bsz - 1, 6 * d), F32)], axis=1)
    d_cc, grad_w_ada, _ = ada_vjp(lax.dynamic_slice_in_dim(d_rows, chip * n_ada, n_ada, axis=2))
    d_cctx_all = exchange("gather_dcctx", [d_cc[N_DEV * bsz:N_DEV * bsz + 1]], MESH_AXES, True)[0]
    grad_c_ctx = sum_parts("sum_dcctx", d_cctx_all[0::2]).reshape(d)

    shard_major = [_shard_major(d_prm[n], BIG[n]).astype(BF16) for n in big_names]
    pair = exchange("rs_pair", shard_major, ("c",), False)
    chip_sum = [sum_parts(f"rs_sum1_{n}", p) for n, p in zip(big_names, pair)]
    quad = exchange("rs_quad", [s.astype(BF16) for s in chip_sum], ("x", "y"), False)
    total = [sum_parts(f"rs_sum2_{n}", q) for n, q in zip(big_names, quad)]
    both = exchange("rs_share", total, ("c",), True)

    grads = dict(summed)
    grads["c_lower_bounds"] = lax.dynamic_slice_in_dim(summed["c_lower_bounds"], chip * c_lower_bounds.shape[2],
                                                       c_lower_bounds.shape[2], axis=2)
    grads.update(c_ctx=grad_c_ctx, w_ada=grad_w_ada, b_ada=grad_b_ada)
    grads.update({n: g for n, g in zip(big_names, both)})

    deltas, new_m, new_v = {}, {}, {}
    for n in WEIGHTS:
        as2d = (lambda a: a[None, :]) if local[n].ndim == 1 else (lambda a: a)
        dl, nm_, nv_ = adamw("adamw_" + n, as2d(local[n]), as2d(grads[n]), as2d(mom[n]), as2d(var[n]))
        deltas[n], new_m[n], new_v[n] = (a.reshape(local[n].shape) for a in (dl, nm_, nv_))
    return (loss, grad_x, *[grads[n] for n in WEIGHTS], *[deltas[n] for n in WEIGHTS],
            *[new_m[n] for n in WEIGHTS], *[new_v[n] for n in WEIGHTS])
```

```python
import functools
from typing import Any, Callable, NamedTuple

import numpy as np
import jax
import jax.numpy as jnp
from jax import lax
from jax.experimental import pallas as pl
from jax.experimental.pallas import tpu as pltpu

F32 = jnp.float32
BF16 = jnp.bfloat16
HIGHEST = lax.Precision.HIGHEST

GRID_W = 64
HEAD_DIM = 64
A_HEADS, A_KV_HEADS = 8, 2
A_GROUP = A_HEADS // A_KV_HEADS
B_HEADS, B_Q_RANK, B_KV_RANK, B_NOPE, B_ROPE, B_V = 4, 192, 128, 64, 32, 64
C_HEADS, C_DK, C_DV = 4, 64, 64
SCAN_CHUNK = 64
SCAN_SUB = 16
ROPE_THETA = 10000.0
EPS = 1e-6
F_TINY = 1e-30
ADAM_LR, ADAM_B1, ADAM_B2, ADAM_EPS, ADAM_WD, ADAM_STEP = 0.001, 0.9, 0.999, 1e-08, 0.01, 10

IN_SIZES = (512, 128, 128, 192, 128, 32, 256, 256, 256, 256, 256)
IN_PAD = (512, 128, 128, 256, 128, 128, 256, 256, 256, 256, 256)
IN_OFF = tuple(int(v) for v in np.cumsum((0,) + IN_PAD)[:-1])
D_IN_PAD = int(sum(IN_PAD))

LANE = 128
SUBLANE = 8
TOKEN_TILE = 256
ATT_A_TQ = 64
ATT_A_TQ_FWD = 128
ATT_B_TQ = 256
ATT_FWD_ROW_GROUPS = 4
MM_ROWS = 256
MM_TK_TOKENS = 512
MM_TN_OUT = 2560 * 1024
VMEM_LIMIT = 56 * 1024 * 1024
MESH_AXES = ("x", "y", "c")
N_DEV = 8
N_CHIP = 4
ADA_ROWS = 24


class Arg(NamedTuple):
    arr: Any
    block: tuple
    imap: Callable
    kind: str
    first: Callable = None


class Out(NamedTuple):
    shape: tuple
    block: tuple
    imap: Callable


def _cparams():
    return pltpu.CompilerParams(vmem_limit_bytes=VMEM_LIMIT)


def tile_op(name, fn, grid, args, outs, residual=None):
    n_in, n_out = len(args), len(outs)
    in_specs = [pl.BlockSpec(a.block, a.imap) for a in args]
    out_specs = [pl.BlockSpec(o.block, o.imap) for o in outs]
    out_shape = [jax.ShapeDtypeStruct(o.shape, F32) for o in outs]
    diff = [i for i, a in enumerate(args) if a.kind != "const"]

    def fwd_call(*arrays):
        def body(*refs):
            ids = tuple(pl.program_id(i) for i in range(len(grid)))
            res = list(fn(ids, *[r[...] for r in refs[:n_in]]))
            out_refs = refs[n_in:]
            if residual is not None:
                res[0] = res[0] + refs[n_in][...]
                out_refs = refs[n_in + 1:]
            for r, o in zip(out_refs, res):
                r[...] = o

        specs = in_specs + ([out_specs[0]] if residual is not None else [])
        return pl.pallas_call(body, grid=grid, in_specs=specs, out_specs=out_specs, out_shape=out_shape,
                              name=name + "_fwd", compiler_params=_cparams())(*arrays)

    def bwd_call(arrays, cts):
        def body(*refs):
            ids = tuple(pl.program_id(i) for i in range(len(grid)))
            vals = [r[...] for r in refs[:n_in]]
            ct = tuple(r[...] for r in refs[n_in:n_in + n_out])
            drefs = refs[n_in + n_out:]

            def g(*dv):
                full = list(vals)
                for i, v in zip(diff, dv):
                    full[i] = v
                return tuple(fn(ids, *full))

            _, vjp = jax.vjp(g, *[vals[i] for i in diff])
            ds = vjp(ct)
            for i, d, r in zip(diff, ds, drefs):
                if args[i].kind == "tile":
                    r[...] = d
                else:
                    is_first = args[i].first(ids)

                    @pl.when(is_first)
                    def _(r=r, d=d):
                        r[...] = d

                    @pl.when(jnp.logical_not(is_first))
                    def _(r=r, d=d):
                        r[...] += d

        d_specs = [in_specs[i] for i in diff]
        d_shape = [jax.ShapeDtypeStruct(arrays[i].shape, F32) for i in diff]
        return pl.pallas_call(body, grid=grid, in_specs=in_specs + out_specs, out_specs=d_specs, out_shape=d_shape,
                              name=name + "_bwd", compiler_params=_cparams())(*arrays, *cts)

    @jax.custom_vjp
    def op(*arrays):
        return tuple(fwd_call(*arrays))

    def op_fwd(*arrays):
        return tuple(fwd_call(*arrays)), arrays[:n_in]

    def op_bwd(arrays, cts):
        ds = bwd_call(arrays, cts)
        res, k = [], 0
        for i, a in enumerate(args):
            if a.kind == "const":
                res.append(jnp.zeros_like(arrays[i]))
            else:
                res.append(ds[k])
                k += 1
        if residual is not None:
            res.append(cts[0])
        return tuple(res)

    op.defvjp(op_fwd, op_bwd)
    return op(*[a.arr for a in args], *([residual] if residual is not None else []))


def _pick(n, cap):
    if n <= cap:
        return n
    best = None
    for t in range(LANE, cap + 1, LANE):
        if n % t == 0:
            best = t
    assert best is not None, (n, cap)
    return best


_NN = (((1,), (0,)), ((), ()))
_NT = (((1,), (1,)), ((), ()))
_TN = (((0,), (0,)), ((), ()))


def _resident(shape):
    return pl.BlockSpec(shape, lambda *ids: (0,) * len(shape), pipeline_mode=pl.Buffered(1))


def _mm_rows(name, a, w, transposed):
    m, k = a.shape
    n = w.shape[0] if transposed else w.shape[1]
    tm = MM_ROWS
    dims = _NT if transposed else _NN

    def body(a_ref, w_ref, o_ref):
        o_ref[...] = lax.dot_general(a_ref[...].astype(BF16), w_ref[...], dims, preferred_element_type=F32)

    return pl.pallas_call(body, grid=(m // tm,), in_specs=[pl.BlockSpec((tm, k), lambda i: (i, 0)), _resident(w.shape)],
                          out_specs=pl.BlockSpec((tm, n), lambda i: (i, 0)), out_shape=jax.ShapeDtypeStruct((m, n), F32),
                          name=name, compiler_params=_cparams())(a, w)


def _mm_tn(name, a, g):
    t, k = a.shape
    n = g.shape[1]
    tko, tno = k, n
    while tko * tno > MM_TN_OUT:
        if tko >= tno:
            tko //= 2
        else:
            tno //= 2
    assert k % tko == 0 and n % tno == 0 and tko % LANE == 0 and tno % LANE == 0
    tt = _pick(t, MM_TK_TOKENS)

    def body(a_ref, g_ref, o_ref):
        p = lax.dot_general(a_ref[...].astype(BF16), g_ref[...].astype(BF16), _TN, preferred_element_type=F32)
        kk = pl.program_id(2)

        @pl.when(kk == 0)
        def _():
            o_ref[...] = p

        @pl.when(kk != 0)
        def _():
            o_ref[...] += p

    return pl.pallas_call(body, grid=(k // tko, n // tno, t // tt),
                          in_specs=[pl.BlockSpec((tt, tko), lambda i, j, kk: (kk, i)), pl.BlockSpec((tt, tno), lambda i, j, kk: (kk, j))],
                          out_specs=pl.BlockSpec((tko, tno), lambda i, j, kk: (i, j)),
                          out_shape=jax.ShapeDtypeStruct((k, n), F32), name=name, compiler_params=_cparams())(a, g)


def linear(name, a, w, w_lo):
    @jax.custom_vjp
    def op(a, w, w_lo):
        return _mm_rows(name + "_fwd", a, w_lo, False)

    def op_fwd(a, w, w_lo):
        return _mm_rows(name + "_fwd", a, w_lo, False), (a, w_lo)

    def op_bwd(res, g):
        a, w_lo = res
        return _mm_rows(name + "_da", g, w_lo, True), _mm_tn(name + "_dw", a, g), jnp.zeros_like(w_lo)

    op.defvjp(op_fwd, op_bwd)
    return op(a, w, w_lo)


def mlp(name, h, w1, w2, w1_lo, w2_lo):
    m, d = h.shape
    f = w1_lo.shape[1]
    tm = MM_ROWS
    row = pl.BlockSpec((tm, d), lambda i: (i, 0))
    wide = pl.BlockSpec((tm, f), lambda i: (i, 0))

    def fwd_call(h, w1_lo, w2_lo):
        def body(h_ref, w1_ref, w2_ref, o_ref):
            u = lax.dot_general(h_ref[...].astype(BF16), w1_ref[...], _NN, preferred_element_type=F32)
            act = jnp.square(jnp.maximum(u, 0.0))
            o_ref[...] = lax.dot_general(act.astype(BF16), w2_ref[...], _NN, preferred_element_type=F32)

        return pl.pallas_call(body, grid=(m // tm,), in_specs=[row, _resident(w1_lo.shape), _resident(w2_lo.shape)],
                              out_specs=row, out_shape=jax.ShapeDtypeStruct((m, d), F32), name=name + "_fwd",
                              compiler_params=_cparams())(h, w1_lo, w2_lo)

    def bwd_call(h, w1_lo, w2_lo, dy):
        def body(h_ref, dy_ref, w1_ref, w2_ref, dh_ref, act_ref, du_ref):
            u = lax.dot_general(h_ref[...].astype(BF16), w1_ref[...], _NN, preferred_element_type=F32)
            r = jnp.maximum(u, 0.0)
            act_ref[...] = (r * r).astype(BF16)
            dact = lax.dot_general(dy_ref[...].astype(BF16), w2_ref[...], _NT, preferred_element_type=F32)
            du = (dact * (2.0 * r)).astype(BF16)
            du_ref[...] = du
            dh_ref[...] = lax.dot_general(du, w1_ref[...], _NT, preferred_element_type=F32)

        return pl.pallas_call(body, grid=(m // tm,), in_specs=[row, row, _resident(w1_lo.shape), _resident(w2_lo.shape)],
                              out_specs=[row, wide, wide],
                              out_shape=[jax.ShapeDtypeStruct((m, d), F32), jax.ShapeDtypeStruct((m, f), BF16),
                                         jax.ShapeDtypeStruct((m, f), BF16)],
                              name=name + "_bwd", compiler_params=_cparams())(h, dy, w1_lo, w2_lo)

    @jax.custom_vjp
    def op(h, w1, w2, w1_lo, w2_lo):
        return fwd_call(h, w1_lo, w2_lo)

    def op_fwd(h, w1, w2, w1_lo, w2_lo):
        return fwd_call(h, w1_lo, w2_lo), (h, w1_lo, w2_lo)

    def op_bwd(res, dy):
        h, w1_lo, w2_lo = res
        dh, act, du = bwd_call(h, w1_lo, w2_lo, dy)
        return (dh, _mm_tn(name + "_dw1", h, du), _mm_tn(name + "_dw2", act, dy), jnp.zeros_like(w1_lo), jnp.zeros_like(w2_lo))

    op.defvjp(op_fwd, op_bwd)
    return op(h, w1, w2, w1_lo, w2_lo)


def _rms(x, g):
    return x * lax.rsqrt(jnp.mean(x * x, axis=-1, keepdims=True) + EPS) * g


def _sigmoid(z):
    return 1.0 / (1.0 + jnp.exp(-z))


def _silu(z):
    return z * _sigmoid(z)


def _rope(y, cos, sin_signed, swap):
    return y * cos + jnp.dot(y, swap, precision=HIGHEST, preferred_element_type=F32) * sin_signed


def _dot_bf16(a, b, dims=((1,), (0,))):
    return lax.dot_general(a.astype(BF16), b.astype(BF16), (dims, ((), ())), preferred_element_type=F32)


def _gla_step(state, q, k, v, g, reverse):
    c, d = q.shape
    sub = SCAN_SUB
    nb = c // sub
    row = lax.broadcasted_iota(jnp.int32, (c, c), 0)
    col = lax.broadcasted_iota(jnp.int32, (c, c), 1)
    tri = (row <= col) if reverse else (row >= col)
    b = jnp.dot(tri.astype(F32), g, precision=HIGHEST, preferred_element_type=F32)
    o = jnp.dot(q * jnp.exp(b), state, preferred_element_type=F32)
    rs = lax.broadcasted_iota(jnp.int32, (sub, sub), 0)
    cs = lax.broadcasted_iota(jnp.int32, (sub, sub), 1)
    tri_s = ((rs <= cs) if reverse else (rs >= cs)).astype(F32)
    rowc = lax.broadcasted_iota(jnp.int32, (c, 1), 0)
    nonpos = lambda x: jnp.where(x > 0.0, 0.0, x)
    diag = []
    for j in range(nb):
        sl = slice(j * sub, (j + 1) * sub)
        bj, kj, vj, qj = b[sl], k[sl], v[sl], q[sl]
        dec = jnp.exp(nonpos(bj[:, None, :] - bj[None, :, :]))
        sc = jnp.sum(qj[:, None, :] * kj[None, :, :] * dec, axis=-1) * tri_s
        diag.append(jnp.dot(sc, vj, preferred_element_type=F32))
        if (j > 0) if reverse else (j < nb - 1):
            ref = bj[0:1] if reverse else bj[sub - 1:sub]
            qa = q * jnp.exp(nonpos(b - ref))
            ks = kj * jnp.exp(ref - bj)
            scj = lax.dot_general(qa, ks, (((1,), (1,)), ((), ())), precision=HIGHEST, preferred_element_type=F32)
            later = (rowc < j * sub) if reverse else (rowc >= (j + 1) * sub)
            o = o + jnp.dot(jnp.where(later, scj, 0.0), vj, preferred_element_type=F32)
    o = o + jnp.concatenate(diag, axis=0)
    b_end = b[0:1, :] if reverse else b[c - 1:c, :]
    kd = k * jnp.exp(b_end - b)
    new_state = state * jnp.exp(b_end).reshape(d, 1) + lax.dot_general(kd, v, (((0,), (0,)), ((), ())), preferred_element_type=F32)
    return new_state, o


def gla_scan(name, q, k, v, g, reverse, n_lat_chunks):
    bsz, nh, t, d = q.shape
    c = SCAN_CHUNK
    n = t // c

    def chunk_of(j):
        return (n - 1 - j) if reverse else lax.rem(j + n_lat_chunks, n)

    blk = (bsz, nh, c, d)
    st_blk = (bsz, nh, None, d, d)
    chains = [(b, h) for b in range(bsz) for h in range(nh)]

    def fwd_call(q, k, v, g):
        def body(q_ref, k_ref, v_ref, g_ref, o_ref, states_ref, st):
            @pl.when(pl.program_id(0) == 0)
            def _():
                st[...] = jnp.zeros_like(st)
            for bh in chains:
                s = st[bh]
                states_ref[bh] = s
                ns, o = _gla_step(s, q_ref[bh], k_ref[bh], v_ref[bh], g_ref[bh], reverse)
                st[bh] = ns
                o_ref[bh] = o

        spec = pl.BlockSpec(blk, lambda j: (0, 0, chunk_of(j), 0))
        return pl.pallas_call(
            body, grid=(n,), in_specs=[spec] * 4,
            out_specs=[spec, pl.BlockSpec(st_blk, lambda j: (0, 0, j, 0, 0))],
            out_shape=[jax.ShapeDtypeStruct(q.shape, F32), jax.ShapeDtypeStruct((bsz, nh, n, d, d), F32)],
            scratch_shapes=[pltpu.VMEM((bsz, nh, d, d), F32)], name=name + "_fwd", compiler_params=_cparams())(q, k, v, g)

    def bwd_call(q, k, v, g, states, do):
        def body(q_ref, k_ref, v_ref, g_ref, s_ref, do_ref, dq_ref, dk_ref, dv_ref, dg_ref, dst):
            @pl.when(pl.program_id(0) == 0)
            def _():
                dst[...] = jnp.zeros_like(dst)
            step = functools.partial(_gla_step, reverse=reverse)
            for bh in chains:
                _, vjp = jax.vjp(step, s_ref[bh], q_ref[bh], k_ref[bh], v_ref[bh], g_ref[bh])
                ds, dq, dk, dv, dg = vjp((dst[bh], do_ref[bh]))
                dst[bh] = ds
                dq_ref[bh] = dq
                dk_ref[bh] = dk
                dv_ref[bh] = dv
                dg_ref[bh] = dg

        spec = pl.BlockSpec(blk, lambda jj: (0, 0, chunk_of(n - 1 - jj), 0))
        s_spec = pl.BlockSpec(st_blk, lambda jj: (0, 0, n - 1 - jj, 0, 0))
        return pl.pallas_call(
            body, grid=(n,), in_specs=[spec] * 4 + [s_spec, spec], out_specs=[spec] * 4,
            out_shape=[jax.ShapeDtypeStruct(q.shape, F32)] * 4,
            scratch_shapes=[pltpu.VMEM((bsz, nh, d, d), F32)], name=name + "_bwd", compiler_params=_cparams())(q, k, v, g, states, do)

    @jax.custom_vjp
    def op(q, k, v, g):
        return fwd_call(q, k, v, g)[0]

    def op_fwd(q, k, v, g):
        o, states = fwd_call(q, k, v, g)
        return o, (q, k, v, g, states)

    def op_bwd(res, do):
        return tuple(bwd_call(*res, do))

    op.defvjp(op_fwd, op_bwd)
    return op(q, k, v, g)


def _zero_ids(ids):
    z = ids[0] == 0
    for i in ids[1:]:
        z = jnp.logical_and(z, i == 0)
    return z


def _token_grid(x, n_lat_tiles):
    bsz, t, d = x.shape
    nt = t // TOKEN_TILE
    row = lambda w: ((None, TOKEN_TILE, w), lambda b, i: (b, i, 0))
    mod_block = (None, None, 6, d)
    mod_imap = lambda b, i: (b, (i >= n_lat_tiles).astype(jnp.int32), 0, 0)
    mod_first = lambda ids: jnp.logical_or(ids[1] == 0, ids[1] == n_lat_tiles)
    return bsz, t, d, nt, row, (mod_block, mod_imap, mod_first)


def premod(name, x, gain, mods, r0, n_lat_tiles):
    bsz, t, d, nt, row, (mb, mi, mf) = _token_grid(x, n_lat_tiles)

    def fn(ids, xb, gb, mod):
        return (_rms(xb, gb) * (1.0 + mod[r0 + 1:r0 + 2]) + mod[r0:r0 + 1],)

    args = [Arg(x, *row(d), "tile"), Arg(gain, (1, d), lambda b, i: (0, 0), "acc", _zero_ids), Arg(mods, mb, mi, "acc", mf)]
    return tile_op(name, fn, (bsz, nt), args, [Out(x.shape, *row(d))])[0]


def resid(name, x, y, gain, mods, r, n_lat_tiles):
    bsz, t, d, nt, row, (mb, mi, mf) = _token_grid(x, n_lat_tiles)

    def fn(ids, yb, gb, mod):
        return (mod[r:r + 1] * _rms(yb, gb),)

    args = [Arg(y, *row(d), "tile"), Arg(gain, (1, d), lambda b, i: (0, 0), "acc", _zero_ids), Arg(mods, mb, mi, "acc", mf)]
    return tile_op(name, fn, (bsz, nt), args, [Out(x.shape, *row(d))], residual=x)[0]


def _heads_spec(nh, w):
    return (None, nh, TOKEN_TILE, w), lambda b, i: (b, 0, i, 0)


def a_prep(name, x, gain, cos, sin, swap):
    bsz, nh, t, d = x.shape
    tab = ((TOKEN_TILE, d), lambda b, i: (i, 0))

    def fn(ids, xb, gb, cb, sb, pb):
        y = _rms(xb, gb)
        swapped = jnp.dot(y.reshape(nh * TOKEN_TILE, d), pb, precision=HIGHEST, preferred_element_type=F32)
        return (y * cb + swapped.reshape(nh, TOKEN_TILE, d) * sb,)

    args = [Arg(x, *_heads_spec(nh, d), "tile"), Arg(gain, (1, d), lambda b, i: (0, 0), "acc", _zero_ids),
            Arg(cos, *tab, "const"), Arg(sin, *tab, "const"), Arg(swap, (d, d), lambda b, i: (0, 0), "const")]
    return tile_op(name, fn, (bsz, t // TOKEN_TILE), args, [Out(x.shape, *_heads_spec(nh, d))])[0]


def b_prep(name, bqd, bkvd, bkr, bqn, wq_nope, wq_pe, bkvn, wkv_nope, wkv_v, cos_q, sin_q, swap_q, cos_k, sin_k, swap_k):
    bsz, t, _ = bqd.shape
    row = lambda w: ((None, TOKEN_TILE, w), lambda b, i: (b, i, 0))
    whole = lambda a: (a.shape, lambda b, i: (0,) * a.ndim)
    tab = lambda w: ((TOKEN_TILE, w), lambda b, i: (i, 0))

    def fn(ids, qd, kvd, kr, qn, wqn, wqp, kvn, wkn, wkv, cq, sq, pq, ck, sk, pk):
        hq = _rms(qd, qn)
        hkv = _rms(kvd, kvn)
        return (_dot_bf16(hq, wqn), _rope(_dot_bf16(hq, wqp), cq, sq, pq), _dot_bf16(hkv, wkn), _dot_bf16(hkv, wkv),
                _rope(kr, ck, sk, pk))

    params = [bqn, wq_nope, wq_pe, bkvn, wkv_nope, wkv_v]
    args = [Arg(bqd, *row(B_Q_RANK), "tile"), Arg(bkvd, *row(B_KV_RANK), "tile"), Arg(bkr, *row(B_ROPE), "tile")]
    args += [Arg(p, *whole(p), "acc", _zero_ids) for p in params]
    args += [Arg(cos_q, *tab(cos_q.shape[1]), "const"), Arg(sin_q, *tab(cos_q.shape[1]), "const"), Arg(swap_q, *whole(swap_q), "const"),
             Arg(cos_k, *tab(B_ROPE), "const"), Arg(sin_k, *tab(B_ROPE), "const"), Arg(swap_k, *whole(swap_k), "const")]
    widths = (B_HEADS * B_NOPE, B_HEADS * B_ROPE, B_HEADS * B_NOPE, B_HEADS * B_V, B_ROPE)
    outs = [Out((bsz, t, w), *row(w)) for w in widths]
    return tile_op(name, fn, (bsz, t // TOKEN_TILE), args, outs)


def c_prep(name, cq, cff, cfb, clb, layer):
    bsz, nh, t, d = cq.shape
    depth = clb.shape[1]
    spec = _heads_spec(nh, d)

    def fn(ids, q, zf, zb, lbs):
        lb = [lbs[:, j] for j in range(depth)]
        m = lb[0]
        for j in range(1, depth):
            m = jnp.maximum(m, lb[j])
        e = [jnp.exp(lb[j] - m) for j in range(depth)]
        tot = e[0]
        for j in range(1, depth):
            tot = tot + e[j]
        p = [ej / tot for ej in e]
        cum = p[0]
        for j in range(1, layer + 1):
            cum = cum + p[j]
        lower = cum - p[0]

        def gate(z, lo):
            f = lo + (1.0 - lo) * _sigmoid(z)
            return jnp.log(jnp.maximum(f, F_TINY)), (1.0 - lo) * _sigmoid(-z)

        gf, kf = gate(zf, lower[:, 0:1])
        gb, kb = gate(zb, lower[:, 1:2])
        return _silu(q), kf, gf, kb, gb

    args = [Arg(cq, *spec, "tile"), Arg(cff, *spec, "tile"), Arg(cfb, *spec, "tile"),
            Arg(clb, clb.shape, lambda b, i: (0, 0, 0, 0), "acc", _zero_ids)]
    return tile_op(name, fn, (bsz, t // TOKEN_TILE), args, [Out(cq.shape, *spec)] * 5)


def c_readout(name, o_f, o_b, gate, gain):
    bsz, nh, t, d = o_f.shape
    spec = _heads_spec(nh, d)

    def fn(ids, of, ob, gt, gn):
        return (_rms(of + ob, gn) * _silu(gt),)

    args = [Arg(o_f, *spec, "tile"), Arg(o_b, *spec, "tile"), Arg(gate, *spec, "tile"),
            Arg(gain, (1, d), lambda b, i: (0, 0), "acc", _zero_ids)]
    return tile_op(name, fn, (bsz, t // TOKEN_TILE), args, [Out(o_f.shape, *spec)])[0]


def attention(name, qs, kts, vt, scale, fold_scale, tq_fwd, tq_bwd, n_lat):
    bsz, hk, grp, t, _ = qs[0].shape
    dv = vt.shape[2]
    n_parts = len(qs)
    shared = [kt.shape[1] == 1 for kt in kts]
    kt_specs = [pl.BlockSpec((None, None, kt.shape[2], t), (lambda b, h, i: (b, 0, 0, 0)) if sh else (lambda b, h, i: (b, h, 0, 0)))
                for kt, sh in zip(kts, shared)]
    vt_spec = pl.BlockSpec((None, None, dv, t), lambda b, h, i: (b, h, 0, 0))
    o_shape = jax.ShapeDtypeStruct((bsz, hk, grp, t, dv), F32)
    lse_shape = jax.ShapeDtypeStruct((bsz, hk, grp, t, LANE), F32)

    def plan(tq_):
        per_head = lambda w: pl.BlockSpec((None, None, grp, tq_, w), lambda b, h, i: (b, h, 0, i, 0))
        return grp * tq_, [per_head(q.shape[-1]) for q in qs], per_head(dv), per_head(LANE), (bsz, hk, t // tq_)

    def load_q(q_refs, rows):
        qq = []
        for q_ref in q_refs:
            q2 = q_ref[...].reshape(rows, q_ref.shape[-1])
            qq.append((q2 * scale if fold_scale else q2).astype(BF16))
        return qq

    def scores(qq, kts_lo):
        s = None
        for q2, kt in zip(qq, kts_lo):
            part = lax.dot_general(q2, kt, _NN, preferred_element_type=F32)
            s = part if s is None else s + part
        return s if fold_scale else s * scale

    ranges = ((lambda i, tq: i * tq < n_lat, 0), (lambda i, tq: i * tq >= n_lat, n_lat))

    def fwd_call(*arrays):
        tq = tq_fwd
        rows, q_specs, o_spec, lse_spec, grid = plan(tq)

        def body(*refs):
            q_refs, kt_refs, vt_ref = refs[:n_parts], refs[n_parts:2 * n_parts], refs[2 * n_parts]
            o_ref, lse_ref = refs[2 * n_parts + 1:]
            i = pl.program_id(2)

            def tile(col0):
                qq = load_q(q_refs, rows)
                kts_lo = [r[:, col0:] for r in kt_refs]
                vt_lo = vt_ref[:, col0:]
                n_grp = ATT_FWD_ROW_GROUPS
                gr = rows // n_grp
                outs, lses = [], []
                for r in range(n_grp):
                    s = scores([q2[r * gr:(r + 1) * gr] for q2 in qq], kts_lo)
                    m = jnp.max(s, axis=-1, keepdims=True)
                    e = jnp.exp(s - m)
                    l = jnp.sum(e, axis=-1, keepdims=True)
                    outs.append(lax.dot_general(e.astype(BF16), vt_lo, _NT, preferred_element_type=F32) * (1.0 / l))
                    lses.append(jnp.broadcast_to(m + jnp.log(l), (gr, LANE)))
                o_ref[...] = jnp.concatenate(outs, axis=0).reshape(grp, tq, dv)
                lse_ref[...] = jnp.concatenate(lses, axis=0).reshape(grp, tq, LANE)

            for cond, col0 in ranges:
                pl.when(cond(i, tq))(functools.partial(tile, col0))

        return pl.pallas_call(body, grid=grid, in_specs=q_specs + kt_specs + [vt_spec], out_specs=[o_spec, lse_spec],
                              out_shape=[o_shape, lse_shape], name=name + "_fwd", compiler_params=_cparams())(*arrays)

    def bwd_call(arrays, o, lse, do):
        tq = tq_bwd
        rows, q_specs, o_spec, lse_spec, grid = plan(tq)

        def body(*refs):
            q_refs, kt_refs, vt_ref = refs[:n_parts], refs[n_parts:2 * n_parts], refs[2 * n_parts]
            o_ref, lse_ref, do_ref = refs[2 * n_parts + 1:2 * n_parts + 4]
            d_refs = refs[2 * n_parts + 4:]
            dq_refs, dkt_refs, dvt_ref = d_refs[:n_parts], d_refs[n_parts:2 * n_parts], d_refs[2 * n_parts]
            h, i = pl.program_id(1), pl.program_id(2)

            def tile(col0):
                qq = load_q(q_refs, rows)
                kts_lo = [r[:, col0:] for r in kt_refs]
                p = jnp.exp(scores(qq, kts_lo) - lse_ref[...].reshape(rows, LANE)[:, 0:1])
                do2 = do_ref[...].reshape(rows, dv)
                delta = jnp.sum(do2 * o_ref[...].reshape(rows, dv), axis=-1, keepdims=True)
                do_lo = do2.astype(BF16)
                dp = lax.dot_general(do_lo, vt_ref[:, col0:], _NN, preferred_element_type=F32)
                ds = p * (dp - delta)
                if not fold_scale:
                    ds = ds * scale
                ds_lo = ds.astype(BF16)

                def accumulate(ref, val, first):
                    if col0:
                        ref[:, col0:] += val
                        return

                    @pl.when(first)
                    def _():
                        ref[...] = val

                    @pl.when(jnp.logical_not(first))
                    def _():
                        ref[...] += val

                accumulate(dvt_ref, lax.dot_general(do_lo, p.astype(BF16), _TN, preferred_element_type=F32), i == 0)
                for q2, kt, dq_ref, dkt_ref, sh in zip(qq, kts_lo, dq_refs, dkt_refs, shared):
                    dq = lax.dot_general(ds_lo, kt, _NT, preferred_element_type=F32)
                    dq_ref[...] = (dq * scale if fold_scale else dq).reshape(dq_ref.shape)
                    first = jnp.logical_and(h == 0, i == 0) if sh else i == 0
                    accumulate(dkt_ref, lax.dot_general(q2, ds_lo, _TN, preferred_element_type=F32), first)

            for cond, col0 in ranges:
                pl.when(cond(i, tq))(functools.partial(tile, col0))

        in_specs = q_specs + kt_specs + [vt_spec, o_spec, lse_spec, o_spec]
        d_shape = [jax.ShapeDtypeStruct(a.shape, F32) for a in arrays]
        return pl.pallas_call(body, grid=grid, in_specs=in_specs, out_specs=q_specs + kt_specs + [vt_spec], out_shape=d_shape,
                              name=name + "_bwd", compiler_params=_cparams())(*arrays, o, lse, do)

    def lo(arrays):
        return arrays[:n_parts] + tuple(a.astype(BF16) for a in arrays[n_parts:])

    @jax.custom_vjp
    def op(*arrays):
        return fwd_call(*lo(arrays))[0]

    def op_fwd(*arrays):
        low = lo(arrays)
        o, lse = fwd_call(*low)
        return o, (low, o, lse)

    def op_bwd(res, do):
        low, o, lse = res
        return tuple(bwd_call(low, o, lse, do))

    op.defvjp(op_fwd, op_bwd)
    return op(*qs, *kts, vt)


def ada_op(cc, w, b):
    depth, d, n = w.shape

    def fn(ids, ccb, wb, bb):
        return (_dot_bf16(_silu(ccb), wb) + bb,)

    args = [Arg(cc, cc.shape, lambda l: (0, 0), "acc", lambda ids: ids[0] == 0),
            Arg(w, (None, d, n), lambda l: (l, 0, 0), "tile"), Arg(b, (None, 1, n), lambda l: (l, 0, 0), "tile")]
    return tile_op("ada", fn, (depth,), args, [Out((depth, cc.shape[0], n), (None, cc.shape[0], n), lambda l: (l, 0, 0))])[0]


def loss_op(xu, target, n_lat_tiles):
    bsz, t, d = xu.shape

    def body(x_ref, t_ref, dx_ref, l_ref):
        b, i = pl.program_id(0), pl.program_id(1)

        @pl.when(jnp.logical_and(b == 0, i == 0))
        def _():
            l_ref[...] = jnp.zeros_like(l_ref)

        @pl.when(i < n_lat_tiles)
        def _():
            err = x_ref[...] - t_ref[...]
            dx_ref[...] = err * (1.0 / d)
            l_ref[...] += 0.5 * jnp.sum(jnp.mean(err * err, axis=-1))

        @pl.when(i >= n_lat_tiles)
        def _():
            dx_ref[...] = jnp.zeros_like(dx_ref)

    row = pl.BlockSpec((None, TOKEN_TILE, d), lambda b, i: (b, i, 0))
    t_spec = pl.BlockSpec((None, TOKEN_TILE, d), lambda b, i: (b, jnp.minimum(i, n_lat_tiles - 1), 0))
    return pl.pallas_call(body, grid=(bsz, t // TOKEN_TILE), in_specs=[row, t_spec],
                          out_specs=[row, pl.BlockSpec((SUBLANE, LANE), lambda b, i: (0, 0))],
                          out_shape=[jax.ShapeDtypeStruct(xu.shape, F32), jax.ShapeDtypeStruct((SUBLANE, LANE), F32)],
                          name="loss", compiler_params=_cparams())(xu, target)


def _row_tile(rows, row_bytes, budget=4 << 20, step=SUBLANE):
    if rows * row_bytes <= budget:
        return rows
    best = None
    for t in range(step, rows, step):
        if rows % t == 0 and t * row_bytes <= budget:
            best = t
    return best if best is not None else rows


def _as3d(x):
    p = x.shape[0]
    c = x.shape[-1]
    return x.reshape(p, -1, c)


def sum_parts(name, x):
    x3 = _as3d(x)
    p, r, c = x3.shape
    tr = _row_tile(r, p * c * 4, step=SUBLANE * (4 // x.dtype.itemsize))

    def body(x_ref, o_ref):
        s = x_ref[0].astype(F32)
        for j in range(1, p):
            s = s + x_ref[j].astype(F32)
        o_ref[...] = s

    out = pl.pallas_call(body, grid=(r // tr,), in_specs=[pl.BlockSpec((p, tr, c), lambda i: (0, i, 0))],
                         out_specs=pl.BlockSpec((tr, c), lambda i: (i, 0)), out_shape=jax.ShapeDtypeStruct((r, c), F32),
                         name=name, compiler_params=_cparams())(x3)
    return out.reshape(x.shape[1:])


def adamw(name, w, g, m, v):
    shape = w.shape
    c = shape[-1]
    to2d = lambda a: a.reshape(-1, c)
    r = to2d(w).shape[0]
    tr = _row_tile(r, 7 * c * 4, budget=6 << 20)

    def body(w_ref, g_ref, m_ref, v_ref, d_ref, nm_ref, nv_ref):
        gg = g_ref[...]
        nm = ADAM_B1 * m_ref[...] + (1.0 - ADAM_B1) * gg
        nv = ADAM_B2 * v_ref[...] + (1.0 - ADAM_B2) * jnp.square(gg)
        m_hat = nm / (1.0 - ADAM_B1 ** ADAM_STEP)
        v_hat = nv / (1.0 - ADAM_B2 ** ADAM_STEP)
        d_ref[...] = -ADAM_LR * (m_hat / (jnp.sqrt(v_hat) + ADAM_EPS) + ADAM_WD * w_ref[...])
        nm_ref[...] = nm
        nv_ref[...] = nv

    spec = pl.BlockSpec((tr, c), lambda i: (i, 0))
    outs = pl.pallas_call(body, grid=(r // tr,), in_specs=[spec] * 4, out_specs=[spec] * 3,
                          out_shape=[jax.ShapeDtypeStruct((r, c), F32)] * 3, name=name, compiler_params=_cparams())(
        to2d(w), to2d(g), to2d(m), to2d(v))
    return tuple(o.reshape(shape) for o in outs)


def exchange(name, srcs, group, same):
    p = 2 ** len(group)
    n = len(srcs)
    out_shape = [jax.ShapeDtypeStruct(((p,) + s.shape) if same else s.shape, s.dtype) for s in srcs]

    def index_of(coords):
        idx = 0
        for a in group:
            idx = idx * 2 + coords[a]
        return idx

    def body(*refs):
        src_refs, out_refs = refs[:n], refs[n:2 * n]
        send_sems, recv_sems = refs[2 * n:]
        pos = {a: lax.axis_index(a) for a in MESH_AXES}
        me = index_of(pos)
        peers = []
        for rel in range(1, p):
            coords = dict(pos)
            for bit, a in enumerate(reversed(group)):
                if (rel >> bit) & 1:
                    coords[a] = 1 - coords[a]
            peers.append((coords, index_of(coords)))

        def src_for(a, idx):
            return src_refs[a] if same else src_refs[a].at[idx]

        sends, recvs = [], []
        for a in range(n):
            for r, (coords, idx) in enumerate(peers):
                dev = tuple(coords[ax] for ax in MESH_AXES)
                send = pltpu.make_async_remote_copy(src_ref=src_for(a, idx), dst_ref=out_refs[a].at[me],
                                                    send_sem=send_sems.at[a, r], recv_sem=recv_sems.at[a, r],
                                                    device_id=dev, device_id_type=pl.DeviceIdType.MESH)
                send.start()
                sends.append(send)
                recvs.append(pltpu.make_async_remote_copy(src_ref=src_for(a, idx), dst_ref=out_refs[a].at[idx],
                                                          send_sem=send_sems.at[a, r], recv_sem=recv_sems.at[a, r],
                                                          device_id=dev, device_id_type=pl.DeviceIdType.MESH))
        for cp in sends:
            cp.wait_send()
        for cp in recvs:
            cp.wait_recv()

    any_spec = pl.BlockSpec(memory_space=pl.ANY)
    outs = pl.pallas_call(body, in_specs=[any_spec] * n, out_specs=[any_spec] * n, out_shape=out_shape,
                          scratch_shapes=[pltpu.SemaphoreType.DMA((n, p - 1)), pltpu.SemaphoreType.DMA((n, p - 1))],
                          name=name)(*srcs)
    me = index_of({a: lax.axis_index(a) for a in MESH_AXES})
    own = [s if same else lax.dynamic_index_in_dim(s, me, axis=0, keepdims=False) for s in srcs]
    return [lax.dynamic_update_index_in_dim(o, w, me, axis=0) for o, w in zip(outs, own)]


def _rope_tables(n_lat, n_ctx, rot_dim, heads):
    n_freq = rot_dim // 4
    tok = jnp.arange(n_lat, dtype=jnp.int32)
    inv = ROPE_THETA ** (-jnp.arange(n_freq, dtype=F32) / n_freq)
    ang = jnp.concatenate([(tok // GRID_W).astype(F32)[:, None] * inv, (tok % GRID_W).astype(F32)[:, None] * inv], axis=-1)
    cos, sin = jnp.cos(ang), jnp.sin(ang)
    cos = jnp.concatenate([jnp.concatenate([cos, cos], -1), jnp.ones((n_ctx, rot_dim), F32)], 0)
    sin = jnp.concatenate([jnp.concatenate([-sin, sin], -1), jnp.zeros((n_ctx, rot_dim), F32)], 0)
    half = rot_dim // 2
    w = heads * rot_dim
    j = np.arange(w)
    src = (j // rot_dim) * rot_dim + (j % rot_dim + half) % rot_dim
    swap = np.zeros((w, w), np.float32)
    swap[src, j] = 1.0
    return jnp.tile(cos, (1, heads)), jnp.tile(sin, (1, heads)), jnp.asarray(swap)


def _to_heads(a, nh):
    b, t, _ = a.shape
    return a.reshape(b, t, nh, -1).transpose(0, 2, 1, 3)


def _to_heads_t(a, nh):
    b, t, _ = a.shape
    return a.reshape(b, t, nh, -1).transpose(0, 2, 3, 1)


def _from_heads(a):
    b, nh, t, w = a.shape
    return a.transpose(0, 2, 1, 3).reshape(b, t, nh * w)


def _pad_w_in(w_in):
    parts, off = [], 0
    for size, pad in zip(IN_SIZES, IN_PAD):
        parts.append(w_in[:, off:off + size])
        if pad > size:
            parts.append(jnp.zeros((w_in.shape[0], pad - size), w_in.dtype))
        off += size
    return jnp.concatenate(parts, axis=1)


def _forward(xu, mods, prm, prm_lo, n_lat, n_ctx):
    bsz, t, d = xu.shape
    depth = mods.shape[0]
    n_lat_tiles = n_lat // TOKEN_TILE
    rope_a = _rope_tables(n_lat, n_ctx, HEAD_DIM, 1)
    rope_bq = _rope_tables(n_lat, n_ctx, B_ROPE, B_HEADS)
    rope_bk = _rope_tables(n_lat, n_ctx, B_ROPE, 1)
    clb = prm["c_lower_bounds"].reshape(depth, 2, C_HEADS, C_DK).transpose(2, 0, 1, 3)
    for l in range(depth):
        nm = lambda s: f"l{l}_{s}"
        vec = lambda name: prm[name][l][None, :]
        h = premod(nm("premix"), xu, vec("g_pre_mix"), mods[l], 0, n_lat_tiles)
        z = linear(nm("w_in"), h.reshape(bsz * t, d), _pad_w_in(prm["w_in"][l]), _pad_w_in(prm_lo["w_in"][l])).reshape(bsz, t, D_IN_PAD)
        seg = lambda i: z[:, :, IN_OFF[i]:IN_OFF[i] + IN_SIZES[i]]
        aq = a_prep(nm("aq"), _to_heads(seg(0), A_HEADS), vec("a_q_norm"), *rope_a)
        ak = a_prep(nm("ak"), _to_heads(seg(1), A_KV_HEADS), vec("a_k_norm"), *rope_a)
        av = _to_heads(seg(2), A_KV_HEADS)
        ya = attention(nm("att_a"), [aq.reshape(bsz, A_KV_HEADS, A_GROUP, t, HEAD_DIM)], [ak.transpose(0, 1, 3, 2)],
                       av.transpose(0, 1, 3, 2), HEAD_DIM ** -0.5, True, ATT_A_TQ_FWD, ATT_A_TQ, n_lat)
        ya = _from_heads(ya.reshape(bsz, A_HEADS, t, HEAD_DIM))
        wq = prm["w_q_up"][l].reshape(B_Q_RANK, B_HEADS, B_NOPE + B_ROPE)
        wkv = prm["w_kv_up"][l].reshape(B_KV_RANK, B_HEADS, B_NOPE + B_V)
        qn, qp, kn, bv, kp = b_prep(
            nm("b_prep"), seg(3), seg(4), seg(5), vec("b_q_norm"),
            wq[:, :, :B_NOPE].reshape(B_Q_RANK, -1), wq[:, :, B_NOPE:].reshape(B_Q_RANK, -1), vec("b_kv_norm"),
            wkv[:, :, :B_NOPE].reshape(B_KV_RANK, -1), wkv[:, :, B_NOPE:].reshape(B_KV_RANK, -1), *rope_bq, *rope_bk)
        qb = jnp.concatenate([_to_heads(qn, B_HEADS), _to_heads(qp, B_HEADS)], axis=-1)[:, :, None]
        kbt = jnp.concatenate([_to_heads_t(kn, B_HEADS),
                               jnp.broadcast_to(kp.transpose(0, 2, 1)[:, None], (bsz, B_HEADS, B_ROPE, t))], axis=2)
        yb = attention(nm("att_b"), [qb], [kbt], _to_heads_t(bv, B_HEADS), (B_NOPE + B_ROPE) ** -0.5, False, ATT_B_TQ,
                       ATT_B_TQ, n_lat)
        yb = _from_heads(yb[:, :, 0])
        cq, kf, gf, kb, gb = c_prep(nm("c_prep"), _to_heads(seg(6), C_HEADS), _to_heads(seg(7), C_HEADS),
                                    _to_heads(seg(8), C_HEADS), clb, l)
        cv = _to_heads(seg(9), C_HEADS)
        o_f = gla_scan(nm("scan_f"), cq, kf, cv, gf, False, n_lat // SCAN_CHUNK)
        o_b = gla_scan(nm("scan_b"), cq, kb, cv, gb, True, n_lat // SCAN_CHUNK)
        yc = _from_heads(c_readout(nm("c_out"), o_f, o_b, _to_heads(seg(10), C_HEADS), vec("c_out_norm")))
        y = linear(nm("w_out"), jnp.concatenate([ya, yb, yc], axis=-1).reshape(bsz * t, d), prm["w_out"][l],
                   prm_lo["w_out"][l]).reshape(bsz, t, d)
        x1 = resid(nm("res_mix"), xu, y, vec("g_post_mix"), mods[l], 2, n_lat_tiles)
        h2 = premod(nm("preffn"), x1, vec("g_pre_ffn"), mods[l], 3, n_lat_tiles)
        f = mlp(nm("mlp"), h2.reshape(bsz * t, d), prm["w_ff1"][l], prm["w_ff2"][l], prm_lo["w_ff1"][l],
                prm_lo["w_ff2"][l]).reshape(bsz, t, d)
        xu = resid(nm("res_ffn"), x1, f, vec("g_post_ffn"), mods[l], 5, n_lat_tiles)
    return xu


BIG = {"w_in": 2, "w_q_up": 2, "w_kv_up": 2, "w_out": 1, "w_ff1": 2, "w_ff2": 1}
SMALL = ("g_pre_mix", "g_post_mix", "g_pre_ffn", "g_post_ffn", "a_q_norm", "a_k_norm", "b_q_norm", "b_kv_norm",
         "c_lower_bounds", "c_out_norm")
WEIGHTS = ("c_ctx", "w_ada", "b_ada", "g_pre_mix", "g_post_mix", "g_pre_ffn", "g_post_ffn", "w_in", "a_q_norm", "a_k_norm",
           "b_q_norm", "w_q_up", "b_kv_norm", "w_kv_up", "c_lower_bounds", "c_out_norm", "w_out", "w_ff1", "w_ff2")


def _unshard(g, axis):
    depth, _, r, c = g.shape
    if axis == 1:
        return g.reshape(depth, N_CHIP * r, c)
    return g.transpose(0, 2, 1, 3).reshape(depth, r, N_CHIP * c)


def _shard_major(g, axis):
    depth, r, c = g.shape
    if axis == 1:
        return g.reshape(depth, N_CHIP, r // N_CHIP, c)
    return g.reshape(depth, r, N_CHIP, c // N_CHIP).transpose(0, 2, 1, 3)


def kernel(x, c, ctx, c_ctx, w_ada, b_ada, g_pre_mix, g_post_mix, g_pre_ffn, g_post_ffn, w_in, a_q_norm, a_k_norm, b_q_norm, w_q_up, b_kv_norm, w_kv_up, c_lower_bounds, c_out_norm, w_out, w_ff1, w_ff2, loss_target, m_c_ctx, m_w_ada, m_b_ada, m_g_pre_mix, m_g_post_mix, m_g_pre_ffn, m_g_post_ffn, m_w_in, m_a_q_norm, m_a_k_norm, m_b_q_norm, m_w_q_up, m_b_kv_norm, m_w_kv_up, m_c_lower_bounds, m_c_out_norm, m_w_out, m_w_ff1, m_w_ff2, v_c_ctx, v_w_ada, v_b_ada, v_g_pre_mix, v_g_post_mix, v_g_pre_ffn, v_g_post_ffn, v_w_in, v_a_q_norm, v_a_k_norm, v_b_q_norm, v_w_q_up, v_b_kv_norm, v_w_kv_up, v_c_lower_bounds, v_c_out_norm, v_w_out, v_w_ff1, v_w_ff2):
    local = dict(c_ctx=c_ctx, w_ada=w_ada, b_ada=b_ada, g_pre_mix=g_pre_mix, g_post_mix=g_post_mix, g_pre_ffn=g_pre_ffn,
                 g_post_ffn=g_post_ffn, w_in=w_in, a_q_norm=a_q_norm, a_k_norm=a_k_norm, b_q_norm=b_q_norm, w_q_up=w_q_up,
                 b_kv_norm=b_kv_norm, w_kv_up=w_kv_up, c_lower_bounds=c_lower_bounds, c_out_norm=c_out_norm, w_out=w_out,
                 w_ff1=w_ff1, w_ff2=w_ff2)
    mom = dict(c_ctx=m_c_ctx, w_ada=m_w_ada, b_ada=m_b_ada, g_pre_mix=m_g_pre_mix, g_post_mix=m_g_post_mix,
               g_pre_ffn=m_g_pre_ffn, g_post_ffn=m_g_post_ffn, w_in=m_w_in, a_q_norm=m_a_q_norm, a_k_norm=m_a_k_norm,
               b_q_norm=m_b_q_norm, w_q_up=m_w_q_up, b_kv_norm=m_b_kv_norm, w_kv_up=m_w_kv_up,
               c_lower_bounds=m_c_lower_bounds, c_out_norm=m_c_out_norm, w_out=m_w_out, w_ff1=m_w_ff1, w_ff2=m_w_ff2)
    var = dict(c_ctx=v_c_ctx, w_ada=v_w_ada, b_ada=v_b_ada, g_pre_mix=v_g_pre_mix, g_post_mix=v_g_post_mix,
               g_pre_ffn=v_g_pre_ffn, g_post_ffn=v_g_post_ffn, w_in=v_w_in, a_q_norm=v_a_q_norm, a_k_norm=v_a_k_norm,
               b_q_norm=v_b_q_norm, w_q_up=v_w_q_up, b_kv_norm=v_b_kv_norm, w_kv_up=v_w_kv_up,
               c_lower_bounds=v_c_lower_bounds, c_out_norm=v_c_out_norm, w_out=v_w_out, w_ff1=v_w_ff1, w_ff2=v_w_ff2)

    bsz, n_lat, d = x.shape
    n_ctx = ctx.shape[1]
    depth = w_ada.shape[0]
    assert depth == 2 and n_lat % TOKEN_TILE == 0 and n_ctx % TOKEN_TILE == 0 and bsz * N_DEV + 1 <= ADA_ROWS
    ax, ay, ac = (lax.axis_index(a) for a in MESH_AXES)
    chip = 2 * ax + ay
    dev = 2 * chip + ac

    c_all = exchange("gather_c", [c], MESH_AXES, True)[0].reshape(N_DEV * bsz, d)
    big_names = list(BIG)
    mine = [lax.dynamic_index_in_dim(local[n].astype(BF16), ac, axis=0, keepdims=False) for n in big_names]
    over_chips = exchange("gather_w_quad", mine + [c_lower_bounds], ("x", "y"), True)
    both_layers = exchange("gather_w_pair", over_chips[:-1], ("c",), True)
    prm_lo = {n: _unshard(g, BIG[n]) for n, g in zip(big_names, both_layers)}
    prm = {n: w.astype(F32) for n, w in prm_lo.items()}
    prm["c_lower_bounds"] = over_chips[-1].transpose(1, 2, 0, 3).reshape(depth, 2, -1)
    for n in SMALL:
        if n != "c_lower_bounds":
            prm[n] = local[n]

    n_ada = w_ada.shape[2]
    cc = jnp.concatenate([c_all, c_ctx[None, :], jnp.zeros((ADA_ROWS - N_DEV * bsz - 1, d), F32)], axis=0)
    b_blk = lax.dynamic_slice_in_dim(b_ada, chip * n_ada, n_ada, axis=1)[:, None, :]
    mod_part, ada_vjp = jax.vjp(ada_op, cc, w_ada, b_blk)
    mod_full = exchange("gather_mod", [mod_part], ("x", "y"), True)[0].transpose(1, 2, 0, 3).reshape(depth, ADA_ROWS, 6 * d)
    mod_lat = lax.dynamic_slice_in_dim(mod_full, dev * bsz, bsz, axis=1)
    mod_ctx = jnp.broadcast_to(mod_full[:, N_DEV * bsz:N_DEV * bsz + 1], mod_lat.shape)
    mods = jnp.stack([mod_lat, mod_ctx], axis=2).reshape(depth, bsz, 2, 6, d)

    xu = jnp.concatenate([x, ctx], axis=1)
    x_out, fwd_vjp = jax.vjp(lambda xu_, mods_, prm_: _forward(xu_, mods_, prm_, prm_lo, n_lat, n_ctx), xu, mods, prm)
    dx_out, loss_blk = loss_op(x_out, loss_target, n_lat // TOKEN_TILE)
    d_xu, d_mods, d_prm = fwd_vjp(dx_out)
    grad_x = d_xu[:, :n_lat]

    d_mods = d_mods.reshape(depth, bsz, 2, 6 * d)
    pieces = [d_mods] + [d_prm[n] for n in SMALL] + [loss_blk[0:1, 0:1]]
    sizes = [int(np.prod(p.shape)) for p in pieces]
    flat = jnp.concatenate([p.reshape(-1) for p in pieces])
    n_flat = -(-flat.shape[0] // (SUBLANE * LANE)) * SUBLANE * LANE
    flat = jnp.concatenate([flat, jnp.zeros((n_flat - flat.shape[0],), F32)]).reshape(-1, LANE)
    small_all = exchange("gather_small", [flat], MESH_AXES, True)[0]
    small_sum = sum_parts("sum_small", small_all).reshape(-1)
    offs = np.cumsum([0] + sizes)
    summed = {n: small_sum[offs[i + 1]:offs[i + 2]].reshape(d_prm[n].shape) for i, n in enumerate(SMALL)}
    loss = small_sum[offs[-2]]
    dm_all = small_all.reshape(N_DEV, -1)[:, :sizes[0]].reshape(N_DEV, depth, bsz, 2, 6 * d)
    dm_rows = dm_all.transpose(3, 0, 2, 1, 4).reshape(2, N_DEV * bsz, depth, 6 * d)
    d_ctx_row = sum_parts("sum_dmod_ctx", dm_rows[1])
    grad_b_ada = sum_parts("sum_b_ada", dm_rows.reshape(2 * N_DEV * bsz, depth, 6 * d))
    d_rows = jnp.concatenate([dm_rows[0].transpose(1, 0, 2), d_ctx_row[:, None, :],
                              jnp.zeros((depth, ADA_ROWS - N_DEV * bsz - 1, 6 * d), F32)], axis=1)
    d_cc, grad_w_ada, _ = ada_vjp(lax.dynamic_slice_in_dim(d_rows, chip * n_ada, n_ada, axis=2))
    d_cctx_all = exchange("gather_dcctx", [d_cc[N_DEV * bsz:N_DEV * bsz + 1]], MESH_AXES, True)[0]
    grad_c_ctx = sum_parts("sum_dcctx", d_cctx_all[0::2]).reshape(d)

    shard_major = [_shard_major(d_prm[n], BIG[n]).astype(BF16) for n in big_names]
    pair = exchange("rs_pair", shard_major, ("c",), False)
    chip_sum = [sum_parts(f"rs_sum1_{n}", p) for n, p in zip(big_names, pair)]
    quad = exchange("rs_quad", [s.astype(BF16) for s in chip_sum], ("x", "y"), False)
    total = [sum_parts(f"rs_sum2_{n}", q) for n, q in zip(big_names, quad)]
    both = exchange("rs_share", total, ("c",), True)

    grads = dict(summed)
    grads["c_lower_bounds"] = lax.dynamic_slice_in_dim(summed["c_lower_bounds"], chip * c_lower_bounds.shape[2],
                                                       c_lower_bounds.shape[2], axis=2)
    grads.update(c_ctx=grad_c_ctx, w_ada=grad_w_ada, b_ada=grad_b_ada)
    grads.update({n: g for n, g in zip(big_names, both)})

    deltas, new_m, new_v = {}, {}, {}
    for n in WEIGHTS:
        as2d = (lambda a: a[None, :]) if local[n].ndim == 1 else (lambda a: a)
        dl, nm_, nv_ = adamw("adamw_" + n, as2d(local[n]), as2d(grads[n]), as2d(mom[n]), as2d(var[n]))
        deltas[n], new_m[n], new_v[n] = (a.reshape(local[n].shape) for a in (dl, nm_, nv_))
    return (loss, grad_x, *[grads[n] for n in WEIGHTS], *[deltas[n] for n in WEIGHTS],
            *[new_m[n] for n in WEIGHTS], *[new_v[n] for n in WEIGHTS])
```

```python
import functools
from typing import Any, Callable, NamedTuple

import numpy as np
import jax
import jax.numpy as jnp
from jax import lax
from jax.experimental import pallas as pl
from jax.experimental.pallas import tpu as pltpu

F32 = jnp.float32
BF16 = jnp.bfloat16
HIGHEST = lax.Precision.HIGHEST

GRID_W = 64
HEAD_DIM = 64
A_HEADS, A_KV_HEADS = 8, 2
A_GROUP = A_HEADS // A_KV_HEADS
B_HEADS, B_Q_RANK, B_KV_RANK, B_NOPE, B_ROPE, B_V = 4, 192, 128, 64, 32, 64
C_HEADS, C_DK, C_DV = 4, 64, 64
SCAN_CHUNK = 64
SCAN_SUB = 16
ROPE_THETA = 10000.0
EPS = 1e-6
F_TINY = 1e-30
ADAM_LR, ADAM_B1, ADAM_B2, ADAM_EPS, ADAM_WD, ADAM_STEP = 0.001, 0.9, 0.999, 1e-08, 0.01, 10

IN_SIZES = (512, 128, 128, 192, 128, 32, 256, 256, 256, 256, 256)
IN_PAD = (512, 128, 128, 256, 128, 128, 256, 256, 256, 256, 256)
IN_OFF = tuple(int(v) for v in np.cumsum((0,) + IN_PAD)[:-1])
D_IN_PAD = int(sum(IN_PAD))

LANE = 128
SUBLANE = 8
TOKEN_TILE = 256
ATT_A_TQ = 64
ATT_A_TQ_FWD = 128
ATT_B_TQ = 256
ATT_FWD_ROW_GROUPS = 4
MM_ROWS = 256
MM_TK_TOKENS = 512
MM_TN_OUT = 2560 * 1024
VMEM_LIMIT = 56 * 1024 * 1024
MESH_AXES = ("x", "y", "c")
N_DEV = 8
N_CHIP = 4
ADA_ROWS = 24


class Arg(NamedTuple):
    arr: Any
    block: tuple
    imap: Callable
    kind: str
    first: Callable = None


class Out(NamedTuple):
    shape: tuple
    block: tuple
    imap: Callable


def _cparams():
    return pltpu.CompilerParams(vmem_limit_bytes=VMEM_LIMIT)


def tile_op(name, fn, grid, args, outs, residual=None):
    n_in, n_out = len(args), len(outs)
    in_specs = [pl.BlockSpec(a.block, a.imap) for a in args]
    out_specs = [pl.BlockSpec(o.block, o.imap) for o in outs]
    out_shape = [jax.ShapeDtypeStruct(o.shape, F32) for o in outs]
    diff = [i for i, a in enumerate(args) if a.kind != "const"]

    def fwd_call(*arrays):
        def body(*refs):
            ids = tuple(pl.program_id(i) for i in range(len(grid)))
            res = list(fn(ids, *[r[...] for r in refs[:n_in]]))
            out_refs = refs[n_in:]
            if residual is not None:
                res[0] = res[0] + refs[n_in][...]
                out_refs = refs[n_in + 1:]
            for r, o in zip(out_refs, res):
                r[...] = o

        specs = in_specs + ([out_specs[0]] if residual is not None else [])
        return pl.pallas_call(body, grid=grid, in_specs=specs, out_specs=out_specs, out_shape=out_shape,
                              name=name + "_fwd", compiler_params=_cparams())(*arrays)

    def bwd_call(arrays, cts):
        def body(*refs):
            ids = tuple(pl.program_id(i) for i in range(len(grid)))
            vals = [r[...] for r in refs[:n_in]]
            ct = tuple(r[...] for r in refs[n_in:n_in + n_out])
            drefs = refs[n_in + n_out:]

            def g(*dv):
                full = list(vals)
                for i, v in zip(diff, dv):
                    full[i] = v
                return tuple(fn(ids, *full))

            _, vjp = jax.vjp(g, *[vals[i] for i in diff])
            ds = vjp(ct)
            for i, d, r in zip(diff, ds, drefs):
                if args[i].kind == "tile":
                    r[...] = d
                else:
                    is_first = args[i].first(ids)

                    @pl.when(is_first)
                    def _(r=r, d=d):
                        r[...] = d

                    @pl.when(jnp.logical_not(is_first))
                    def _(r=r, d=d):
                        r[...] += d

        d_specs = [in_specs[i] for i in diff]
        d_shape = [jax.ShapeDtypeStruct(arrays[i].shape, F32) for i in diff]
        return pl.pallas_call(body, grid=grid, in_specs=in_specs + out_specs, out_specs=d_specs, out_shape=d_shape,
                              name=name + "_bwd", compiler_params=_cparams())(*arrays, *cts)

    @jax.custom_vjp
    def op(*arrays):
        return tuple(fwd_call(*arrays))

    def op_fwd(*arrays):
        return tuple(fwd_call(*arrays)), arrays[:n_in]

    def op_bwd(arrays, cts):
        ds = bwd_call(arrays, cts)
        res, k = [], 0
        for i, a in enumerate(args):
            if a.kind == "const":
                res.append(jnp.zeros_like(arrays[i]))
            else:
                res.append(ds[k])
                k += 1
        if residual is not None:
            res.append(cts[0])
        return tuple(res)

    op.defvjp(op_fwd, op_bwd)
    return op(*[a.arr for a in args], *([residual] if residual is not None else []))


def _pick(n, cap):
    if n <= cap:
        return n
    best = None
    for t in range(LANE, cap + 1, LANE):
        if n % t == 0:
            best = t
    assert best is not None, (n, cap)
    return best


_NN = (((1,), (0,)), ((), ()))
_NT = (((1,), (1,)), ((), ()))
_TN = (((0,), (0,)), ((), ()))


def _resident(shape):
    return pl.BlockSpec(shape, lambda *ids: (0,) * len(shape), pipeline_mode=pl.Buffered(1))


def _mm_rows(name, a, w, transposed):
    m, k = a.shape
    n = w.shape[0] if transposed else w.shape[1]
    tm = MM_ROWS
    dims = _NT if transposed else _NN

    def body(a_ref, w_ref, o_ref):
        o_ref[...] = lax.dot_general(a_ref[...].astype(BF16), w_ref[...], dims, preferred_element_type=F32)

    return pl.pallas_call(body, grid=(m // tm,), in_specs=[pl.BlockSpec((tm, k), lambda i: (i, 0)), _resident(w.shape)],
                          out_specs=pl.BlockSpec((tm, n), lambda i: (i, 0)), out_shape=jax.ShapeDtypeStruct((m, n), F32),
                          name=name, compiler_params=_cparams())(a, w)


def _mm_tn(name, a, g):
    t, k = a.shape
    n = g.shape[1]
    tko, tno = k, n
    while tko * tno > MM_TN_OUT:
        if tko >= tno:
            tko //= 2
        else:
            tno //= 2
    assert k % tko == 0 and n % tno == 0 and tko % LANE == 0 and tno % LANE == 0
    tt = _pick(t, MM_TK_TOKENS)

    def body(a_ref, g_ref, o_ref):
        p = lax.dot_general(a_ref[...].astype(BF16), g_ref[...].astype(BF16), _TN, preferred_element_type=F32)
        kk = pl.program_id(2)

        @pl.when(kk == 0)
        def _():
            o_ref[...] = p

        @pl.when(kk != 0)
        def _():
            o_ref[...] += p

    return pl.pallas_call(body, grid=(k // tko, n // tno, t // tt),
                          in_specs=[pl.BlockSpec((tt, tko), lambda i, j, kk: (kk, i)), pl.BlockSpec((tt, tno), lambda i, j, kk: (kk, j))],
                          out_specs=pl.BlockSpec((tko, tno), lambda i, j, kk: (i, j)),
                          out_shape=jax.ShapeDtypeStruct((k, n), F32), name=name, compiler_params=_cparams())(a, g)


def linear(name, a, w, w_lo):
    @jax.custom_vjp
    def op(a, w, w_lo):
        return _mm_rows(name + "_fwd", a, w_lo, False)

    def op_fwd(a, w, w_lo):
        return _mm_rows(name + "_fwd", a, w_lo, False), (a, w_lo)

    def op_bwd(res, g):
        a, w_lo = res
        return _mm_rows(name + "_da", g, w_lo, True), _mm_tn(name + "_dw", a, g), jnp.zeros_like(w_lo)

    op.defvjp(op_fwd, op_bwd)
    return op(a, w, w_lo)


def mlp(name, h, w1, w2, w1_lo, w2_lo):
    m, d = h.shape
    f = w1_lo.shape[1]
    tm = MM_ROWS
    row = pl.BlockSpec((tm, d), lambda i: (i, 0))
    wide = pl.BlockSpec((tm, f), lambda i: (i, 0))

    def fwd_call(h, w1_lo, w2_lo):
        def body(h_ref, w1_ref, w2_ref, o_ref):
            u = lax.dot_general(h_ref[...].astype(BF16), w1_ref[...], _NN, preferred_element_type=F32)
            act = jnp.square(jnp.maximum(u, 0.0))
            o_ref[...] = lax.dot_general(act.astype(BF16), w2_ref[...], _NN, preferred_element_type=F32)

        return pl.pallas_call(body, grid=(m // tm,), in_specs=[row, _resident(w1_lo.shape), _resident(w2_lo.shape)],
                              out_specs=row, out_shape=jax.ShapeDtypeStruct((m, d), F32), name=name + "_fwd",
                              compiler_params=_cparams())(h, w1_lo, w2_lo)

    def bwd_call(h, w1_lo, w2_lo, dy):
        def body(h_ref, dy_ref, w1_ref, w2_ref, dh_ref, act_ref, du_ref):
            u = lax.dot_general(h_ref[...].astype(BF16), w1_ref[...], _NN, preferred_element_type=F32)
            r = jnp.maximum(u, 0.0)
            act_ref[...] = (r * r).astype(BF16)
            dact = lax.dot_general(dy_ref[...].astype(BF16), w2_ref[...], _NT, preferred_element_type=F32)
            du = (dact * (2.0 * r)).astype(BF16)
            du_ref[...] = du
            dh_ref[...] = lax.dot_general(du, w1_ref[...], _NT, preferred_element_type=F32)

        return pl.pallas_call(body, grid=(m // tm,), in_specs=[row, row, _resident(w1_lo.shape), _resident(w2_lo.shape)],
                              out_specs=[row, wide, wide],
                              out_shape=[jax.ShapeDtypeStruct((m, d), F32), jax.ShapeDtypeStruct((m, f), BF16),
                                         jax.ShapeDtypeStruct((m, f), BF16)],
                              name=name + "_bwd", compiler_params=_cparams())(h, dy, w1_lo, w2_lo)

    @jax.custom_vjp
    def op(h, w1, w2, w1_lo, w2_lo):
        return fwd_call(h, w1_lo, w2_lo)

    def op_fwd(h, w1, w2, w1_lo, w2_lo):
        return fwd_call(h, w1_lo, w2_lo), (h, w1_lo, w2_lo)

    def op_bwd(res, dy):
        h, w1_lo, w2_lo = res
        dh, act, du = bwd_call(h, w1_lo, w2_lo, dy)
        return (dh, _mm_tn(name + "_dw1", h, du), _mm_tn(name + "_dw2", act, dy), jnp.zeros_like(w1_lo), jnp.zeros_like(w2_lo))

    op.defvjp(op_fwd, op_bwd)
    return op(h, w1, w2, w1_lo, w2_lo)


def _rms(x, g):
    return x * lax.rsqrt(jnp.mean(x * x, axis=-1, keepdims=True) + EPS) * g


def _sigmoid(z):
    return 1.0 / (1.0 + jnp.exp(-z))


def _silu(z):
    return z * _sigmoid(z)


def _rope(y, cos, sin_signed, swap):
    return y * cos + jnp.dot(y, swap, precision=HIGHEST, preferred_element_type=F32) * sin_signed


def _dot_bf16(a, b, dims=((1,), (0,))):
    return lax.dot_general(a.astype(BF16), b.astype(BF16), (dims, ((), ())), preferred_element_type=F32)


def _gla_step(state, q, k, v, g, reverse):
    c, d = q.shape
    sub = SCAN_SUB
    nb = c // sub
    row = lax.broadcasted_iota(jnp.int32, (c, c), 0)
    col = lax.broadcasted_iota(jnp.int32, (c, c), 1)
    tri = (row <= col) if reverse else (row >= col)
    b = jnp.dot(tri.astype(F32), g, precision=HIGHEST, preferred_element_type=F32)
    o = jnp.dot(q * jnp.exp(b), state, preferred_element_type=F32)
    rs = lax.broadcasted_iota(jnp.int32, (sub, sub), 0)
    cs = lax.broadcasted_iota(jnp.int32, (sub, sub), 1)
    tri_s = ((rs <= cs) if reverse else (rs >= cs)).astype(F32)
    rowc = lax.broadcasted_iota(jnp.int32, (c, 1), 0)
    nonpos = lambda x: jnp.where(x > 0.0, 0.0, x)
    diag = []
    for j in range(nb):
        sl = slice(j * sub, (j + 1) * sub)
        bj, kj, vj, qj = b[sl], k[sl], v[sl], q[sl]
        dec = jnp.exp(nonpos(bj[:, None, :] - bj[None, :, :]))
        sc = jnp.sum(qj[:, None, :] * kj[None, :, :] * dec, axis=-1) * tri_s
        diag.append(jnp.dot(sc, vj, preferred_element_type=F32))
        if (j > 0) if reverse else (j < nb - 1):
            ref = bj[0:1] if reverse else bj[sub - 1:sub]
            qa = q * jnp.exp(nonpos(b - ref))
            ks = kj * jnp.exp(ref - bj)
            scj = lax.dot_general(qa, ks, (((1,), (1,)), ((), ())), precision=HIGHEST, preferred_element_type=F32)
            later = (rowc < j * sub) if reverse else (rowc >= (j + 1) * sub)
            o = o + jnp.dot(jnp.where(later, scj, 0.0), vj, preferred_element_type=F32)
    o = o + jnp.concatenate(diag, axis=0)
    b_end = b[0:1, :] if reverse else b[c - 1:c, :]
    kd = k * jnp.exp(b_end - b)
    new_state = state * jnp.exp(b_end).reshape(d, 1) + lax.dot_general(kd, v, (((0,), (0,)), ((), ())), preferred_element_type=F32)
    return new_state, o


def gla_scan(name, q, k, v, g, reverse, n_lat_chunks):
    bsz, nh, t, d = q.shape
    c = SCAN_CHUNK
    n = t // c

    def chunk_of(j):
        return (n - 1 - j) if reverse else lax.rem(j + n_lat_chunks, n)

    blk = (bsz, nh, c, d)
    st_blk = (bsz, nh, None, d, d)
    chains = [(b, h) for b in range(bsz) for h in range(nh)]

    def fwd_call(q, k, v, g):
        def body(q_ref, k_ref, v_ref, g_ref, o_ref, states_ref, st):
            @pl.when(pl.program_id(0) == 0)
            def _():
                st[...] = jnp.zeros_like(st)
            for bh in chains:
                s = st[bh]
                states_ref[bh] = s
                ns, o = _gla_step(s, q_ref[bh], k_ref[bh], v_ref[bh], g_ref[bh], reverse)
                st[bh] = ns
                o_ref[bh] = o

        spec = pl.BlockSpec(blk, lambda j: (0, 0, chunk_of(j), 0))
        return pl.pallas_call(
            body, grid=(n,), in_specs=[spec] * 4,
            out_specs=[spec, pl.BlockSpec(st_blk, lambda j: (0, 0, j, 0, 0))],
            out_shape=[jax.ShapeDtypeStruct(q.shape, F32), jax.ShapeDtypeStruct((bsz, nh, n, d, d), F32)],
            scratch_shapes=[pltpu.VMEM((bsz, nh, d, d), F32)], name=name + "_fwd", compiler_params=_cparams())(q, k, v, g)

    def bwd_call(q, k, v, g, states, do):
        def body(q_ref, k_ref, v_ref, g_ref, s_ref, do_ref, dq_ref, dk_ref, dv_ref, dg_ref, dst):
            @pl.when(pl.program_id(0) == 0)
            def _():
                dst[...] = jnp.zeros_like(dst)
            step = functools.partial(_gla_step, reverse=reverse)
            for bh in chains:
                _, vjp = jax.vjp(step, s_ref[bh], q_ref[bh], k_ref[bh], v_ref[bh], g_ref[bh])
                ds, dq, dk, dv, dg = vjp((dst[bh], do_ref[bh]))
                dst[bh] = ds
                dq_ref[bh] = dq
                dk_ref[bh] = dk
                dv_ref[bh] = dv
                dg_ref[bh] = dg

        spec = pl.BlockSpec(blk, lambda jj: (0, 0, chunk_of(n - 1 - jj), 0))
        s_spec = pl.BlockSpec(st_blk, lambda jj: (0, 0, n - 1 - jj, 0, 0))
        return pl.pallas_call(
            body, grid=(n,), in_specs=[spec] * 4 + [s_spec, spec], out_specs=[spec] * 4,
            out_shape=[jax.ShapeDtypeStruct(q.shape, F32)] * 4,
            scratch_shapes=[pltpu.VMEM((bsz, nh, d, d), F32)], name=name + "_bwd", compiler_params=_cparams())(q, k, v, g, states, do)

    @jax.custom_vjp
    def op(q, k, v, g):
        return fwd_call(q, k, v, g)[0]

    def op_fwd(q, k, v, g):
        o, states = fwd_call(q, k, v, g)
        return o, (q, k, v, g, states)

    def op_bwd(res, do):
        return tuple(bwd_call(*res, do))

    op.defvjp(op_fwd, op_bwd)
    return op(q, k, v, g)


def _zero_ids(ids):
    z = ids[0] == 0
    for i in ids[1:]:
        z = jnp.logical_and(z, i == 0)
    return z


def _token_grid(x, n_lat_tiles):
    bsz, t, d = x.shape
    nt = t // TOKEN_TILE
    row = lambda w: ((None, TOKEN_TILE, w), lambda b, i: (b, i, 0))
    mod_block = (None, None, 6, d)
    mod_imap = lambda b, i: (b, (i >= n_lat_tiles).astype(jnp.int32), 0, 0)
    mod_first = lambda ids: jnp.logical_or(ids[1] == 0, ids[1] == n_lat_tiles)
    return bsz, t, d, nt, row, (mod_block, mod_imap, mod_first)


def premod(name, x, gain, mods, r0, n_lat_tiles):
    bsz, t, d, nt, row, (mb, mi, mf) = _token_grid(x, n_lat_tiles)

    def fn(ids, xb, gb, mod):
        return (_rms(xb, gb) * (1.0 + mod[r0 + 1:r0 + 2]) + mod[r0:r0 + 1],)

    args = [Arg(x, *row(d), "tile"), Arg(gain, (1, d), lambda b, i: (0, 0), "acc", _zero_ids), Arg(mods, mb, mi, "acc", mf)]
    return tile_op(name, fn, (bsz, nt), args, [Out(x.shape, *row(d))])[0]


def resid(name, x, y, gain, mods, r, n_lat_tiles):
    bsz, t, d, nt, row, (mb, mi, mf) = _token_grid(x, n_lat_tiles)

    def fn(ids, yb, gb, mod):
        return (mod[r:r + 1] * _rms(yb, gb),)

    args = [Arg(y, *row(d), "tile"), Arg(gain, (1, d), lambda b, i: (0, 0), "acc", _zero_ids), Arg(mods, mb, mi, "acc", mf)]
    return tile_op(name, fn, (bsz, nt), args, [Out(x.shape, *row(d))], residual=x)[0]


def _heads_spec(nh, w):
    return (None, nh, TOKEN_TILE, w), lambda b, i: (b, 0, i, 0)


def a_prep(name, x, gain, cos, sin, swap):
    bsz, nh, t, d = x.shape
    tab = ((TOKEN_TILE, d), lambda b, i: (i, 0))

    def fn(ids, xb, gb, cb, sb, pb):
        y = _rms(xb, gb)
        swapped = jnp.dot(y.reshape(nh * TOKEN_TILE, d), pb, precision=HIGHEST, preferred_element_type=F32)
        return (y * cb + swapped.reshape(nh, TOKEN_TILE, d) * sb,)

    args = [Arg(x, *_heads_spec(nh, d), "tile"), Arg(gain, (1, d), lambda b, i: (0, 0), "acc", _zero_ids),
            Arg(cos, *tab, "const"), Arg(sin, *tab, "const"), Arg(swap, (d, d), lambda b, i: (0, 0), "const")]
    return tile_op(name, fn, (bsz, t // TOKEN_TILE), args, [Out(x.shape, *_heads_spec(nh, d))])[0]


def b_prep(name, bqd, bkvd, bkr, bqn, wq_nope, wq_pe, bkvn, wkv_nope, wkv_v, cos_q, sin_q, swap_q, cos_k, sin_k, swap_k):
    bsz, t, _ = bqd.shape
    row = lambda w: ((None, TOKEN_TILE, w), lambda b, i: (b, i, 0))
    whole = lambda a: (a.shape, lambda b, i: (0,) * a.ndim)
    tab = lambda w: ((TOKEN_TILE, w), lambda b, i: (i, 0))

    def fn(ids, qd, kvd, kr, qn, wqn, wqp, kvn, wkn, wkv, cq, sq, pq, ck, sk, pk):
        hq = _rms(qd, qn)
        hkv = _rms(kvd, kvn)
        return (_dot_bf16(hq, wqn), _rope(_dot_bf16(hq, wqp), cq, sq, pq), _dot_bf16(hkv, wkn), _dot_bf16(hkv, wkv),
                _rope(kr, ck, sk, pk))

    params = [bqn, wq_nope, wq_pe, bkvn, wkv_nope, wkv_v]
    args = [Arg(bqd, *row(B_Q_RANK), "tile"), Arg(bkvd, *row(B_KV_RANK), "tile"), Arg(bkr, *row(B_ROPE), "tile")]
    args += [Arg(p, *whole(p), "acc", _zero_ids) for p in params]
    args += [Arg(cos_q, *tab(cos_q.shape[1]), "const"), Arg(sin_q, *tab(cos_q.shape[1]), "const"), Arg(swap_q, *whole(swap_q), "const"),
             Arg(cos_k, *tab(B_ROPE), "const"), Arg(sin_k, *tab(B_ROPE), "const"), Arg(swap_k, *whole(swap_k), "const")]
    widths = (B_HEADS * B_NOPE, B_HEADS * B_ROPE, B_HEADS * B_NOPE, B_HEADS * B_V, B_ROPE)
    outs = [Out((bsz, t, w), *row(w)) for w in widths]
    return tile_op(name, fn, (bsz, t // TOKEN_TILE), args, outs)


def c_prep(name, cq, cff, cfb, clb, layer):
    bsz, nh, t, d = cq.shape
    depth = clb.shape[1]
    spec = _heads_spec(nh, d)

    def fn(ids, q, zf, zb, lbs):
        lb = [lbs[:, j] for j in range(depth)]
        m = lb[0]
        for j in range(1, depth):
            m = jnp.maximum(m, lb[j])
        e = [jnp.exp(lb[j] - m) for j in range(depth)]
        tot = e[0]
        for j in range(1, depth):
            tot = tot + e[j]
        p = [ej / tot for ej in e]
        cum = p[0]
        for j in range(1, layer + 1):
            cum = cum + p[j]
        lower = cum - p[0]

        def gate(z, lo):
            f = lo + (1.0 - lo) * _sigmoid(z)
            return jnp.log(jnp.maximum(f, F_TINY)), (1.0 - lo) * _sigmoid(-z)

        gf, kf = gate(zf, lower[:, 0:1])
        gb, kb = gate(zb, lower[:, 1:2])
        return _silu(q), kf, gf, kb, gb

    args = [Arg(cq, *spec, "tile"), Arg(cff, *spec, "tile"), Arg(cfb, *spec, "tile"),
            Arg(clb, clb.shape, lambda b, i: (0, 0, 0, 0), "acc", _zero_ids)]
    return tile_op(name, fn, (bsz, t // TOKEN_TILE), args, [Out(cq.shape, *spec)] * 5)


def c_readout(name, o_f, o_b, gate, gain):
    bsz, nh, t, d = o_f.shape
    spec = _heads_spec(nh, d)

    def fn(ids, of, ob, gt, gn):
        return (_rms(of + ob, gn) * _silu(gt),)

    args = [Arg(o_f, *spec, "tile"), Arg(o_b, *spec, "tile"), Arg(gate, *spec, "tile"),
            Arg(gain, (1, d), lambda b, i: (0, 0), "acc", _zero_ids)]
    return tile_op(name, fn, (bsz, t // TOKEN_TILE), args, [Out(o_f.shape, *spec)])[0]


def attention(name, qs, kts, vt, scale, fold_scale, tq_fwd, tq_bwd, n_lat):
    bsz, hk, grp, t, _ = qs[0].shape
    dv = vt.shape[2]
    n_parts = len(qs)
    shared = [kt.shape[1] == 1 for kt in kts]
    kt_specs = [pl.BlockSpec((None, None, kt.shape[2], t), (lambda b, h, i: (b, 0, 0, 0)) if sh else (lambda b, h, i: (b, h, 0, 0)))
                for kt, sh in zip(kts, shared)]
    vt_spec = pl.BlockSpec((None, None, dv, t), lambda b, h, i: (b, h, 0, 0))
    o_shape = jax.ShapeDtypeStruct((bsz, hk, grp, t, dv), F32)
    lse_shape = jax.ShapeDtypeStruct((bsz, hk, grp, t, LANE), F32)

    def plan(tq_):
        per_head = lambda w: pl.BlockSpec((None, None, grp, tq_, w), lambda b, h, i: (b, h, 0, i, 0))
        return grp * tq_, [per_head(q.shape[-1]) for q in qs], per_head(dv), per_head(LANE), (bsz, hk, t // tq_)

    def load_q(q_refs, rows):
        qq = []
        for q_ref in q_refs:
            q2 = q_ref[...].reshape(rows, q_ref.shape[-1])
            qq.append((q2 * scale if fold_scale else q2).astype(BF16))
        return qq

    def scores(qq, kts_lo):
        s = None
        for q2, kt in zip(qq, kts_lo):
            part = lax.dot_general(q2, kt, _NN, preferred_element_type=F32)
            s = part if s is None else s + part
        return s if fold_scale else s * scale

    ranges = ((lambda i, tq: i * tq < n_lat, 0), (lambda i, tq: i * tq >= n_lat, n_lat))

    def fwd_call(*arrays):
        tq = tq_fwd
        rows, q_specs, o_spec, lse_spec, grid = plan(tq)

        def body(*refs):
            q_refs, kt_refs, vt_ref = refs[:n_parts], refs[n_parts:2 * n_parts], refs[2 * n_parts]
            o_ref, lse_ref = refs[2 * n_parts + 1:]
            i = pl.program_id(2)

            def tile(col0):
                qq = load_q(q_refs, rows)
                kts_lo = [r[:, col0:] for r in kt_refs]
                vt_lo = vt_ref[:, col0:]
                n_grp = ATT_FWD_ROW_GROUPS
                gr = rows // n_grp
                outs, lses = [], []
                for r in range(n_grp):
                    s = scores([q2[r * gr:(r + 1) * gr] for q2 in qq], kts_lo)
                    m = jnp.max(s, axis=-1, keepdims=True)
                    e = jnp.exp(s - m)
                    l = jnp.sum(e, axis=-1, keepdims=True)
                    outs.append(lax.dot_general(e.astype(BF16), vt_lo, _NT, preferred_element_type=F32) * (1.0 / l))
                    lses.append(jnp.broadcast_to(m + jnp.log(l), (gr, LANE)))
                o_ref[...] = jnp.concatenate(outs, axis=0).reshape(grp, tq, dv)
                lse_ref[...] = jnp.concatenate(lses, axis=0).reshape(grp, tq, LANE)

            for cond, col0 in ranges:
                pl.when(cond(i, tq))(functools.partial(tile, col0))

        return pl.pallas_call(body, grid=grid, in_specs=q_specs + kt_specs + [vt_spec], out_specs=[o_spec, lse_spec],
                              out_shape=[o_shape, lse_shape], name=name + "_fwd", compiler_params=_cparams())(*arrays)

    def bwd_call(arrays, o, lse, do):
        tq = tq_bwd
        rows, q_specs, o_spec, lse_spec, grid = plan(tq)

        def body(*refs):
            q_refs, kt_refs, vt_ref = refs[:n_parts], refs[n_parts:2 * n_parts], refs[2 * n_parts]
            o_ref, lse_ref, do_ref = refs[2 * n_parts + 1:2 * n_parts + 4]
            d_refs = refs[2 * n_parts + 4:]
            dq_refs, dkt_refs, dvt_ref = d_refs[:n_parts], d_refs[n_parts:2 * n_parts], d_refs[2 * n_parts]
            h, i = pl.program_id(1), pl.program_id(2)

            def tile(col0):
                qq = load_q(q_refs, rows)
                kts_lo = [r[:, col0:] for r in kt_refs]
                p = jnp.exp(scores(qq, kts_lo) - lse_ref[...].reshape(rows, LANE)[:, 0:1])
                do2 = do_ref[...].reshape(rows, dv)
                delta = jnp.sum(do2 * o_ref[...].reshape(rows, dv), axis=-1, keepdims=True)
                do_lo = do2.astype(BF16)
                dp = lax.dot_general(do_lo, vt_ref[:, col0:], _NN, preferred_element_type=F32)
                ds = p * (dp - delta)
                if not fold_scale:
                    ds = ds * scale
                ds_lo = ds.astype(BF16)

                def accumulate(ref, val, first):
                    if col0:
                        ref[:, col0:] += val
                        return

                    @pl.when(first)
                    def _():
                        ref[...] = val

                    @pl.when(jnp.logical_not(first))
                    def _():
                        ref[...] += val

                accumulate(dvt_ref, lax.dot_general(do_lo, p.astype(BF16), _TN, preferred_element_type=F32), i == 0)
                for q2, kt, dq_ref, dkt_ref, sh in zip(qq, kts_lo, dq_refs, dkt_refs, shared):
                    dq = lax.dot_general(ds_lo, kt, _NT, preferred_element_type=F32)
                    dq_ref[...] = (dq * scale if fold_scale else dq).reshape(dq_ref.shape)
                    first = jnp.logical_and(h == 0, i == 0) if sh else i == 0
                    accumulate(dkt_ref, lax.dot_general(q2, ds_lo, _TN, preferred_element_type=F32), first)

            for cond, col0 in ranges:
                pl.when(cond(i, tq))(functools.partial(tile, col0))

        in_specs = q_specs + kt_specs + [vt_spec, o_spec, lse_spec, o_spec]
        d_shape = [jax.ShapeDtypeStruct(a.shape, F32) for a in arrays]
        return pl.pallas_call(body, grid=grid, in_specs=in_specs, out_specs=q_specs + kt_specs + [vt_spec], out_shape=d_shape,
                              name=name + "_bwd", compiler_params=_cparams())(*arrays, o, lse, do)

    def lo(arrays):
        return arrays[:n_parts] + tuple(a.astype(BF16) for a in arrays[n_parts:])

    @jax.custom_vjp
    def op(*arrays):
        return fwd_call(*lo(arrays))[0]

    def op_fwd(*arrays):
        low = lo(arrays)
        o, lse = fwd_call(*low)
        return o, (low, o, lse)

    def op_bwd(res, do):
        low, o, lse = res
        return tuple(bwd_call(low, o, lse, do))

    op.defvjp(op_fwd, op_bwd)
    return op(*qs, *kts, vt)


def _att_rows_fwd(qq, kts_lo, vt_lo, scale, fold_scale, n_grp):
    rows = qq[0].shape[0]
    gr = rows // n_grp
    outs, lses = [], []
    for r in range(n_grp):
        s = None
        for q2, kt in zip(qq, kts_lo):
            part = lax.dot_general(q2[r * gr:(r + 1) * gr], kt, _NN, preferred_element_type=F32)
            s = part if s is None else s + part
        if not fold_scale:
            s = s * scale
        m = jnp.max(s, axis=-1, keepdims=True)
        e = jnp.exp(s - m)
        l = jnp.sum(e, axis=-1, keepdims=True)
        outs.append(lax.dot_general(e.astype(BF16), vt_lo, _NT, preferred_element_type=F32) * (1.0 / l))
        lses.append(m + jnp.log(l))
    return jnp.concatenate(outs, axis=0), jnp.concatenate(lses, axis=0)


def _att_rows_bwd(qq, kts_lo, vt_lo, o2, lse, do2, scale, fold_scale):
    s = None
    for q2, kt in zip(qq, kts_lo):
        part = lax.dot_general(q2, kt, _NN, preferred_element_type=F32)
        s = part if s is None else s + part
    if not fold_scale:
        s = s * scale
    p = jnp.exp(s - lse)
    delta = jnp.sum(do2 * o2, axis=-1, keepdims=True)
    do_lo = do2.astype(BF16)
    ds = p * (lax.dot_general(do_lo, vt_lo, _NN, preferred_element_type=F32) - delta)
    if not fold_scale:
        ds = ds * scale
    ds_lo = ds.astype(BF16)
    dvt = lax.dot_general(do_lo, p.astype(BF16), _TN, preferred_element_type=F32)
    dqs, dkts = [], []
    for q2, kt in zip(qq, kts_lo):
        dq = lax.dot_general(ds_lo, kt, _NT, preferred_element_type=F32)
        dqs.append(dq * scale if fold_scale else dq)
        dkts.append(lax.dot_general(q2, ds_lo, _TN, preferred_element_type=F32))
    return dqs, dkts, dvt


def mix_pair(name, qs, kts, vt, scale, fold_scale, sub_fwd, sub_bwd, n_lat, sq, sk, sv, sg, reverse, n_lat_chunks):
    bsz, hk, grp, t, _ = qs[0].shape
    dv = vt.shape[2]
    n_parts = len(qs)
    nh, d = sq.shape[1], sq.shape[3]
    c = SCAN_CHUNK
    n = t // c
    tph = n // hk
    tq = t // tph
    assert tph * hk == n and tq * tph == t and tq % sub_fwd == 0 and tq % sub_bwd == 0 and n_lat % tq == 0
    chains = [(b, h) for b in range(bsz) for h in range(nh)]
    widths = [q.shape[-1] for q in qs]

    def chunk_of(j):
        return (n - 1 - j) if reverse else lax.rem(j + n_lat_chunks, n)

    head_of = lambda j: j // tph
    tile_of = lambda j: lax.rem(j, tph)
    per_tile = lambda w: pl.BlockSpec((bsz, None, grp, tq, w), lambda j: (0, head_of(j), 0, tile_of(j), 0))
    per_head = lambda w: pl.BlockSpec((bsz, None, w, t), lambda j: (0, head_of(j), 0, 0))
    q_specs = [per_tile(w) for w in widths]
    kt_specs = [per_head(w) for w in widths]
    att_in = q_specs + kt_specs + [per_head(dv)]
    o_shape = jax.ShapeDtypeStruct((bsz, hk, grp, t, dv), F32)
    lse_shape = jax.ShapeDtypeStruct((bsz, hk, grp, t, LANE), F32)
    st_shape = jax.ShapeDtypeStruct((bsz, nh, n, d, d), F32)
    ranges = ((lambda j: tile_of(j) * tq < n_lat, 0), (lambda j: tile_of(j) * tq >= n_lat, n_lat))

    def load_q(q_refs, b, r0, sub):
        qq = []
        for q_ref, w in zip(q_refs, widths):
            q2 = q_ref[b, :, r0:r0 + sub, :].reshape(grp * sub, w)
            qq.append((q2 * scale if fold_scale else q2).astype(BF16))
        return qq

    def fwd_call(*arrays):
        def body(*refs):
            q_refs, kt_refs, vt_ref = refs[:n_parts], refs[n_parts:2 * n_parts], refs[2 * n_parts]
            sq_ref, sk_ref, sv_ref, sg_ref = refs[2 * n_parts + 1:2 * n_parts + 5]
            o_ref, lse_ref, so_ref, states_ref, st = refs[2 * n_parts + 5:]
            j = pl.program_id(0)

            @pl.when(j == 0)
            def _():
                st[...] = jnp.zeros_like(st)

            def step(col0):
                for bh in chains:
                    s = st[bh]
                    states_ref[bh] = s
                    ns, o = _gla_step(s, sq_ref[bh], sk_ref[bh], sv_ref[bh], sg_ref[bh], reverse)
                    st[bh] = ns
                    so_ref[bh] = o
                for b in range(bsz):
                    kts_lo = [r[b, :, col0:] for r in kt_refs]
                    vt_lo = vt_ref[b, :, col0:]
                    for r0 in range(0, tq, sub_fwd):
                        o, lse = _att_rows_fwd(load_q(q_refs, b, r0, sub_fwd), kts_lo, vt_lo, scale, fold_scale, ATT_FWD_ROW_GROUPS)
                        o_ref[b, :, r0:r0 + sub_fwd, :] = o.reshape(grp, sub_fwd, dv)
                        lse_ref[b, :, r0:r0 + sub_fwd, :] = jnp.broadcast_to(lse, (grp * sub_fwd, LANE)).reshape(grp, sub_fwd, LANE)

            for cond, col0 in ranges:
                pl.when(cond(j))(functools.partial(step, col0))

        scan_spec = pl.BlockSpec((bsz, nh, c, d), lambda j: (0, 0, chunk_of(j), 0))
        return pl.pallas_call(
            body, grid=(n,), in_specs=att_in + [scan_spec] * 4,
            out_specs=[per_tile(dv), per_tile(LANE), scan_spec, pl.BlockSpec((bsz, nh, None, d, d), lambda j: (0, 0, j, 0, 0))],
            out_shape=[o_shape, lse_shape, jax.ShapeDtypeStruct(sq.shape, F32), st_shape],
            scratch_shapes=[pltpu.VMEM((bsz, nh, d, d), F32)], name=name + "_fwd", compiler_params=_cparams())(*arrays)

    def bwd_call(att_lo, scan_in, o, lse, states, do, dso):
        def body(*refs):
            q_refs, kt_refs, vt_ref = refs[:n_parts], refs[n_parts:2 * n_parts], refs[2 * n_parts]
            k0 = 2 * n_parts + 1
            sq_ref, sk_ref, sv_ref, sg_ref, s_ref = refs[k0:k0 + 5]
            o_ref, lse_ref, do_ref, dso_ref = refs[k0 + 5:k0 + 9]
            d_refs = refs[k0 + 9:]
            dq_refs, dkt_refs, dvt_ref = d_refs[:n_parts], d_refs[n_parts:2 * n_parts], d_refs[2 * n_parts]
            dsq_ref, dsk_ref, dsv_ref, dsg_ref, dst = d_refs[2 * n_parts + 1:]
            jj = pl.program_id(0)

            @pl.when(jj == 0)
            def _():
                dst[...] = jnp.zeros_like(dst)

            @pl.when(tile_of(jj) == 0)
            def _():
                for ref in list(dkt_refs) + [dvt_ref]:
                    ref[...] = jnp.zeros_like(ref)

            def step(col0):
                chunk_step = functools.partial(_gla_step, reverse=reverse)
                for bh in chains:
                    _, vjp = jax.vjp(chunk_step, s_ref[bh], sq_ref[bh], sk_ref[bh], sv_ref[bh], sg_ref[bh])
                    ds, dq, dk, dv_, dg = vjp((dst[bh], dso_ref[bh]))
                    dst[bh] = ds
                    dsq_ref[bh] = dq
                    dsk_ref[bh] = dk
                    dsv_ref[bh] = dv_
                    dsg_ref[bh] = dg
                for b in range(bsz):
                    kts_lo = [r[b, :, col0:] for r in kt_refs]
                    vt_lo = vt_ref[b, :, col0:]
                    for r0 in range(0, tq, sub_bwd):
                        rows = grp * sub_bwd
                        rsl = slice(r0, r0 + sub_bwd)
                        dqs, dkts, dvt = _att_rows_bwd(
                            load_q(q_refs, b, r0, sub_bwd), kts_lo, vt_lo, o_ref[b, :, rsl, :].reshape(rows, dv),
                            lse_ref[b, :, rsl, :].reshape(rows, LANE)[:, 0:1], do_ref[b, :, rsl, :].reshape(rows, dv), scale, fold_scale)
                        for dq_ref, dq, w in zip(dq_refs, dqs, widths):
                            dq_ref[b, :, rsl, :] = dq.reshape(grp, sub_bwd, w)
                        for dkt_ref, dkt in zip(dkt_refs, dkts):
                            dkt_ref[b, :, col0:] += dkt
                        dvt_ref[b, :, col0:] += dvt

            for cond, col0 in ranges:
                pl.when(cond(jj))(functools.partial(step, col0))

        scan_spec = pl.BlockSpec((bsz, nh, c, d), lambda jj: (0, 0, chunk_of(n - 1 - jj), 0))
        st_spec = pl.BlockSpec((bsz, nh, None, d, d), lambda jj: (0, 0, n - 1 - jj, 0, 0))
        in_specs = att_in + [scan_spec] * 4 + [st_spec, per_tile(dv), per_tile(LANE), per_tile(dv), scan_spec]
        d_shape = [jax.ShapeDtypeStruct(a.shape, F32) for a in att_lo] + [jax.ShapeDtypeStruct(sq.shape, F32)] * 4
        return pl.pallas_call(body, grid=(n,), in_specs=in_specs, out_specs=att_in + [scan_spec] * 4, out_shape=d_shape,
                              scratch_shapes=[pltpu.VMEM((bsz, nh, d, d), F32)], name=name + "_bwd",
                              compiler_params=_cparams())(*att_lo, *scan_in, states, o, lse, do, dso)

    n_att = 2 * n_parts + 1

    def lo(arrays):
        return arrays[:n_parts] + tuple(a.astype(BF16) for a in arrays[n_parts:n_att])

    @jax.custom_vjp
    def op(*arrays):
        res = fwd_call(*lo(arrays), *arrays[n_att:])
        return res[0], res[2]

    def op_fwd(*arrays):
        att_lo, scan_in = lo(arrays), arrays[n_att:]
        o, lse, so, states = fwd_call(*att_lo, *scan_in)
        return (o, so), (att_lo, scan_in, o, lse, states)

    def op_bwd(res, cts):
        att_lo, scan_in, o, lse, states = res
        return tuple(bwd_call(att_lo, scan_in, o, lse, states, cts[0], cts[1]))

    op.defvjp(op_fwd, op_bwd)
    return op(*qs, *kts, vt, sq, sk, sv, sg)


def ada_op(cc, w, b):
    depth, d, n = w.shape

    def fn(ids, ccb, wb, bb):
        return (_dot_bf16(_silu(ccb), wb) + bb,)

    args = [Arg(cc, cc.shape, lambda l: (0, 0), "acc", lambda ids: ids[0] == 0),
            Arg(w, (None, d, n), lambda l: (l, 0, 0), "tile"), Arg(b, (None, 1, n), lambda l: (l, 0, 0), "tile")]
    return tile_op("ada", fn, (depth,), args, [Out((depth, cc.shape[0], n), (None, cc.shape[0], n), lambda l: (l, 0, 0))])[0]


def loss_op(xu, target, n_lat_tiles):
    bsz, t, d = xu.shape

    def body(x_ref, t_ref, dx_ref, l_ref):
        b, i = pl.program_id(0), pl.program_id(1)

        @pl.when(jnp.logical_and(b == 0, i == 0))
        def _():
            l_ref[...] = jnp.zeros_like(l_ref)

        @pl.when(i < n_lat_tiles)
        def _():
            err = x_ref[...] - t_ref[...]
            dx_ref[...] = err * (1.0 / d)
            l_ref[...] += 0.5 * jnp.sum(jnp.mean(err * err, axis=-1))

        @pl.when(i >= n_lat_tiles)
        def _():
            dx_ref[...] = jnp.zeros_like(dx_ref)

    row = pl.BlockSpec((None, TOKEN_TILE, d), lambda b, i: (b, i, 0))
    t_spec = pl.BlockSpec((None, TOKEN_TILE, d), lambda b, i: (b, jnp.minimum(i, n_lat_tiles - 1), 0))
    return pl.pallas_call(body, grid=(bsz, t // TOKEN_TILE), in_specs=[row, t_spec],
                          out_specs=[row, pl.BlockSpec((SUBLANE, LANE), lambda b, i: (0, 0))],
                          out_shape=[jax.ShapeDtypeStruct(xu.shape, F32), jax.ShapeDtypeStruct((SUBLANE, LANE), F32)],
                          name="loss", compiler_params=_cparams())(xu, target)


def _row_tile(rows, row_bytes, budget=4 << 20, step=SUBLANE):
    if rows * row_bytes <= budget:
        return rows
    best = None
    for t in range(step, rows, step):
        if rows % t == 0 and t * row_bytes <= budget:
            best = t
    return best if best is not None else rows


def _as3d(x):
    p = x.shape[0]
    c = x.shape[-1]
    return x.reshape(p, -1, c)


def sum_parts(name, x):
    x3 = _as3d(x)
    p, r, c = x3.shape
    tr = _row_tile(r, p * c * 4, step=SUBLANE * (4 // x.dtype.itemsize))

    def body(x_ref, o_ref):
        s = x_ref[0].astype(F32)
        for j in range(1, p):
            s = s + x_ref[j].astype(F32)
        o_ref[...] = s

    out = pl.pallas_call(body, grid=(r // tr,), in_specs=[pl.BlockSpec((p, tr, c), lambda i: (0, i, 0))],
                         out_specs=pl.BlockSpec((tr, c), lambda i: (i, 0)), out_shape=jax.ShapeDtypeStruct((r, c), F32),
                         name=name, compiler_params=_cparams())(x3)
    return out.reshape(x.shape[1:])


def adamw(name, w, g, m, v):
    shape = w.shape
    c = shape[-1]
    to2d = lambda a: a.reshape(-1, c)
    r = to2d(w).shape[0]
    tr = _row_tile(r, 7 * c * 4, budget=6 << 20)

    def body(w_ref, g_ref, m_ref, v_ref, d_ref, nm_ref, nv_ref):
        gg = g_ref[...]
        nm = ADAM_B1 * m_ref[...] + (1.0 - ADAM_B1) * gg
        nv = ADAM_B2 * v_ref[...] + (1.0 - ADAM_B2) * jnp.square(gg)
        m_hat = nm / (1.0 - ADAM_B1 ** ADAM_STEP)
        v_hat = nv / (1.0 - ADAM_B2 ** ADAM_STEP)
        d_ref[...] = -ADAM_LR * (m_hat / (jnp.sqrt(v_hat) + ADAM_EPS) + ADAM_WD * w_ref[...])
        nm_ref[...] = nm
        nv_ref[...] = nv

    spec = pl.BlockSpec((tr, c), lambda i: (i, 0))
    outs = pl.pallas_call(body, grid=(r // tr,), in_specs=[spec] * 4, out_specs=[spec] * 3,
                          out_shape=[jax.ShapeDtypeStruct((r, c), F32)] * 3, name=name, compiler_params=_cparams())(
        to2d(w), to2d(g), to2d(m), to2d(v))
    return tuple(o.reshape(shape) for o in outs)


def exchange(name, srcs, group, same):
    p = 2 ** len(group)
    n = len(srcs)
    out_shape = [jax.ShapeDtypeStruct(((p,) + s.shape) if same else s.shape, s.dtype) for s in srcs]

    def index_of(coords):
        idx = 0
        for a in group:
            idx = idx * 2 + coords[a]
        return idx

    def body(*refs):
        src_refs, out_refs = refs[:n], refs[n:2 * n]
        send_sems, recv_sems = refs[2 * n:]
        pos = {a: lax.axis_index(a) for a in MESH_AXES}
        me = index_of(pos)
        peers = []
        for rel in range(1, p):
            coords = dict(pos)
            for bit, a in enumerate(reversed(group)):
                if (rel >> bit) & 1:
                    coords[a] = 1 - coords[a]
            peers.append((coords, index_of(coords)))

        def src_for(a, idx):
            return src_refs[a] if same else src_refs[a].at[idx]

        sends, recvs = [], []
        for a in range(n):
            for r, (coords, idx) in enumerate(peers):
                dev = tuple(coords[ax] for ax in MESH_AXES)
                send = pltpu.make_async_remote_copy(src_ref=src_for(a, idx), dst_ref=out_refs[a].at[me],
                                                    send_sem=send_sems.at[a, r], recv_sem=recv_sems.at[a, r],
                                                    device_id=dev, device_id_type=pl.DeviceIdType.MESH)
                send.start()
                sends.append(send)
                recvs.append(pltpu.make_async_remote_copy(src_ref=src_for(a, idx), dst_ref=out_refs[a].at[idx],
                                                          send_sem=send_sems.at[a, r], recv_sem=recv_sems.at[a, r],
                                                          device_id=dev, device_id_type=pl.DeviceIdType.MESH))
        for cp in sends:
            cp.wait_send()
        for cp in recvs:
            cp.wait_recv()

    any_spec = pl.BlockSpec(memory_space=pl.ANY)
    outs = pl.pallas_call(body, in_specs=[any_spec] * n, out_specs=[any_spec] * n, out_shape=out_shape,
                          scratch_shapes=[pltpu.SemaphoreType.DMA((n, p - 1)), pltpu.SemaphoreType.DMA((n, p - 1))],
                          name=name)(*srcs)
    me = index_of({a: lax.axis_index(a) for a in MESH_AXES})
    own = [s if same else lax.dynamic_index_in_dim(s, me, axis=0, keepdims=False) for s in srcs]
    return [lax.dynamic_update_index_in_dim(o, w, me, axis=0) for o, w in zip(outs, own)]


def _rope_tables(n_lat, n_ctx, rot_dim, heads):
    n_freq = rot_dim // 4
    tok = jnp.arange(n_lat, dtype=jnp.int32)
    inv = ROPE_THETA ** (-jnp.arange(n_freq, dtype=F32) / n_freq)
    ang = jnp.concatenate([(tok // GRID_W).astype(F32)[:, None] * inv, (tok % GRID_W).astype(F32)[:, None] * inv], axis=-1)
    cos, sin = jnp.cos(ang), jnp.sin(ang)
    cos = jnp.concatenate([jnp.concatenate([cos, cos], -1), jnp.ones((n_ctx, rot_dim), F32)], 0)
    sin = jnp.concatenate([jnp.concatenate([-sin, sin], -1), jnp.zeros((n_ctx, rot_dim), F32)], 0)
    half = rot_dim // 2
    w = heads * rot_dim
    j = np.arange(w)
    src = (j // rot_dim) * rot_dim + (j % rot_dim + half) % rot_dim
    swap = np.zeros((w, w), np.float32)
    swap[src, j] = 1.0
    return jnp.tile(cos, (1, heads)), jnp.tile(sin, (1, heads)), jnp.asarray(swap)


def _to_heads(a, nh):
    b, t, _ = a.shape
    return a.reshape(b, t, nh, -1).transpose(0, 2, 1, 3)


def _to_heads_t(a, nh):
    b, t, _ = a.shape
    return a.reshape(b, t, nh, -1).transpose(0, 2, 3, 1)


def _from_heads(a):
    b, nh, t, w = a.shape
    return a.transpose(0, 2, 1, 3).reshape(b, t, nh * w)


def _pad_w_in(w_in):
    parts, off = [], 0
    for size, pad in zip(IN_SIZES, IN_PAD):
        parts.append(w_in[:, off:off + size])
        if pad > size:
            parts.append(jnp.zeros((w_in.shape[0], pad - size), w_in.dtype))
        off += size
    return jnp.concatenate(parts, axis=1)


def _forward(xu, mods, prm, prm_lo, n_lat, n_ctx):
    bsz, t, d = xu.shape
    depth = mods.shape[0]
    n_lat_tiles = n_lat // TOKEN_TILE
    rope_a = _rope_tables(n_lat, n_ctx, HEAD_DIM, 1)
    rope_bq = _rope_tables(n_lat, n_ctx, B_ROPE, B_HEADS)
    rope_bk = _rope_tables(n_lat, n_ctx, B_ROPE, 1)
    clb = prm["c_lower_bounds"].reshape(depth, 2, C_HEADS, C_DK).transpose(2, 0, 1, 3)
    for l in range(depth):
        nm = lambda s: f"l{l}_{s}"
        vec = lambda name: prm[name][l][None, :]
        h = premod(nm("premix"), xu, vec("g_pre_mix"), mods[l], 0, n_lat_tiles)
        z = linear(nm("w_in"), h.reshape(bsz * t, d), _pad_w_in(prm["w_in"][l]), _pad_w_in(prm_lo["w_in"][l])).reshape(bsz, t, D_IN_PAD)
        seg = lambda i: z[:, :, IN_OFF[i]:IN_OFF[i] + IN_SIZES[i]]
        aq = a_prep(nm("aq"), _to_heads(seg(0), A_HEADS), vec("a_q_norm"), *rope_a)
        ak = a_prep(nm("ak"), _to_heads(seg(1), A_KV_HEADS), vec("a_k_norm"), *rope_a)
        av = _to_heads(seg(2), A_KV_HEADS)
        cq, kf, gf, kb, gb = c_prep(nm("c_prep"), _to_heads(seg(6), C_HEADS), _to_heads(seg(7), C_HEADS),
                                    _to_heads(seg(8), C_HEADS), clb, l)
        cv = _to_heads(seg(9), C_HEADS)
        ya, o_f = mix_pair(nm("mix_a"), [aq.reshape(bsz, A_KV_HEADS, A_GROUP, t, HEAD_DIM)], [ak.transpose(0, 1, 3, 2)],
                           av.transpose(0, 1, 3, 2), HEAD_DIM ** -0.5, True, ATT_A_TQ_FWD, ATT_A_TQ, n_lat,
                           cq, kf, cv, gf, False, n_lat // SCAN_CHUNK)
        ya = _from_heads(ya.reshape(bsz, A_HEADS, t, HEAD_DIM))
        wq = prm["w_q_up"][l].reshape(B_Q_RANK, B_HEADS, B_NOPE + B_ROPE)
        wkv = prm["w_kv_up"][l].reshape(B_KV_RANK, B_HEADS, B_NOPE + B_V)
        qn, qp, kn, bv, kp = b_prep(
            nm("b_prep"), seg(3), seg(4), seg(5), vec("b_q_norm"),
            wq[:, :, :B_NOPE].reshape(B_Q_RANK, -1), wq[:, :, B_NOPE:].reshape(B_Q_RANK, -1), vec("b_kv_norm"),
            wkv[:, :, :B_NOPE].reshape(B_KV_RANK, -1), wkv[:, :, B_NOPE:].reshape(B_KV_RANK, -1), *rope_bq, *rope_bk)
        qb = jnp.concatenate([_to_heads(qn, B_HEADS), _to_heads(qp, B_HEADS)], axis=-1)[:, :, None]
        kbt = jnp.concatenate([_to_heads_t(kn, B_HEADS),
                               jnp.broadcast_to(kp.transpose(0, 2, 1)[:, None], (bsz, B_HEADS, B_ROPE, t))], axis=2)
        yb, o_b = mix_pair(nm("mix_b"), [qb], [kbt], _to_heads_t(bv, B_HEADS), (B_NOPE + B_ROPE) ** -0.5, False, ATT_B_TQ,
                           ATT_B_TQ, n_lat, cq, kb, cv, gb, True, n_lat // SCAN_CHUNK)
        yb = _from_heads(yb[:, :, 0])
        yc = _from_heads(c_readout(nm("c_out"), o_f, o_b, _to_heads(seg(10), C_HEADS), vec("c_out_norm")))
        y = linear(nm("w_out"), jnp.concatenate([ya, yb, yc], axis=-1).reshape(bsz * t, d), prm["w_out"][l],
                   prm_lo["w_out"][l]).reshape(bsz, t, d)
        x1 = resid(nm("res_mix"), xu, y, vec("g_post_mix"), mods[l], 2, n_lat_tiles)
        h2 = premod(nm("preffn"), x1, vec("g_pre_ffn"), mods[l], 3, n_lat_tiles)
        f = mlp(nm("mlp"), h2.reshape(bsz * t, d), prm["w_ff1"][l], prm["w_ff2"][l], prm_lo["w_ff1"][l],
                prm_lo["w_ff2"][l]).reshape(bsz, t, d)
        xu = resid(nm("res_ffn"), x1, f, vec("g_post_ffn"), mods[l], 5, n_lat_tiles)
    return xu


BIG = {"w_in": 2, "w_q_up": 2, "w_kv_up": 2, "w_out": 1, "w_ff1": 2, "w_ff2": 1}
SMALL = ("g_pre_mix", "g_post_mix", "g_pre_ffn", "g_post_ffn", "a_q_norm", "a_k_norm", "b_q_norm", "b_kv_norm",
         "c_lower_bounds", "c_out_norm")
WEIGHTS = ("c_ctx", "w_ada", "b_ada", "g_pre_mix", "g_post_mix", "g_pre_ffn", "g_post_ffn", "w_in", "a_q_norm", "a_k_norm",
           "b_q_norm", "w_q_up", "b_kv_norm", "w_kv_up", "c_lower_bounds", "c_out_norm", "w_out", "w_ff1", "w_ff2")


def _unshard(g, axis):
    depth, _, r, c = g.shape
    if axis == 1:
        return g.reshape(depth, N_CHIP * r, c)
    return g.transpose(0, 2, 1, 3).reshape(depth, r, N_CHIP * c)


def _shard_major(g, axis):
    depth, r, c = g.shape
    if axis == 1:
        return g.reshape(depth, N_CHIP, r // N_CHIP, c)
    return g.reshape(depth, r, N_CHIP, c // N_CHIP).transpose(0, 2, 1, 3)


def kernel(x, c, ctx, c_ctx, w_ada, b_ada, g_pre_mix, g_post_mix, g_pre_ffn, g_post_ffn, w_in, a_q_norm, a_k_norm, b_q_norm, w_q_up, b_kv_norm, w_kv_up, c_lower_bounds, c_out_norm, w_out, w_ff1, w_ff2, loss_target, m_c_ctx, m_w_ada, m_b_ada, m_g_pre_mix, m_g_post_mix, m_g_pre_ffn, m_g_post_ffn, m_w_in, m_a_q_norm, m_a_k_norm, m_b_q_norm, m_w_q_up, m_b_kv_norm, m_w_kv_up, m_c_lower_bounds, m_c_out_norm, m_w_out, m_w_ff1, m_w_ff2, v_c_ctx, v_w_ada, v_b_ada, v_g_pre_mix, v_g_post_mix, v_g_pre_ffn, v_g_post_ffn, v_w_in, v_a_q_norm, v_a_k_norm, v_b_q_norm, v_w_q_up, v_b_kv_norm, v_w_kv_up, v_c_lower_bounds, v_c_out_norm, v_w_out, v_w_ff1, v_w_ff2):
    local = dict(c_ctx=c_ctx, w_ada=w_ada, b_ada=b_ada, g_pre_mix=g_pre_mix, g_post_mix=g_post_mix, g_pre_ffn=g_pre_ffn,
                 g_post_ffn=g_post_ffn, w_in=w_in, a_q_norm=a_q_norm, a_k_norm=a_k_norm, b_q_norm=b_q_norm, w_q_up=w_q_up,
                 b_kv_norm=b_kv_norm, w_kv_up=w_kv_up, c_lower_bounds=c_lower_bounds, c_out_norm=c_out_norm, w_out=w_out,
                 w_ff1=w_ff1, w_ff2=w_ff2)
    mom = dict(c_ctx=m_c_ctx, w_ada=m_w_ada, b_ada=m_b_ada, g_pre_mix=m_g_pre_mix, g_post_mix=m_g_post_mix,
               g_pre_ffn=m_g_pre_ffn, g_post_ffn=m_g_post_ffn, w_in=m_w_in, a_q_norm=m_a_q_norm, a_k_norm=m_a_k_norm,
               b_q_norm=m_b_q_norm, w_q_up=m_w_q_up, b_kv_norm=m_b_kv_norm, w_kv_up=m_w_kv_up,
               c_lower_bounds=m_c_lower_bounds, c_out_norm=m_c_out_norm, w_out=m_w_out, w_ff1=m_w_ff1, w_ff2=m_w_ff2)
    var = dict(c_ctx=v_c_ctx, w_ada=v_w_ada, b_ada=v_b_ada, g_pre_mix=v_g_pre_mix, g_post_mix=v_g_post_mix,
               g_pre_ffn=v_g_pre_ffn, g_post_ffn=v_g_post_ffn, w_in=v_w_in, a_q_norm=v_a_q_norm, a_k_norm=v_a_k_norm,
               b_q_norm=v_b_q_norm, w_q_up=v_w_q_up, b_kv_norm=v_b_kv_norm, w_kv_up=v_w_kv_up,
               c_lower_bounds=v_c_lower_bounds, c_out_norm=v_c_out_norm, w_out=v_w_out, w_ff1=v_w_ff1, w_ff2=v_w_ff2)

    bsz, n_lat, d = x.shape
    n_ctx = ctx.shape[1]
    depth = w_ada.shape[0]
    assert depth == 2 and n_lat % TOKEN_TILE == 0 and n_ctx % TOKEN_TILE == 0 and bsz * N_DEV + 1 <= ADA_ROWS
    ax, ay, ac = (lax.axis_index(a) for a in MESH_AXES)
    chip = 2 * ax + ay
    dev = 2 * chip + ac

    c_all = exchange("gather_c", [c], MESH_AXES, True)[0].reshape(N_DEV * bsz, d)
    big_names = list(BIG)
    mine = [lax.dynamic_index_in_dim(local[n].astype(BF16), ac, axis=0, keepdims=False) for n in big_names]
    over_chips = exchange("gather_w_quad", mine + [c_lower_bounds], ("x", "y"), True)
    both_layers = exchange("gather_w_pair", over_chips[:-1], ("c",), True)
    prm_lo = {n: _unshard(g, BIG[n]) for n, g in zip(big_names, both_layers)}
    prm = {n: w.astype(F32) for n, w in prm_lo.items()}
    prm["c_lower_bounds"] = over_chips[-1].transpose(1, 2, 0, 3).reshape(depth, 2, -1)
    for n in SMALL:
        if n != "c_lower_bounds":
            prm[n] = local[n]

    n_ada = w_ada.shape[2]
    cc = jnp.concatenate([c_all, c_ctx[None, :], jnp.zeros((ADA_ROWS - N_DEV * bsz - 1, d), F32)], axis=0)
    b_blk = lax.dynamic_slice_in_dim(b_ada, chip * n_ada, n_ada, axis=1)[:, None, :]
    mod_part, ada_vjp = jax.vjp(ada_op, cc, w_ada, b_blk)
    mod_full = exchange("gather_mod", [mod_part], ("x", "y"), True)[0].transpose(1, 2, 0, 3).reshape(depth, ADA_ROWS, 6 * d)
    mod_lat = lax.dynamic_slice_in_dim(mod_full, dev * bsz, bsz, axis=1)
    mod_ctx = jnp.broadcast_to(mod_full[:, N_DEV * bsz:N_DEV * bsz + 1], mod_lat.shape)
    mods = jnp.stack([mod_lat, mod_ctx], axis=2).reshape(depth, bsz, 2, 6, d)

    xu = jnp.concatenate([x, ctx], axis=1)
    x_out, fwd_vjp = jax.vjp(lambda xu_, mods_, prm_: _forward(xu_, mods_, prm_, prm_lo, n_lat, n_ctx), xu, mods, prm)
    dx_out, loss_blk = loss_op(x_out, loss_target, n_lat // TOKEN_TILE)
    d_xu, d_mods, d_prm = fwd_vjp(dx_out)
    grad_x = d_xu[:, :n_lat]

    d_mods = d_mods.reshape(depth, bsz, 2, 6 * d)
    pieces = [d_mods] + [d_prm[n] for n in SMALL] + [loss_blk[0:1, 0:1]]
    sizes = [int(np.prod(p.shape)) for p in pieces]
    flat = jnp.concatenate([p.reshape(-1) for p in pieces])
    n_flat = -(-flat.shape[0] // (SUBLANE * LANE)) * SUBLANE * LANE
    flat = jnp.concatenate([flat, jnp.zeros((n_flat - flat.shape[0],), F32)]).reshape(-1, LANE)
    small_all = exchange("gather_small", [flat], MESH_AXES, True)[0]
    small_sum = sum_parts("sum_small", small_all).reshape(-1)
    offs = np.cumsum([0] + sizes)
    summed = {n: small_sum[offs[i + 1]:offs[i + 2]].reshape(d_prm[n].shape) for i, n in enumerate(SMALL)}
    loss = small_sum[offs[-2]]
    dm_all = small_all.reshape(N_DEV, -1)[:, :sizes[0]].reshape(N_DEV, depth, bsz, 2, 6 * d)
    dm_rows = dm_all.transpose(3, 0, 2, 1, 4).reshape(2, N_DEV * bsz, depth, 6 * d)
    d_ctx_row = sum_parts("sum_dmod_ctx", dm_rows[1])
    grad_b_ada = sum_parts("sum_b_ada", dm_rows.reshape(2 * N_DEV * bsz, depth, 6 * d))
    d_rows = jnp.concatenate([dm_rows[0].transpose(1, 0, 2), d_ctx_row[:, None, :],
                              jnp.zeros((depth, ADA_ROWS - N_DEV * bsz - 1, 6 * d), F32)], axis=1)
    d_cc, grad_w_ada, _ = ada_vjp(lax.dynamic_slice_in_dim(d_rows, chip * n_ada, n_ada, axis=2))
    d_cctx_all = exchange("gather_dcctx", [d_cc[N_DEV * bsz:N_DEV * bsz + 1]], MESH_AXES, True)[0]
    grad_c_ctx = sum_parts("sum_dcctx", d_cctx_all[0::2]).reshape(d)

    shard_major = [_shard_major(d_prm[n], BIG[n]).astype(BF16) for n in big_names]
    pair = exchange("rs_pair", shard_major, ("c",), False)
    chip_sum = [sum_parts(f"rs_sum1_{n}", p) for n, p in zip(big_names, pair)]
    quad = exchange("rs_quad", [s.astype(BF16) for s in chip_sum], ("x", "y"), False)
    total = [sum_parts(f"rs_sum2_{n}", q) for n, q in zip(big_names, quad)]
    both = exchange("rs_share", total, ("c",), True)

    grads = dict(summed)
    grads["c_lower_bounds"] = lax.dynamic_slice_in_dim(summed["c_lower_bounds"], chip * c_lower_bounds.shape[2],
                                                       c_lower_bounds.shape[2], axis=2)
    grads.update(c_ctx=grad_c_ctx, w_ada=grad_w_ada, b_ada=grad_b_ada)
    grads.update({n: g for n, g in zip(big_names, both)})

    deltas, new_m, new_v = {}, {}, {}
    for n in WEIGHTS:
        as2d = (lambda a: a[None, :]) if local[n].ndim == 1 else (lambda a: a)
        dl, nm_, nv_ = adamw("adamw_" + n, as2d(local[n]), as2d(grads[n]), as2d(mom[n]), as2d(var[n]))
        deltas[n], new_m[n], new_v[n] = (a.reshape(local[n].shape) for a in (dl, nm_, nv_))
    return (loss, grad_x, *[grads[n] for n in WEIGHTS], *[deltas[n] for n in WEIGHTS],
            *[new_m[n] for n in WEIGHTS], *[new_v[n] for n in WEIGHTS])
```

```python
import functools
from typing import Any, Callable, NamedTuple

import numpy as np
import jax
import jax.numpy as jnp
from jax import lax
from jax.experimental import pallas as pl
from jax.experimental.pallas import tpu as pltpu

F32 = jnp.float32
BF16 = jnp.bfloat16
HIGHEST = lax.Precision.HIGHEST

GRID_W = 64
HEAD_DIM = 64
A_HEADS, A_KV_HEADS = 8, 2
A_GROUP = A_HEADS // A_KV_HEADS
B_HEADS, B_Q_RANK, B_KV_RANK, B_NOPE, B_ROPE, B_V = 4, 192, 128, 64, 32, 64
C_HEADS, C_DK, C_DV = 4, 64, 64
SCAN_CHUNK = 64
SCAN_SUB = 16
ROPE_THETA = 10000.0
EPS = 1e-6
F_TINY = 1e-30
ADAM_LR, ADAM_B1, ADAM_B2, ADAM_EPS, ADAM_WD, ADAM_STEP = 0.001, 0.9, 0.999, 1e-08, 0.01, 10

IN_SIZES = (512, 128, 128, 192, 128, 32, 256, 256, 256, 256, 256)
IN_PAD = (512, 128, 128, 256, 128, 128, 256, 256, 256, 256, 256)
IN_OFF = tuple(int(v) for v in np.cumsum((0,) + IN_PAD)[:-1])
D_IN_PAD = int(sum(IN_PAD))

LANE = 128
SUBLANE = 8
TOKEN_TILE = 256
ATT_A_TQ = 64
ATT_A_TQ_FWD = 128
ATT_B_TQ = 256
ATT_FWD_ROW_GROUPS = 4
MM_ROWS = 256
MM_TK_TOKENS = 512
MM_TN_OUT = 2560 * 1024
VMEM_LIMIT = 56 * 1024 * 1024
MESH_AXES = ("x", "y", "c")
N_DEV = 8
N_CHIP = 4
ADA_ROWS = 24


class Arg(NamedTuple):
    arr: Any
    block: tuple
    imap: Callable
    kind: str
    first: Callable = None


class Out(NamedTuple):
    shape: tuple
    block: tuple
    imap: Callable


def _cparams():
    return pltpu.CompilerParams(vmem_limit_bytes=VMEM_LIMIT)


def tile_op(name, fn, grid, args, outs, residual=None, passthrough=False):
    n_in, n_out = len(args), len(outs)
    in_specs = [pl.BlockSpec(a.block, a.imap) for a in args]
    out_specs = [pl.BlockSpec(o.block, o.imap) for o in outs]
    out_shape = [jax.ShapeDtypeStruct(o.shape, F32) for o in outs]
    diff = [i for i, a in enumerate(args) if a.kind != "const"]

    def fwd_call(*arrays):
        def body(*refs):
            ids = tuple(pl.program_id(i) for i in range(len(grid)))
            res = list(fn(ids, *[r[...] for r in refs[:n_in]]))
            out_refs = refs[n_in:]
            if residual is not None:
                res[0] = res[0] + refs[n_in][...]
                out_refs = refs[n_in + 1:]
            for r, o in zip(out_refs, res):
                r[...] = o

        specs = in_specs + ([out_specs[0]] if residual is not None else [])
        return pl.pallas_call(body, grid=grid, in_specs=specs, out_specs=out_specs, out_shape=out_shape,
                              name=name + "_fwd", compiler_params=_cparams())(*arrays)

    def bwd_call(arrays, cts):
        def body(*refs):
            ids = tuple(pl.program_id(i) for i in range(len(grid)))
            vals = [r[...] for r in refs[:n_in]]
            ct = tuple(r[...] for r in refs[n_in:n_in + n_out])
            drefs = refs[n_in + n_out:]

            def g(*dv):
                full = list(vals)
                for i, v in zip(diff, dv):
                    full[i] = v
                return tuple(fn(ids, *full))

            _, vjp = jax.vjp(g, *[vals[i] for i in diff])
            ds = list(vjp(ct))
            if passthrough:
                ds[0] = ds[0] + refs[n_in + n_out][...]
                drefs = refs[n_in + n_out + 1:]
            for i, d, r in zip(diff, ds, drefs):
                if args[i].kind == "tile":
                    r[...] = d
                else:
                    is_first = args[i].first(ids)

                    @pl.when(is_first)
                    def _(r=r, d=d):
                        r[...] = d

                    @pl.when(jnp.logical_not(is_first))
                    def _(r=r, d=d):
                        r[...] += d

        d_specs = [in_specs[i] for i in diff]
        d_shape = [jax.ShapeDtypeStruct(arrays[i].shape, F32) for i in diff]
        specs = in_specs + out_specs + ([in_specs[0]] if passthrough else [])
        return pl.pallas_call(body, grid=grid, in_specs=specs, out_specs=d_specs, out_shape=d_shape,
                              name=name + "_bwd", compiler_params=_cparams())(*arrays, *cts)

    assert not passthrough or (diff and diff[0] == 0 and args[0].kind == "tile")

    def results(arrays):
        res = tuple(fwd_call(*arrays))
        return ((arrays[0],) + res) if passthrough else res

    @jax.custom_vjp
    def op(*arrays):
        return results(arrays)

    def op_fwd(*arrays):
        return results(arrays), arrays[:n_in]

    def op_bwd(arrays, cts):
        if passthrough:
            cts = tuple(cts[1:]) + (cts[0],)
        ds = bwd_call(arrays, cts)
        res, k = [], 0
        for i, a in enumerate(args):
            if a.kind == "const":
                res.append(jnp.zeros_like(arrays[i]))
            else:
                res.append(ds[k])
                k += 1
        if residual is not None:
            res.append(cts[0])
        return tuple(res)

    op.defvjp(op_fwd, op_bwd)
    return op(*[a.arr for a in args], *([residual] if residual is not None else []))


def _pick(n, cap):
    if n <= cap:
        return n
    best = None
    for t in range(LANE, cap + 1, LANE):
        if n % t == 0:
            best = t
    assert best is not None, (n, cap)
    return best


_NN = (((1,), (0,)), ((), ()))
_NT = (((1,), (1,)), ((), ()))
_TN = (((0,), (0,)), ((), ()))


def _resident(shape):
    return pl.BlockSpec(shape, lambda *ids: (0,) * len(shape), pipeline_mode=pl.Buffered(1))


def _mm_rows(name, a, w, transposed):
    m, k = a.shape
    n = w.shape[0] if transposed else w.shape[1]
    tm = MM_ROWS
    dims = _NT if transposed else _NN

    def body(a_ref, w_ref, o_ref):
        o_ref[...] = lax.dot_general(a_ref[...].astype(BF16), w_ref[...], dims, preferred_element_type=F32)

    return pl.pallas_call(body, grid=(m // tm,), in_specs=[pl.BlockSpec((tm, k), lambda i: (i, 0)), _resident(w.shape)],
                          out_specs=pl.BlockSpec((tm, n), lambda i: (i, 0)), out_shape=jax.ShapeDtypeStruct((m, n), F32),
                          name=name, compiler_params=_cparams())(a, w)


def _mm_tn(name, a, g):
    t, k = a.shape
    n = g.shape[1]
    tko, tno = k, n
    while tko * tno > MM_TN_OUT:
        if tko >= tno:
            tko //= 2
        else:
            tno //= 2
    assert k % tko == 0 and n % tno == 0 and tko % LANE == 0 and tno % LANE == 0
    tt = _pick(t, MM_TK_TOKENS)

    def body(a_ref, g_ref, o_ref):
        p = lax.dot_general(a_ref[...].astype(BF16), g_ref[...].astype(BF16), _TN, preferred_element_type=F32)
        kk = pl.program_id(2)

        @pl.when(kk == 0)
        def _():
            o_ref[...] = p

        @pl.when(kk != 0)
        def _():
            o_ref[...] += p

    return pl.pallas_call(body, grid=(k // tko, n // tno, t // tt),
                          in_specs=[pl.BlockSpec((tt, tko), lambda i, j, kk: (kk, i)), pl.BlockSpec((tt, tno), lambda i, j, kk: (kk, j))],
                          out_specs=pl.BlockSpec((tko, tno), lambda i, j, kk: (i, j)),
                          out_shape=jax.ShapeDtypeStruct((k, n), F32), name=name, compiler_params=_cparams())(a, g)


def linear(name, a, w, w_lo):
    @jax.custom_vjp
    def op(a, w, w_lo):
        return _mm_rows(name + "_fwd", a, w_lo, False)

    def op_fwd(a, w, w_lo):
        return _mm_rows(name + "_fwd", a, w_lo, False), (a, w_lo)

    def op_bwd(res, g):
        a, w_lo = res
        return _mm_rows(name + "_da", g, w_lo, True), _mm_tn(name + "_dw", a, g), jnp.zeros_like(w_lo)

    op.defvjp(op_fwd, op_bwd)
    return op(a, w, w_lo)


def mlp(name, h, w1, w2, w1_lo, w2_lo):
    m, d = h.shape
    f = w1_lo.shape[1]
    tm = MM_ROWS
    row = pl.BlockSpec((tm, d), lambda i: (i, 0))
    wide = pl.BlockSpec((tm, f), lambda i: (i, 0))

    def fwd_call(h, w1_lo, w2_lo):
        def body(h_ref, w1_ref, w2_ref, o_ref):
            u = lax.dot_general(h_ref[...].astype(BF16), w1_ref[...], _NN, preferred_element_type=F32)
            act = jnp.square(jnp.maximum(u, 0.0))
            o_ref[...] = lax.dot_general(act.astype(BF16), w2_ref[...], _NN, preferred_element_type=F32)

        return pl.pallas_call(body, grid=(m // tm,), in_specs=[row, _resident(w1_lo.shape), _resident(w2_lo.shape)],
                              out_specs=row, out_shape=jax.ShapeDtypeStruct((m, d), F32), name=name + "_fwd",
                              compiler_params=_cparams())(h, w1_lo, w2_lo)

    def bwd_call(h, w1_lo, w2_lo, dy):
        def body(h_ref, dy_ref, w1_ref, w2_ref, dh_ref, act_ref, du_ref):
            u = lax.dot_general(h_ref[...].astype(BF16), w1_ref[...], _NN, preferred_element_type=F32)
            r = jnp.maximum(u, 0.0)
            act_ref[...] = (r * r).astype(BF16)
            dact = lax.dot_general(dy_ref[...].astype(BF16), w2_ref[...], _NT, preferred_element_type=F32)
            du = (dact * (2.0 * r)).astype(BF16)
            du_ref[...] = du
            dh_ref[...] = lax.dot_general(du, w1_ref[...], _NT, preferred_element_type=F32)

        return pl.pallas_call(body, grid=(m // tm,), in_specs=[row, row, _resident(w1_lo.shape), _resident(w2_lo.shape)],
                              out_specs=[row, wide, wide],
                              out_shape=[jax.ShapeDtypeStruct((m, d), F32), jax.ShapeDtypeStruct((m, f), BF16),
                                         jax.ShapeDtypeStruct((m, f), BF16)],
                              name=name + "_bwd", compiler_params=_cparams())(h, dy, w1_lo, w2_lo)

    @jax.custom_vjp
    def op(h, w1, w2, w1_lo, w2_lo):
        return fwd_call(h, w1_lo, w2_lo)

    def op_fwd(h, w1, w2, w1_lo, w2_lo):
        return fwd_call(h, w1_lo, w2_lo), (h, w1_lo, w2_lo)

    def op_bwd(res, dy):
        h, w1_lo, w2_lo = res
        dh, act, du = bwd_call(h, w1_lo, w2_lo, dy)
        return (dh, _mm_tn(name + "_dw1", h, du), _mm_tn(name + "_dw2", act, dy), jnp.zeros_like(w1_lo), jnp.zeros_like(w2_lo))

    op.defvjp(op_fwd, op_bwd)
    return op(h, w1, w2, w1_lo, w2_lo)


def _rms(x, g):
    return x * lax.rsqrt(jnp.mean(x * x, axis=-1, keepdims=True) + EPS) * g


def _sigmoid(z):
    return 1.0 / (1.0 + jnp.exp(-z))


def _silu(z):
    return z * _sigmoid(z)


def _rope(y, cos, sin_signed, swap):
    return y * cos + jnp.dot(y, swap, precision=HIGHEST, preferred_element_type=F32) * sin_signed


def _dot_bf16(a, b, dims=((1,), (0,))):
    return lax.dot_general(a.astype(BF16), b.astype(BF16), (dims, ((), ())), preferred_element_type=F32)


def _gla_step(state, q, k, v, g, reverse):
    c, d = q.shape
    sub = SCAN_SUB
    nb = c // sub
    row = lax.broadcasted_iota(jnp.int32, (c, c), 0)
    col = lax.broadcasted_iota(jnp.int32, (c, c), 1)
    tri = (row <= col) if reverse else (row >= col)
    b = jnp.dot(tri.astype(F32), g, precision=HIGHEST, preferred_element_type=F32)
    o = jnp.dot(q * jnp.exp(b), state, preferred_element_type=F32)
    rs = lax.broadcasted_iota(jnp.int32, (sub, sub), 0)
    cs = lax.broadcasted_iota(jnp.int32, (sub, sub), 1)
    tri_s = ((rs <= cs) if reverse else (rs >= cs)).astype(F32)
    rowc = lax.broadcasted_iota(jnp.int32, (c, 1), 0)
    nonpos = lambda x: jnp.where(x > 0.0, 0.0, x)
    diag = []
    for j in range(nb):
        sl = slice(j * sub, (j + 1) * sub)
        bj, kj, vj, qj = b[sl], k[sl], v[sl], q[sl]
        dec = jnp.exp(nonpos(bj[:, None, :] - bj[None, :, :]))
        sc = jnp.sum(qj[:, None, :] * kj[None, :, :] * dec, axis=-1) * tri_s
        diag.append(jnp.dot(sc, vj, preferred_element_type=F32))
        if (j > 0) if reverse else (j < nb - 1):
            ref = bj[0:1] if reverse else bj[sub - 1:sub]
            qa = q * jnp.exp(nonpos(b - ref))
            ks = kj * jnp.exp(ref - bj)
            scj = lax.dot_general(qa, ks, (((1,), (1,)), ((), ())), precision=HIGHEST, preferred_element_type=F32)
            later = (rowc < j * sub) if reverse else (rowc >= (j + 1) * sub)
            o = o + jnp.dot(jnp.where(later, scj, 0.0), vj, preferred_element_type=F32)
    o = o + jnp.concatenate(diag, axis=0)
    b_end = b[0:1, :] if reverse else b[c - 1:c, :]
    kd = k * jnp.exp(b_end - b)
    new_state = state * jnp.exp(b_end).reshape(d, 1) + lax.dot_general(kd, v, (((0,), (0,)), ((), ())), preferred_element_type=F32)
    return new_state, o


def gla_scan(name, q, k, v, g, reverse, n_lat_chunks):
    bsz, nh, t, d = q.shape
    c = SCAN_CHUNK
    n = t // c

    def chunk_of(j):
        return (n - 1 - j) if reverse else lax.rem(j + n_lat_chunks, n)

    blk = (bsz, nh, c, d)
    st_blk = (bsz, nh, None, d, d)
    chains = [(b, h) for b in range(bsz) for h in range(nh)]

    def fwd_call(q, k, v, g):
        def body(q_ref, k_ref, v_ref, g_ref, o_ref, states_ref, st):
            @pl.when(pl.program_id(0) == 0)
            def _():
                st[...] = jnp.zeros_like(st)
            for bh in chains:
                s = st[bh]
                states_ref[bh] = s
                ns, o = _gla_step(s, q_ref[bh], k_ref[bh], v_ref[bh], g_ref[bh], reverse)
                st[bh] = ns
                o_ref[bh] = o

        spec = pl.BlockSpec(blk, lambda j: (0, 0, chunk_of(j), 0))
        return pl.pallas_call(
            body, grid=(n,), in_specs=[spec] * 4,
            out_specs=[spec, pl.BlockSpec(st_blk, lambda j: (0, 0, j, 0, 0))],
            out_shape=[jax.ShapeDtypeStruct(q.shape, F32), jax.ShapeDtypeStruct((bsz, nh, n, d, d), F32)],
            scratch_shapes=[pltpu.VMEM((bsz, nh, d, d), F32)], name=name + "_fwd", compiler_params=_cparams())(q, k, v, g)

    def bwd_call(q, k, v, g, states, do):
        def body(q_ref, k_ref, v_ref, g_ref, s_ref, do_ref, dq_ref, dk_ref, dv_ref, dg_ref, dst):
            @pl.when(pl.program_id(0) == 0)
            def _():
                dst[...] = jnp.zeros_like(dst)
            step = functools.partial(_gla_step, reverse=reverse)
            for bh in chains:
                _, vjp = jax.vjp(step, s_ref[bh], q_ref[bh], k_ref[bh], v_ref[bh], g_ref[bh])
                ds, dq, dk, dv, dg = vjp((dst[bh], do_ref[bh]))
                dst[bh] = ds
                dq_ref[bh] = dq
                dk_ref[bh] = dk
                dv_ref[bh] = dv
                dg_ref[bh] = dg

        spec = pl.BlockSpec(blk, lambda jj: (0, 0, chunk_of(n - 1 - jj), 0))
        s_spec = pl.BlockSpec(st_blk, lambda jj: (0, 0, n - 1 - jj, 0, 0))
        return pl.pallas_call(
            body, grid=(n,), in_specs=[spec] * 4 + [s_spec, spec], out_specs=[spec] * 4,
            out_shape=[jax.ShapeDtypeStruct(q.shape, F32)] * 4,
            scratch_shapes=[pltpu.VMEM((bsz, nh, d, d), F32)], name=name + "_bwd", compiler_params=_cparams())(q, k, v, g, states, do)

    @jax.custom_vjp
    def op(q, k, v, g):
        return fwd_call(q, k, v, g)[0]

    def op_fwd(q, k, v, g):
        o, states = fwd_call(q, k, v, g)
        return o, (q, k, v, g, states)

    def op_bwd(res, do):
        return tuple(bwd_call(*res, do))

    op.defvjp(op_fwd, op_bwd)
    return op(q, k, v, g)


def _zero_ids(ids):
    z = ids[0] == 0
    for i in ids[1:]:
        z = jnp.logical_and(z, i == 0)
    return z


def _token_grid(x, n_lat_tiles):
    bsz, t, d = x.shape
    nt = t // TOKEN_TILE
    row = lambda w: ((None, TOKEN_TILE, w), lambda b, i: (b, i, 0))
    mod_block = (None, None, 6, d)
    mod_imap = lambda b, i: (b, (i >= n_lat_tiles).astype(jnp.int32), 0, 0)
    mod_first = lambda ids: jnp.logical_or(ids[1] == 0, ids[1] == n_lat_tiles)
    return bsz, t, d, nt, row, (mod_block, mod_imap, mod_first)


def premod(name, x, gain, mods, r0, n_lat_tiles):
    bsz, t, d, nt, row, (mb, mi, mf) = _token_grid(x, n_lat_tiles)

    def fn(ids, xb, gb, mod):
        return (_rms(xb, gb) * (1.0 + mod[r0 + 1:r0 + 2]) + mod[r0:r0 + 1],)

    args = [Arg(x, *row(d), "tile"), Arg(gain, (1, d), lambda b, i: (0, 0), "acc", _zero_ids), Arg(mods, mb, mi, "acc", mf)]
    return tile_op(name, fn, (bsz, nt), args, [Out(x.shape, *row(d))], passthrough=True)


def resid(name, x, y, gain, mods, r, n_lat_tiles):
    bsz, t, d, nt, row, (mb, mi, mf) = _token_grid(x, n_lat_tiles)

    def fn(ids, yb, gb, mod):
        return (mod[r:r + 1] * _rms(yb, gb),)

    args = [Arg(y, *row(d), "tile"), Arg(gain, (1, d), lambda b, i: (0, 0), "acc", _zero_ids), Arg(mods, mb, mi, "acc", mf)]
    return tile_op(name, fn, (bsz, nt), args, [Out(x.shape, *row(d))], residual=x)[0]


def _heads_spec(nh, w):
    return (None, nh, TOKEN_TILE, w), lambda b, i: (b, 0, i, 0)


def a_prep(name, x, gain, cos, sin, swap):
    bsz, nh, t, d = x.shape
    tab = ((TOKEN_TILE, d), lambda b, i: (i, 0))

    def fn(ids, xb, gb, cb, sb, pb):
        y = _rms(xb, gb)
        swapped = jnp.dot(y.reshape(nh * TOKEN_TILE, d), pb, precision=HIGHEST, preferred_element_type=F32)
        return (y * cb + swapped.reshape(nh, TOKEN_TILE, d) * sb,)

    args = [Arg(x, *_heads_spec(nh, d), "tile"), Arg(gain, (1, d), lambda b, i: (0, 0), "acc", _zero_ids),
            Arg(cos, *tab, "const"), Arg(sin, *tab, "const"), Arg(swap, (d, d), lambda b, i: (0, 0), "const")]
    return tile_op(name, fn, (bsz, t // TOKEN_TILE), args, [Out(x.shape, *_heads_spec(nh, d))])[0]


def b_prep(name, bqd, bkvd, bkr, bqn, wq_nope, wq_pe, bkvn, wkv_nope, wkv_v, cos_q, sin_q, swap_q, cos_k, sin_k, swap_k):
    bsz, t, _ = bqd.shape
    row = lambda w: ((None, TOKEN_TILE, w), lambda b, i: (b, i, 0))
    whole = lambda a: (a.shape, lambda b, i: (0,) * a.ndim)
    tab = lambda w: ((TOKEN_TILE, w), lambda b, i: (i, 0))

    def fn(ids, qd, kvd, kr, qn, wqn, wqp, kvn, wkn, wkv, cq, sq, pq, ck, sk, pk):
        hq = _rms(qd, qn)
        hkv = _rms(kvd, kvn)
        return (_dot_bf16(hq, wqn), _rope(_dot_bf16(hq, wqp), cq, sq, pq), _dot_bf16(hkv, wkn), _dot_bf16(hkv, wkv),
                _rope(kr, ck, sk, pk))

    params = [bqn, wq_nope, wq_pe, bkvn, wkv_nope, wkv_v]
    args = [Arg(bqd, *row(B_Q_RANK), "tile"), Arg(bkvd, *row(B_KV_RANK), "tile"), Arg(bkr, *row(B_ROPE), "tile")]
    args += [Arg(p, *whole(p), "acc", _zero_ids) for p in params]
    args += [Arg(cos_q, *tab(cos_q.shape[1]), "const"), Arg(sin_q, *tab(cos_q.shape[1]), "const"), Arg(swap_q, *whole(swap_q), "const"),
             Arg(cos_k, *tab(B_ROPE), "const"), Arg(sin_k, *tab(B_ROPE), "const"), Arg(swap_k, *whole(swap_k), "const")]
    widths = (B_HEADS * B_NOPE, B_HEADS * B_ROPE, B_HEADS * B_NOPE, B_HEADS * B_V, B_ROPE)
    outs = [Out((bsz, t, w), *row(w)) for w in widths]
    return tile_op(name, fn, (bsz, t // TOKEN_TILE), args, outs)


def c_lower(name, clb, layer):
    nh, depth, _, d = clb.shape

    def fn(ids, lbs):
        lb = [lbs[:, j] for j in range(depth)]
        m = lb[0]
        for j in range(1, depth):
            m = jnp.maximum(m, lb[j])
        e = [jnp.exp(lb[j] - m) for j in range(depth)]
        tot = e[0]
        for j in range(1, depth):
            tot = tot + e[j]
        p = [ej / tot for ej in e]
        cum = p[0]
        for j in range(1, layer + 1):
            cum = cum + p[j]
        return (cum - p[0],)

    whole = lambda shape: (shape, lambda i: (0,) * len(shape))
    return tile_op(name, fn, (1,), [Arg(clb, *whole(clb.shape), "tile")], [Out((nh, 2, d), *whole((nh, 2, d)))])[0]


def c_readout(name, o_f, o_b, gate, gain):
    bsz, nh, t, d = o_f.shape
    spec = _heads_spec(nh, d)

    def fn(ids, of, ob, gt, gn):
        return (_rms(of + ob, gn) * _silu(gt),)

    args = [Arg(o_f, *spec, "tile"), Arg(o_b, *spec, "tile"), Arg(gate, *spec, "tile"),
            Arg(gain, (1, d), lambda b, i: (0, 0), "acc", _zero_ids)]
    return tile_op(name, fn, (bsz, t // TOKEN_TILE), args, [Out(o_f.shape, *spec)])[0]


def attention(name, qs, kts, vt, scale, fold_scale, tq_fwd, tq_bwd, n_lat):
    bsz, hk, grp, t, _ = qs[0].shape
    dv = vt.shape[2]
    n_parts = len(qs)
    shared = [kt.shape[1] == 1 for kt in kts]
    kt_specs = [pl.BlockSpec((None, None, kt.shape[2], t), (lambda b, h, i: (b, 0, 0, 0)) if sh else (lambda b, h, i: (b, h, 0, 0)))
                for kt, sh in zip(kts, shared)]
    vt_spec = pl.BlockSpec((None, None, dv, t), lambda b, h, i: (b, h, 0, 0))
    o_shape = jax.ShapeDtypeStruct((bsz, hk, grp, t, dv), F32)
    lse_shape = jax.ShapeDtypeStruct((bsz, hk, grp, t, LANE), F32)

    def plan(tq_):
        per_head = lambda w: pl.BlockSpec((None, None, grp, tq_, w), lambda b, h, i: (b, h, 0, i, 0))
        return grp * tq_, [per_head(q.shape[-1]) for q in qs], per_head(dv), per_head(LANE), (bsz, hk, t // tq_)

    def load_q(q_refs, rows):
        qq = []
        for q_ref in q_refs:
            q2 = q_ref[...].reshape(rows, q_ref.shape[-1])
            qq.append((q2 * scale if fold_scale else q2).astype(BF16))
        return qq

    def scores(qq, kts_lo):
        s = None
        for q2, kt in zip(qq, kts_lo):
            part = lax.dot_general(q2, kt, _NN, preferred_element_type=F32)
            s = part if s is None else s + part
        return s if fold_scale else s * scale

    ranges = ((lambda i, tq: i * tq < n_lat, 0), (lambda i, tq: i * tq >= n_lat, n_lat))

    def fwd_call(*arrays):
        tq = tq_fwd
        rows, q_specs, o_spec, lse_spec, grid = plan(tq)

        def body(*refs):
            q_refs, kt_refs, vt_ref = refs[:n_parts], refs[n_parts:2 * n_parts], refs[2 * n_parts]
            o_ref, lse_ref = refs[2 * n_parts + 1:]
            i = pl.program_id(2)

            def tile(col0):
                qq = load_q(q_refs, rows)
                kts_lo = [r[:, col0:] for r in kt_refs]
                vt_lo = vt_ref[:, col0:]
                n_grp = ATT_FWD_ROW_GROUPS
                gr = rows // n_grp
                outs, lses = [], []
                for r in range(n_grp):
                    s = scores([q2[r * gr:(r + 1) * gr] for q2 in qq], kts_lo)
                    m = jnp.max(s, axis=-1, keepdims=True)
                    e = jnp.exp(s - m)
                    l = jnp.sum(e, axis=-1, keepdims=True)
                    outs.append(lax.dot_general(e.astype(BF16), vt_lo, _NT, preferred_element_type=F32) * (1.0 / l))
                    lses.append(jnp.broadcast_to(m + jnp.log(l), (gr, LANE)))
                o_ref[...] = jnp.concatenate(outs, axis=0).reshape(grp, tq, dv)
                lse_ref[...] = jnp.concatenate(lses, axis=0).reshape(grp, tq, LANE)

            for cond, col0 in ranges:
                pl.when(cond(i, tq))(functools.partial(tile, col0))

        return pl.pallas_call(body, grid=grid, in_specs=q_specs + kt_specs + [vt_spec], out_specs=[o_spec, lse_spec],
                              out_shape=[o_shape, lse_shape], name=name + "_fwd", compiler_params=_cparams())(*arrays)

    def bwd_call(arrays, o, lse, do):
        tq = tq_bwd
        rows, q_specs, o_spec, lse_spec, grid = plan(tq)

        def body(*refs):
            q_refs, kt_refs, vt_ref = refs[:n_parts], refs[n_parts:2 * n_parts], refs[2 * n_parts]
            o_ref, lse_ref, do_ref = refs[2 * n_parts + 1:2 * n_parts + 4]
            d_refs = refs[2 * n_parts + 4:]
            dq_refs, dkt_refs, dvt_ref = d_refs[:n_parts], d_refs[n_parts:2 * n_parts], d_refs[2 * n_parts]
            h, i = pl.program_id(1), pl.program_id(2)

            def tile(col0):
                qq = load_q(q_refs, rows)
                kts_lo = [r[:, col0:] for r in kt_refs]
                p = jnp.exp(scores(qq, kts_lo) - lse_ref[...].reshape(rows, LANE)[:, 0:1])
                do2 = do_ref[...].reshape(rows, dv)
                delta = jnp.sum(do2 * o_ref[...].reshape(rows, dv), axis=-1, keepdims=True)
                do_lo = do2.astype(BF16)
                dp = lax.dot_general(do_lo, vt_ref[:, col0:], _NN, preferred_element_type=F32)
                ds = p * (dp - delta)
                if not fold_scale:
                    ds = ds * scale
                ds_lo = ds.astype(BF16)

                def accumulate(ref, val, first):
                    if col0:
                        ref[:, col0:] += val
                        return

                    @pl.when(first)
                    def _():
                        ref[...] = val

                    @pl.when(jnp.logical_not(first))
                    def _():
                        ref[...] += val

                accumulate(dvt_ref, lax.dot_general(do_lo, p.astype(BF16), _TN, preferred_element_type=F32), i == 0)
                for q2, kt, dq_ref, dkt_ref, sh in zip(qq, kts_lo, dq_refs, dkt_refs, shared):
                    dq = lax.dot_general(ds_lo, kt, _NT, preferred_element_type=F32)
                    dq_ref[...] = (dq * scale if fold_scale else dq).reshape(dq_ref.shape)
                    first = jnp.logical_and(h == 0, i == 0) if sh else i == 0
                    accumulate(dkt_ref, lax.dot_general(q2, ds_lo, _TN, preferred_element_type=F32), first)

            for cond, col0 in ranges:
                pl.when(cond(i, tq))(functools.partial(tile, col0))

        in_specs = q_specs + kt_specs + [vt_spec, o_spec, lse_spec, o_spec]
        d_shape = [jax.ShapeDtypeStruct(a.shape, F32) for a in arrays]
        return pl.pallas_call(body, grid=grid, in_specs=in_specs, out_specs=q_specs + kt_specs + [vt_spec], out_shape=d_shape,
                              name=name + "_bwd", compiler_params=_cparams())(*arrays, o, lse, do)

    def lo(arrays):
        return arrays[:n_parts] + tuple(a.astype(BF16) for a in arrays[n_parts:])

    @jax.custom_vjp
    def op(*arrays):
        return fwd_call(*lo(arrays))[0]

    def op_fwd(*arrays):
        low = lo(arrays)
        o, lse = fwd_call(*low)
        return o, (low, o, lse)

    def op_bwd(res, do):
        low, o, lse = res
        return tuple(bwd_call(low, o, lse, do))

    op.defvjp(op_fwd, op_bwd)
    return op(*qs, *kts, vt)


def _att_rows_fwd(qq, kts_lo, vt_lo, scale, fold_scale, n_grp):
    rows = qq[0].shape[0]
    gr = rows // n_grp
    outs, lses = [], []
    for r in range(n_grp):
        s = None
        for q2, kt in zip(qq, kts_lo):
            part = lax.dot_general(q2[r * gr:(r + 1) * gr], kt, _NN, preferred_element_type=F32)
            s = part if s is None else s + part
        if not fold_scale:
            s = s * scale
        m = jnp.max(s, axis=-1, keepdims=True)
        e = jnp.exp(s - m)
        l = jnp.sum(e, axis=-1, keepdims=True)
        outs.append(lax.dot_general(e.astype(BF16), vt_lo, _NT, preferred_element_type=F32) * (1.0 / l))
        lses.append(m + jnp.log(l))
    return jnp.concatenate(outs, axis=0), jnp.concatenate(lses, axis=0)


def _att_rows_bwd(qq, kts_lo, vt_lo, o2, lse, do2, scale, fold_scale):
    s = None
    for q2, kt in zip(qq, kts_lo):
        part = lax.dot_general(q2, kt, _NN, preferred_element_type=F32)
        s = part if s is None else s + part
    if not fold_scale:
        s = s * scale
    p = jnp.exp(s - lse)
    delta = jnp.sum(do2 * o2, axis=-1, keepdims=True)
    do_lo = do2.astype(BF16)
    ds = p * (lax.dot_general(do_lo, vt_lo, _NN, preferred_element_type=F32) - delta)
    if not fold_scale:
        ds = ds * scale
    ds_lo = ds.astype(BF16)
    dvt = lax.dot_general(do_lo, p.astype(BF16), _TN, preferred_element_type=F32)
    dqs, dkts = [], []
    for q2, kt in zip(qq, kts_lo):
        dq = lax.dot_general(ds_lo, kt, _NT, preferred_element_type=F32)
        dqs.append(dq * scale if fold_scale else dq)
        dkts.append(lax.dot_general(q2, ds_lo, _TN, preferred_element_type=F32))
    return dqs, dkts, dvt


def mix_pair(name, qs, kts, vt, scale, fold_scale, sub_fwd, sub_bwd, n_lat, sq, sz, sv, lb, reverse, n_lat_chunks):
    bsz, hk, grp, t, _ = qs[0].shape
    dv = vt.shape[2]
    n_parts = len(qs)
    nh, d = sq.shape[1], sq.shape[3]
    c = SCAN_CHUNK
    n = t // c
    tph = n // hk
    tq = t // tph
    assert tph * hk == n and tq * tph == t and tq % sub_fwd == 0 and tq % sub_bwd == 0 and n_lat % tq == 0
    chains = [(b, h) for b in range(bsz) for h in range(nh)]
    widths = [q.shape[-1] for q in qs]

    def chunk_of(j):
        return (n - 1 - j) if reverse else lax.rem(j + n_lat_chunks, n)

    def chain(state, q_raw, z, v, lo_b):
        f = lo_b + (1.0 - lo_b) * _sigmoid(z)
        return _gla_step(state, _silu(q_raw), (1.0 - lo_b) * _sigmoid(-z), v, jnp.log(jnp.maximum(f, F_TINY)), reverse)

    lb_spec = pl.BlockSpec(lb.shape, lambda j: (0, 0, 0))
    head_of = lambda j: j // tph
    tile_of = lambda j: lax.rem(j, tph)
    per_tile = lambda w: pl.BlockSpec((bsz, None, grp, tq, w), lambda j: (0, head_of(j), 0, tile_of(j), 0))
    per_head = lambda w: pl.BlockSpec((bsz, None, w, t), lambda j: (0, head_of(j), 0, 0))
    q_specs = [per_tile(w) for w in widths]
    kt_specs = [per_head(w) for w in widths]
    att_in = q_specs + kt_specs + [per_head(dv)]
    o_shape = jax.ShapeDtypeStruct((bsz, hk, grp, t, dv), F32)
    lse_shape = jax.ShapeDtypeStruct((bsz, hk, grp, t, LANE), F32)
    st_shape = jax.ShapeDtypeStruct((bsz, nh, n, d, d), F32)
    ranges = ((lambda j: tile_of(j) * tq < n_lat, 0), (lambda j: tile_of(j) * tq >= n_lat, n_lat))

    def load_q(q_refs, b, r0, sub):
        qq = []
        for q_ref, w in zip(q_refs, widths):
            q2 = q_ref[b, :, r0:r0 + sub, :].reshape(grp * sub, w)
            qq.append((q2 * scale if fold_scale else q2).astype(BF16))
        return qq

    def fwd_call(*arrays):
        def body(*refs):
            q_refs, kt_refs, vt_ref = refs[:n_parts], refs[n_parts:2 * n_parts], refs[2 * n_parts]
            sq_ref, sz_ref, sv_ref, lb_ref = refs[2 * n_parts + 1:2 * n_parts + 5]
            o_ref, lse_ref, so_ref, states_ref, st = refs[2 * n_parts + 5:]
            j = pl.program_id(0)

            @pl.when(j == 0)
            def _():
                st[...] = jnp.zeros_like(st)

            def step(col0):
                for bh in chains:
                    s = st[bh]
                    states_ref[bh] = s
                    ns, o = chain(s, sq_ref[bh], sz_ref[bh], sv_ref[bh], lb_ref[bh[1]])
                    st[bh] = ns
                    so_ref[bh] = o
                for b in range(bsz):
                    kts_lo = [r[b, :, col0:] for r in kt_refs]
                    vt_lo = vt_ref[b, :, col0:]
                    for r0 in range(0, tq, sub_fwd):
                        o, lse = _att_rows_fwd(load_q(q_refs, b, r0, sub_fwd), kts_lo, vt_lo, scale, fold_scale, ATT_FWD_ROW_GROUPS)
                        o_ref[b, :, r0:r0 + sub_fwd, :] = o.reshape(grp, sub_fwd, dv)
                        lse_ref[b, :, r0:r0 + sub_fwd, :] = jnp.broadcast_to(lse, (grp * sub_fwd, LANE)).reshape(grp, sub_fwd, LANE)

            for cond, col0 in ranges:
                pl.when(cond(j))(functools.partial(step, col0))

        scan_spec = pl.BlockSpec((bsz, nh, c, d), lambda j: (0, 0, chunk_of(j), 0))
        return pl.pallas_call(
            body, grid=(n,), in_specs=att_in + [scan_spec] * 3 + [lb_spec],
            out_specs=[per_tile(dv), per_tile(LANE), scan_spec, pl.BlockSpec((bsz, nh, None, d, d), lambda j: (0, 0, j, 0, 0))],
            out_shape=[o_shape, lse_shape, jax.ShapeDtypeStruct(sq.shape, F32), st_shape],
            scratch_shapes=[pltpu.VMEM((bsz, nh, d, d), F32)], name=name + "_fwd", compiler_params=_cparams())(*arrays)

    def bwd_call(att_lo, scan_in, o, lse, states, do, dso):
        def body(*refs):
            q_refs, kt_refs, vt_ref = refs[:n_parts], refs[n_parts:2 * n_parts], refs[2 * n_parts]
            k0 = 2 * n_parts + 1
            sq_ref, sz_ref, sv_ref, lb_ref, s_ref = refs[k0:k0 + 5]
            o_ref, lse_ref, do_ref, dso_ref = refs[k0 + 5:k0 + 9]
            d_refs = refs[k0 + 9:]
            dq_refs, dkt_refs, dvt_ref = d_refs[:n_parts], d_refs[n_parts:2 * n_parts], d_refs[2 * n_parts]
            dsq_ref, dsz_ref, dsv_ref, dlb_ref, dst = d_refs[2 * n_parts + 1:]
            jj = pl.program_id(0)

            @pl.when(jj == 0)
            def _():
                dst[...] = jnp.zeros_like(dst)
                dlb_ref[...] = jnp.zeros_like(dlb_ref)

            @pl.when(tile_of(jj) == 0)
            def _():
                for ref in list(dkt_refs) + [dvt_ref]:
                    ref[...] = jnp.zeros_like(ref)

            def step(col0):
                for bh in chains:
                    _, vjp = jax.vjp(chain, s_ref[bh], sq_ref[bh], sz_ref[bh], sv_ref[bh], lb_ref[bh[1]])
                    ds, dq, dz, dv_, dlb = vjp((dst[bh], dso_ref[bh]))
                    dst[bh] = ds
                    dsq_ref[bh] = dq
                    dsz_ref[bh] = dz
                    dsv_ref[bh] = dv_
                    dlb_ref[bh[1]] += dlb
                for b in range(bsz):
                    kts_lo = [r[b, :, col0:] for r in kt_refs]
                    vt_lo = vt_ref[b, :, col0:]
                    for r0 in range(0, tq, sub_bwd):
                        rows = grp * sub_bwd
                        rsl = slice(r0, r0 + sub_bwd)
                        dqs, dkts, dvt = _att_rows_bwd(
                            load_q(q_refs, b, r0, sub_bwd), kts_lo, vt_lo, o_ref[b, :, rsl, :].reshape(rows, dv),
                            lse_ref[b, :, rsl, :].reshape(rows, LANE)[:, 0:1], do_ref[b, :, rsl, :].reshape(rows, dv), scale, fold_scale)
                        for dq_ref, dq, w in zip(dq_refs, dqs, widths):
                            dq_ref[b, :, rsl, :] = dq.reshape(grp, sub_bwd, w)
                        for dkt_ref, dkt in zip(dkt_refs, dkts):
                            dkt_ref[b, :, col0:] += dkt
                        dvt_ref[b, :, col0:] += dvt

            for cond, col0 in ranges:
                pl.when(cond(jj))(functools.partial(step, col0))

        scan_spec = pl.BlockSpec((bsz, nh, c, d), lambda jj: (0, 0, chunk_of(n - 1 - jj), 0))
        st_spec = pl.BlockSpec((bsz, nh, None, d, d), lambda jj: (0, 0, n - 1 - jj, 0, 0))
        in_specs = att_in + [scan_spec] * 3 + [lb_spec, st_spec, per_tile(dv), per_tile(LANE), per_tile(dv), scan_spec]
        d_shape = ([jax.ShapeDtypeStruct(a.shape, F32) for a in att_lo] + [jax.ShapeDtypeStruct(sq.shape, F32)] * 3
                   + [jax.ShapeDtypeStruct(lb.shape, F32)])
        return pl.pallas_call(body, grid=(n,), in_specs=in_specs, out_specs=att_in + [scan_spec] * 3 + [lb_spec], out_shape=d_shape,
                              scratch_shapes=[pltpu.VMEM((bsz, nh, d, d), F32)], name=name + "_bwd",
                              compiler_params=_cparams())(*att_lo, *scan_in, states, o, lse, do, dso)

    n_att = 2 * n_parts + 1

    def lo(arrays):
        return arrays[:n_parts] + tuple(a.astype(BF16) for a in arrays[n_parts:n_att])

    @jax.custom_vjp
    def op(*arrays):
        res = fwd_call(*lo(arrays), *arrays[n_att:])
        return res[0], res[2]

    def op_fwd(*arrays):
        att_lo, scan_in = lo(arrays), arrays[n_att:]
        o, lse, so, states = fwd_call(*att_lo, *scan_in)
        return (o, so), (att_lo, scan_in, o, lse, states)

    def op_bwd(res, cts):
        att_lo, scan_in, o, lse, states = res
        return tuple(bwd_call(att_lo, scan_in, o, lse, states, cts[0], cts[1]))

    op.defvjp(op_fwd, op_bwd)
    return op(*qs, *kts, vt, sq, sz, sv, lb)


def ada_op(cc, w, b):
    depth, d, n = w.shape

    def fn(ids, ccb, wb, bb):
        return (_dot_bf16(_silu(ccb), wb) + bb,)

    args = [Arg(cc, cc.shape, lambda l: (0, 0), "acc", lambda ids: ids[0] == 0),
            Arg(w, (None, d, n), lambda l: (l, 0, 0), "tile"), Arg(b, (None, 1, n), lambda l: (l, 0, 0), "tile")]
    return tile_op("ada", fn, (depth,), args, [Out((depth, cc.shape[0], n), (None, cc.shape[0], n), lambda l: (l, 0, 0))])[0]


def loss_op(xu, target, n_lat_tiles):
    bsz, t, d = xu.shape

    def body(x_ref, t_ref, dx_ref, l_ref):
        b, i = pl.program_id(0), pl.program_id(1)

        @pl.when(jnp.logical_and(b == 0, i == 0))
        def _():
            l_ref[...] = jnp.zeros_like(l_ref)

        @pl.when(i < n_lat_tiles)
        def _():
            err = x_ref[...] - t_ref[...]
            dx_ref[...] = err * (1.0 / d)
            l_ref[...] += 0.5 * jnp.sum(jnp.mean(err * err, axis=-1))

        @pl.when(i >= n_lat_tiles)
        def _():
            dx_ref[...] = jnp.zeros_like(dx_ref)

    row = pl.BlockSpec((None, TOKEN_TILE, d), lambda b, i: (b, i, 0))
    t_spec = pl.BlockSpec((None, TOKEN_TILE, d), lambda b, i: (b, jnp.minimum(i, n_lat_tiles - 1), 0))
    return pl.pallas_call(body, grid=(bsz, t // TOKEN_TILE), in_specs=[row, t_spec],
                          out_specs=[row, pl.BlockSpec((SUBLANE, LANE), lambda b, i: (0, 0))],
                          out_shape=[jax.ShapeDtypeStruct(xu.shape, F32), jax.ShapeDtypeStruct((SUBLANE, LANE), F32)],
                          name="loss", compiler_params=_cparams())(xu, target)


def _row_tile(rows, row_bytes, budget=4 << 20, step=SUBLANE):
    if rows * row_bytes <= budget:
        return rows
    best = None
    for t in range(step, rows, step):
        if rows % t == 0 and t * row_bytes <= budget:
            best = t
    return best if best is not None else rows


def _as3d(x):
    p = x.shape[0]
    c = x.shape[-1]
    return x.reshape(p, -1, c)


def sum_parts(name, x):
    x3 = _as3d(x)
    p, r, c = x3.shape
    tr = _row_tile(r, p * c * 4, step=SUBLANE * (4 // x.dtype.itemsize))

    def body(x_ref, o_ref):
        s = x_ref[0].astype(F32)
        for j in range(1, p):
            s = s + x_ref[j].astype(F32)
        o_ref[...] = s

    out = pl.pallas_call(body, grid=(r // tr,), in_specs=[pl.BlockSpec((p, tr, c), lambda i: (0, i, 0))],
                         out_specs=pl.BlockSpec((tr, c), lambda i: (i, 0)), out_shape=jax.ShapeDtypeStruct((r, c), F32),
                         name=name, compiler_params=_cparams())(x3)
    return out.reshape(x.shape[1:])


def adamw(name, w, g, m, v):
    shape = w.shape
    c = shape[-1]
    to2d = lambda a: a.reshape(-1, c)
    r = to2d(w).shape[0]
    tr = _row_tile(r, 7 * c * 4, budget=6 << 20)

    def body(w_ref, g_ref, m_ref, v_ref, d_ref, nm_ref, nv_ref):
        gg = g_ref[...]
        nm = ADAM_B1 * m_ref[...] + (1.0 - ADAM_B1) * gg
        nv = ADAM_B2 * v_ref[...] + (1.0 - ADAM_B2) * jnp.square(gg)
        m_hat = nm / (1.0 - ADAM_B1 ** ADAM_STEP)
        v_hat = nv / (1.0 - ADAM_B2 ** ADAM_STEP)
        d_ref[...] = -ADAM_LR * (m_hat / (jnp.sqrt(v_hat) + ADAM_EPS) + ADAM_WD * w_ref[...])
        nm_ref[...] = nm
        nv_ref[...] = nv

    spec = pl.BlockSpec((tr, c), lambda i: (i, 0))
    outs = pl.pallas_call(body, grid=(r // tr,), in_specs=[spec] * 4, out_specs=[spec] * 3,
                          out_shape=[jax.ShapeDtypeStruct((r, c), F32)] * 3, name=name, compiler_params=_cparams())(
        to2d(w), to2d(g), to2d(m), to2d(v))
    return tuple(o.reshape(shape) for o in outs)


def exchange(name, srcs, group, same):
    p = 2 ** len(group)
    n = len(srcs)
    out_shape = [jax.ShapeDtypeStruct(((p,) + s.shape) if same else s.shape, s.dtype) for s in srcs]

    def index_of(coords):
        idx = 0
        for a in group:
            idx = idx * 2 + coords[a]
        return idx

    def body(*refs):
        src_refs, out_refs = refs[:n], refs[n:2 * n]
        send_sems, recv_sems = refs[2 * n:]
        pos = {a: lax.axis_index(a) for a in MESH_AXES}
        me = index_of(pos)
        peers = []
        for rel in range(1, p):
            coords = dict(pos)
            for bit, a in enumerate(reversed(group)):
                if (rel >> bit) & 1:
                    coords[a] = 1 - coords[a]
            peers.append((coords, index_of(coords)))

        def src_for(a, idx):
            return src_refs[a] if same else src_refs[a].at[idx]

        sends, recvs = [], []
        for a in range(n):
            for r, (coords, idx) in enumerate(peers):
                dev = tuple(coords[ax] for ax in MESH_AXES)
                send = pltpu.make_async_remote_copy(src_ref=src_for(a, idx), dst_ref=out_refs[a].at[me],
                                                    send_sem=send_sems.at[a, r], recv_sem=recv_sems.at[a, r],
                                                    device_id=dev, device_id_type=pl.DeviceIdType.MESH)
                send.start()
                sends.append(send)
                recvs.append(pltpu.make_async_remote_copy(src_ref=src_for(a, idx), dst_ref=out_refs[a].at[idx],
                                                          send_sem=send_sems.at[a, r], recv_sem=recv_sems.at[a, r],
                                                          device_id=dev, device_id_type=pl.DeviceIdType.MESH))
        for cp in sends:
            cp.wait_send()
        for cp in recvs:
            cp.wait_recv()

    any_spec = pl.BlockSpec(memory_space=pl.ANY)
    outs = pl.pallas_call(body, in_specs=[any_spec] * n, out_specs=[any_spec] * n, out_shape=out_shape,
                          scratch_shapes=[pltpu.SemaphoreType.DMA((n, p - 1)), pltpu.SemaphoreType.DMA((n, p - 1))],
                          name=name)(*srcs)
    me = index_of({a: lax.axis_index(a) for a in MESH_AXES})
    own = [s if same else lax.dynamic_index_in_dim(s, me, axis=0, keepdims=False) for s in srcs]
    return [lax.dynamic_update_index_in_dim(o, w, me, axis=0) for o, w in zip(outs, own)]


def _rope_tables(n_lat, n_ctx, rot_dim, heads):
    n_freq = rot_dim // 4
    tok = jnp.arange(n_lat, dtype=jnp.int32)
    inv = ROPE_THETA ** (-jnp.arange(n_freq, dtype=F32) / n_freq)
    ang = jnp.concatenate([(tok // GRID_W).astype(F32)[:, None] * inv, (tok % GRID_W).astype(F32)[:, None] * inv], axis=-1)
    cos, sin = jnp.cos(ang), jnp.sin(ang)
    cos = jnp.concatenate([jnp.concatenate([cos, cos], -1), jnp.ones((n_ctx, rot_dim), F32)], 0)
    sin = jnp.concatenate([jnp.concatenate([-sin, sin], -1), jnp.zeros((n_ctx, rot_dim), F32)], 0)
    half = rot_dim // 2
    w = heads * rot_dim
    j = np.arange(w)
    src = (j // rot_dim) * rot_dim + (j % rot_dim + half) % rot_dim
    swap = np.zeros((w, w), np.float32)
    swap[src, j] = 1.0
    return jnp.tile(cos, (1, heads)), jnp.tile(sin, (1, heads)), jnp.asarray(swap)


def _to_heads(a, nh):
    b, t, _ = a.shape
    return a.reshape(b, t, nh, -1).transpose(0, 2, 1, 3)


def _to_heads_t(a, nh):
    b, t, _ = a.shape
    return a.reshape(b, t, nh, -1).transpose(0, 2, 3, 1)


def _from_heads(a):
    b, nh, t, w = a.shape
    return a.transpose(0, 2, 1, 3).reshape(b, t, nh * w)


def _pad_w_in(w_in):
    parts, off = [], 0
    for size, pad in zip(IN_SIZES, IN_PAD):
        parts.append(w_in[:, off:off + size])
        if pad > size:
            parts.append(jnp.zeros((w_in.shape[0], pad - size), w_in.dtype))
        off += size
    return jnp.concatenate(parts, axis=1)


def _forward(xu, mods, prm, prm_lo, n_lat, n_ctx):
    bsz, t, d = xu.shape
    depth = mods.shape[0]
    n_lat_tiles = n_lat // TOKEN_TILE
    rope_a = _rope_tables(n_lat, n_ctx, HEAD_DIM, 1)
    rope_bq = _rope_tables(n_lat, n_ctx, B_ROPE, B_HEADS)
    rope_bk = _rope_tables(n_lat, n_ctx, B_ROPE, 1)
    clb = prm["c_lower_bounds"].reshape(depth, 2, C_HEADS, C_DK).transpose(2, 0, 1, 3)
    for l in range(depth):
        nm = lambda s: f"l{l}_{s}"
        vec = lambda name: prm[name][l][None, :]
        xu, h = premod(nm("premix"), xu, vec("g_pre_mix"), mods[l], 0, n_lat_tiles)
        z = linear(nm("w_in"), h.reshape(bsz * t, d), _pad_w_in(prm["w_in"][l]), _pad_w_in(prm_lo["w_in"][l])).reshape(bsz, t, D_IN_PAD)
        seg = lambda i: z[:, :, IN_OFF[i]:IN_OFF[i] + IN_SIZES[i]]
        aq = a_prep(nm("aq"), _to_heads(seg(0), A_HEADS), vec("a_q_norm"), *rope_a)
        ak = a_prep(nm("ak"), _to_heads(seg(1), A_KV_HEADS), vec("a_k_norm"), *rope_a)
        av = _to_heads(seg(2), A_KV_HEADS)
        lower = c_lower(nm("c_lower"), clb, l)
        cq, cv = _to_heads(seg(6), C_HEADS), _to_heads(seg(9), C_HEADS)
        ya, o_f = mix_pair(nm("mix_a"), [aq.reshape(bsz, A_KV_HEADS, A_GROUP, t, HEAD_DIM)], [ak.transpose(0, 1, 3, 2)],
                           av.transpose(0, 1, 3, 2), HEAD_DIM ** -0.5, True, ATT_A_TQ_FWD, ATT_A_TQ, n_lat,
                           cq, _to_heads(seg(7), C_HEADS), cv, lower[:, 0:1], False, n_lat // SCAN_CHUNK)
        ya = _from_heads(ya.reshape(bsz, A_HEADS, t, HEAD_DIM))
        wq = prm["w_q_up"][l].reshape(B_Q_RANK, B_HEADS, B_NOPE + B_ROPE)
        wkv = prm["w_kv_up"][l].reshape(B_KV_RANK, B_HEADS, B_NOPE + B_V)
        qn, qp, kn, bv, kp = b_prep(
            nm("b_prep"), seg(3), seg(4), seg(5), vec("b_q_norm"),
            wq[:, :, :B_NOPE].reshape(B_Q_RANK, -1), wq[:, :, B_NOPE:].reshape(B_Q_RANK, -1), vec("b_kv_norm"),
            wkv[:, :, :B_NOPE].reshape(B_KV_RANK, -1), wkv[:, :, B_NOPE:].reshape(B_KV_RANK, -1), *rope_bq, *rope_bk)
        qb = jnp.concatenate([_to_heads(qn, B_HEADS), _to_heads(qp, B_HEADS)], axis=-1)[:, :, None]
        kbt = jnp.concatenate([_to_heads_t(kn, B_HEADS),
                               jnp.broadcast_to(kp.transpose(0, 2, 1)[:, None], (bsz, B_HEADS, B_ROPE, t))], axis=2)
        yb, o_b = mix_pair(nm("mix_b"), [qb], [kbt], _to_heads_t(bv, B_HEADS), (B_NOPE + B_ROPE) ** -0.5, False, ATT_B_TQ,
                           ATT_B_TQ, n_lat, cq, _to_heads(seg(8), C_HEADS), cv, lower[:, 1:2], True, n_lat // SCAN_CHUNK)
        yb = _from_heads(yb[:, :, 0])
        yc = _from_heads(c_readout(nm("c_out"), o_f, o_b, _to_heads(seg(10), C_HEADS), vec("c_out_norm")))
        y = linear(nm("w_out"), jnp.concatenate([ya, yb, yc], axis=-1).reshape(bsz * t, d), prm["w_out"][l],
                   prm_lo["w_out"][l]).reshape(bsz, t, d)
        x1 = resid(nm("res_mix"), xu, y, vec("g_post_mix"), mods[l], 2, n_lat_tiles)
        x1, h2 = premod(nm("preffn"), x1, vec("g_pre_ffn"), mods[l], 3, n_lat_tiles)
        f = mlp(nm("mlp"), h2.reshape(bsz * t, d), prm["w_ff1"][l], prm["w_ff2"][l], prm_lo["w_ff1"][l],
                prm_lo["w_ff2"][l]).reshape(bsz, t, d)
        xu = resid(nm("res_ffn"), x1, f, vec("g_post_ffn"), mods[l], 5, n_lat_tiles)
    return xu


BIG = {"w_in": 2, "w_q_up": 2, "w_kv_up": 2, "w_out": 1, "w_ff1": 2, "w_ff2": 1}
SMALL = ("g_pre_mix", "g_post_mix", "g_pre_ffn", "g_post_ffn", "a_q_norm", "a_k_norm", "b_q_norm", "b_kv_norm",
         "c_lower_bounds", "c_out_norm")
WEIGHTS = ("c_ctx", "w_ada", "b_ada", "g_pre_mix", "g_post_mix", "g_pre_ffn", "g_post_ffn", "w_in", "a_q_norm", "a_k_norm",
           "b_q_norm", "w_q_up", "b_kv_norm", "w_kv_up", "c_lower_bounds", "c_out_norm", "w_out", "w_ff1", "w_ff2")


def _unshard(g, axis):
    depth, _, r, c = g.shape
    if axis == 1:
        return g.reshape(depth, N_CHIP * r, c)
    return g.transpose(0, 2, 1, 3).reshape(depth, r, N_CHIP * c)


def _shard_major(g, axis):
    depth, r, c = g.shape
    if axis == 1:
        return g.reshape(depth, N_CHIP, r // N_CHIP, c)
    return g.reshape(depth, r, N_CHIP, c // N_CHIP).transpose(0, 2, 1, 3)


def kernel(x, c, ctx, c_ctx, w_ada, b_ada, g_pre_mix, g_post_mix, g_pre_ffn, g_post_ffn, w_in, a_q_norm, a_k_norm, b_q_norm, w_q_up, b_kv_norm, w_kv_up, c_lower_bounds, c_out_norm, w_out, w_ff1, w_ff2, loss_target, m_c_ctx, m_w_ada, m_b_ada, m_g_pre_mix, m_g_post_mix, m_g_pre_ffn, m_g_post_ffn, m_w_in, m_a_q_norm, m_a_k_norm, m_b_q_norm, m_w_q_up, m_b_kv_norm, m_w_kv_up, m_c_lower_bounds, m_c_out_norm, m_w_out, m_w_ff1, m_w_ff2, v_c_ctx, v_w_ada, v_b_ada, v_g_pre_mix, v_g_post_mix, v_g_pre_ffn, v_g_post_ffn, v_w_in, v_a_q_norm, v_a_k_norm, v_b_q_norm, v_w_q_up, v_b_kv_norm, v_w_kv_up, v_c_lower_bounds, v_c_out_norm, v_w_out, v_w_ff1, v_w_ff2):
    local = dict(c_ctx=c_ctx, w_ada=w_ada, b_ada=b_ada, g_pre_mix=g_pre_mix, g_post_mix=g_post_mix, g_pre_ffn=g_pre_ffn,
                 g_post_ffn=g_post_ffn, w_in=w_in, a_q_norm=a_q_norm, a_k_norm=a_k_norm, b_q_norm=b_q_norm, w_q_up=w_q_up,
                 b_kv_norm=b_kv_norm, w_kv_up=w_kv_up, c_lower_bounds=c_lower_bounds, c_out_norm=c_out_norm, w_out=w_out,
                 w_ff1=w_ff1, w_ff2=w_ff2)
    mom = dict(c_ctx=m_c_ctx, w_ada=m_w_ada, b_ada=m_b_ada, g_pre_mix=m_g_pre_mix, g_post_mix=m_g_post_mix,
               g_pre_ffn=m_g_pre_ffn, g_post_ffn=m_g_post_ffn, w_in=m_w_in, a_q_norm=m_a_q_norm, a_k_norm=m_a_k_norm,
               b_q_norm=m_b_q_norm, w_q_up=m_w_q_up, b_kv_norm=m_b_kv_norm, w_kv_up=m_w_kv_up,
               c_lower_bounds=m_c_lower_bounds, c_out_norm=m_c_out_norm, w_out=m_w_out, w_ff1=m_w_ff1, w_ff2=m_w_ff2)
    var = dict(c_ctx=v_c_ctx, w_ada=v_w_ada, b_ada=v_b_ada, g_pre_mix=v_g_pre_mix, g_post_mix=v_g_post_mix,
               g_pre_ffn=v_g_pre_ffn, g_post_ffn=v_g_post_ffn, w_in=v_w_in, a_q_norm=v_a_q_norm, a_k_norm=v_a_k_norm,
               b_q_norm=v_b_q_norm, w_q_up=v_w_q_up, b_kv_norm=v_b_kv_norm, w_kv_up=v_w_kv_up,
               c_lower_bounds=v_c_lower_bounds, c_out_norm=v_c_out_norm, w_out=v_w_out, w_ff1=v_w_ff1, w_ff2=v_w_ff2)

    bsz, n_lat, d = x.shape
    n_ctx = ctx.shape[1]
    depth = w_ada.shape[0]
    assert depth == 2 and n_lat % TOKEN_TILE == 0 and n_ctx % TOKEN_TILE == 0 and bsz * N_DEV + 1 <= ADA_ROWS
    ax, ay, ac = (lax.axis_index(a) for a in MESH_AXES)
    chip = 2 * ax + ay
    dev = 2 * chip + ac

    c_all = exchange("gather_c", [c], MESH_AXES, True)[0].reshape(N_DEV * bsz, d)
    big_names = list(BIG)
    mine = [lax.dynamic_index_in_dim(local[n].astype(BF16), ac, axis=0, keepdims=False) for n in big_names]
    over_chips = exchange("gather_w_quad", mine + [c_lower_bounds], ("x", "y"), True)
    both_layers = exchange("gather_w_pair", over_chips[:-1], ("c",), True)
    prm_lo = {n: _unshard(g, BIG[n]) for n, g in zip(big_names, both_layers)}
    prm = {n: w.astype(F32) for n, w in prm_lo.items()}
    prm["c_lower_bounds"] = over_chips[-1].transpose(1, 2, 0, 3).reshape(depth, 2, -1)
    for n in SMALL:
        if n != "c_lower_bounds":
            prm[n] = local[n]

    n_ada = w_ada.shape[2]
    cc = jnp.concatenate([c_all, c_ctx[None, :], jnp.zeros((ADA_ROWS - N_DEV * bsz - 1, d), F32)], axis=0)
    b_blk = lax.dynamic_slice_in_dim(b_ada, chip * n_ada, n_ada, axis=1)[:, None, :]
    mod_part, ada_vjp = jax.vjp(ada_op, cc, w_ada, b_blk)
    mod_full = exchange("gather_mod", [mod_part], ("x", "y"), True)[0].transpose(1, 2, 0, 3).reshape(depth, ADA_ROWS, 6 * d)
    mod_lat = lax.dynamic_slice_in_dim(mod_full, dev * bsz, bsz, axis=1)
    mod_ctx = jnp.broadcast_to(mod_full[:, N_DEV * bsz:N_DEV * bsz + 1], mod_lat.shape)
    mods = jnp.stack([mod_lat, mod_ctx], axis=2).reshape(depth, bsz, 2, 6, d)

    xu = jnp.concatenate([x, ctx], axis=1)
    x_out, fwd_vjp = jax.vjp(lambda xu_, mods_, prm_: _forward(xu_, mods_, prm_, prm_lo, n_lat, n_ctx), xu, mods, prm)
    dx_out, loss_blk = loss_op(x_out, loss_target, n_lat // TOKEN_TILE)
    d_xu, d_mods, d_prm = fwd_vjp(dx_out)
    grad_x = d_xu[:, :n_lat]

    d_mods = d_mods.reshape(depth, bsz, 2, 6 * d)
    pieces = [d_mods] + [d_prm[n] for n in SMALL] + [loss_blk[0:1, 0:1]]
    sizes = [int(np.prod(p.shape)) for p in pieces]
    flat = jnp.concatenate([p.reshape(-1) for p in pieces])
    n_flat = -(-flat.shape[0] // (SUBLANE * LANE)) * SUBLANE * LANE
    flat = jnp.concatenate([flat, jnp.zeros((n_flat - flat.shape[0],), F32)]).reshape(-1, LANE)
    small_all = exchange("gather_small", [flat], MESH_AXES, True)[0]
    small_sum = sum_parts("sum_small", small_all).reshape(-1)
    offs = np.cumsum([0] + sizes)
    summed = {n: small_sum[offs[i + 1]:offs[i + 2]].reshape(d_prm[n].shape) for i, n in enumerate(SMALL)}
    loss = small_sum[offs[-2]]
    dm_all = small_all.reshape(N_DEV, -1)[:, :sizes[0]].reshape(N_DEV, depth, bsz, 2, 6 * d)
    dm_rows = dm_all.transpose(3, 0, 2, 1, 4).reshape(2, N_DEV * bsz, depth, 6 * d)
    d_ctx_row = sum_parts("sum_dmod_ctx", dm_rows[1])
    grad_b_ada = sum_parts("sum_b_ada", dm_rows.reshape(2 * N_DEV * bsz, depth, 6 * d))
    d_rows = jnp.concatenate([dm_rows[0].transpose(1, 0, 2), d_ctx_row[:, None, :],
                              jnp.zeros((depth, ADA_ROWS - N_DEV * bsz - 1, 6 * d), F32)], axis=1)
    d_cc, grad_w_ada, _ = ada_vjp(lax.dynamic_slice_in_dim(d_rows, chip * n_ada, n_ada, axis=2))
    d_cctx_all = exchange("gather_dcctx", [d_cc[N_DEV * bsz:N_DEV * bsz + 1]], MESH_AXES, True)[0]
    grad_c_ctx = sum_parts("sum_dcctx", d_cctx_all[0::2]).reshape(d)

    shard_major = [_shard_major(d_prm[n], BIG[n]).astype(BF16) for n in big_names]
    pair = exchange("rs_pair", shard_major, ("c",), False)
    chip_sum = [sum_parts(f"rs_sum1_{n}", p) for n, p in zip(big_names, pair)]
    quad = exchange("rs_quad", [s.astype(BF16) for s in chip_sum], ("x", "y"), False)
    total = [sum_parts(f"rs_sum2_{n}", q) for n, q in zip(big_names, quad)]
    both = exchange("rs_share", total, ("c",), True)

    grads = dict(summed)
    grads["c_lower_bounds"] = lax.dynamic_slice_in_dim(summed["c_lower_bounds"], chip * c_lower_bounds.shape[2],
                                                       c_lower_bounds.shape[2], axis=2)
    grads.update(c_ctx=grad_c_ctx, w_ada=grad_w_ada, b_ada=grad_b_ada)
    grads.update({n: g for n, g in zip(big_names, both)})

    deltas, new_m, new_v = {}, {}, {}
    for n in WEIGHTS:
        as2d = (lambda a: a[None, :]) if local[n].ndim == 1 else (lambda a: a)
        dl, nm_, nv_ = adamw("adamw_" + n, as2d(local[n]), as2d(grads[n]), as2d(mom[n]), as2d(var[n]))
        deltas[n], new_m[n], new_v[n] = (a.reshape(local[n].shape) for a in (dl, nm_, nv_))
    return (loss, grad_x, *[grads[n] for n in WEIGHTS], *[deltas[n] for n in WEIGHTS],
            *[new_m[n] for n in WEIGHTS], *[new_v[n] for n in WEIGHTS])
```

```python
import functools
from typing import Any, Callable, NamedTuple

import numpy as np
import jax
import jax.numpy as jnp
from jax import lax
from jax.experimental import pallas as pl
from jax.experimental.pallas import tpu as pltpu

F32 = jnp.float32
BF16 = jnp.bfloat16
HIGHEST = lax.Precision.HIGHEST

GRID_W = 64
HEAD_DIM = 64
A_HEADS, A_KV_HEADS = 8, 2
A_GROUP = A_HEADS // A_KV_HEADS
B_HEADS, B_Q_RANK, B_KV_RANK, B_NOPE, B_ROPE, B_V = 4, 192, 128, 64, 32, 64
C_HEADS, C_DK, C_DV = 4, 64, 64
SCAN_CHUNK = 64
SCAN_SUB = 16
ROPE_THETA = 10000.0
EPS = 1e-6
F_TINY = 1e-30
EXP_ARG_MAX = 80.0
ADAM_LR, ADAM_B1, ADAM_B2, ADAM_EPS, ADAM_WD, ADAM_STEP = 0.001, 0.9, 0.999, 1e-08, 0.01, 10

IN_SIZES = (512, 128, 128, 192, 128, 32, 256, 256, 256, 256, 256)
IN_PAD = (512, 128, 128, 256, 128, 128, 256, 256, 256, 256, 256)
IN_OFF = tuple(int(v) for v in np.cumsum((0,) + IN_PAD)[:-1])
D_IN_PAD = int(sum(IN_PAD))

LANE = 128
SUBLANE = 8
TOKEN_TILE = 256
ATT_A_TQ = 64
ATT_A_TQ_FWD = 128
ATT_B_TQ = 256
ATT_FWD_ROW_GROUPS = 4
MM_ROWS = 256
MM_TK_TOKENS = 512
MM_TN_OUT = 2560 * 1024
VMEM_LIMIT = 56 * 1024 * 1024
MESH_AXES = ("x", "y", "c")
N_DEV = 8
N_CHIP = 4
ADA_ROWS = 24


class Arg(NamedTuple):
    arr: Any
    block: tuple
    imap: Callable
    kind: str
    first: Callable = None


class Out(NamedTuple):
    shape: tuple
    block: tuple
    imap: Callable


def _cparams():
    return pltpu.CompilerParams(vmem_limit_bytes=VMEM_LIMIT)


def tile_op(name, fn, grid, args, outs, residual=None, passthrough=False):
    n_in, n_out = len(args), len(outs)
    in_specs = [pl.BlockSpec(a.block, a.imap) for a in args]
    out_specs = [pl.BlockSpec(o.block, o.imap) for o in outs]
    out_shape = [jax.ShapeDtypeStruct(o.shape, F32) for o in outs]
    diff = [i for i, a in enumerate(args) if a.kind != "const"]

    def fwd_call(*arrays):
        def body(*refs):
            ids = tuple(pl.program_id(i) for i in range(len(grid)))
            res = list(fn(ids, *[r[...] for r in refs[:n_in]]))
            out_refs = refs[n_in:]
            if residual is not None:
                res[0] = res[0] + refs[n_in][...]
                out_refs = refs[n_in + 1:]
            for r, o in zip(out_refs, res):
                r[...] = o

        specs = in_specs + ([out_specs[0]] if residual is not None else [])
        return pl.pallas_call(body, grid=grid, in_specs=specs, out_specs=out_specs, out_shape=out_shape,
                              name=name + "_fwd", compiler_params=_cparams())(*arrays)

    def bwd_call(arrays, cts):
        def body(*refs):
            ids = tuple(pl.program_id(i) for i in range(len(grid)))
            vals = [r[...] for r in refs[:n_in]]
            ct = tuple(r[...] for r in refs[n_in:n_in + n_out])
            drefs = refs[n_in + n_out:]

            def g(*dv):
                full = list(vals)
                for i, v in zip(diff, dv):
                    full[i] = v
                return tuple(fn(ids, *full))

            _, vjp = jax.vjp(g, *[vals[i] for i in diff])
            ds = list(vjp(ct))
            if passthrough:
                ds[0] = ds[0] + refs[n_in + n_out][...]
                drefs = refs[n_in + n_out + 1:]
            for i, d, r in zip(diff, ds, drefs):
                if args[i].kind == "tile":
                    r[...] = d
                else:
                    is_first = args[i].first(ids)

                    @pl.when(is_first)
                    def _(r=r, d=d):
                        r[...] = d

                    @pl.when(jnp.logical_not(is_first))
                    def _(r=r, d=d):
                        r[...] += d

        d_specs = [in_specs[i] for i in diff]
        d_shape = [jax.ShapeDtypeStruct(arrays[i].shape, F32) for i in diff]
        specs = in_specs + out_specs + ([in_specs[0]] if passthrough else [])
        return pl.pallas_call(body, grid=grid, in_specs=specs, out_specs=d_specs, out_shape=d_shape,
                              name=name + "_bwd", compiler_params=_cparams())(*arrays, *cts)

    assert not passthrough or (diff and diff[0] == 0 and args[0].kind == "tile")

    def results(arrays):
        res = tuple(fwd_call(*arrays))
        return ((arrays[0],) + res) if passthrough else res

    @jax.custom_vjp
    def op(*arrays):
        return results(arrays)

    def op_fwd(*arrays):
        return results(arrays), arrays[:n_in]

    def op_bwd(arrays, cts):
        if passthrough:
            cts = tuple(cts[1:]) + (cts[0],)
        ds = bwd_call(arrays, cts)
        res, k = [], 0
        for i, a in enumerate(args):
            if a.kind == "const":
                res.append(jnp.zeros_like(arrays[i]))
            else:
                res.append(ds[k])
                k += 1
        if residual is not None:
            res.append(cts[0])
        return tuple(res)

    op.defvjp(op_fwd, op_bwd)
    return op(*[a.arr for a in args], *([residual] if residual is not None else []))


def _pick(n, cap):
    if n <= cap:
        return n
    best = None
    for t in range(LANE, cap + 1, LANE):
        if n % t == 0:
            best = t
    assert best is not None, (n, cap)
    return best


_NN = (((1,), (0,)), ((), ()))
_NT = (((1,), (1,)), ((), ()))
_TN = (((0,), (0,)), ((), ()))


def _resident(shape):
    return pl.BlockSpec(shape, lambda *ids: (0,) * len(shape), pipeline_mode=pl.Buffered(1))


def _mm_rows(name, a, w, transposed):
    m, k = a.shape
    n = w.shape[0] if transposed else w.shape[1]
    tm = MM_ROWS
    dims = _NT if transposed else _NN

    def body(a_ref, w_ref, o_ref):
        o_ref[...] = lax.dot_general(a_ref[...].astype(BF16), w_ref[...], dims, preferred_element_type=F32)

    return pl.pallas_call(body, grid=(m // tm,), in_specs=[pl.BlockSpec((tm, k), lambda i: (i, 0)), _resident(w.shape)],
                          out_specs=pl.BlockSpec((tm, n), lambda i: (i, 0)), out_shape=jax.ShapeDtypeStruct((m, n), F32),
                          name=name, compiler_params=_cparams())(a, w)


def _mm_tn(name, a, g):
    t, k = a.shape
    n = g.shape[1]
    tko, tno = k, n
    while tko * tno > MM_TN_OUT:
        if tko >= tno:
            tko //= 2
        else:
            tno //= 2
    assert k % tko == 0 and n % tno == 0 and tko % LANE == 0 and tno % LANE == 0
    tt = _pick(t, MM_TK_TOKENS)

    def body(a_ref, g_ref, o_ref):
        p = lax.dot_general(a_ref[...].astype(BF16), g_ref[...].astype(BF16), _TN, preferred_element_type=F32)
        kk = pl.program_id(2)

        @pl.when(kk == 0)
        def _():
            o_ref[...] = p

        @pl.when(kk != 0)
        def _():
            o_ref[...] += p

    return pl.pallas_call(body, grid=(k // tko, n // tno, t // tt),
                          in_specs=[pl.BlockSpec((tt, tko), lambda i, j, kk: (kk, i)), pl.BlockSpec((tt, tno), lambda i, j, kk: (kk, j))],
                          out_specs=pl.BlockSpec((tko, tno), lambda i, j, kk: (i, j)),
                          out_shape=jax.ShapeDtypeStruct((k, n), F32), name=name, compiler_params=_cparams())(a, g)


def linear(name, a, w, w_lo):
    @jax.custom_vjp
    def op(a, w, w_lo):
        return _mm_rows(name + "_fwd", a, w_lo, False)

    def op_fwd(a, w, w_lo):
        return _mm_rows(name + "_fwd", a, w_lo, False), (a, w_lo)

    def op_bwd(res, g):
        a, w_lo = res
        return _mm_rows(name + "_da", g, w_lo, True), _mm_tn(name + "_dw", a, g), jnp.zeros_like(w_lo)

    op.defvjp(op_fwd, op_bwd)
    return op(a, w, w_lo)


def mlp(name, h, w1, w2, w1_lo, w2_lo):
    m, d = h.shape
    f = w1_lo.shape[1]
    tm = MM_ROWS
    row = pl.BlockSpec((tm, d), lambda i: (i, 0))
    wide = pl.BlockSpec((tm, f), lambda i: (i, 0))

    def fwd_call(h, w1_lo, w2_lo):
        def body(h_ref, w1_ref, w2_ref, o_ref):
            u = lax.dot_general(h_ref[...].astype(BF16), w1_ref[...], _NN, preferred_element_type=F32)
            act = jnp.square(jnp.maximum(u, 0.0))
            o_ref[...] = lax.dot_general(act.astype(BF16), w2_ref[...], _NN, preferred_element_type=F32)

        return pl.pallas_call(body, grid=(m // tm,), in_specs=[row, _resident(w1_lo.shape), _resident(w2_lo.shape)],
                              out_specs=row, out_shape=jax.ShapeDtypeStruct((m, d), F32), name=name + "_fwd",
                              compiler_params=_cparams())(h, w1_lo, w2_lo)

    def bwd_call(h, w1_lo, w2_lo, dy):
        def body(h_ref, dy_ref, w1_ref, w2_ref, dh_ref, act_ref, du_ref):
            u = lax.dot_general(h_ref[...].astype(BF16), w1_ref[...], _NN, preferred_element_type=F32)
            r = jnp.maximum(u, 0.0)
            act_ref[...] = (r * r).astype(BF16)
            dact = lax.dot_general(dy_ref[...].astype(BF16), w2_ref[...], _NT, preferred_element_type=F32)
            du = (dact * (2.0 * r)).astype(BF16)
            du_ref[...] = du
            dh_ref[...] = lax.dot_general(du, w1_ref[...], _NT, preferred_element_type=F32)

        return pl.pallas_call(body, grid=(m // tm,), in_specs=[row, row, _resident(w1_lo.shape), _resident(w2_lo.shape)],
                              out_specs=[row, wide, wide],
                              out_shape=[jax.ShapeDtypeStruct((m, d), F32), jax.ShapeDtypeStruct((m, f), BF16),
                                         jax.ShapeDtypeStruct((m, f), BF16)],
                              name=name + "_bwd", compiler_params=_cparams())(h, dy, w1_lo, w2_lo)

    @jax.custom_vjp
    def op(h, w1, w2, w1_lo, w2_lo):
        return fwd_call(h, w1_lo, w2_lo)

    def op_fwd(h, w1, w2, w1_lo, w2_lo):
        return fwd_call(h, w1_lo, w2_lo), (h, w1_lo, w2_lo)

    def op_bwd(res, dy):
        h, w1_lo, w2_lo = res
        dh, act, du = bwd_call(h, w1_lo, w2_lo, dy)
        return (dh, _mm_tn(name + "_dw1", h, du), _mm_tn(name + "_dw2", act, dy), jnp.zeros_like(w1_lo), jnp.zeros_like(w2_lo))

    op.defvjp(op_fwd, op_bwd)
    return op(h, w1, w2, w1_lo, w2_lo)


def _rms(x, g):
    return x * lax.rsqrt(jnp.mean(x * x, axis=-1, keepdims=True) + EPS) * g


def _sigmoid(z):
    return 1.0 / (1.0 + jnp.exp(jnp.minimum(-z, EXP_ARG_MAX)))


def _silu(z):
    return z * _sigmoid(z)


def _rope(y, cos, sin_signed, swap):
    return y * cos + jnp.dot(y, swap, precision=HIGHEST, preferred_element_type=F32) * sin_signed


def _dot_bf16(a, b, dims=((1,), (0,))):
    return lax.dot_general(a.astype(BF16), b.astype(BF16), (dims, ((), ())), preferred_element_type=F32)


def _gla_step(state, q, k, v, g, reverse):
    c, d = q.shape
    sub = SCAN_SUB
    nb = c // sub
    row = lax.broadcasted_iota(jnp.int32, (c, c), 0)
    col = lax.broadcasted_iota(jnp.int32, (c, c), 1)
    tri = (row <= col) if reverse else (row >= col)
    b = jnp.dot(tri.astype(F32), g, precision=HIGHEST, preferred_element_type=F32)
    o = jnp.dot(q * jnp.exp(b), state, preferred_element_type=F32)
    rs = lax.broadcasted_iota(jnp.int32, (sub, sub), 0)
    cs = lax.broadcasted_iota(jnp.int32, (sub, sub), 1)
    tri_s = ((rs <= cs) if reverse else (rs >= cs)).astype(F32)
    rowc = lax.broadcasted_iota(jnp.int32, (c, 1), 0)
    nonpos = lambda x: jnp.where(x > 0.0, 0.0, x)
    diag = []
    for j in range(nb):
        sl = slice(j * sub, (j + 1) * sub)
        bj, kj, vj, qj = b[sl], k[sl], v[sl], q[sl]
        dec = jnp.exp(nonpos(bj[:, None, :] - bj[None, :, :]))
        sc = jnp.sum(qj[:, None, :] * kj[None, :, :] * dec, axis=-1) * tri_s
        diag.append(jnp.dot(sc, vj, preferred_element_type=F32))
        if (j > 0) if reverse else (j < nb - 1):
            ref = bj[0:1] if reverse else bj[sub - 1:sub]
            qa = q * jnp.exp(nonpos(b - ref))
            ks = kj * jnp.exp(ref - bj)
            scj = lax.dot_general(qa, ks, (((1,), (1,)), ((), ())), precision=HIGHEST, preferred_element_type=F32)
            later = (rowc < j * sub) if reverse else (rowc >= (j + 1) * sub)
            o = o + jnp.dot(jnp.where(later, scj, 0.0), vj, preferred_element_type=F32)
    o = o + jnp.concatenate(diag, axis=0)
    b_end = b[0:1, :] if reverse else b[c - 1:c, :]
    kd = k * jnp.exp(b_end - b)
    new_state = state * jnp.exp(b_end).reshape(d, 1) + lax.dot_general(kd, v, (((0,), (0,)), ((), ())), preferred_element_type=F32)
    return new_state, o


def _zero_ids(ids):
    z = ids[0] == 0
    for i in ids[1:]:
        z = jnp.logical_and(z, i == 0)
    return z


def _token_grid(x, n_lat_tiles):
    bsz, t, d = x.shape
    nt = t // TOKEN_TILE
    row = lambda w: ((None, TOKEN_TILE, w), lambda b, i: (b, i, 0))
    mod_block = (None, None, 6, d)
    mod_imap = lambda b, i: (b, (i >= n_lat_tiles).astype(jnp.int32), 0, 0)
    mod_first = lambda ids: jnp.logical_or(ids[1] == 0, ids[1] == n_lat_tiles)
    return bsz, t, d, nt, row, (mod_block, mod_imap, mod_first)


def premod(name, x, gain, mods, r0, n_lat_tiles):
    bsz, t, d, nt, row, (mb, mi, mf) = _token_grid(x, n_lat_tiles)

    def fn(ids, xb, gb, mod):
        return (_rms(xb, gb) * (1.0 + mod[r0 + 1:r0 + 2]) + mod[r0:r0 + 1],)

    args = [Arg(x, *row(d), "tile"), Arg(gain, (1, d), lambda b, i: (0, 0), "acc", _zero_ids), Arg(mods, mb, mi, "acc", mf)]
    return tile_op(name, fn, (bsz, nt), args, [Out(x.shape, *row(d))], passthrough=True)


def resid(name, x, y, gain, mods, r, n_lat_tiles):
    bsz, t, d, nt, row, (mb, mi, mf) = _token_grid(x, n_lat_tiles)

    def fn(ids, yb, gb, mod):
        return (mod[r:r + 1] * _rms(yb, gb),)

    args = [Arg(y, *row(d), "tile"), Arg(gain, (1, d), lambda b, i: (0, 0), "acc", _zero_ids), Arg(mods, mb, mi, "acc", mf)]
    return tile_op(name, fn, (bsz, nt), args, [Out(x.shape, *row(d))], residual=x)[0]


def _heads_spec(nh, w):
    return (None, nh, TOKEN_TILE, w), lambda b, i: (b, 0, i, 0)


def a_prep(name, x, gain, cos, sin, swap):
    bsz, nh, t, d = x.shape
    tab = ((TOKEN_TILE, d), lambda b, i: (i, 0))

    def fn(ids, xb, gb, cb, sb, pb):
        y = _rms(xb, gb)
        swapped = jnp.dot(y.reshape(nh * TOKEN_TILE, d), pb, precision=HIGHEST, preferred_element_type=F32)
        return (y * cb + swapped.reshape(nh, TOKEN_TILE, d) * sb,)

    args = [Arg(x, *_heads_spec(nh, d), "tile"), Arg(gain, (1, d), lambda b, i: (0, 0), "acc", _zero_ids),
            Arg(cos, *tab, "const"), Arg(sin, *tab, "const"), Arg(swap, (d, d), lambda b, i: (0, 0), "const")]
    return tile_op(name, fn, (bsz, t // TOKEN_TILE), args, [Out(x.shape, *_heads_spec(nh, d))])[0]


def b_prep(name, bqd, bkvd, bkr, bqn, wq_nope, wq_pe, bkvn, wkv_nope, wkv_v, cos_q, sin_q, swap_q, cos_k, sin_k, swap_k):
    bsz, t, _ = bqd.shape
    row = lambda w: ((None, TOKEN_TILE, w), lambda b, i: (b, i, 0))
    whole = lambda a: (a.shape, lambda b, i: (0,) * a.ndim)
    tab = lambda w: ((TOKEN_TILE, w), lambda b, i: (i, 0))

    def fn(ids, qd, kvd, kr, qn, wqn, wqp, kvn, wkn, wkv, cq, sq, pq, ck, sk, pk):
        hq = _rms(qd, qn)
        hkv = _rms(kvd, kvn)
        return (_dot_bf16(hq, wqn), _rope(_dot_bf16(hq, wqp), cq, sq, pq), _dot_bf16(hkv, wkn), _dot_bf16(hkv, wkv),
                _rope(kr, ck, sk, pk))

    params = [bqn, wq_nope, wq_pe, bkvn, wkv_nope, wkv_v]
    args = [Arg(bqd, *row(B_Q_RANK), "tile"), Arg(bkvd, *row(B_KV_RANK), "tile"), Arg(bkr, *row(B_ROPE), "tile")]
    args += [Arg(p, *whole(p), "acc", _zero_ids) for p in params]
    args += [Arg(cos_q, *tab(cos_q.shape[1]), "const"), Arg(sin_q, *tab(cos_q.shape[1]), "const"), Arg(swap_q, *whole(swap_q), "const"),
             Arg(cos_k, *tab(B_ROPE), "const"), Arg(sin_k, *tab(B_ROPE), "const"), Arg(swap_k, *whole(swap_k), "const")]
    widths = (B_HEADS * B_NOPE, B_HEADS * B_ROPE, B_HEADS * B_NOPE, B_HEADS * B_V, B_ROPE)
    outs = [Out((bsz, t, w), *row(w)) for w in widths]
    return tile_op(name, fn, (bsz, t // TOKEN_TILE), args, outs)


def c_lower(name, clb, layer):
    nh, depth, _, d = clb.shape

    def fn(ids, lbs):
        lb = [lbs[:, j] for j in range(depth)]
        m = lb[0]
        for j in range(1, depth):
            m = jnp.maximum(m, lb[j])
        e = [jnp.exp(lb[j] - m) for j in range(depth)]
        tot = e[0]
        for j in range(1, depth):
            tot = tot + e[j]
        p = [ej / tot for ej in e]
        cum = p[0]
        for j in range(1, layer + 1):
            cum = cum + p[j]
        return (cum - p[0],)

    whole = lambda shape: (shape, lambda i: (0,) * len(shape))
    return tile_op(name, fn, (1,), [Arg(clb, *whole(clb.shape), "tile")], [Out((nh, 2, d), *whole((nh, 2, d)))])[0]


def c_readout(name, o_f, o_b, gate, gain):
    bsz, nh, t, d = o_f.shape
    spec = _heads_spec(nh, d)

    def fn(ids, of, ob, gt, gn):
        return (_rms(of + ob, gn) * _silu(gt),)

    args = [Arg(o_f, *spec, "tile"), Arg(o_b, *spec, "tile"), Arg(gate, *spec, "tile"),
            Arg(gain, (1, d), lambda b, i: (0, 0), "acc", _zero_ids)]
    return tile_op(name, fn, (bsz, t // TOKEN_TILE), args, [Out(o_f.shape, *spec)])[0]


def _att_rows_fwd(qq, kts_lo, vt_lo, scale, fold_scale, n_grp):
    rows = qq[0].shape[0]
    gr = rows // n_grp
    outs, lses = [], []
    for r in range(n_grp):
        s = None
        for q2, kt in zip(qq, kts_lo):
            part = lax.dot_general(q2[r * gr:(r + 1) * gr], kt, _NN, preferred_element_type=F32)
            s = part if s is None else s + part
        if not fold_scale:
            s = s * scale
        m = jnp.max(s, axis=-1, keepdims=True)
        e = jnp.exp(s - m)
        l = jnp.sum(e, axis=-1, keepdims=True)
        outs.append(lax.dot_general(e.astype(BF16), vt_lo, _NT, preferred_element_type=F32) * (1.0 / l))
        lses.append(m + jnp.log(l))
    return jnp.concatenate(outs, axis=0), jnp.concatenate(lses, axis=0)


def _att_rows_bwd(qq, kts_lo, vt_lo, o2, lse, do2, scale, fold_scale):
    s = None
    for q2, kt in zip(qq, kts_lo):
        part = lax.dot_general(q2, kt, _NN, preferred_element_type=F32)
        s = part if s is None else s + part
    if not fold_scale:
        s = s * scale
    p = jnp.exp(s - lse)
    delta = jnp.sum(do2 * o2, axis=-1, keepdims=True)
    do_lo = do2.astype(BF16)
    ds = p * (lax.dot_general(do_lo, vt_lo, _NN, preferred_element_type=F32) - delta)
    if not fold_scale:
        ds = ds * scale
    ds_lo = ds.astype(BF16)
    dvt = lax.dot_general(do_lo, p.astype(BF16), _TN, preferred_element_type=F32)
    dqs, dkts = [], []
    for q2, kt in zip(qq, kts_lo):
        dq = lax.dot_general(ds_lo, kt, _NT, preferred_element_type=F32)
        dqs.append(dq * scale if fold_scale else dq)
        dkts.append(lax.dot_general(q2, ds_lo, _TN, preferred_element_type=F32))
    return dqs, dkts, dvt


def mix_pair(name, qs, kts, vt, scale, fold_scale, sub_fwd, sub_bwd, n_lat, sq, sz, sv, lb, reverse, n_lat_chunks):
    bsz, hk, grp, t, _ = qs[0].shape
    dv = vt.shape[2]
    n_parts = len(qs)
    nh, d = sq.shape[1], sq.shape[3]
    c = SCAN_CHUNK
    n = t // c
    tph = n // hk
    tq = t // tph
    assert tph * hk == n and tq * tph == t and tq % sub_fwd == 0 and tq % sub_bwd == 0 and n_lat % tq == 0
    chains = [(b, h) for b in range(bsz) for h in range(nh)]
    widths = [q.shape[-1] for q in qs]

    def chunk_of(j):
        return (n - 1 - j) if reverse else lax.rem(j + n_lat_chunks, n)

    def chain(state, q_raw, z, v, lo_b):
        f = lo_b + (1.0 - lo_b) * _sigmoid(z)
        return _gla_step(state, _silu(q_raw), (1.0 - lo_b) * _sigmoid(-z), v, jnp.log(jnp.maximum(f, F_TINY)), reverse)

    lb_spec = pl.BlockSpec(lb.shape, lambda j: (0, 0, 0))
    head_of = lambda j: j // tph
    tile_of = lambda j: lax.rem(j, tph)
    per_tile = lambda w: pl.BlockSpec((bsz, None, grp, tq, w), lambda j: (0, head_of(j), 0, tile_of(j), 0))
    per_head = lambda w: pl.BlockSpec((bsz, None, w, t), lambda j: (0, head_of(j), 0, 0))
    q_specs = [per_tile(w) for w in widths]
    kt_specs = [per_head(w) for w in widths]
    att_in = q_specs + kt_specs + [per_head(dv)]
    o_shape = jax.ShapeDtypeStruct((bsz, hk, grp, t, dv), F32)
    lse_shape = jax.ShapeDtypeStruct((bsz, hk, grp, t, LANE), F32)
    st_shape = jax.ShapeDtypeStruct((bsz, nh, n, d, d), F32)
    ranges = ((lambda j: tile_of(j) * tq < n_lat, 0), (lambda j: tile_of(j) * tq >= n_lat, n_lat))

    def load_q(q_refs, b, r0, sub):
        qq = []
        for q_ref, w in zip(q_refs, widths):
            q2 = q_ref[b, :, r0:r0 + sub, :].reshape(grp * sub, w)
            qq.append((q2 * scale if fold_scale else q2).astype(BF16))
        return qq

    def fwd_call(*arrays):
        def body(*refs):
            q_refs, kt_refs, vt_ref = refs[:n_parts], refs[n_parts:2 * n_parts], refs[2 * n_parts]
            sq_ref, sz_ref, sv_ref, lb_ref = refs[2 * n_parts + 1:2 * n_parts + 5]
            o_ref, lse_ref, so_ref, states_ref, st = refs[2 * n_parts + 5:]
            j = pl.program_id(0)

            @pl.when(j == 0)
            def _():
                st[...] = jnp.zeros_like(st)

            def step(col0):
                for bh in chains:
                    s = st[bh]
                    states_ref[bh] = s
                    ns, o = chain(s, sq_ref[bh], sz_ref[bh], sv_ref[bh], lb_ref[bh[1]])
                    st[bh] = ns
                    so_ref[bh] = o
                for b in range(bsz):
                    kts_lo = [r[b, :, col0:] for r in kt_refs]
                    vt_lo = vt_ref[b, :, col0:]
                    for r0 in range(0, tq, sub_fwd):
                        o, lse = _att_rows_fwd(load_q(q_refs, b, r0, sub_fwd), kts_lo, vt_lo, scale, fold_scale, ATT_FWD_ROW_GROUPS)
                        o_ref[b, :, r0:r0 + sub_fwd, :] = o.reshape(grp, sub_fwd, dv)
                        lse_ref[b, :, r0:r0 + sub_fwd, :] = jnp.broadcast_to(lse, (grp * sub_fwd, LANE)).reshape(grp, sub_fwd, LANE)

            for cond, col0 in ranges:
                pl.when(cond(j))(functools.partial(step, col0))

        scan_spec = pl.BlockSpec((bsz, nh, c, d), lambda j: (0, 0, chunk_of(j), 0))
        return pl.pallas_call(
            body, grid=(n,), in_specs=att_in + [scan_spec] * 3 + [lb_spec],
            out_specs=[per_tile(dv), per_tile(LANE), scan_spec, pl.BlockSpec((bsz, nh, None, d, d), lambda j: (0, 0, j, 0, 0))],
            out_shape=[o_shape, lse_shape, jax.ShapeDtypeStruct(sq.shape, F32), st_shape],
            scratch_shapes=[pltpu.VMEM((bsz, nh, d, d), F32)], name=name + "_fwd", compiler_params=_cparams())(*arrays)

    def bwd_call(att_lo, scan_in, o, lse, states, do, dso):
        def body(*refs):
            q_refs, kt_refs, vt_ref = refs[:n_parts], refs[n_parts:2 * n_parts], refs[2 * n_parts]
            k0 = 2 * n_parts + 1
            sq_ref, sz_ref, sv_ref, lb_ref, s_ref = refs[k0:k0 + 5]
            o_ref, lse_ref, do_ref, dso_ref = refs[k0 + 5:k0 + 9]
            d_refs = refs[k0 + 9:]
            dq_refs, dkt_refs, dvt_ref = d_refs[:n_parts], d_refs[n_parts:2 * n_parts], d_refs[2 * n_parts]
            dsq_ref, dsz_ref, dsv_ref, dlb_ref, dst = d_refs[2 * n_parts + 1:]
            jj = pl.program_id(0)

            @pl.when(jj == 0)
            def _():
                dst[...] = jnp.zeros_like(dst)
                dlb_ref[...] = jnp.zeros_like(dlb_ref)

            @pl.when(tile_of(jj) == 0)
            def _():
                for ref in list(dkt_refs) + [dvt_ref]:
                    ref[...] = jnp.zeros_like(ref)

            def step(col0):
                for bh in chains:
                    _, vjp = jax.vjp(chain, s_ref[bh], sq_ref[bh], sz_ref[bh], sv_ref[bh], lb_ref[bh[1]])
                    ds, dq, dz, dv_, dlb = vjp((dst[bh], dso_ref[bh]))
                    dst[bh] = ds
                    dsq_ref[bh] = dq
                    dsz_ref[bh] = dz
                    dsv_ref[bh] = dv_
                    dlb_ref[bh[1]] += dlb
                for b in range(bsz):
                    kts_lo = [r[b, :, col0:] for r in kt_refs]
                    vt_lo = vt_ref[b, :, col0:]
                    for r0 in range(0, tq, sub_bwd):
                        rows = grp * sub_bwd
                        rsl = slice(r0, r0 + sub_bwd)
                        dqs, dkts, dvt = _att_rows_bwd(
                            load_q(q_refs, b, r0, sub_bwd), kts_lo, vt_lo, o_ref[b, :, rsl, :].reshape(rows, dv),
                            lse_ref[b, :, rsl, :].reshape(rows, LANE)[:, 0:1], do_ref[b, :, rsl, :].reshape(rows, dv), scale, fold_scale)
                        for dq_ref, dq, w in zip(dq_refs, dqs, widths):
                            dq_ref[b, :, rsl, :] = dq.reshape(grp, sub_bwd, w)
                        for dkt_ref, dkt in zip(dkt_refs, dkts):
                            dkt_ref[b, :, col0:] += dkt
                        dvt_ref[b, :, col0:] += dvt

            for cond, col0 in ranges:
                pl.when(cond(jj))(functools.partial(step, col0))

        scan_spec = pl.BlockSpec((bsz, nh, c, d), lambda jj: (0, 0, chunk_of(n - 1 - jj), 0))
        st_spec = pl.BlockSpec((bsz, nh, None, d, d), lambda jj: (0, 0, n - 1 - jj, 0, 0))
        in_specs = att_in + [scan_spec] * 3 + [lb_spec, st_spec, per_tile(dv), per_tile(LANE), per_tile(dv), scan_spec]
        d_shape = ([jax.ShapeDtypeStruct(a.shape, F32) for a in att_lo] + [jax.ShapeDtypeStruct(sq.shape, F32)] * 3
                   + [jax.ShapeDtypeStruct(lb.shape, F32)])
        return pl.pallas_call(body, grid=(n,), in_specs=in_specs, out_specs=att_in + [scan_spec] * 3 + [lb_spec], out_shape=d_shape,
                              scratch_shapes=[pltpu.VMEM((bsz, nh, d, d), F32)], name=name + "_bwd",
                              compiler_params=_cparams())(*att_lo, *scan_in, states, o, lse, do, dso)

    n_att = 2 * n_parts + 1

    def lo(arrays):
        return arrays[:n_parts] + tuple(a.astype(BF16) for a in arrays[n_parts:n_att])

    @jax.custom_vjp
    def op(*arrays):
        res = fwd_call(*lo(arrays), *arrays[n_att:])
        return res[0], res[2]

    def op_fwd(*arrays):
        att_lo, scan_in = lo(arrays), arrays[n_att:]
        o, lse, so, states = fwd_call(*att_lo, *scan_in)
        return (o, so), (att_lo, scan_in, o, lse, states)

    def op_bwd(res, cts):
        att_lo, scan_in, o, lse, states = res
        return tuple(bwd_call(att_lo, scan_in, o, lse, states, cts[0], cts[1]))

    op.defvjp(op_fwd, op_bwd)
    return op(*qs, *kts, vt, sq, sz, sv, lb)


def ada_op(cc, w, b):
    depth, d, n = w.shape

    def fn(ids, ccb, wb, bb):
        return (_dot_bf16(_silu(ccb), wb) + bb,)

    args = [Arg(cc, cc.shape, lambda l: (0, 0), "acc", lambda ids: ids[0] == 0),
            Arg(w, (None, d, n), lambda l: (l, 0, 0), "tile"), Arg(b, (None, 1, n), lambda l: (l, 0, 0), "tile")]
    return tile_op("ada", fn, (depth,), args, [Out((depth, cc.shape[0], n), (None, cc.shape[0], n), lambda l: (l, 0, 0))])[0]


def loss_op(xu, target, n_lat_tiles):
    bsz, t, d = xu.shape

    def body(x_ref, t_ref, dx_ref, l_ref):
        b, i = pl.program_id(0), pl.program_id(1)

        @pl.when(jnp.logical_and(b == 0, i == 0))
        def _():
            l_ref[...] = jnp.zeros_like(l_ref)

        @pl.when(i < n_lat_tiles)
        def _():
            err = x_ref[...] - t_ref[...]
            dx_ref[...] = err * (1.0 / d)
            l_ref[...] += 0.5 * jnp.sum(jnp.mean(err * err, axis=-1))

        @pl.when(i >= n_lat_tiles)
        def _():
            dx_ref[...] = jnp.zeros_like(dx_ref)

    row = pl.BlockSpec((None, TOKEN_TILE, d), lambda b, i: (b, i, 0))
    t_spec = pl.BlockSpec((None, TOKEN_TILE, d), lambda b, i: (b, jnp.minimum(i, n_lat_tiles - 1), 0))
    return pl.pallas_call(body, grid=(bsz, t // TOKEN_TILE), in_specs=[row, t_spec],
                          out_specs=[row, pl.BlockSpec((SUBLANE, LANE), lambda b, i: (0, 0))],
                          out_shape=[jax.ShapeDtypeStruct(xu.shape, F32), jax.ShapeDtypeStruct((SUBLANE, LANE), F32)],
                          name="loss", compiler_params=_cparams())(xu, target)


def _row_tile(rows, row_bytes, budget=4 << 20, step=SUBLANE):
    if rows * row_bytes <= budget:
        return rows
    best = None
    for t in range(step, rows, step):
        if rows % t == 0 and t * row_bytes <= budget:
            best = t
    return best if best is not None else rows


def _as3d(x):
    p = x.shape[0]
    c = x.shape[-1]
    return x.reshape(p, -1, c)


def sum_parts(name, x):
    x3 = _as3d(x)
    p, r, c = x3.shape
    tr = _row_tile(r, p * c * 4, step=SUBLANE * (4 // x.dtype.itemsize))

    def body(x_ref, o_ref):
        s = x_ref[0].astype(F32)
        for j in range(1, p):
            s = s + x_ref[j].astype(F32)
        o_ref[...] = s

    out = pl.pallas_call(body, grid=(r // tr,), in_specs=[pl.BlockSpec((p, tr, c), lambda i: (0, i, 0))],
                         out_specs=pl.BlockSpec((tr, c), lambda i: (i, 0)), out_shape=jax.ShapeDtypeStruct((r, c), F32),
                         name=name, compiler_params=_cparams())(x3)
    return out.reshape(x.shape[1:])


def adamw(name, w, g, m, v):
    shape = w.shape
    c = shape[-1]
    to2d = lambda a: a.reshape(-1, c)
    r = to2d(w).shape[0]
    tr = _row_tile(r, 7 * c * 4, budget=6 << 20)

    def body(w_ref, g_ref, m_ref, v_ref, d_ref, nm_ref, nv_ref):
        gg = g_ref[...]
        nm = ADAM_B1 * m_ref[...] + (1.0 - ADAM_B1) * gg
        nv = ADAM_B2 * v_ref[...] + (1.0 - ADAM_B2) * jnp.square(gg)
        m_hat = nm / (1.0 - ADAM_B1 ** ADAM_STEP)
        v_hat = nv / (1.0 - ADAM_B2 ** ADAM_STEP)
        d_ref[...] = -ADAM_LR * (m_hat / (jnp.sqrt(v_hat) + ADAM_EPS) + ADAM_WD * w_ref[...])
        nm_ref[...] = nm
        nv_ref[...] = nv

    spec = pl.BlockSpec((tr, c), lambda i: (i, 0))
    outs = pl.pallas_call(body, grid=(r // tr,), in_specs=[spec] * 4, out_specs=[spec] * 3,
                          out_shape=[jax.ShapeDtypeStruct((r, c), F32)] * 3, name=name, compiler_params=_cparams())(
        to2d(w), to2d(g), to2d(m), to2d(v))
    return tuple(o.reshape(shape) for o in outs)


def exchange(name, srcs, group, same):
    p = 2 ** len(group)
    n = len(srcs)
    out_shape = [jax.ShapeDtypeStruct(((p,) + s.shape) if same else s.shape, s.dtype) for s in srcs]

    def index_of(coords):
        idx = 0
        for a in group:
            idx = idx * 2 + coords[a]
        return idx

    def body(*refs):
        src_refs, out_refs = refs[:n], refs[n:2 * n]
        send_sems, recv_sems = refs[2 * n:]
        pos = {a: lax.axis_index(a) for a in MESH_AXES}
        me = index_of(pos)
        peers = []
        for rel in range(1, p):
            coords = dict(pos)
            for bit, a in enumerate(reversed(group)):
                if (rel >> bit) & 1:
                    coords[a] = 1 - coords[a]
            peers.append((coords, index_of(coords)))

        def src_for(a, idx):
            return src_refs[a] if same else src_refs[a].at[idx]

        sends, recvs = [], []
        for a in range(n):
            for r, (coords, idx) in enumerate(peers):
                dev = tuple(coords[ax] for ax in MESH_AXES)
                send = pltpu.make_async_remote_copy(src_ref=src_for(a, idx), dst_ref=out_refs[a].at[me],
                                                    send_sem=send_sems.at[a, r], recv_sem=recv_sems.at[a, r],
                                                    device_id=dev, device_id_type=pl.DeviceIdType.MESH)
                send.start()
                sends.append(send)
                recvs.append(pltpu.make_async_remote_copy(src_ref=src_for(a, idx), dst_ref=out_refs[a].at[idx],
                                                          send_sem=send_sems.at[a, r], recv_sem=recv_sems.at[a, r],
                                                          device_id=dev, device_id_type=pl.DeviceIdType.MESH))
        for cp in sends:
            cp.wait_send()
        for cp in recvs:
            cp.wait_recv()

    any_spec = pl.BlockSpec(memory_space=pl.ANY)
    outs = pl.pallas_call(body, in_specs=[any_spec] * n, out_specs=[any_spec] * n, out_shape=out_shape,
                          scratch_shapes=[pltpu.SemaphoreType.DMA((n, p - 1)), pltpu.SemaphoreType.DMA((n, p - 1))],
                          name=name)(*srcs)
    me = index_of({a: lax.axis_index(a) for a in MESH_AXES})
    own = [s if same else lax.dynamic_index_in_dim(s, me, axis=0, keepdims=False) for s in srcs]
    return [lax.dynamic_update_index_in_dim(o, w, me, axis=0) for o, w in zip(outs, own)]


def _rope_tables(n_lat, n_ctx, rot_dim, heads):
    n_freq = rot_dim // 4
    tok = jnp.arange(n_lat, dtype=jnp.int32)
    inv = ROPE_THETA ** (-jnp.arange(n_freq, dtype=F32) / n_freq)
    ang = jnp.concatenate([(tok // GRID_W).astype(F32)[:, None] * inv, (tok % GRID_W).astype(F32)[:, None] * inv], axis=-1)
    cos, sin = jnp.cos(ang), jnp.sin(ang)
    cos = jnp.concatenate([jnp.concatenate([cos, cos], -1), jnp.ones((n_ctx, rot_dim), F32)], 0)
    sin = jnp.concatenate([jnp.concatenate([-sin, sin], -1), jnp.zeros((n_ctx, rot_dim), F32)], 0)
    half = rot_dim // 2
    w = heads * rot_dim
    j = np.arange(w)
    src = (j // rot_dim) * rot_dim + (j % rot_dim + half) % rot_dim
    swap = np.zeros((w, w), np.float32)
    swap[src, j] = 1.0
    return jnp.tile(cos, (1, heads)), jnp.tile(sin, (1, heads)), jnp.asarray(swap)


def _to_heads(a, nh):
    b, t, _ = a.shape
    return a.reshape(b, t, nh, -1).transpose(0, 2, 1, 3)


def _to_heads_t(a, nh):
    b, t, _ = a.shape
    return a.reshape(b, t, nh, -1).transpose(0, 2, 3, 1)


def _from_heads(a):
    b, nh, t, w = a.shape
    return a.transpose(0, 2, 1, 3).reshape(b, t, nh * w)


def _pad_w_in(w_in):
    parts, off = [], 0
    for size, pad in zip(IN_SIZES, IN_PAD):
        parts.append(w_in[:, off:off + size])
        if pad > size:
            parts.append(jnp.zeros((w_in.shape[0], pad - size), w_in.dtype))
        off += size
    return jnp.concatenate(parts, axis=1)


def _forward(xu, mods, prm, prm_lo, n_lat, n_ctx):
    bsz, t, d = xu.shape
    depth = mods.shape[0]
    n_lat_tiles = n_lat // TOKEN_TILE
    rope_a = _rope_tables(n_lat, n_ctx, HEAD_DIM, 1)
    rope_bq = _rope_tables(n_lat, n_ctx, B_ROPE, B_HEADS)
    rope_bk = _rope_tables(n_lat, n_ctx, B_ROPE, 1)
    clb = prm["c_lower_bounds"].reshape(depth, 2, C_HEADS, C_DK).transpose(2, 0, 1, 3)
    for l in range(depth):
        nm = lambda s: f"l{l}_{s}"
        vec = lambda name: prm[name][l][None, :]
        xu, h = premod(nm("premix"), xu, vec("g_pre_mix"), mods[l], 0, n_lat_tiles)
        z = linear(nm("w_in"), h.reshape(bsz * t, d), _pad_w_in(prm["w_in"][l]), _pad_w_in(prm_lo["w_in"][l])).reshape(bsz, t, D_IN_PAD)
        seg = lambda i: z[:, :, IN_OFF[i]:IN_OFF[i] + IN_SIZES[i]]
        aq = a_prep(nm("aq"), _to_heads(seg(0), A_HEADS), vec("a_q_norm"), *rope_a)
        ak = a_prep(nm("ak"), _to_heads(seg(1), A_KV_HEADS), vec("a_k_norm"), *rope_a)
        av = _to_heads(seg(2), A_KV_HEADS)
        lower = c_lower(nm("c_lower"), clb, l)
        cq, cv = _to_heads(seg(6), C_HEADS), _to_heads(seg(9), C_HEADS)
        ya, o_f = mix_pair(nm("mix_a"), [aq.reshape(bsz, A_KV_HEADS, A_GROUP, t, HEAD_DIM)], [ak.transpose(0, 1, 3, 2)],
                           av.transpose(0, 1, 3, 2), HEAD_DIM ** -0.5, True, ATT_A_TQ_FWD, ATT_A_TQ, n_lat,
                           cq, _to_heads(seg(7), C_HEADS), cv, lower[:, 0:1], False, n_lat // SCAN_CHUNK)
        ya = _from_heads(ya.reshape(bsz, A_HEADS, t, HEAD_DIM))
        wq = prm["w_q_up"][l].reshape(B_Q_RANK, B_HEADS, B_NOPE + B_ROPE)
        wkv = prm["w_kv_up"][l].reshape(B_KV_RANK, B_HEADS, B_NOPE + B_V)
        qn, qp, kn, bv, kp = b_prep(
            nm("b_prep"), seg(3), seg(4), seg(5), vec("b_q_norm"),
            wq[:, :, :B_NOPE].reshape(B_Q_RANK, -1), wq[:, :, B_NOPE:].reshape(B_Q_RANK, -1), vec("b_kv_norm"),
            wkv[:, :, :B_NOPE].reshape(B_KV_RANK, -1), wkv[:, :, B_NOPE:].reshape(B_KV_RANK, -1), *rope_bq, *rope_bk)
        qb = jnp.concatenate([_to_heads(qn, B_HEADS), _to_heads(qp, B_HEADS)], axis=-1)[:, :, None]
        kbt = jnp.concatenate([_to_heads_t(kn, B_HEADS),
                               jnp.broadcast_to(kp.transpose(0, 2, 1)[:, None], (bsz, B_HEADS, B_ROPE, t))], axis=2)
        yb, o_b = mix_pair(nm("mix_b"), [qb], [kbt], _to_heads_t(bv, B_HEADS), (B_NOPE + B_ROPE) ** -0.5, False, ATT_B_TQ,
                           ATT_B_TQ, n_lat, cq, _to_heads(seg(8), C_HEADS), cv, lower[:, 1:2], True, n_lat // SCAN_CHUNK)
        yb = _from_heads(yb[:, :, 0])
        yc = _from_heads(c_readout(nm("c_out"), o_f, o_b, _to_heads(seg(10), C_HEADS), vec("c_out_norm")))
        y = linear(nm("w_out"), jnp.concatenate([ya, yb, yc], axis=-1).reshape(bsz * t, d), prm["w_out"][l],
                   prm_lo["w_out"][l]).reshape(bsz, t, d)
        x1 = resid(nm("res_mix"), xu, y, vec("g_post_mix"), mods[l], 2, n_lat_tiles)
        x1, h2 = premod(nm("preffn"), x1, vec("g_pre_ffn"), mods[l], 3, n_lat_tiles)
        f = mlp(nm("mlp"), h2.reshape(bsz * t, d), prm["w_ff1"][l], prm["w_ff2"][l], prm_lo["w_ff1"][l],
                prm_lo["w_ff2"][l]).reshape(bsz, t, d)
        xu = resid(nm("res_ffn"), x1, f, vec("g_post_ffn"), mods[l], 5, n_lat_tiles)
    return xu


BIG = {"w_in": 2, "w_q_up": 2, "w_kv_up": 2, "w_out": 1, "w_ff1": 2, "w_ff2": 1}
SMALL = ("g_pre_mix", "g_post_mix", "g_pre_ffn", "g_post_ffn", "a_q_norm", "a_k_norm", "b_q_norm", "b_kv_norm",
         "c_lower_bounds", "c_out_norm")
WEIGHTS = ("c_ctx", "w_ada", "b_ada", "g_pre_mix", "g_post_mix", "g_pre_ffn", "g_post_ffn", "w_in", "a_q_norm", "a_k_norm",
           "b_q_norm", "w_q_up", "b_kv_norm", "w_kv_up", "c_lower_bounds", "c_out_norm", "w_out", "w_ff1", "w_ff2")


def _unshard(g, axis):
    depth, _, r, c = g.shape
    if axis == 1:
        return g.reshape(depth, N_CHIP * r, c)
    return g.transpose(0, 2, 1, 3).reshape(depth, r, N_CHIP * c)


def _shard_major(g, axis):
    depth, r, c = g.shape
    if axis == 1:
        return g.reshape(depth, N_CHIP, r // N_CHIP, c)
    return g.reshape(depth, r, N_CHIP, c // N_CHIP).transpose(0, 2, 1, 3)


def kernel(x, c, ctx, c_ctx, w_ada, b_ada, g_pre_mix, g_post_mix, g_pre_ffn, g_post_ffn, w_in, a_q_norm, a_k_norm, b_q_norm, w_q_up, b_kv_norm, w_kv_up, c_lower_bounds, c_out_norm, w_out, w_ff1, w_ff2, loss_target, m_c_ctx, m_w_ada, m_b_ada, m_g_pre_mix, m_g_post_mix, m_g_pre_ffn, m_g_post_ffn, m_w_in, m_a_q_norm, m_a_k_norm, m_b_q_norm, m_w_q_up, m_b_kv_norm, m_w_kv_up, m_c_lower_bounds, m_c_out_norm, m_w_out, m_w_ff1, m_w_ff2, v_c_ctx, v_w_ada, v_b_ada, v_g_pre_mix, v_g_post_mix, v_g_pre_ffn, v_g_post_ffn, v_w_in, v_a_q_norm, v_a_k_norm, v_b_q_norm, v_w_q_up, v_b_kv_norm, v_w_kv_up, v_c_lower_bounds, v_c_out_norm, v_w_out, v_w_ff1, v_w_ff2):
    local = dict(c_ctx=c_ctx, w_ada=w_ada, b_ada=b_ada, g_pre_mix=g_pre_mix, g_post_mix=g_post_mix, g_pre_ffn=g_pre_ffn,
                 g_post_ffn=g_post_ffn, w_in=w_in, a_q_norm=a_q_norm, a_k_norm=a_k_norm, b_q_norm=b_q_norm, w_q_up=w_q_up,
                 b_kv_norm=b_kv_norm, w_kv_up=w_kv_up, c_lower_bounds=c_lower_bounds, c_out_norm=c_out_norm, w_out=w_out,
                 w_ff1=w_ff1, w_ff2=w_ff2)
    mom = dict(c_ctx=m_c_ctx, w_ada=m_w_ada, b_ada=m_b_ada, g_pre_mix=m_g_pre_mix, g_post_mix=m_g_post_mix,
               g_pre_ffn=m_g_pre_ffn, g_post_ffn=m_g_post_ffn, w_in=m_w_in, a_q_norm=m_a_q_norm, a_k_norm=m_a_k_norm,
               b_q_norm=m_b_q_norm, w_q_up=m_w_q_up, b_kv_norm=m_b_kv_norm, w_kv_up=m_w_kv_up,
               c_lower_bounds=m_c_lower_bounds, c_out_norm=m_c_out_norm, w_out=m_w_out, w_ff1=m_w_ff1, w_ff2=m_w_ff2)
    var = dict(c_ctx=v_c_ctx, w_ada=v_w_ada, b_ada=v_b_ada, g_pre_mix=v_g_pre_mix, g_post_mix=v_g_post_mix,
               g_pre_ffn=v_g_pre_ffn, g_post_ffn=v_g_post_ffn, w_in=v_w_in, a_q_norm=v_a_q_norm, a_k_norm=v_a_k_norm,
               b_q_norm=v_b_q_norm, w_q_up=v_w_q_up, b_kv_norm=v_b_kv_norm, w_kv_up=v_w_kv_up,
               c_lower_bounds=v_c_lower_bounds, c_out_norm=v_c_out_norm, w_out=v_w_out, w_ff1=v_w_ff1, w_ff2=v_w_ff2)

    bsz, n_lat, d = x.shape
    n_ctx = ctx.shape[1]
    depth = w_ada.shape[0]
    assert depth == 2 and n_lat % TOKEN_TILE == 0 and n_ctx % TOKEN_TILE == 0 and bsz * N_DEV + 1 <= ADA_ROWS
    ax, ay, ac = (lax.axis_index(a) for a in MESH_AXES)
    chip = 2 * ax + ay
    dev = 2 * chip + ac

    c_all = exchange("gather_c", [c], MESH_AXES, True)[0].reshape(N_DEV * bsz, d)
    big_names = list(BIG)
    mine = [lax.dynamic_index_in_dim(local[n].astype(BF16), ac, axis=0, keepdims=False) for n in big_names]
    over_chips = exchange("gather_w_quad", mine + [c_lower_bounds], ("x", "y"), True)
    both_layers = exchange("gather_w_pair", over_chips[:-1], ("c",), True)
    prm_lo = {n: _unshard(g, BIG[n]) for n, g in zip(big_names, both_layers)}
    prm = {n: w.astype(F32) for n, w in prm_lo.items()}
    prm["c_lower_bounds"] = over_chips[-1].transpose(1, 2, 0, 3).reshape(depth, 2, -1)
    for n in SMALL:
        if n != "c_lower_bounds":
            prm[n] = local[n]

    n_ada = w_ada.shape[2]
    cc = jnp.concatenate([c_all, c_ctx[None, :], jnp.zeros((ADA_ROWS - N_DEV * bsz - 1, d), F32)], axis=0)
    b_blk = lax.dynamic_slice_in_dim(b_ada, chip * n_ada, n_ada, axis=1)[:, None, :]
    mod_part, ada_vjp = jax.vjp(ada_op, cc, w_ada, b_blk)
    mod_full = exchange("gather_mod", [mod_part], ("x", "y"), True)[0].transpose(1, 2, 0, 3).reshape(depth, ADA_ROWS, 6 * d)
    mod_lat = lax.dynamic_slice_in_dim(mod_full, dev * bsz, bsz, axis=1)
    mod_ctx = jnp.broadcast_to(mod_full[:, N_DEV * bsz:N_DEV * bsz + 1], mod_lat.shape)
    mods = jnp.stack([mod_lat, mod_ctx], axis=2).reshape(depth, bsz, 2, 6, d)

    xu = jnp.concatenate([x, ctx], axis=1)
    x_out, fwd_vjp = jax.vjp(lambda xu_, mods_, prm_: _forward(xu_, mods_, prm_, prm_lo, n_lat, n_ctx), xu, mods, prm)
    dx_out, loss_blk = loss_op(x_out, loss_target, n_lat // TOKEN_TILE)
    d_xu, d_mods, d_prm = fwd_vjp(dx_out)
    grad_x = d_xu[:, :n_lat]

    d_mods = d_mods.reshape(depth, bsz, 2, 6 * d)
    pieces = [d_mods] + [d_prm[n] for n in SMALL] + [loss_blk[0:1, 0:1]]
    sizes = [int(np.prod(p.shape)) for p in pieces]
    flat = jnp.concatenate([p.reshape(-1) for p in pieces])
    n_flat = -(-flat.shape[0] // (SUBLANE * LANE)) * SUBLANE * LANE
    flat = jnp.concatenate([flat, jnp.zeros((n_flat - flat.shape[0],), F32)]).reshape(-1, LANE)
    small_all = exchange("gather_small", [flat], MESH_AXES, True)[0]
    small_sum = sum_parts("sum_small", small_all).reshape(-1)
    offs = np.cumsum([0] + sizes)
    summed = {n: small_sum[offs[i + 1]:offs[i + 2]].reshape(d_prm[n].shape) for i, n in enumerate(SMALL)}
    loss = small_sum[offs[-2]]
    dm_all = small_all.reshape(N_DEV, -1)[:, :sizes[0]].reshape(N_DEV, depth, bsz, 2, 6 * d)
    dm_rows = dm_all.transpose(3, 0, 2, 1, 4).reshape(2, N_DEV * bsz, depth, 6 * d)
    d_ctx_row = sum_parts("sum_dmod_ctx", dm_rows[1])
    grad_b_ada = sum_parts("sum_b_ada", dm_rows.reshape(2 * N_DEV * bsz, depth, 6 * d))
    d_rows = jnp.concatenate([dm_rows[0].transpose(1, 0, 2), d_ctx_row[:, None, :],
                              jnp.zeros((depth, ADA_ROWS - N_DEV * bsz - 1, 6 * d), F32)], axis=1)
    d_cc, grad_w_ada, _ = ada_vjp(lax.dynamic_slice_in_dim(d_rows, chip * n_ada, n_ada, axis=2))
    d_cctx_all = exchange("gather_dcctx", [d_cc[N_DEV * bsz:N_DEV * bsz + 1]], MESH_AXES, True)[0]
    grad_c_ctx = sum_parts("sum_dcctx", d_cctx_all[0::2]).reshape(d)

    shard_major = [_shard_major(d_prm[n], BIG[n]).astype(BF16) for n in big_names]
    pair = exchange("rs_pair", shard_major, ("c",), False)
    chip_sum = [sum_parts(f"rs_sum1_{n}", p) for n, p in zip(big_names, pair)]
    quad = exchange("rs_quad", [s.astype(BF16) for s in chip_sum], ("x", "y"), False)
    total = [sum_parts(f"rs_sum2_{n}", q) for n, q in zip(big_names, quad)]
    both = exchange("rs_share", total, ("c",), True)

    grads = dict(summed)
    grads["c_lower_bounds"] = lax.dynamic_slice_in_dim(summed["c_lower_bounds"], chip * c_lower_bounds.shape[2],
                                                       c_lower_bounds.shape[2], axis=2)
    grads.update(c_ctx=grad_c_ctx, w_ada=grad_w_ada, b_ada=grad_b_ada)
    grads.update({n: g for n, g in zip(big_names, both)})

    deltas, new_m, new_v = {}, {}, {}
    for n in WEIGHTS:
        as2d = (lambda a: a[None, :]) if local[n].ndim == 1 else (lambda a: a)
        dl, nm_, nv_ = adamw("adamw_" + n, as2d(local[n]), as2d(grads[n]), as2d(mom[n]), as2d(var[n]))
        deltas[n], new_m[n], new_v[n] = (a.reshape(local[n].shape) for a in (dl, nm_, nv_))
    return (loss, grad_x, *[grads[n] for n in WEIGHTS], *[deltas[n] for n in WEIGHTS],
            *[new_m[n] for n in WEIGHTS], *[new_v[n] for n in WEIGHTS])
```

```python
import functools
from typing import Any, Callable, NamedTuple

import numpy as np
import jax
import jax.numpy as jnp
from jax import lax
from jax.experimental import pallas as pl
from jax.experimental.pallas import tpu as pltpu

F32 = jnp.float32
BF16 = jnp.bfloat16
HIGHEST = lax.Precision.HIGHEST

GRID_W = 64
HEAD_DIM = 64
A_HEADS, A_KV_HEADS = 8, 2
A_GROUP = A_HEADS // A_KV_HEADS
B_HEADS, B_Q_RANK, B_KV_RANK, B_NOPE, B_ROPE, B_V = 4, 192, 128, 64, 32, 64
C_HEADS, C_DK, C_DV = 4, 64, 64
SCAN_CHUNK = 64
SCAN_SUB = 16
ROPE_THETA = 10000.0
EPS = 1e-6
F_TINY = 1e-30
EXP_ARG_MAX = 80.0
ADAM_LR, ADAM_B1, ADAM_B2, ADAM_EPS, ADAM_WD, ADAM_STEP = 0.001, 0.9, 0.999, 1e-08, 0.01, 10

IN_SIZES = (512, 128, 128, 192, 128, 32, 256, 256, 256, 256, 256)
IN_PAD = (512, 128, 128, 256, 128, 128, 256, 256, 256, 256, 256)
IN_OFF = tuple(int(v) for v in np.cumsum((0,) + IN_PAD)[:-1])
D_IN_PAD = int(sum(IN_PAD))

LANE = 128
SUBLANE = 8
TOKEN_TILE = 256
ATT_A_TQ = 64
ATT_A_TQ_FWD = 128
ATT_B_TQ = 256
ATT_FWD_ROW_GROUPS = 4
MM_ROWS = 256
LINEAR_ROWS = 512
MM_TK_TOKENS = 512
MM_TN_OUT = 2560 * 1024
VMEM_LIMIT = 56 * 1024 * 1024
MESH_AXES = ("x", "y", "c")
N_DEV = 8
N_CHIP = 4
ADA_ROWS = 24


class Arg(NamedTuple):
    arr: Any
    block: tuple
    imap: Callable
    kind: str
    first: Callable = None


class Out(NamedTuple):
    shape: tuple
    block: tuple
    imap: Callable


def _cparams():
    return pltpu.CompilerParams(vmem_limit_bytes=VMEM_LIMIT)


def tile_op(name, fn, grid, args, outs, residual=None, passthrough=False):
    n_in, n_out = len(args), len(outs)
    in_specs = [pl.BlockSpec(a.block, a.imap) for a in args]
    out_specs = [pl.BlockSpec(o.block, o.imap) for o in outs]
    out_shape = [jax.ShapeDtypeStruct(o.shape, F32) for o in outs]
    diff = [i for i, a in enumerate(args) if a.kind != "const"]

    def fwd_call(*arrays):
        def body(*refs):
            ids = tuple(pl.program_id(i) for i in range(len(grid)))
            res = list(fn(ids, *[r[...] for r in refs[:n_in]]))
            out_refs = refs[n_in:]
            if residual is not None:
                res[0] = res[0] + refs[n_in][...]
                out_refs = refs[n_in + 1:]
            for r, o in zip(out_refs, res):
                r[...] = o

        specs = in_specs + ([out_specs[0]] if residual is not None else [])
        return pl.pallas_call(body, grid=grid, in_specs=specs, out_specs=out_specs, out_shape=out_shape,
                              name=name + "_fwd", compiler_params=_cparams())(*arrays)

    def bwd_call(arrays, cts):
        def body(*refs):
            ids = tuple(pl.program_id(i) for i in range(len(grid)))
            vals = [r[...] for r in refs[:n_in]]
            ct = tuple(r[...] for r in refs[n_in:n_in + n_out])
            drefs = refs[n_in + n_out:]

            def g(*dv):
                full = list(vals)
                for i, v in zip(diff, dv):
                    full[i] = v
                return tuple(fn(ids, *full))

            _, vjp = jax.vjp(g, *[vals[i] for i in diff])
            ds = list(vjp(ct))
            if passthrough:
                ds[0] = ds[0] + refs[n_in + n_out][...]
                drefs = refs[n_in + n_out + 1:]
            for i, d, r in zip(diff, ds, drefs):
                if args[i].kind == "tile":
                    r[...] = d
                else:
                    is_first = args[i].first(ids)

                    @pl.when(is_first)
                    def _(r=r, d=d):
                        r[...] = d

                    @pl.when(jnp.logical_not(is_first))
                    def _(r=r, d=d):
                        r[...] += d

        d_specs = [in_specs[i] for i in diff]
        d_shape = [jax.ShapeDtypeStruct(arrays[i].shape, F32) for i in diff]
        specs = in_specs + out_specs + ([in_specs[0]] if passthrough else [])
        return pl.pallas_call(body, grid=grid, in_specs=specs, out_specs=d_specs, out_shape=d_shape,
                              name=name + "_bwd", compiler_params=_cparams())(*arrays, *cts)

    assert not passthrough or (diff and diff[0] == 0 and args[0].kind == "tile")

    def results(arrays):
        res = tuple(fwd_call(*arrays))
        return ((arrays[0],) + res) if passthrough else res

    @jax.custom_vjp
    def op(*arrays):
        return results(arrays)

    def op_fwd(*arrays):
        return results(arrays), arrays[:n_in]

    def op_bwd(arrays, cts):
        if passthrough:
            cts = tuple(cts[1:]) + (cts[0],)
        ds = bwd_call(arrays, cts)
        res, k = [], 0
        for i, a in enumerate(args):
            if a.kind == "const":
                res.append(jnp.zeros_like(arrays[i]))
            else:
                res.append(ds[k])
                k += 1
        if residual is not None:
            res.append(cts[0])
        return tuple(res)

    op.defvjp(op_fwd, op_bwd)
    return op(*[a.arr for a in args], *([residual] if residual is not None else []))


def _pick(n, cap):
    if n <= cap:
        return n
    best = None
    for t in range(LANE, cap + 1, LANE):
        if n % t == 0:
            best = t
    assert best is not None, (n, cap)
    return best


_NN = (((1,), (0,)), ((), ()))
_NT = (((1,), (1,)), ((), ()))
_TN = (((0,), (0,)), ((), ()))


def _resident(shape):
    return pl.BlockSpec(shape, lambda *ids: (0,) * len(shape), pipeline_mode=pl.Buffered(1))


def _mm_rows(name, a, w, transposed):
    m, k = a.shape
    n = w.shape[0] if transposed else w.shape[1]
    tm = LINEAR_ROWS
    dims = _NT if transposed else _NN

    def body(a_ref, w_ref, o_ref):
        o_ref[...] = lax.dot_general(a_ref[...].astype(BF16), w_ref[...], dims, preferred_element_type=F32)

    return pl.pallas_call(body, grid=(m // tm,), in_specs=[pl.BlockSpec((tm, k), lambda i: (i, 0)), _resident(w.shape)],
                          out_specs=pl.BlockSpec((tm, n), lambda i: (i, 0)), out_shape=jax.ShapeDtypeStruct((m, n), F32),
                          name=name, compiler_params=_cparams())(a, w)


def _mm_tn(name, a, g):
    t, k = a.shape
    n = g.shape[1]
    tko, tno = k, n
    while tko * tno > MM_TN_OUT:
        if tko >= tno:
            tko //= 2
        else:
            tno //= 2
    assert k % tko == 0 and n % tno == 0 and tko % LANE == 0 and tno % LANE == 0
    tt = _pick(t, MM_TK_TOKENS)

    def body(a_ref, g_ref, o_ref):
        p = lax.dot_general(a_ref[...].astype(BF16), g_ref[...].astype(BF16), _TN, preferred_element_type=F32)
        kk = pl.program_id(2)

        @pl.when(kk == 0)
        def _():
            o_ref[...] = p

        @pl.when(kk != 0)
        def _():
            o_ref[...] += p

    return pl.pallas_call(body, grid=(k // tko, n // tno, t // tt),
                          in_specs=[pl.BlockSpec((tt, tko), lambda i, j, kk: (kk, i)), pl.BlockSpec((tt, tno), lambda i, j, kk: (kk, j))],
                          out_specs=pl.BlockSpec((tko, tno), lambda i, j, kk: (i, j)),
                          out_shape=jax.ShapeDtypeStruct((k, n), F32), name=name, compiler_params=_cparams())(a, g)


def linear(name, a, w, w_lo):
    @jax.custom_vjp
    def op(a, w, w_lo):
        return _mm_rows(name + "_fwd", a, w_lo, False)

    def op_fwd(a, w, w_lo):
        return _mm_rows(name + "_fwd", a, w_lo, False), (a, w_lo)

    def op_bwd(res, g):
        a, w_lo = res
        return _mm_rows(name + "_da", g, w_lo, True), _mm_tn(name + "_dw", a, g), jnp.zeros_like(w_lo)

    op.defvjp(op_fwd, op_bwd)
    return op(a, w, w_lo)


def mlp(name, h, w1, w2, w1_lo, w2_lo):
    m, d = h.shape
    f = w1_lo.shape[1]
    tm = MM_ROWS
    row = pl.BlockSpec((tm, d), lambda i: (i, 0))
    wide = pl.BlockSpec((tm, f), lambda i: (i, 0))

    def fwd_call(h, w1_lo, w2_lo):
        def body(h_ref, w1_ref, w2_ref, o_ref):
            u = lax.dot_general(h_ref[...].astype(BF16), w1_ref[...], _NN, preferred_element_type=F32)
            act = jnp.square(jnp.maximum(u, 0.0))
            o_ref[...] = lax.dot_general(act.astype(BF16), w2_ref[...], _NN, preferred_element_type=F32)

        return pl.pallas_call(body, grid=(m // tm,), in_specs=[row, _resident(w1_lo.shape), _resident(w2_lo.shape)],
                              out_specs=row, out_shape=jax.ShapeDtypeStruct((m, d), F32), name=name + "_fwd",
                              compiler_params=_cparams())(h, w1_lo, w2_lo)

    def bwd_call(h, w1_lo, w2_lo, dy):
        def body(h_ref, dy_ref, w1_ref, w2_ref, dh_ref, act_ref, du_ref):
            u = lax.dot_general(h_ref[...].astype(BF16), w1_ref[...], _NN, preferred_element_type=F32)
            r = jnp.maximum(u, 0.0)
            act_ref[...] = (r * r).astype(BF16)
            dact = lax.dot_general(dy_ref[...].astype(BF16), w2_ref[...], _NT, preferred_element_type=F32)
            du = (dact * (2.0 * r)).astype(BF16)
            du_ref[...] = du
            dh_ref[...] = lax.dot_general(du, w1_ref[...], _NT, preferred_element_type=F32)

        return pl.pallas_call(body, grid=(m // tm,), in_specs=[row, row, _resident(w1_lo.shape), _resident(w2_lo.shape)],
                              out_specs=[row, wide, wide],
                              out_shape=[jax.ShapeDtypeStruct((m, d), F32), jax.ShapeDtypeStruct((m, f), BF16),
                                         jax.ShapeDtypeStruct((m, f), BF16)],
                              name=name + "_bwd", compiler_params=_cparams())(h, dy, w1_lo, w2_lo)

    @jax.custom_vjp
    def op(h, w1, w2, w1_lo, w2_lo):
        return fwd_call(h, w1_lo, w2_lo)

    def op_fwd(h, w1, w2, w1_lo, w2_lo):
        return fwd_call(h, w1_lo, w2_lo), (h, w1_lo, w2_lo)

    def op_bwd(res, dy):
        h, w1_lo, w2_lo = res
        dh, act, du = bwd_call(h, w1_lo, w2_lo, dy)
        return (dh, _mm_tn(name + "_dw1", h, du), _mm_tn(name + "_dw2", act, dy), jnp.zeros_like(w1_lo), jnp.zeros_like(w2_lo))

    op.defvjp(op_fwd, op_bwd)
    return op(h, w1, w2, w1_lo, w2_lo)


def _rms(x, g):
    return x * lax.rsqrt(jnp.mean(x * x, axis=-1, keepdims=True) + EPS) * g


def _sigmoid(z):
    return 1.0 / (1.0 + jnp.exp(jnp.minimum(-z, EXP_ARG_MAX)))


def _silu(z):
    return z * _sigmoid(z)


def _rope(y, cos, sin_signed, swap):
    return y * cos + jnp.dot(y, swap, precision=HIGHEST, preferred_element_type=F32) * sin_signed


def _dot_bf16(a, b, dims=((1,), (0,))):
    return lax.dot_general(a.astype(BF16), b.astype(BF16), (dims, ((), ())), preferred_element_type=F32)


def _gla_step(state, q, k, v, g, reverse):
    c, d = q.shape
    sub = SCAN_SUB
    nb = c // sub
    row = lax.broadcasted_iota(jnp.int32, (c, c), 0)
    col = lax.broadcasted_iota(jnp.int32, (c, c), 1)
    tri = (row <= col) if reverse else (row >= col)
    b = jnp.dot(tri.astype(F32), g, precision=HIGHEST, preferred_element_type=F32)
    o = jnp.dot(q * jnp.exp(b), state, preferred_element_type=F32)
    rs = lax.broadcasted_iota(jnp.int32, (sub, sub), 0)
    cs = lax.broadcasted_iota(jnp.int32, (sub, sub), 1)
    tri_s = ((rs <= cs) if reverse else (rs >= cs)).astype(F32)
    rowc = lax.broadcasted_iota(jnp.int32, (c, 1), 0)
    nonpos = lambda x: jnp.where(x > 0.0, 0.0, x)
    diag = []
    for j in range(nb):
        sl = slice(j * sub, (j + 1) * sub)
        bj, kj, vj, qj = b[sl], k[sl], v[sl], q[sl]
        dec = jnp.exp(nonpos(bj[:, None, :] - bj[None, :, :]))
        sc = jnp.sum(qj[:, None, :] * kj[None, :, :] * dec, axis=-1) * tri_s
        diag.append(jnp.dot(sc, vj, preferred_element_type=F32))
        if (j > 0) if reverse else (j < nb - 1):
            ref = bj[0:1] if reverse else bj[sub - 1:sub]
            qa = q * jnp.exp(nonpos(b - ref))
            ks = kj * jnp.exp(ref - bj)
            scj = lax.dot_general(qa, ks, (((1,), (1,)), ((), ())), precision=HIGHEST, preferred_element_type=F32)
            later = (rowc < j * sub) if reverse else (rowc >= (j + 1) * sub)
            o = o + jnp.dot(jnp.where(later, scj, 0.0), vj, preferred_element_type=F32)
    o = o + jnp.concatenate(diag, axis=0)
    b_end = b[0:1, :] if reverse else b[c - 1:c, :]
    kd = k * jnp.exp(b_end - b)
    new_state = state * jnp.exp(b_end).reshape(d, 1) + lax.dot_general(kd, v, (((0,), (0,)), ((), ())), preferred_element_type=F32)
    return new_state, o


def _zero_ids(ids):
    z = ids[0] == 0
    for i in ids[1:]:
        z = jnp.logical_and(z, i == 0)
    return z


def _token_grid(x, n_lat_tiles):
    bsz, t, d = x.shape
    nt = t // TOKEN_TILE
    row = lambda w: ((None, TOKEN_TILE, w), lambda b, i: (b, i, 0))
    mod_block = (None, None, 6, d)
    mod_imap = lambda b, i: (b, (i >= n_lat_tiles).astype(jnp.int32), 0, 0)
    mod_first = lambda ids: jnp.logical_or(ids[1] == 0, ids[1] == n_lat_tiles)
    return bsz, t, d, nt, row, (mod_block, mod_imap, mod_first)


def premod(name, x, gain, mods, r0, n_lat_tiles):
    bsz, t, d, nt, row, (mb, mi, mf) = _token_grid(x, n_lat_tiles)

    def fn(ids, xb, gb, mod):
        return (_rms(xb, gb) * (1.0 + mod[r0 + 1:r0 + 2]) + mod[r0:r0 + 1],)

    args = [Arg(x, *row(d), "tile"), Arg(gain, (1, d), lambda b, i: (0, 0), "acc", _zero_ids), Arg(mods, mb, mi, "acc", mf)]
    return tile_op(name, fn, (bsz, nt), args, [Out(x.shape, *row(d))], passthrough=True)


def resid(name, x, y, gain, mods, r, n_lat_tiles):
    bsz, t, d, nt, row, (mb, mi, mf) = _token_grid(x, n_lat_tiles)

    def fn(ids, yb, gb, mod):
        return (mod[r:r + 1] * _rms(yb, gb),)

    args = [Arg(y, *row(d), "tile"), Arg(gain, (1, d), lambda b, i: (0, 0), "acc", _zero_ids), Arg(mods, mb, mi, "acc", mf)]
    return tile_op(name, fn, (bsz, nt), args, [Out(x.shape, *row(d))], residual=x)[0]


def _heads_spec(nh, w):
    return (None, nh, TOKEN_TILE, w), lambda b, i: (b, 0, i, 0)


def a_prep(name, x, gain, cos, sin, swap):
    bsz, nh, t, d = x.shape
    tab = ((TOKEN_TILE, d), lambda b, i: (i, 0))

    def fn(ids, xb, gb, cb, sb, pb):
        y = _rms(xb, gb)
        swapped = jnp.dot(y.reshape(nh * TOKEN_TILE, d), pb, precision=HIGHEST, preferred_element_type=F32)
        return (y * cb + swapped.reshape(nh, TOKEN_TILE, d) * sb,)

    args = [Arg(x, *_heads_spec(nh, d), "tile"), Arg(gain, (1, d), lambda b, i: (0, 0), "acc", _zero_ids),
            Arg(cos, *tab, "const"), Arg(sin, *tab, "const"), Arg(swap, (d, d), lambda b, i: (0, 0), "const")]
    return tile_op(name, fn, (bsz, t // TOKEN_TILE), args, [Out(x.shape, *_heads_spec(nh, d))])[0]


def b_prep(name, bqd, bkvd, bkr, bqn, wq_nope, wq_pe, bkvn, wkv_nope, wkv_v, cos_q, sin_q, swap_q, cos_k, sin_k, swap_k):
    bsz, t, _ = bqd.shape
    row = lambda w: ((None, TOKEN_TILE, w), lambda b, i: (b, i, 0))
    whole = lambda a: (a.shape, lambda b, i: (0,) * a.ndim)
    tab = lambda w: ((TOKEN_TILE, w), lambda b, i: (i, 0))

    def fn(ids, qd, kvd, kr, qn, wqn, wqp, kvn, wkn, wkv, cq, sq, pq, ck, sk, pk):
        hq = _rms(qd, qn)
        hkv = _rms(kvd, kvn)
        return (_dot_bf16(hq, wqn), _rope(_dot_bf16(hq, wqp), cq, sq, pq), _dot_bf16(hkv, wkn), _dot_bf16(hkv, wkv),
                _rope(kr, ck, sk, pk))

    params = [bqn, wq_nope, wq_pe, bkvn, wkv_nope, wkv_v]
    args = [Arg(bqd, *row(B_Q_RANK), "tile"), Arg(bkvd, *row(B_KV_RANK), "tile"), Arg(bkr, *row(B_ROPE), "tile")]
    args += [Arg(p, *whole(p), "acc", _zero_ids) for p in params]
    args += [Arg(cos_q, *tab(cos_q.shape[1]), "const"), Arg(sin_q, *tab(cos_q.shape[1]), "const"), Arg(swap_q, *whole(swap_q), "const"),
             Arg(cos_k, *tab(B_ROPE), "const"), Arg(sin_k, *tab(B_ROPE), "const"), Arg(swap_k, *whole(swap_k), "const")]
    widths = (B_HEADS * B_NOPE, B_HEADS * B_ROPE, B_HEADS * B_NOPE, B_HEADS * B_V, B_ROPE)
    outs = [Out((bsz, t, w), *row(w)) for w in widths]
    return tile_op(name, fn, (bsz, t // TOKEN_TILE), args, outs)


def c_lower(name, clb, layer):
    nh, depth, _, d = clb.shape

    def fn(ids, lbs):
        lb = [lbs[:, j] for j in range(depth)]
        m = lb[0]
        for j in range(1, depth):
            m = jnp.maximum(m, lb[j])
        e = [jnp.exp(lb[j] - m) for j in range(depth)]
        tot = e[0]
        for j in range(1, depth):
            tot = tot + e[j]
        p = [ej / tot for ej in e]
        cum = p[0]
        for j in range(1, layer + 1):
            cum = cum + p[j]
        return (cum - p[0],)

    whole = lambda shape: (shape, lambda i: (0,) * len(shape))
    return tile_op(name, fn, (1,), [Arg(clb, *whole(clb.shape), "tile")], [Out((nh, 2, d), *whole((nh, 2, d)))])[0]


def c_readout(name, o_f, o_b, gate, gain):
    bsz, nh, t, d = o_f.shape
    spec = _heads_spec(nh, d)

    def fn(ids, of, ob, gt, gn):
        return (_rms(of + ob, gn) * _silu(gt),)

    args = [Arg(o_f, *spec, "tile"), Arg(o_b, *spec, "tile"), Arg(gate, *spec, "tile"),
            Arg(gain, (1, d), lambda b, i: (0, 0), "acc", _zero_ids)]
    return tile_op(name, fn, (bsz, t // TOKEN_TILE), args, [Out(o_f.shape, *spec)])[0]


def _att_rows_fwd(qq, kts_lo, vt_lo, scale, fold_scale, n_grp):
    rows = qq[0].shape[0]
    gr = rows // n_grp
    outs, lses = [], []
    for r in range(n_grp):
        s = None
        for q2, kt in zip(qq, kts_lo):
            part = lax.dot_general(q2[r * gr:(r + 1) * gr], kt, _NN, preferred_element_type=F32)
            s = part if s is None else s + part
        if not fold_scale:
            s = s * scale
        m = jnp.max(s, axis=-1, keepdims=True)
        e = jnp.exp(s - m)
        l = jnp.sum(e, axis=-1, keepdims=True)
        outs.append(lax.dot_general(e.astype(BF16), vt_lo, _NT, preferred_element_type=F32) * (1.0 / l))
        lses.append(m + jnp.log(l))
    return jnp.concatenate(outs, axis=0), jnp.concatenate(lses, axis=0)


def _att_rows_bwd(qq, kts_lo, vt_lo, o2, lse, do2, scale, fold_scale):
    s = None
    for q2, kt in zip(qq, kts_lo):
        part = lax.dot_general(q2, kt, _NN, preferred_element_type=F32)
        s = part if s is None else s + part
    if not fold_scale:
        s = s * scale
    p = jnp.exp(s - lse)
    delta = jnp.sum(do2 * o2, axis=-1, keepdims=True)
    do_lo = do2.astype(BF16)
    ds = p * (lax.dot_general(do_lo, vt_lo, _NN, preferred_element_type=F32) - delta)
    if not fold_scale:
        ds = ds * scale
    ds_lo = ds.astype(BF16)
    dvt = lax.dot_general(do_lo, p.astype(BF16), _TN, preferred_element_type=F32)
    dqs, dkts = [], []
    for q2, kt in zip(qq, kts_lo):
        dq = lax.dot_general(ds_lo, kt, _NT, preferred_element_type=F32)
        dqs.append(dq * scale if fold_scale else dq)
        dkts.append(lax.dot_general(q2, ds_lo, _TN, preferred_element_type=F32))
    return dqs, dkts, dvt


def mix_pair(name, qs, kts, vt, scale, fold_scale, sub_fwd, sub_bwd, n_lat, sq, sz, sv, lb, reverse, n_lat_chunks):
    bsz, hk, grp, t, _ = qs[0].shape
    dv = vt.shape[2]
    n_parts = len(qs)
    nh, d = sq.shape[1], sq.shape[3]
    c = SCAN_CHUNK
    n = t // c
    tph = n // hk
    tq = t // tph
    assert tph * hk == n and tq * tph == t and tq % sub_fwd == 0 and tq % sub_bwd == 0 and n_lat % tq == 0
    chains = [(b, h) for b in range(bsz) for h in range(nh)]
    widths = [q.shape[-1] for q in qs]

    def chunk_of(j):
        return (n - 1 - j) if reverse else lax.rem(j + n_lat_chunks, n)

    def chain(state, q_raw, z, v, lo_b):
        f = lo_b + (1.0 - lo_b) * _sigmoid(z)
        return _gla_step(state, _silu(q_raw), (1.0 - lo_b) * _sigmoid(-z), v, jnp.log(jnp.maximum(f, F_TINY)), reverse)

    lb_spec = pl.BlockSpec(lb.shape, lambda j: (0, 0, 0))
    head_of = lambda j: j // tph
    tile_of = lambda j: lax.rem(j, tph)
    per_tile = lambda w: pl.BlockSpec((bsz, None, grp, tq, w), lambda j: (0, head_of(j), 0, tile_of(j), 0))
    per_head = lambda w: pl.BlockSpec((bsz, None, w, t), lambda j: (0, head_of(j), 0, 0))
    q_specs = [per_tile(w) for w in widths]
    kt_specs = [per_head(w) for w in widths]
    att_in = q_specs + kt_specs + [per_head(dv)]
    o_shape = jax.ShapeDtypeStruct((bsz, hk, grp, t, dv), F32)
    lse_shape = jax.ShapeDtypeStruct((bsz, hk, grp, t, LANE), F32)
    st_shape = jax.ShapeDtypeStruct((bsz, nh, n, d, d), F32)
    ranges = ((lambda j: tile_of(j) * tq < n_lat, 0), (lambda j: tile_of(j) * tq >= n_lat, n_lat))

    def load_q(q_refs, b, r0, sub):
        qq = []
        for q_ref, w in zip(q_refs, widths):
            q2 = q_ref[b, :, r0:r0 + sub, :].reshape(grp * sub, w)
            qq.append((q2 * scale if fold_scale else q2).astype(BF16))
        return qq

    def fwd_call(*arrays):
        def body(*refs):
            q_refs, kt_refs, vt_ref = refs[:n_parts], refs[n_parts:2 * n_parts], refs[2 * n_parts]
            sq_ref, sz_ref, sv_ref, lb_ref = refs[2 * n_parts + 1:2 * n_parts + 5]
            o_ref, lse_ref, so_ref, states_ref, st = refs[2 * n_parts + 5:]
            j = pl.program_id(0)

            @pl.when(j == 0)
            def _():
                st[...] = jnp.zeros_like(st)

            def step(col0):
                for bh in chains:
                    s = st[bh]
                    states_ref[bh] = s
                    ns, o = chain(s, sq_ref[bh], sz_ref[bh], sv_ref[bh], lb_ref[bh[1]])
                    st[bh] = ns
                    so_ref[bh] = o
                for b in range(bsz):
                    kts_lo = [r[b, :, col0:] for r in kt_refs]
                    vt_lo = vt_ref[b, :, col0:]
                    for r0 in range(0, tq, sub_fwd):
                        o, lse = _att_rows_fwd(load_q(q_refs, b, r0, sub_fwd), kts_lo, vt_lo, scale, fold_scale, ATT_FWD_ROW_GROUPS)
                        o_ref[b, :, r0:r0 + sub_fwd, :] = o.reshape(grp, sub_fwd, dv)
                        lse_ref[b, :, r0:r0 + sub_fwd, :] = jnp.broadcast_to(lse, (grp * sub_fwd, LANE)).reshape(grp, sub_fwd, LANE)

            for cond, col0 in ranges:
                pl.when(cond(j))(functools.partial(step, col0))

        scan_spec = pl.BlockSpec((bsz, nh, c, d), lambda j: (0, 0, chunk_of(j), 0))
        return pl.pallas_call(
            body, grid=(n,), in_specs=att_in + [scan_spec] * 3 + [lb_spec],
            out_specs=[per_tile(dv), per_tile(LANE), scan_spec, pl.BlockSpec((bsz, nh, None, d, d), lambda j: (0, 0, j, 0, 0))],
            out_shape=[o_shape, lse_shape, jax.ShapeDtypeStruct(sq.shape, F32), st_shape],
            scratch_shapes=[pltpu.VMEM((bsz, nh, d, d), F32)], name=name + "_fwd", compiler_params=_cparams())(*arrays)

    def bwd_call(att_lo, scan_in, o, lse, states, do, dso):
        def body(*refs):
            q_refs, kt_refs, vt_ref = refs[:n_parts], refs[n_parts:2 * n_parts], refs[2 * n_parts]
            k0 = 2 * n_parts + 1
            sq_ref, sz_ref, sv_ref, lb_ref, s_ref = refs[k0:k0 + 5]
            o_ref, lse_ref, do_ref, dso_ref = refs[k0 + 5:k0 + 9]
            d_refs = refs[k0 + 9:]
            dq_refs, dkt_refs, dvt_ref = d_refs[:n_parts], d_refs[n_parts:2 * n_parts], d_refs[2 * n_parts]
            dsq_ref, dsz_ref, dsv_ref, dlb_ref, dst = d_refs[2 * n_parts + 1:]
            jj = pl.program_id(0)

            @pl.when(jj == 0)
            def _():
                dst[...] = jnp.zeros_like(dst)
                dlb_ref[...] = jnp.zeros_like(dlb_ref)

            @pl.when(tile_of(jj) == 0)
            def _():
                for ref in list(dkt_refs) + [dvt_ref]:
                    ref[...] = jnp.zeros_like(ref)

            def step(col0):
                for bh in chains:
                    _, vjp = jax.vjp(chain, s_ref[bh], sq_ref[bh], sz_ref[bh], sv_ref[bh], lb_ref[bh[1]])
                    ds, dq, dz, dv_, dlb = vjp((dst[bh], dso_ref[bh]))
                    dst[bh] = ds
                    dsq_ref[bh] = dq
                    dsz_ref[bh] = dz
                    dsv_ref[bh] = dv_
                    dlb_ref[bh[1]] += dlb
                for b in range(bsz):
                    kts_lo = [r[b, :, col0:] for r in kt_refs]
                    vt_lo = vt_ref[b, :, col0:]
                    for r0 in range(0, tq, sub_bwd):
                        rows = grp * sub_bwd
                        rsl = slice(r0, r0 + sub_bwd)
                        dqs, dkts, dvt = _att_rows_bwd(
                            load_q(q_refs, b, r0, sub_bwd), kts_lo, vt_lo, o_ref[b, :, rsl, :].reshape(rows, dv),
                            lse_ref[b, :, rsl, :].reshape(rows, LANE)[:, 0:1], do_ref[b, :, rsl, :].reshape(rows, dv), scale, fold_scale)
                        for dq_ref, dq, w in zip(dq_refs, dqs, widths):
                            dq_ref[b, :, rsl, :] = dq.reshape(grp, sub_bwd, w)
                        for dkt_ref, dkt in zip(dkt_refs, dkts):
                            dkt_ref[b, :, col0:] += dkt
                        dvt_ref[b, :, col0:] += dvt

            for cond, col0 in ranges:
                pl.when(cond(jj))(functools.partial(step, col0))

        scan_spec = pl.BlockSpec((bsz, nh, c, d), lambda jj: (0, 0, chunk_of(n - 1 - jj), 0))
        st_spec = pl.BlockSpec((bsz, nh, None, d, d), lambda jj: (0, 0, n - 1 - jj, 0, 0))
        in_specs = att_in + [scan_spec] * 3 + [lb_spec, st_spec, per_tile(dv), per_tile(LANE), per_tile(dv), scan_spec]
        d_shape = ([jax.ShapeDtypeStruct(a.shape, F32) for a in att_lo] + [jax.ShapeDtypeStruct(sq.shape, F32)] * 3
                   + [jax.ShapeDtypeStruct(lb.shape, F32)])
        return pl.pallas_call(body, grid=(n,), in_specs=in_specs, out_specs=att_in + [scan_spec] * 3 + [lb_spec], out_shape=d_shape,
                              scratch_shapes=[pltpu.VMEM((bsz, nh, d, d), F32)], name=name + "_bwd",
                              compiler_params=_cparams())(*att_lo, *scan_in, states, o, lse, do, dso)

    n_att = 2 * n_parts + 1

    def lo(arrays):
        return arrays[:n_parts] + tuple(a.astype(BF16) for a in arrays[n_parts:n_att])

    @jax.custom_vjp
    def op(*arrays):
        res = fwd_call(*lo(arrays), *arrays[n_att:])
        return res[0], res[2]

    def op_fwd(*arrays):
        att_lo, scan_in = lo(arrays), arrays[n_att:]
        o, lse, so, states = fwd_call(*att_lo, *scan_in)
        return (o, so), (att_lo, scan_in, o, lse, states)

    def op_bwd(res, cts):
        att_lo, scan_in, o, lse, states = res
        return tuple(bwd_call(att_lo, scan_in, o, lse, states, cts[0], cts[1]))

    op.defvjp(op_fwd, op_bwd)
    return op(*qs, *kts, vt, sq, sz, sv, lb)


def ada_op(cc, w, b):
    depth, d, n = w.shape

    def fn(ids, ccb, wb, bb):
        return (_dot_bf16(_silu(ccb), wb) + bb,)

    args = [Arg(cc, cc.shape, lambda l: (0, 0), "acc", lambda ids: ids[0] == 0),
            Arg(w, (None, d, n), lambda l: (l, 0, 0), "tile"), Arg(b, (None, 1, n), lambda l: (l, 0, 0), "tile")]
    return tile_op("ada", fn, (depth,), args, [Out((depth, cc.shape[0], n), (None, cc.shape[0], n), lambda l: (l, 0, 0))])[0]


def loss_op(xu, target, n_lat_tiles):
    bsz, t, d = xu.shape

    def body(x_ref, t_ref, dx_ref, l_ref):
        b, i = pl.program_id(0), pl.program_id(1)

        @pl.when(jnp.logical_and(b == 0, i == 0))
        def _():
            l_ref[...] = jnp.zeros_like(l_ref)

        @pl.when(i < n_lat_tiles)
        def _():
            err = x_ref[...] - t_ref[...]
            dx_ref[...] = err * (1.0 / d)
            l_ref[...] += 0.5 * jnp.sum(jnp.mean(err * err, axis=-1))

        @pl.when(i >= n_lat_tiles)
        def _():
            dx_ref[...] = jnp.zeros_like(dx_ref)

    row = pl.BlockSpec((None, TOKEN_TILE, d), lambda b, i: (b, i, 0))
    t_spec = pl.BlockSpec((None, TOKEN_TILE, d), lambda b, i: (b, jnp.minimum(i, n_lat_tiles - 1), 0))
    return pl.pallas_call(body, grid=(bsz, t // TOKEN_TILE), in_specs=[row, t_spec],
                          out_specs=[row, pl.BlockSpec((SUBLANE, LANE), lambda b, i: (0, 0))],
                          out_shape=[jax.ShapeDtypeStruct(xu.shape, F32), jax.ShapeDtypeStruct((SUBLANE, LANE), F32)],
                          name="loss", compiler_params=_cparams())(xu, target)


def _row_tile(rows, row_bytes, budget=4 << 20, step=SUBLANE):
    if rows * row_bytes <= budget:
        return rows
    best = None
    for t in range(step, rows, step):
        if rows % t == 0 and t * row_bytes <= budget:
            best = t
    return best if best is not None else rows


def _as3d(x):
    p = x.shape[0]
    c = x.shape[-1]
    return x.reshape(p, -1, c)


def sum_parts(name, x):
    x3 = _as3d(x)
    p, r, c = x3.shape
    tr = _row_tile(r, p * c * 4, step=SUBLANE * (4 // x.dtype.itemsize))

    def body(x_ref, o_ref):
        s = x_ref[0].astype(F32)
        for j in range(1, p):
            s = s + x_ref[j].astype(F32)
        o_ref[...] = s

    out = pl.pallas_call(body, grid=(r // tr,), in_specs=[pl.BlockSpec((p, tr, c), lambda i: (0, i, 0))],
                         out_specs=pl.BlockSpec((tr, c), lambda i: (i, 0)), out_shape=jax.ShapeDtypeStruct((r, c), F32),
                         name=name, compiler_params=_cparams())(x3)
    return out.reshape(x.shape[1:])


def adamw(name, w, g, m, v):
    shape = w.shape
    c = shape[-1]
    to2d = lambda a: a.reshape(-1, c)
    r = to2d(w).shape[0]
    tr = _row_tile(r, 7 * c * 4, budget=6 << 20)

    def body(w_ref, g_ref, m_ref, v_ref, d_ref, nm_ref, nv_ref):
        gg = g_ref[...]
        nm = ADAM_B1 * m_ref[...] + (1.0 - ADAM_B1) * gg
        nv = ADAM_B2 * v_ref[...] + (1.0 - ADAM_B2) * jnp.square(gg)
        m_hat = nm / (1.0 - ADAM_B1 ** ADAM_STEP)
        v_hat = nv / (1.0 - ADAM_B2 ** ADAM_STEP)
        d_ref[...] = -ADAM_LR * (m_hat / (jnp.sqrt(v_hat) + ADAM_EPS) + ADAM_WD * w_ref[...])
        nm_ref[...] = nm
        nv_ref[...] = nv

    spec = pl.BlockSpec((tr, c), lambda i: (i, 0))
    outs = pl.pallas_call(body, grid=(r // tr,), in_specs=[spec] * 4, out_specs=[spec] * 3,
                          out_shape=[jax.ShapeDtypeStruct((r, c), F32)] * 3, name=name, compiler_params=_cparams())(
        to2d(w), to2d(g), to2d(m), to2d(v))
    return tuple(o.reshape(shape) for o in outs)


def exchange(name, srcs, group, same):
    p = 2 ** len(group)
    n = len(srcs)
    out_shape = [jax.ShapeDtypeStruct(((p,) + s.shape) if same else s.shape, s.dtype) for s in srcs]

    def index_of(coords):
        idx = 0
        for a in group:
            idx = idx * 2 + coords[a]
        return idx

    def body(*refs):
        src_refs, out_refs = refs[:n], refs[n:2 * n]
        send_sems, recv_sems = refs[2 * n:]
        pos = {a: lax.axis_index(a) for a in MESH_AXES}
        me = index_of(pos)
        peers = []
        for rel in range(1, p):
            coords = dict(pos)
            for bit, a in enumerate(reversed(group)):
                if (rel >> bit) & 1:
                    coords[a] = 1 - coords[a]
            peers.append((coords, index_of(coords)))

        def src_for(a, idx):
            return src_refs[a] if same else src_refs[a].at[idx]

        sends, recvs = [], []
        for a in range(n):
            for r, (coords, idx) in enumerate(peers):
                dev = tuple(coords[ax] for ax in MESH_AXES)
                send = pltpu.make_async_remote_copy(src_ref=src_for(a, idx), dst_ref=out_refs[a].at[me],
                                                    send_sem=send_sems.at[a, r], recv_sem=recv_sems.at[a, r],
                                                    device_id=dev, device_id_type=pl.DeviceIdType.MESH)
                send.start()
                sends.append(send)
                recvs.append(pltpu.make_async_remote_copy(src_ref=src_for(a, idx), dst_ref=out_refs[a].at[idx],
                                                          send_sem=send_sems.at[a, r], recv_sem=recv_sems.at[a, r],
                                                          device_id=dev, device_id_type=pl.DeviceIdType.MESH))
        for cp in sends:
            cp.wait_send()
        for cp in recvs:
            cp.wait_recv()

    any_spec = pl.BlockSpec(memory_space=pl.ANY)
    outs = pl.pallas_call(body, in_specs=[any_spec] * n, out_specs=[any_spec] * n, out_shape=out_shape,
                          scratch_shapes=[pltpu.SemaphoreType.DMA((n, p - 1)), pltpu.SemaphoreType.DMA((n, p - 1))],
                          name=name)(*srcs)
    me = index_of({a: lax.axis_index(a) for a in MESH_AXES})
    own = [s if same else lax.dynamic_index_in_dim(s, me, axis=0, keepdims=False) for s in srcs]
    return [lax.dynamic_update_index_in_dim(o, w, me, axis=0) for o, w in zip(outs, own)]


def _rope_tables(n_lat, n_ctx, rot_dim, heads):
    n_freq = rot_dim // 4
    tok = jnp.arange(n_lat, dtype=jnp.int32)
    inv = ROPE_THETA ** (-jnp.arange(n_freq, dtype=F32) / n_freq)
    ang = jnp.concatenate([(tok // GRID_W).astype(F32)[:, None] * inv, (tok % GRID_W).astype(F32)[:, None] * inv], axis=-1)
    cos, sin = jnp.cos(ang), jnp.sin(ang)
    cos = jnp.concatenate([jnp.concatenate([cos, cos], -1), jnp.ones((n_ctx, rot_dim), F32)], 0)
    sin = jnp.concatenate([jnp.concatenate([-sin, sin], -1), jnp.zeros((n_ctx, rot_dim), F32)], 0)
    half = rot_dim // 2
    w = heads * rot_dim
    j = np.arange(w)
    src = (j // rot_dim) * rot_dim + (j % rot_dim + half) % rot_dim
    swap = np.zeros((w, w), np.float32)
    swap[src, j] = 1.0
    return jnp.tile(cos, (1, heads)), jnp.tile(sin, (1, heads)), jnp.asarray(swap)


def _to_heads(a, nh):
    b, t, _ = a.shape
    return a.reshape(b, t, nh, -1).transpose(0, 2, 1, 3)


def _to_heads_t(a, nh):
    b, t, _ = a.shape
    return a.reshape(b, t, nh, -1).transpose(0, 2, 3, 1)


def _from_heads(a):
    b, nh, t, w = a.shape
    return a.transpose(0, 2, 1, 3).reshape(b, t, nh * w)


def _pad_w_in(w_in):
    parts, off = [], 0
    for size, pad in zip(IN_SIZES, IN_PAD):
        parts.append(w_in[:, off:off + size])
        if pad > size:
            parts.append(jnp.zeros((w_in.shape[0], pad - size), w_in.dtype))
        off += size
    return jnp.concatenate(parts, axis=1)


def _forward(xu, mods, prm, prm_lo, n_lat, n_ctx):
    bsz, t, d = xu.shape
    depth = mods.shape[0]
    n_lat_tiles = n_lat // TOKEN_TILE
    rope_a = _rope_tables(n_lat, n_ctx, HEAD_DIM, 1)
    rope_bq = _rope_tables(n_lat, n_ctx, B_ROPE, B_HEADS)
    rope_bk = _rope_tables(n_lat, n_ctx, B_ROPE, 1)
    clb = prm["c_lower_bounds"].reshape(depth, 2, C_HEADS, C_DK).transpose(2, 0, 1, 3)
    for l in range(depth):
        nm = lambda s: f"l{l}_{s}"
        vec = lambda name: prm[name][l][None, :]
        xu, h = premod(nm("premix"), xu, vec("g_pre_mix"), mods[l], 0, n_lat_tiles)
        z = linear(nm("w_in"), h.reshape(bsz * t, d), _pad_w_in(prm["w_in"][l]), _pad_w_in(prm_lo["w_in"][l])).reshape(bsz, t, D_IN_PAD)
        seg = lambda i: z[:, :, IN_OFF[i]:IN_OFF[i] + IN_SIZES[i]]
        aq = a_prep(nm("aq"), _to_heads(seg(0), A_HEADS), vec("a_q_norm"), *rope_a)
        ak = a_prep(nm("ak"), _to_heads(seg(1), A_KV_HEADS), vec("a_k_norm"), *rope_a)
        av = _to_heads(seg(2), A_KV_HEADS)
        lower = c_lower(nm("c_lower"), clb, l)
        cq, cv = _to_heads(seg(6), C_HEADS), _to_heads(seg(9), C_HEADS)
        ya, o_f = mix_pair(nm("mix_a"), [aq.reshape(bsz, A_KV_HEADS, A_GROUP, t, HEAD_DIM)], [ak.transpose(0, 1, 3, 2)],
                           av.transpose(0, 1, 3, 2), HEAD_DIM ** -0.5, True, ATT_A_TQ_FWD, ATT_A_TQ, n_lat,
                           cq, _to_heads(seg(7), C_HEADS), cv, lower[:, 0:1], False, n_lat // SCAN_CHUNK)
        ya = _from_heads(ya.reshape(bsz, A_HEADS, t, HEAD_DIM))
        wq = prm["w_q_up"][l].reshape(B_Q_RANK, B_HEADS, B_NOPE + B_ROPE)
        wkv = prm["w_kv_up"][l].reshape(B_KV_RANK, B_HEADS, B_NOPE + B_V)
        qn, qp, kn, bv, kp = b_prep(
            nm("b_prep"), seg(3), seg(4), seg(5), vec("b_q_norm"),
            wq[:, :, :B_NOPE].reshape(B_Q_RANK, -1), wq[:, :, B_NOPE:].reshape(B_Q_RANK, -1), vec("b_kv_norm"),
            wkv[:, :, :B_NOPE].reshape(B_KV_RANK, -1), wkv[:, :, B_NOPE:].reshape(B_KV_RANK, -1), *rope_bq, *rope_bk)
        qb = jnp.concatenate([_to_heads(qn, B_HEADS), _to_heads(qp, B_HEADS)], axis=-1)[:, :, None]
        kbt = jnp.concatenate([_to_heads_t(kn, B_HEADS),
                               jnp.broadcast_to(kp.transpose(0, 2, 1)[:, None], (bsz, B_HEADS, B_ROPE, t))], axis=2)
        yb, o_b = mix_pair(nm("mix_b"), [qb], [kbt], _to_heads_t(bv, B_HEADS), (B_NOPE + B_ROPE) ** -0.5, False, ATT_B_TQ,
                           ATT_B_TQ, n_lat, cq, _to_heads(seg(8), C_HEADS), cv, lower[:, 1:2], True, n_lat // SCAN_CHUNK)
        yb = _from_heads(yb[:, :, 0])
        yc = _from_heads(c_readout(nm("c_out"), o_f, o_b, _to_heads(seg(10), C_HEADS), vec("c_out_norm")))
        y = linear(nm("w_out"), jnp.concatenate([ya, yb, yc], axis=-1).reshape(bsz * t, d), prm["w_out"][l],
                   prm_lo["w_out"][l]).reshape(bsz, t, d)
        x1 = resid(nm("res_mix"), xu, y, vec("g_post_mix"), mods[l], 2, n_lat_tiles)
        x1, h2 = premod(nm("preffn"), x1, vec("g_pre_ffn"), mods[l], 3, n_lat_tiles)
        f = mlp(nm("mlp"), h2.reshape(bsz * t, d), prm["w_ff1"][l], prm["w_ff2"][l], prm_lo["w_ff1"][l],
                prm_lo["w_ff2"][l]).reshape(bsz, t, d)
        xu = resid(nm("res_ffn"), x1, f, vec("g_post_ffn"), mods[l], 5, n_lat_tiles)
    return xu


BIG = {"w_in": 2, "w_q_up": 2, "w_kv_up": 2, "w_out": 1, "w_ff1": 2, "w_ff2": 1}
SMALL = ("g_pre_mix", "g_post_mix", "g_pre_ffn", "g_post_ffn", "a_q_norm", "a_k_norm", "b_q_norm", "b_kv_norm",
         "c_lower_bounds", "c_out_norm")
WEIGHTS = ("c_ctx", "w_ada", "b_ada", "g_pre_mix", "g_post_mix", "g_pre_ffn", "g_post_ffn", "w_in", "a_q_norm", "a_k_norm",
           "b_q_norm", "w_q_up", "b_kv_norm", "w_kv_up", "c_lower_bounds", "c_out_norm", "w_out", "w_ff1", "w_ff2")


def _unshard(g, axis):
    depth, _, r, c = g.shape
    if axis == 1:
        return g.reshape(depth, N_CHIP * r, c)
    return g.transpose(0, 2, 1, 3).reshape(depth, r, N_CHIP * c)


def _shard_major(g, axis):
    depth, r, c = g.shape
    if axis == 1:
        return g.reshape(depth, N_CHIP, r // N_CHIP, c)
    return g.reshape(depth, r, N_CHIP, c // N_CHIP).transpose(0, 2, 1, 3)


def kernel(x, c, ctx, c_ctx, w_ada, b_ada, g_pre_mix, g_post_mix, g_pre_ffn, g_post_ffn, w_in, a_q_norm, a_k_norm, b_q_norm, w_q_up, b_kv_norm, w_kv_up, c_lower_bounds, c_out_norm, w_out, w_ff1, w_ff2, loss_target, m_c_ctx, m_w_ada, m_b_ada, m_g_pre_mix, m_g_post_mix, m_g_pre_ffn, m_g_post_ffn, m_w_in, m_a_q_norm, m_a_k_norm, m_b_q_norm, m_w_q_up, m_b_kv_norm, m_w_kv_up, m_c_lower_bounds, m_c_out_norm, m_w_out, m_w_ff1, m_w_ff2, v_c_ctx, v_w_ada, v_b_ada, v_g_pre_mix, v_g_post_mix, v_g_pre_ffn, v_g_post_ffn, v_w_in, v_a_q_norm, v_a_k_norm, v_b_q_norm, v_w_q_up, v_b_kv_norm, v_w_kv_up, v_c_lower_bounds, v_c_out_norm, v_w_out, v_w_ff1, v_w_ff2):
    local = dict(c_ctx=c_ctx, w_ada=w_ada, b_ada=b_ada, g_pre_mix=g_pre_mix, g_post_mix=g_post_mix, g_pre_ffn=g_pre_ffn,
                 g_post_ffn=g_post_ffn, w_in=w_in, a_q_norm=a_q_norm, a_k_norm=a_k_norm, b_q_norm=b_q_norm, w_q_up=w_q_up,
                 b_kv_norm=b_kv_norm, w_kv_up=w_kv_up, c_lower_bounds=c_lower_bounds, c_out_norm=c_out_norm, w_out=w_out,
                 w_ff1=w_ff1, w_ff2=w_ff2)
    mom = dict(c_ctx=m_c_ctx, w_ada=m_w_ada, b_ada=m_b_ada, g_pre_mix=m_g_pre_mix, g_post_mix=m_g_post_mix,
               g_pre_ffn=m_g_pre_ffn, g_post_ffn=m_g_post_ffn, w_in=m_w_in, a_q_norm=m_a_q_norm, a_k_norm=m_a_k_norm,
               b_q_norm=m_b_q_norm, w_q_up=m_w_q_up, b_kv_norm=m_b_kv_norm, w_kv_up=m_w_kv_up,
               c_lower_bounds=m_c_lower_bounds, c_out_norm=m_c_out_norm, w_out=m_w_out, w_ff1=m_w_ff1, w_ff2=m_w_ff2)
    var = dict(c_ctx=v_c_ctx, w_ada=v_w_ada, b_ada=v_b_ada, g_pre_mix=v_g_pre_mix, g_post_mix=v_g_post_mix,
               g_pre_ffn=v_g_pre_ffn, g_post_ffn=v_g_post_ffn, w_in=v_w_in, a_q_norm=v_a_q_norm, a_k_norm=v_a_k_norm,
               b_q_norm=v_b_q_norm, w_q_up=v_w_q_up, b_kv_norm=v_b_kv_norm, w_kv_up=v_w_kv_up,
               c_lower_bounds=v_c_lower_bounds, c_out_norm=v_c_out_norm, w_out=v_w_out, w_ff1=v_w_ff1, w_ff2=v_w_ff2)

    bsz, n_lat, d = x.shape
    n_ctx = ctx.shape[1]
    depth = w_ada.shape[0]
    assert depth == 2 and n_lat % TOKEN_TILE == 0 and n_ctx % TOKEN_TILE == 0 and bsz * N_DEV + 1 <= ADA_ROWS
    ax, ay, ac = (lax.axis_index(a) for a in MESH_AXES)
    chip = 2 * ax + ay
    dev = 2 * chip + ac

    c_all = exchange("gather_c", [c], MESH_AXES, True)[0].reshape(N_DEV * bsz, d)
    big_names = list(BIG)
    mine = [lax.dynamic_index_in_dim(local[n].astype(BF16), ac, axis=0, keepdims=False) for n in big_names]
    over_chips = exchange("gather_w_quad", mine + [c_lower_bounds], ("x", "y"), True)
    both_layers = exchange("gather_w_pair", over_chips[:-1], ("c",), True)
    prm_lo = {n: _unshard(g, BIG[n]) for n, g in zip(big_names, both_layers)}
    prm = {n: w.astype(F32) for n, w in prm_lo.items()}
    prm["c_lower_bounds"] = over_chips[-1].transpose(1, 2, 0, 3).reshape(depth, 2, -1)
    for n in SMALL:
        if n != "c_lower_bounds":
            prm[n] = local[n]

    n_ada = w_ada.shape[2]
    cc = jnp.concatenate([c_all, c_ctx[None, :], jnp.zeros((ADA_ROWS - N_DEV * bsz - 1, d), F32)], axis=0)
    b_blk = lax.dynamic_slice_in_dim(b_ada, chip * n_ada, n_ada, axis=1)[:, None, :]
    mod_part, ada_vjp = jax.vjp(ada_op, cc, w_ada, b_blk)
    mod_full = exchange("gather_mod", [mod_part], ("x", "y"), True)[0].transpose(1, 2, 0, 3).reshape(depth, ADA_ROWS, 6 * d)
    mod_lat = lax.dynamic_slice_in_dim(mod_full, dev * bsz, bsz, axis=1)
    mod_ctx = jnp.broadcast_to(mod_full[:, N_DEV * bsz:N_DEV * bsz + 1], mod_lat.shape)
    mods = jnp.stack([mod_lat, mod_ctx], axis=2).reshape(depth, bsz, 2, 6, d)

    xu = jnp.concatenate([x, ctx], axis=1)
    x_out, fwd_vjp = jax.vjp(lambda xu_, mods_, prm_: _forward(xu_, mods_, prm_, prm_lo, n_lat, n_ctx), xu, mods, prm)
    dx_out, loss_blk = loss_op(x_out, loss_target, n_lat // TOKEN_TILE)
    d_xu, d_mods, d_prm = fwd_vjp(dx_out)
    grad_x = d_xu[:, :n_lat]

    d_mods = d_mods.reshape(depth, bsz, 2, 6 * d)
    pieces = [d_mods] + [d_prm[n] for n in SMALL] + [loss_blk[0:1, 0:1]]
    sizes = [int(np.prod(p.shape)) for p in pieces]
    flat = jnp.concatenate([p.reshape(-1) for p in pieces])
    n_flat = -(-flat.shape[0] // (SUBLANE * LANE)) * SUBLANE * LANE
    flat = jnp.concatenate([flat, jnp.zeros((n_flat - flat.shape[0],), F32)]).reshape(-1, LANE)
    small_all = exchange("gather_small", [flat], MESH_AXES, True)[0]
    small_sum = sum_parts("sum_small", small_all).reshape(-1)
    offs = np.cumsum([0] + sizes)
    summed = {n: small_sum[offs[i + 1]:offs[i + 2]].reshape(d_prm[n].shape) for i, n in enumerate(SMALL)}
    loss = small_sum[offs[-2]]
    dm_all = small_all.reshape(N_DEV, -1)[:, :sizes[0]].reshape(N_DEV, depth, bsz, 2, 6 * d)
    dm_rows = dm_all.transpose(3, 0, 2, 1, 4).reshape(2, N_DEV * bsz, depth, 6 * d)
    d_ctx_row = sum_parts("sum_dmod_ctx", dm_rows[1])
    grad_b_ada = sum_parts("sum_b_ada", dm_rows.reshape(2 * N_DEV * bsz, depth, 6 * d))
    d_rows = jnp.concatenate([dm_rows[0].transpose(1, 0, 2), d_ctx_row[:, None, :],
                              jnp.zeros((depth, ADA_ROWS - N_DEV * bsz - 1, 6 * d), F32)], axis=1)
    d_cc, grad_w_ada, _ = ada_vjp(lax.dynamic_slice_in_dim(d_rows, chip * n_ada, n_ada, axis=2))
    d_cctx_all = exchange("gather_dcctx", [d_cc[N_DEV * bsz:N_DEV * bsz + 1]], MESH_AXES, True)[0]
    grad_c_ctx = sum_parts("sum_dcctx", d_cctx_all[0::2]).reshape(d)

    shard_major = [_shard_major(d_prm[n], BIG[n]).astype(BF16) for n in big_names]
    pair = exchange("rs_pair", shard_major, ("c",), False)
    chip_sum = [sum_parts(f"rs_sum1_{n}", p) for n, p in zip(big_names, pair)]
    quad = exchange("rs_quad", [s.astype(BF16) for s in chip_sum], ("x", "y"), False)
    total = [sum_parts(f"rs_sum2_{n}", q) for n, q in zip(big_names, quad)]
    both = exchange("rs_share", total, ("c",), True)

    grads = dict(summed)
    grads["c_lower_bounds"] = lax.dynamic_slice_in_dim(summed["c_lower_bounds"], chip * c_lower_bounds.shape[2],
                                                       c_lower_bounds.shape[2], axis=2)
    grads.update(c_ctx=grad_c_ctx, w_ada=grad_w_ada, b_ada=grad_b_ada)
    grads.update({n: g for n, g in zip(big_names, both)})

    deltas, new_m, new_v = {}, {}, {}
    for n in WEIGHTS:
        as2d = (lambda a: a[None, :]) if local[n].ndim == 1 else (lambda a: a)
        dl, nm_, nv_ = adamw("adamw_" + n, as2d(local[n]), as2d(grads[n]), as2d(mom[n]), as2d(var[n]))
        deltas[n], new_m[n], new_v[n] = (a.reshape(local[n].shape) for a in (dl, nm_, nv_))
    return (loss, grad_x, *[grads[n] for n in WEIGHTS], *[deltas[n] for n in WEIGHTS],
            *[new_m[n] for n in WEIGHTS], *[new_v[n] for n in WEIGHTS])
```

```python
import functools
from typing import Any, Callable, NamedTuple

import numpy as np
import jax
import jax.numpy as jnp
from jax import lax
from jax.experimental import pallas as pl
from jax.experimental.pallas import tpu as pltpu

F32 = jnp.float32
BF16 = jnp.bfloat16
HIGHEST = lax.Precision.HIGHEST

GRID_W = 64
HEAD_DIM = 64
A_HEADS, A_KV_HEADS = 8, 2
A_GROUP = A_HEADS // A_KV_HEADS
B_HEADS, B_Q_RANK, B_KV_RANK, B_NOPE, B_ROPE, B_V = 4, 192, 128, 64, 32, 64
C_HEADS, C_DK, C_DV = 4, 64, 64
SCAN_CHUNK = 64
SCAN_SUB = 16
ROPE_THETA = 10000.0
EPS = 1e-6
F_TINY = 1e-30
EXP_ARG_MAX = 80.0
ADAM_LR, ADAM_B1, ADAM_B2, ADAM_EPS, ADAM_WD, ADAM_STEP = 0.001, 0.9, 0.999, 1e-08, 0.01, 10

IN_SIZES = (512, 128, 128, 192, 128, 32, 256, 256, 256, 256, 256)
IN_PAD = (512, 128, 128, 256, 128, 128, 256, 256, 256, 256, 256)
IN_OFF = tuple(int(v) for v in np.cumsum((0,) + IN_PAD)[:-1])
D_IN_PAD = int(sum(IN_PAD))

LANE = 128
SUBLANE = 8
TOKEN_TILE = 256
ATT_A_TQ = 64
ATT_A_TQ_FWD = 128
ATT_B_TQ = 256
ATT_FWD_ROW_GROUPS = 4
MM_ROWS = 256
LINEAR_ROWS = 512
MM_TK_TOKENS = 512
MM_TN_OUT = 4096 * 1024
VMEM_LIMIT = 56 * 1024 * 1024
MESH_AXES = ("x", "y", "c")
N_DEV = 8
N_CHIP = 4
ADA_ROWS = 24


class Arg(NamedTuple):
    arr: Any
    block: tuple
    imap: Callable
    kind: str
    first: Callable = None


class Out(NamedTuple):
    shape: tuple
    block: tuple
    imap: Callable


def _cparams():
    return pltpu.CompilerParams(vmem_limit_bytes=VMEM_LIMIT)


def tile_op(name, fn, grid, args, outs, residual=None, passthrough=False):
    n_in, n_out = len(args), len(outs)
    in_specs = [pl.BlockSpec(a.block, a.imap) for a in args]
    out_specs = [pl.BlockSpec(o.block, o.imap) for o in outs]
    out_shape = [jax.ShapeDtypeStruct(o.shape, F32) for o in outs]
    diff = [i for i, a in enumerate(args) if a.kind != "const"]

    def fwd_call(*arrays):
        def body(*refs):
            ids = tuple(pl.program_id(i) for i in range(len(grid)))
            res = list(fn(ids, *[r[...] for r in refs[:n_in]]))
            out_refs = refs[n_in:]
            if residual is not None:
                res[0] = res[0] + refs[n_in][...]
                out_refs = refs[n_in + 1:]
            for r, o in zip(out_refs, res):
                r[...] = o

        specs = in_specs + ([out_specs[0]] if residual is not None else [])
        return pl.pallas_call(body, grid=grid, in_specs=specs, out_specs=out_specs, out_shape=out_shape,
                              name=name + "_fwd", compiler_params=_cparams())(*arrays)

    def bwd_call(arrays, cts):
        def body(*refs):
            ids = tuple(pl.program_id(i) for i in range(len(grid)))
            vals = [r[...] for r in refs[:n_in]]
            ct = tuple(r[...] for r in refs[n_in:n_in + n_out])
            drefs = refs[n_in + n_out:]

            def g(*dv):
                full = list(vals)
                for i, v in zip(diff, dv):
                    full[i] = v
                return tuple(fn(ids, *full))

            _, vjp = jax.vjp(g, *[vals[i] for i in diff])
            ds = list(vjp(ct))
            if passthrough:
                ds[0] = ds[0] + refs[n_in + n_out][...]
                drefs = refs[n_in + n_out + 1:]
            for i, d, r in zip(diff, ds, drefs):
                if args[i].kind == "tile":
                    r[...] = d
                else:
                    is_first = args[i].first(ids)

                    @pl.when(is_first)
                    def _(r=r, d=d):
                        r[...] = d

                    @pl.when(jnp.logical_not(is_first))
                    def _(r=r, d=d):
                        r[...] += d

        d_specs = [in_specs[i] for i in diff]
        d_shape = [jax.ShapeDtypeStruct(arrays[i].shape, F32) for i in diff]
        specs = in_specs + out_specs + ([in_specs[0]] if passthrough else [])
        return pl.pallas_call(body, grid=grid, in_specs=specs, out_specs=d_specs, out_shape=d_shape,
                              name=name + "_bwd", compiler_params=_cparams())(*arrays, *cts)

    assert not passthrough or (diff and diff[0] == 0 and args[0].kind == "tile")

    def results(arrays):
        res = tuple(fwd_call(*arrays))
        return ((arrays[0],) + res) if passthrough else res

    @jax.custom_vjp
    def op(*arrays):
        return results(arrays)

    def op_fwd(*arrays):
        return results(arrays), arrays[:n_in]

    def op_bwd(arrays, cts):
        if passthrough:
            cts = tuple(cts[1:]) + (cts[0],)
        ds = bwd_call(arrays, cts)
        res, k = [], 0
        for i, a in enumerate(args):
            if a.kind == "const":
                res.append(jnp.zeros_like(arrays[i]))
            else:
                res.append(ds[k])
                k += 1
        if residual is not None:
            res.append(cts[0])
        return tuple(res)

    op.defvjp(op_fwd, op_bwd)
    return op(*[a.arr for a in args], *([residual] if residual is not None else []))


def _pick(n, cap):
    if n <= cap:
        return n
    best = None
    for t in range(LANE, cap + 1, LANE):
        if n % t == 0:
            best = t
    assert best is not None, (n, cap)
    return best


_NN = (((1,), (0,)), ((), ()))
_NT = (((1,), (1,)), ((), ()))
_TN = (((0,), (0,)), ((), ()))


def _resident(shape):
    return pl.BlockSpec(shape, lambda *ids: (0,) * len(shape), pipeline_mode=pl.Buffered(1))


def _mm_rows(name, a, w, transposed):
    m, k = a.shape
    n = w.shape[0] if transposed else w.shape[1]
    tm = LINEAR_ROWS
    dims = _NT if transposed else _NN

    def body(a_ref, w_ref, o_ref):
        o_ref[...] = lax.dot_general(a_ref[...].astype(BF16), w_ref[...], dims, preferred_element_type=F32)

    return pl.pallas_call(body, grid=(m // tm,), in_specs=[pl.BlockSpec((tm, k), lambda i: (i, 0)), _resident(w.shape)],
                          out_specs=pl.BlockSpec((tm, n), lambda i: (i, 0)), out_shape=jax.ShapeDtypeStruct((m, n), F32),
                          name=name, compiler_params=_cparams())(a, w)


def _mm_tn(name, a, g):
    t, k = a.shape
    n = g.shape[1]
    tko, tno = k, n
    while tko * tno > MM_TN_OUT:
        if tko >= tno:
            tko //= 2
        else:
            tno //= 2
    assert k % tko == 0 and n % tno == 0 and tko % LANE == 0 and tno % LANE == 0
    tt = _pick(t, MM_TK_TOKENS)

    def body(a_ref, g_ref, o_ref):
        p = lax.dot_general(a_ref[...].astype(BF16), g_ref[...].astype(BF16), _TN, preferred_element_type=F32)
        kk = pl.program_id(2)

        @pl.when(kk == 0)
        def _():
            o_ref[...] = p

        @pl.when(kk != 0)
        def _():
            o_ref[...] += p

    return pl.pallas_call(body, grid=(k // tko, n // tno, t // tt),
                          in_specs=[pl.BlockSpec((tt, tko), lambda i, j, kk: (kk, i)), pl.BlockSpec((tt, tno), lambda i, j, kk: (kk, j))],
                          out_specs=pl.BlockSpec((tko, tno), lambda i, j, kk: (i, j)),
                          out_shape=jax.ShapeDtypeStruct((k, n), F32), name=name, compiler_params=_cparams())(a, g)


def linear(name, a, w, w_lo):
    @jax.custom_vjp
    def op(a, w, w_lo):
        return _mm_rows(name + "_fwd", a, w_lo, False)

    def op_fwd(a, w, w_lo):
        return _mm_rows(name + "_fwd", a, w_lo, False), (a, w_lo)

    def op_bwd(res, g):
        a, w_lo = res
        return _mm_rows(name + "_da", g, w_lo, True), _mm_tn(name + "_dw", a, g), jnp.zeros_like(w_lo)

    op.defvjp(op_fwd, op_bwd)
    return op(a, w, w_lo)


def mlp(name, h, w1, w2, w1_lo, w2_lo):
    m, d = h.shape
    f = w1_lo.shape[1]
    tm = MM_ROWS
    row = pl.BlockSpec((tm, d), lambda i: (i, 0))
    wide = pl.BlockSpec((tm, f), lambda i: (i, 0))

    def fwd_call(h, w1_lo, w2_lo):
        def body(h_ref, w1_ref, w2_ref, o_ref):
            u = lax.dot_general(h_ref[...].astype(BF16), w1_ref[...], _NN, preferred_element_type=F32)
            act = jnp.square(jnp.maximum(u, 0.0))
            o_ref[...] = lax.dot_general(act.astype(BF16), w2_ref[...], _NN, preferred_element_type=F32)

        return pl.pallas_call(body, grid=(m // tm,), in_specs=[row, _resident(w1_lo.shape), _resident(w2_lo.shape)],
                              out_specs=row, out_shape=jax.ShapeDtypeStruct((m, d), F32), name=name + "_fwd",
                              compiler_params=_cparams())(h, w1_lo, w2_lo)

    def bwd_call(h, w1_lo, w2_lo, dy):
        def body(h_ref, dy_ref, w1_ref, w2_ref, dh_ref, act_ref, du_ref):
            u = lax.dot_general(h_ref[...].astype(BF16), w1_ref[...], _NN, preferred_element_type=F32)
            r = jnp.maximum(u, 0.0)
            act_ref[...] = (r * r).astype(BF16)
            dact = lax.dot_general(dy_ref[...].astype(BF16), w2_ref[...], _NT, preferred_element_type=F32)
            du = (dact * (2.0 * r)).astype(BF16)
            du_ref[...] = du
            dh_ref[...] = lax.dot_general(du, w1_ref[...], _NT, preferred_element_type=F32)

        return pl.pallas_call(body, grid=(m // tm,), in_specs=[row, row, _resident(w1_lo.shape), _resident(w2_lo.shape)],
                              out_specs=[row, wide, wide],
                              out_shape=[jax.ShapeDtypeStruct((m, d), F32), jax.ShapeDtypeStruct((m, f), BF16),
                                         jax.ShapeDtypeStruct((m, f), BF16)],
                              name=name + "_bwd", compiler_params=_cparams())(h, dy, w1_lo, w2_lo)

    @jax.custom_vjp
    def op(h, w1, w2, w1_lo, w2_lo):
        return fwd_call(h, w1_lo, w2_lo)

    def op_fwd(h, w1, w2, w1_lo, w2_lo):
        return fwd_call(h, w1_lo, w2_lo), (h, w1_lo, w2_lo)

    def op_bwd(res, dy):
        h, w1_lo, w2_lo = res
        dh, act, du = bwd_call(h, w1_lo, w2_lo, dy)
        return (dh, _mm_tn(name + "_dw1", h, du), _mm_tn(name + "_dw2", act, dy), jnp.zeros_like(w1_lo), jnp.zeros_like(w2_lo))

    op.defvjp(op_fwd, op_bwd)
    return op(h, w1, w2, w1_lo, w2_lo)


def _rms(x, g):
    return x * lax.rsqrt(jnp.mean(x * x, axis=-1, keepdims=True) + EPS) * g


def _sigmoid(z):
    return 1.0 / (1.0 + jnp.exp(jnp.minimum(-z, EXP_ARG_MAX)))


def _silu(z):
    return z * _sigmoid(z)


def _rope(y, cos, sin_signed, swap):
    return y * cos + jnp.dot(y, swap, precision=HIGHEST, preferred_element_type=F32) * sin_signed


def _dot_bf16(a, b, dims=((1,), (0,))):
    return lax.dot_general(a.astype(BF16), b.astype(BF16), (dims, ((), ())), preferred_element_type=F32)


def _gla_step(state, q, k, v, g, reverse):
    c, d = q.shape
    sub = SCAN_SUB
    nb = c // sub
    row = lax.broadcasted_iota(jnp.int32, (c, c), 0)
    col = lax.broadcasted_iota(jnp.int32, (c, c), 1)
    tri = (row <= col) if reverse else (row >= col)
    b = jnp.dot(tri.astype(F32), g, precision=HIGHEST, preferred_element_type=F32)
    o = jnp.dot(q * jnp.exp(b), state, preferred_element_type=F32)
    rs = lax.broadcasted_iota(jnp.int32, (sub, sub), 0)
    cs = lax.broadcasted_iota(jnp.int32, (sub, sub), 1)
    tri_s = ((rs <= cs) if reverse else (rs >= cs)).astype(F32)
    rowc = lax.broadcasted_iota(jnp.int32, (c, 1), 0)
    nonpos = lambda x: jnp.where(x > 0.0, 0.0, x)
    diag = []
    for j in range(nb):
        sl = slice(j * sub, (j + 1) * sub)
        bj, kj, vj, qj = b[sl], k[sl], v[sl], q[sl]
        dec = jnp.exp(nonpos(bj[:, None, :] - bj[None, :, :]))
        sc = jnp.sum(qj[:, None, :] * kj[None, :, :] * dec, axis=-1) * tri_s
        diag.append(jnp.dot(sc, vj, preferred_element_type=F32))
        if (j > 0) if reverse else (j < nb - 1):
            ref = bj[0:1] if reverse else bj[sub - 1:sub]
            qa = q * jnp.exp(nonpos(b - ref))
            ks = kj * jnp.exp(ref - bj)
            scj = lax.dot_general(qa, ks, (((1,), (1,)), ((), ())), precision=HIGHEST, preferred_element_type=F32)
            later = (rowc < j * sub) if reverse else (rowc >= (j + 1) * sub)
            o = o + jnp.dot(jnp.where(later, scj, 0.0), vj, preferred_element_type=F32)
    o = o + jnp.concatenate(diag, axis=0)
    b_end = b[0:1, :] if reverse else b[c - 1:c, :]
    kd = k * jnp.exp(b_end - b)
    new_state = state * jnp.exp(b_end).reshape(d, 1) + lax.dot_general(kd, v, (((0,), (0,)), ((), ())), preferred_element_type=F32)
    return new_state, o


def _zero_ids(ids):
    z = ids[0] == 0
    for i in ids[1:]:
        z = jnp.logical_and(z, i == 0)
    return z


def _token_grid(x, n_lat_tiles):
    bsz, t, d = x.shape
    nt = t // TOKEN_TILE
    row = lambda w: ((None, TOKEN_TILE, w), lambda b, i: (b, i, 0))
    mod_block = (None, None, 6, d)
    mod_imap = lambda b, i: (b, (i >= n_lat_tiles).astype(jnp.int32), 0, 0)
    mod_first = lambda ids: jnp.logical_or(ids[1] == 0, ids[1] == n_lat_tiles)
    return bsz, t, d, nt, row, (mod_block, mod_imap, mod_first)


def premod(name, x, gain, mods, r0, n_lat_tiles):
    bsz, t, d, nt, row, (mb, mi, mf) = _token_grid(x, n_lat_tiles)

    def fn(ids, xb, gb, mod):
        return (_rms(xb, gb) * (1.0 + mod[r0 + 1:r0 + 2]) + mod[r0:r0 + 1],)

    args = [Arg(x, *row(d), "tile"), Arg(gain, (1, d), lambda b, i: (0, 0), "acc", _zero_ids), Arg(mods, mb, mi, "acc", mf)]
    return tile_op(name, fn, (bsz, nt), args, [Out(x.shape, *row(d))], passthrough=True)


def resid(name, x, y, gain, mods, r, n_lat_tiles):
    bsz, t, d, nt, row, (mb, mi, mf) = _token_grid(x, n_lat_tiles)

    def fn(ids, yb, gb, mod):
        return (mod[r:r + 1] * _rms(yb, gb),)

    args = [Arg(y, *row(d), "tile"), Arg(gain, (1, d), lambda b, i: (0, 0), "acc", _zero_ids), Arg(mods, mb, mi, "acc", mf)]
    return tile_op(name, fn, (bsz, nt), args, [Out(x.shape, *row(d))], residual=x)[0]


def _heads_spec(nh, w):
    return (None, nh, TOKEN_TILE, w), lambda b, i: (b, 0, i, 0)


def a_prep(name, x, gain, cos, sin, swap):
    bsz, nh, t, d = x.shape
    tab = ((TOKEN_TILE, d), lambda b, i: (i, 0))

    def fn(ids, xb, gb, cb, sb, pb):
        y = _rms(xb, gb)
        swapped = jnp.dot(y.reshape(nh * TOKEN_TILE, d), pb, precision=HIGHEST, preferred_element_type=F32)
        return (y * cb + swapped.reshape(nh, TOKEN_TILE, d) * sb,)

    args = [Arg(x, *_heads_spec(nh, d), "tile"), Arg(gain, (1, d), lambda b, i: (0, 0), "acc", _zero_ids),
            Arg(cos, *tab, "const"), Arg(sin, *tab, "const"), Arg(swap, (d, d), lambda b, i: (0, 0), "const")]
    return tile_op(name, fn, (bsz, t // TOKEN_TILE), args, [Out(x.shape, *_heads_spec(nh, d))])[0]


def b_prep(name, bqd, bkvd, bkr, bqn, wq_nope, wq_pe, bkvn, wkv_nope, wkv_v, cos_q, sin_q, swap_q, cos_k, sin_k, swap_k):
    bsz, t, _ = bqd.shape
    row = lambda w: ((None, TOKEN_TILE, w), lambda b, i: (b, i, 0))
    whole = lambda a: (a.shape, lambda b, i: (0,) * a.ndim)
    tab = lambda w: ((TOKEN_TILE, w), lambda b, i: (i, 0))

    def fn(ids, qd, kvd, kr, qn, wqn, wqp, kvn, wkn, wkv, cq, sq, pq, ck, sk, pk):
        hq = _rms(qd, qn)
        hkv = _rms(kvd, kvn)
        return (_dot_bf16(hq, wqn), _rope(_dot_bf16(hq, wqp), cq, sq, pq), _dot_bf16(hkv, wkn), _dot_bf16(hkv, wkv),
                _rope(kr, ck, sk, pk))

    params = [bqn, wq_nope, wq_pe, bkvn, wkv_nope, wkv_v]
    args = [Arg(bqd, *row(B_Q_RANK), "tile"), Arg(bkvd, *row(B_KV_RANK), "tile"), Arg(bkr, *row(B_ROPE), "tile")]
    args += [Arg(p, *whole(p), "acc", _zero_ids) for p in params]
    args += [Arg(cos_q, *tab(cos_q.shape[1]), "const"), Arg(sin_q, *tab(cos_q.shape[1]), "const"), Arg(swap_q, *whole(swap_q), "const"),
             Arg(cos_k, *tab(B_ROPE), "const"), Arg(sin_k, *tab(B_ROPE), "const"), Arg(swap_k, *whole(swap_k), "const")]
    widths = (B_HEADS * B_NOPE, B_HEADS * B_ROPE, B_HEADS * B_NOPE, B_HEADS * B_V, B_ROPE)
    outs = [Out((bsz, t, w), *row(w)) for w in widths]
    return tile_op(name, fn, (bsz, t // TOKEN_TILE), args, outs)


def c_lower(name, clb, layer):
    nh, depth, _, d = clb.shape

    def fn(ids, lbs):
        lb = [lbs[:, j] for j in range(depth)]
        m = lb[0]
        for j in range(1, depth):
            m = jnp.maximum(m, lb[j])
        e = [jnp.exp(lb[j] - m) for j in range(depth)]
        tot = e[0]
        for j in range(1, depth):
            tot = tot + e[j]
        p = [ej / tot for ej in e]
        cum = p[0]
        for j in range(1, layer + 1):
            cum = cum + p[j]
        return (cum - p[0],)

    whole = lambda shape: (shape, lambda i: (0,) * len(shape))
    return tile_op(name, fn, (1,), [Arg(clb, *whole(clb.shape), "tile")], [Out((nh, 2, d), *whole((nh, 2, d)))])[0]


def c_readout(name, o_f, o_b, gate, gain):
    bsz, nh, t, d = o_f.shape
    spec = _heads_spec(nh, d)

    def fn(ids, of, ob, gt, gn):
        return (_rms(of + ob, gn) * _silu(gt),)

    args = [Arg(o_f, *spec, "tile"), Arg(o_b, *spec, "tile"), Arg(gate, *spec, "tile"),
            Arg(gain, (1, d), lambda b, i: (0, 0), "acc", _zero_ids)]
    return tile_op(name, fn, (bsz, t // TOKEN_TILE), args, [Out(o_f.shape, *spec)])[0]


def _att_rows_fwd(qq, kts_lo, vt_lo, scale, fold_scale, n_grp):
    rows = qq[0].shape[0]
    gr = rows // n_grp
    outs, lses = [], []
    for r in range(n_grp):
        s = None
        for q2, kt in zip(qq, kts_lo):
            part = lax.dot_general(q2[r * gr:(r + 1) * gr], kt, _NN, preferred_element_type=F32)
            s = part if s is None else s + part
        if not fold_scale:
            s = s * scale
        m = jnp.max(s, axis=-1, keepdims=True)
        e = jnp.exp(s - m)
        l = jnp.sum(e, axis=-1, keepdims=True)
        outs.append(lax.dot_general(e.astype(BF16), vt_lo, _NT, preferred_element_type=F32) * (1.0 / l))
        lses.append(m + jnp.log(l))
    return jnp.concatenate(outs, axis=0), jnp.concatenate(lses, axis=0)


def _att_rows_bwd(qq, kts_lo, vt_lo, o2, lse, do2, scale, fold_scale):
    s = None
    for q2, kt in zip(qq, kts_lo):
        part = lax.dot_general(q2, kt, _NN, preferred_element_type=F32)
        s = part if s is None else s + part
    if not fold_scale:
        s = s * scale
    p = jnp.exp(s - lse)
    delta = jnp.sum(do2 * o2, axis=-1, keepdims=True)
    do_lo = do2.astype(BF16)
    ds = p * (lax.dot_general(do_lo, vt_lo, _NN, preferred_element_type=F32) - delta)
    if not fold_scale:
        ds = ds * scale
    ds_lo = ds.astype(BF16)
    dvt = lax.dot_general(do_lo, p.astype(BF16), _TN, preferred_element_type=F32)
    dqs, dkts = [], []
    for q2, kt in zip(qq, kts_lo):
        dq = lax.dot_general(ds_lo, kt, _NT, preferred_element_type=F32)
        dqs.append(dq * scale if fold_scale else dq)
        dkts.append(lax.dot_general(q2, ds_lo, _TN, preferred_element_type=F32))
    return dqs, dkts, dvt


def mix_pair(name, qs, kts, vt, scale, fold_scale, sub_fwd, sub_bwd, n_lat, sq, sz, sv, lb, reverse, n_lat_chunks):
    bsz, hk, grp, t, _ = qs[0].shape
    dv = vt.shape[2]
    n_parts = len(qs)
    nh, d = sq.shape[1], sq.shape[3]
    c = SCAN_CHUNK
    n = t // c
    tph = n // hk
    tq = t // tph
    assert tph * hk == n and tq * tph == t and tq % sub_fwd == 0 and tq % sub_bwd == 0 and n_lat % tq == 0
    chains = [(b, h) for b in range(bsz) for h in range(nh)]
    widths = [q.shape[-1] for q in qs]

    def chunk_of(j):
        return (n - 1 - j) if reverse else lax.rem(j + n_lat_chunks, n)

    def chain(state, q_raw, z, v, lo_b):
        f = lo_b + (1.0 - lo_b) * _sigmoid(z)
        return _gla_step(state, _silu(q_raw), (1.0 - lo_b) * _sigmoid(-z), v, jnp.log(jnp.maximum(f, F_TINY)), reverse)

    lb_spec = pl.BlockSpec(lb.shape, lambda j: (0, 0, 0))
    head_of = lambda j: j // tph
    tile_of = lambda j: lax.rem(j, tph)
    per_tile = lambda w: pl.BlockSpec((bsz, None, grp, tq, w), lambda j: (0, head_of(j), 0, tile_of(j), 0))
    per_head = lambda w: pl.BlockSpec((bsz, None, w, t), lambda j: (0, head_of(j), 0, 0))
    q_specs = [per_tile(w) for w in widths]
    kt_specs = [per_head(w) for w in widths]
    att_in = q_specs + kt_specs + [per_head(dv)]
    o_shape = jax.ShapeDtypeStruct((bsz, hk, grp, t, dv), F32)
    lse_shape = jax.ShapeDtypeStruct((bsz, hk, grp, t, LANE), F32)
    st_shape = jax.ShapeDtypeStruct((bsz, nh, n, d, d), F32)
    ranges = ((lambda j: tile_of(j) * tq < n_lat, 0), (lambda j: tile_of(j) * tq >= n_lat, n_lat))

    def load_q(q_refs, b, r0, sub):
        qq = []
        for q_ref, w in zip(q_refs, widths):
            q2 = q_ref[b, :, r0:r0 + sub, :].reshape(grp * sub, w)
            qq.append((q2 * scale if fold_scale else q2).astype(BF16))
        return qq

    def fwd_call(*arrays):
        def body(*refs):
            q_refs, kt_refs, vt_ref = refs[:n_parts], refs[n_parts:2 * n_parts], refs[2 * n_parts]
            sq_ref, sz_ref, sv_ref, lb_ref = refs[2 * n_parts + 1:2 * n_parts + 5]
            o_ref, lse_ref, so_ref, states_ref, st = refs[2 * n_parts + 5:]
            j = pl.program_id(0)

            @pl.when(j == 0)
            def _():
                st[...] = jnp.zeros_like(st)

            def step(col0):
                for bh in chains:
                    s = st[bh]
                    states_ref[bh] = s
                    ns, o = chain(s, sq_ref[bh], sz_ref[bh], sv_ref[bh], lb_ref[bh[1]])
                    st[bh] = ns
                    so_ref[bh] = o
                for b in range(bsz):
                    kts_lo = [r[b, :, col0:] for r in kt_refs]
                    vt_lo = vt_ref[b, :, col0:]
                    for r0 in range(0, tq, sub_fwd):
                        o, lse = _att_rows_fwd(load_q(q_refs, b, r0, sub_fwd), kts_lo, vt_lo, scale, fold_scale, ATT_FWD_ROW_GROUPS)
                        o_ref[b, :, r0:r0 + sub_fwd, :] = o.reshape(grp, sub_fwd, dv)
                        lse_ref[b, :, r0:r0 + sub_fwd, :] = jnp.broadcast_to(lse, (grp * sub_fwd, LANE)).reshape(grp, sub_fwd, LANE)

            for cond, col0 in ranges:
                pl.when(cond(j))(functools.partial(step, col0))

        scan_spec = pl.BlockSpec((bsz, nh, c, d), lambda j: (0, 0, chunk_of(j), 0))
        return pl.pallas_call(
            body, grid=(n,), in_specs=att_in + [scan_spec] * 3 + [lb_spec],
            out_specs=[per_tile(dv), per_tile(LANE), scan_spec, pl.BlockSpec((bsz, nh, None, d, d), lambda j: (0, 0, j, 0, 0))],
            out_shape=[o_shape, lse_shape, jax.ShapeDtypeStruct(sq.shape, F32), st_shape],
            scratch_shapes=[pltpu.VMEM((bsz, nh, d, d), F32)], name=name + "_fwd", compiler_params=_cparams())(*arrays)

    def bwd_call(att_lo, scan_in, o, lse, states, do, dso):
        def body(*refs):
            q_refs, kt_refs, vt_ref = refs[:n_parts], refs[n_parts:2 * n_parts], refs[2 * n_parts]
            k0 = 2 * n_parts + 1
            sq_ref, sz_ref, sv_ref, lb_ref, s_ref = refs[k0:k0 + 5]
            o_ref, lse_ref, do_ref, dso_ref = refs[k0 + 5:k0 + 9]
            d_refs = refs[k0 + 9:]
            dq_refs, dkt_refs, dvt_ref = d_refs[:n_parts], d_refs[n_parts:2 * n_parts], d_refs[2 * n_parts]
            dsq_ref, dsz_ref, dsv_ref, dlb_ref, dst = d_refs[2 * n_parts + 1:]
            jj = pl.program_id(0)

            @pl.when(jj == 0)
            def _():
                dst[...] = jnp.zeros_like(dst)
                dlb_ref[...] = jnp.zeros_like(dlb_ref)

            @pl.when(tile_of(jj) == 0)
            def _():
                for ref in list(dkt_refs) + [dvt_ref]:
                    ref[...] = jnp.zeros_like(ref)

            def step(col0):
                for bh in chains:
                    _, vjp = jax.vjp(chain, s_ref[bh], sq_ref[bh], sz_ref[bh], sv_ref[bh], lb_ref[bh[1]])
                    ds, dq, dz, dv_, dlb = vjp((dst[bh], dso_ref[bh]))
                    dst[bh] = ds
                    dsq_ref[bh] = dq
                    dsz_ref[bh] = dz
                    dsv_ref[bh] = dv_
                    dlb_ref[bh[1]] += dlb
                for b in range(bsz):
                    kts_lo = [r[b, :, col0:] for r in kt_refs]
                    vt_lo = vt_ref[b, :, col0:]
                    for r0 in range(0, tq, sub_bwd):
                        rows = grp * sub_bwd
                        rsl = slice(r0, r0 + sub_bwd)
                        dqs, dkts, dvt = _att_rows_bwd(
                            load_q(q_refs, b, r0, sub_bwd), kts_lo, vt_lo, o_ref[b, :, rsl, :].reshape(rows, dv),
                            lse_ref[b, :, rsl, :].reshape(rows, LANE)[:, 0:1], do_ref[b, :, rsl, :].reshape(rows, dv), scale, fold_scale)
                        for dq_ref, dq, w in zip(dq_refs, dqs, widths):
                            dq_ref[b, :, rsl, :] = dq.reshape(grp, sub_bwd, w)
                        for dkt_ref, dkt in zip(dkt_refs, dkts):
                            dkt_ref[b, :, col0:] += dkt
                        dvt_ref[b, :, col0:] += dvt

            for cond, col0 in ranges:
                pl.when(cond(jj))(functools.partial(step, col0))

        scan_spec = pl.BlockSpec((bsz, nh, c, d), lambda jj: (0, 0, chunk_of(n - 1 - jj), 0))
        st_spec = pl.BlockSpec((bsz, nh, None, d, d), lambda jj: (0, 0, n - 1 - jj, 0, 0))
        in_specs = att_in + [scan_spec] * 3 + [lb_spec, st_spec, per_tile(dv), per_tile(LANE), per_tile(dv), scan_spec]
        d_shape = ([jax.ShapeDtypeStruct(a.shape, F32) for a in att_lo] + [jax.ShapeDtypeStruct(sq.shape, F32)] * 3
                   + [jax.ShapeDtypeStruct(lb.shape, F32)])
        return pl.pallas_call(body, grid=(n,), in_specs=in_specs, out_specs=att_in + [scan_spec] * 3 + [lb_spec], out_shape=d_shape,
                              scratch_shapes=[pltpu.VMEM((bsz, nh, d, d), F32)], name=name + "_bwd",
                              compiler_params=_cparams())(*att_lo, *scan_in, states, o, lse, do, dso)

    n_att = 2 * n_parts + 1

    def lo(arrays):
        return arrays[:n_parts] + tuple(a.astype(BF16) for a in arrays[n_parts:n_att])

    @jax.custom_vjp
    def op(*arrays):
        res = fwd_call(*lo(arrays), *arrays[n_att:])
        return res[0], res[2]

    def op_fwd(*arrays):
        att_lo, scan_in = lo(arrays), arrays[n_att:]
        o, lse, so, states = fwd_call(*att_lo, *scan_in)
        return (o, so), (att_lo, scan_in, o, lse, states)

    def op_bwd(res, cts):
        att_lo, scan_in, o, lse, states = res
        return tuple(bwd_call(att_lo, scan_in, o, lse, states, cts[0], cts[1]))

    op.defvjp(op_fwd, op_bwd)
    return op(*qs, *kts, vt, sq, sz, sv, lb)


def ada_op(cc, w, b):
    depth, d, n = w.shape

    def fn(ids, ccb, wb, bb):
        return (_dot_bf16(_silu(ccb), wb) + bb,)

    args = [Arg(cc, cc.shape, lambda l: (0, 0), "acc", lambda ids: ids[0] == 0),
            Arg(w, (None, d, n), lambda l: (l, 0, 0), "tile"), Arg(b, (None, 1, n), lambda l: (l, 0, 0), "tile")]
    return tile_op("ada", fn, (depth,), args, [Out((depth, cc.shape[0], n), (None, cc.shape[0], n), lambda l: (l, 0, 0))])[0]


def loss_op(xu, target, n_lat_tiles):
    bsz, t, d = xu.shape

    def body(x_ref, t_ref, dx_ref, l_ref):
        b, i = pl.program_id(0), pl.program_id(1)

        @pl.when(jnp.logical_and(b == 0, i == 0))
        def _():
            l_ref[...] = jnp.zeros_like(l_ref)

        @pl.when(i < n_lat_tiles)
        def _():
            err = x_ref[...] - t_ref[...]
            dx_ref[...] = err * (1.0 / d)
            l_ref[...] += 0.5 * jnp.sum(jnp.mean(err * err, axis=-1))

        @pl.when(i >= n_lat_tiles)
        def _():
            dx_ref[...] = jnp.zeros_like(dx_ref)

    row = pl.BlockSpec((None, TOKEN_TILE, d), lambda b, i: (b, i, 0))
    t_spec = pl.BlockSpec((None, TOKEN_TILE, d), lambda b, i: (b, jnp.minimum(i, n_lat_tiles - 1), 0))
    return pl.pallas_call(body, grid=(bsz, t // TOKEN_TILE), in_specs=[row, t_spec],
                          out_specs=[row, pl.BlockSpec((SUBLANE, LANE), lambda b, i: (0, 0))],
                          out_shape=[jax.ShapeDtypeStruct(xu.shape, F32), jax.ShapeDtypeStruct((SUBLANE, LANE), F32)],
                          name="loss", compiler_params=_cparams())(xu, target)


def _row_tile(rows, row_bytes, budget=4 << 20, step=SUBLANE):
    if rows * row_bytes <= budget:
        return rows
    best = None
    for t in range(step, rows, step):
        if rows % t == 0 and t * row_bytes <= budget:
            best = t
    return best if best is not None else rows


def _as3d(x):
    p = x.shape[0]
    c = x.shape[-1]
    return x.reshape(p, -1, c)


def sum_parts(name, x):
    x3 = _as3d(x)
    p, r, c = x3.shape
    tr = _row_tile(r, p * c * 4, step=SUBLANE * (4 // x.dtype.itemsize))

    def body(x_ref, o_ref):
        s = x_ref[0].astype(F32)
        for j in range(1, p):
            s = s + x_ref[j].astype(F32)
        o_ref[...] = s

    out = pl.pallas_call(body, grid=(r // tr,), in_specs=[pl.BlockSpec((p, tr, c), lambda i: (0, i, 0))],
                         out_specs=pl.BlockSpec((tr, c), lambda i: (i, 0)), out_shape=jax.ShapeDtypeStruct((r, c), F32),
                         name=name, compiler_params=_cparams())(x3)
    return out.reshape(x.shape[1:])


def adamw(name, w, g, m, v):
    shape = w.shape
    c = shape[-1]
    to2d = lambda a: a.reshape(-1, c)
    r = to2d(w).shape[0]
    tr = _row_tile(r, 7 * c * 4, budget=6 << 20)

    def body(w_ref, g_ref, m_ref, v_ref, d_ref, nm_ref, nv_ref):
        gg = g_ref[...]
        nm = ADAM_B1 * m_ref[...] + (1.0 - ADAM_B1) * gg
        nv = ADAM_B2 * v_ref[...] + (1.0 - ADAM_B2) * jnp.square(gg)
        m_hat = nm / (1.0 - ADAM_B1 ** ADAM_STEP)
        v_hat = nv / (1.0 - ADAM_B2 ** ADAM_STEP)
        d_ref[...] = -ADAM_LR * (m_hat / (jnp.sqrt(v_hat) + ADAM_EPS) + ADAM_WD * w_ref[...])
        nm_ref[...] = nm
        nv_ref[...] = nv

    spec = pl.BlockSpec((tr, c), lambda i: (i, 0))
    outs = pl.pallas_call(body, grid=(r // tr,), in_specs=[spec] * 4, out_specs=[spec] * 3,
                          out_shape=[jax.ShapeDtypeStruct((r, c), F32)] * 3, name=name, compiler_params=_cparams())(
        to2d(w), to2d(g), to2d(m), to2d(v))
    return tuple(o.reshape(shape) for o in outs)


def exchange(name, srcs, group, same):
    p = 2 ** len(group)
    n = len(srcs)
    out_shape = [jax.ShapeDtypeStruct(((p,) + s.shape) if same else s.shape, s.dtype) for s in srcs]

    def index_of(coords):
        idx = 0
        for a in group:
            idx = idx * 2 + coords[a]
        return idx

    def body(*refs):
        src_refs, out_refs = refs[:n], refs[n:2 * n]
        send_sems, recv_sems = refs[2 * n:]
        pos = {a: lax.axis_index(a) for a in MESH_AXES}
        me = index_of(pos)
        peers = []
        for rel in range(1, p):
            coords = dict(pos)
            for bit, a in enumerate(reversed(group)):
                if (rel >> bit) & 1:
                    coords[a] = 1 - coords[a]
            peers.append((coords, index_of(coords)))

        def src_for(a, idx):
            return src_refs[a] if same else src_refs[a].at[idx]

        sends, recvs = [], []
        for a in range(n):
            for r, (coords, idx) in enumerate(peers):
                dev = tuple(coords[ax] for ax in MESH_AXES)
                send = pltpu.make_async_remote_copy(src_ref=src_for(a, idx), dst_ref=out_refs[a].at[me],
                                                    send_sem=send_sems.at[a, r], recv_sem=recv_sems.at[a, r],
                                                    device_id=dev, device_id_type=pl.DeviceIdType.MESH)
                send.start()
                sends.append(send)
                recvs.append(pltpu.make_async_remote_copy(src_ref=src_for(a, idx), dst_ref=out_refs[a].at[idx],
                                                          send_sem=send_sems.at[a, r], recv_sem=recv_sems.at[a, r],
                                                          device_id=dev, device_id_type=pl.DeviceIdType.MESH))
        for cp in sends:
            cp.wait_send()
        for cp in recvs:
            cp.wait_recv()

    any_spec = pl.BlockSpec(memory_space=pl.ANY)
    outs = pl.pallas_call(body, in_specs=[any_spec] * n, out_specs=[any_spec] * n, out_shape=out_shape,
                          scratch_shapes=[pltpu.SemaphoreType.DMA((n, p - 1)), pltpu.SemaphoreType.DMA((n, p - 1))],
                          name=name)(*srcs)
    me = index_of({a: lax.axis_index(a) for a in MESH_AXES})
    own = [s if same else lax.dynamic_index_in_dim(s, me, axis=0, keepdims=False) for s in srcs]
    return [lax.dynamic_update_index_in_dim(o, w, me, axis=0) for o, w in zip(outs, own)]


def _rope_tables(n_lat, n_ctx, rot_dim, heads):
    n_freq = rot_dim // 4
    tok = jnp.arange(n_lat, dtype=jnp.int32)
    inv = ROPE_THETA ** (-jnp.arange(n_freq, dtype=F32) / n_freq)
    ang = jnp.concatenate([(tok // GRID_W).astype(F32)[:, None] * inv, (tok % GRID_W).astype(F32)[:, None] * inv], axis=-1)
    cos, sin = jnp.cos(ang), jnp.sin(ang)
    cos = jnp.concatenate([jnp.concatenate([cos, cos], -1), jnp.ones((n_ctx, rot_dim), F32)], 0)
    sin = jnp.concatenate([jnp.concatenate([-sin, sin], -1), jnp.zeros((n_ctx, rot_dim), F32)], 0)
    half = rot_dim // 2
    w = heads * rot_dim
    j = np.arange(w)
    src = (j // rot_dim) * rot_dim + (j % rot_dim + half) % rot_dim
    swap = np.zeros((w, w), np.float32)
    swap[src, j] = 1.0
    return jnp.tile(cos, (1, heads)), jnp.tile(sin, (1, heads)), jnp.asarray(swap)


def _to_heads(a, nh):
    b, t, _ = a.shape
    return a.reshape(b, t, nh, -1).transpose(0, 2, 1, 3)


def _to_heads_t(a, nh):
    b, t, _ = a.shape
    return a.reshape(b, t, nh, -1).transpose(0, 2, 3, 1)


def _from_heads(a):
    b, nh, t, w = a.shape
    return a.transpose(0, 2, 1, 3).reshape(b, t, nh * w)


def _pad_w_in(w_in):
    parts, off = [], 0
    for size, pad in zip(IN_SIZES, IN_PAD):
        parts.append(w_in[:, off:off + size])
        if pad > size:
            parts.append(jnp.zeros((w_in.shape[0], pad - size), w_in.dtype))
        off += size
    return jnp.concatenate(parts, axis=1)


def _forward(xu, mods, prm, prm_lo, n_lat, n_ctx):
    bsz, t, d = xu.shape
    depth = mods.shape[0]
    n_lat_tiles = n_lat // TOKEN_TILE
    rope_a = _rope_tables(n_lat, n_ctx, HEAD_DIM, 1)
    rope_bq = _rope_tables(n_lat, n_ctx, B_ROPE, B_HEADS)
    rope_bk = _rope_tables(n_lat, n_ctx, B_ROPE, 1)
    clb = prm["c_lower_bounds"].reshape(depth, 2, C_HEADS, C_DK).transpose(2, 0, 1, 3)
    for l in range(depth):
        nm = lambda s: f"l{l}_{s}"
        vec = lambda name: prm[name][l][None, :]
        xu, h = premod(nm("premix"), xu, vec("g_pre_mix"), mods[l], 0, n_lat_tiles)
        z = linear(nm("w_in"), h.reshape(bsz * t, d), _pad_w_in(prm["w_in"][l]), _pad_w_in(prm_lo["w_in"][l])).reshape(bsz, t, D_IN_PAD)
        seg = lambda i: z[:, :, IN_OFF[i]:IN_OFF[i] + IN_SIZES[i]]
        aq = a_prep(nm("aq"), _to_heads(seg(0), A_HEADS), vec("a_q_norm"), *rope_a)
        ak = a_prep(nm("ak"), _to_heads(seg(1), A_KV_HEADS), vec("a_k_norm"), *rope_a)
        av = _to_heads(seg(2), A_KV_HEADS)
        lower = c_lower(nm("c_lower"), clb, l)
        cq, cv = _to_heads(seg(6), C_HEADS), _to_heads(seg(9), C_HEADS)
        ya, o_f = mix_pair(nm("mix_a"), [aq.reshape(bsz, A_KV_HEADS, A_GROUP, t, HEAD_DIM)], [ak.transpose(0, 1, 3, 2)],
                           av.transpose(0, 1, 3, 2), HEAD_DIM ** -0.5, True, ATT_A_TQ_FWD, ATT_A_TQ, n_lat,
                           cq, _to_heads(seg(7), C_HEADS), cv, lower[:, 0:1], False, n_lat // SCAN_CHUNK)
        ya = _from_heads(ya.reshape(bsz, A_HEADS, t, HEAD_DIM))
        wq = prm["w_q_up"][l].reshape(B_Q_RANK, B_HEADS, B_NOPE + B_ROPE)
        wkv = prm["w_kv_up"][l].reshape(B_KV_RANK, B_HEADS, B_NOPE + B_V)
        qn, qp, kn, bv, kp = b_prep(
            nm("b_prep"), seg(3), seg(4), seg(5), vec("b_q_norm"),
            wq[:, :, :B_NOPE].reshape(B_Q_RANK, -1), wq[:, :, B_NOPE:].reshape(B_Q_RANK, -1), vec("b_kv_norm"),
            wkv[:, :, :B_NOPE].reshape(B_KV_RANK, -1), wkv[:, :, B_NOPE:].reshape(B_KV_RANK, -1), *rope_bq, *rope_bk)
        qb = jnp.concatenate([_to_heads(qn, B_HEADS), _to_heads(qp, B_HEADS)], axis=-1)[:, :, None]
        kbt = jnp.concatenate([_to_heads_t(kn, B_HEADS),
                               jnp.broadcast_to(kp.transpose(0, 2, 1)[:, None], (bsz, B_HEADS, B_ROPE, t))], axis=2)
        yb, o_b = mix_pair(nm("mix_b"), [qb], [kbt], _to_heads_t(bv, B_HEADS), (B_NOPE + B_ROPE) ** -0.5, False, ATT_B_TQ,
                           ATT_B_TQ, n_lat, cq, _to_heads(seg(8), C_HEADS), cv, lower[:, 1:2], True, n_lat // SCAN_CHUNK)
        yb = _from_heads(yb[:, :, 0])
        yc = _from_heads(c_readout(nm("c_out"), o_f, o_b, _to_heads(seg(10), C_HEADS), vec("c_out_norm")))
        y = linear(nm("w_out"), jnp.concatenate([ya, yb, yc], axis=-1).reshape(bsz * t, d), prm["w_out"][l],
                   prm_lo["w_out"][l]).reshape(bsz, t, d)
        x1 = resid(nm("res_mix"), xu, y, vec("g_post_mix"), mods[l], 2, n_lat_tiles)
        x1, h2 = premod(nm("preffn"), x1, vec("g_pre_ffn"), mods[l], 3, n_lat_tiles)
        f = mlp(nm("mlp"), h2.reshape(bsz * t, d), prm["w_ff1"][l], prm["w_ff2"][l], prm_lo["w_ff1"][l],
                prm_lo["w_ff2"][l]).reshape(bsz, t, d)
        xu = resid(nm("res_ffn"), x1, f, vec("g_post_ffn"), mods[l], 5, n_lat_tiles)
    return xu


BIG = {"w_in": 2, "w_q_up": 2, "w_kv_up": 2, "w_out": 1, "w_ff1": 2, "w_ff2": 1}
SMALL = ("g_pre_mix", "g_post_mix", "g_pre_ffn", "g_post_ffn", "a_q_norm", "a_k_norm", "b_q_norm", "b_kv_norm",
         "c_lower_bounds", "c_out_norm")
WEIGHTS = ("c_ctx", "w_ada", "b_ada", "g_pre_mix", "g_post_mix", "g_pre_ffn", "g_post_ffn", "w_in", "a_q_norm", "a_k_norm",
           "b_q_norm", "w_q_up", "b_kv_norm", "w_kv_up", "c_lower_bounds", "c_out_norm", "w_out", "w_ff1", "w_ff2")


def _unshard(g, axis):
    depth, _, r, c = g.shape
    if axis == 1:
        return g.reshape(depth, N_CHIP * r, c)
    return g.transpose(0, 2, 1, 3).reshape(depth, r, N_CHIP * c)


def _shard_major(g, axis):
    depth, r, c = g.shape
    if axis == 1:
        return g.reshape(depth, N_CHIP, r // N_CHIP, c)
    return g.reshape(depth, r, N_CHIP, c // N_CHIP).transpose(0, 2, 1, 3)


def kernel(x, c, ctx, c_ctx, w_ada, b_ada, g_pre_mix, g_post_mix, g_pre_ffn, g_post_ffn, w_in, a_q_norm, a_k_norm, b_q_norm, w_q_up, b_kv_norm, w_kv_up, c_lower_bounds, c_out_norm, w_out, w_ff1, w_ff2, loss_target, m_c_ctx, m_w_ada, m_b_ada, m_g_pre_mix, m_g_post_mix, m_g_pre_ffn, m_g_post_ffn, m_w_in, m_a_q_norm, m_a_k_norm, m_b_q_norm, m_w_q_up, m_b_kv_norm, m_w_kv_up, m_c_lower_bounds, m_c_out_norm, m_w_out, m_w_ff1, m_w_ff2, v_c_ctx, v_w_ada, v_b_ada, v_g_pre_mix, v_g_post_mix, v_g_pre_ffn, v_g_post_ffn, v_w_in, v_a_q_norm, v_a_k_norm, v_b_q_norm, v_w_q_up, v_b_kv_norm, v_w_kv_up, v_c_lower_bounds, v_c_out_norm, v_w_out, v_w_ff1, v_w_ff2):
    local = dict(c_ctx=c_ctx, w_ada=w_ada, b_ada=b_ada, g_pre_mix=g_pre_mix, g_post_mix=g_post_mix, g_pre_ffn=g_pre_ffn,
                 g_post_ffn=g_post_ffn, w_in=w_in, a_q_norm=a_q_norm, a_k_norm=a_k_norm, b_q_norm=b_q_norm, w_q_up=w_q_up,
                 b_kv_norm=b_kv_norm, w_kv_up=w_kv_up, c_lower_bounds=c_lower_bounds, c_out_norm=c_out_norm, w_out=w_out,
                 w_ff1=w_ff1, w_ff2=w_ff2)
    mom = dict(c_ctx=m_c_ctx, w_ada=m_w_ada, b_ada=m_b_ada, g_pre_mix=m_g_pre_mix, g_post_mix=m_g_post_mix,
               g_pre_ffn=m_g_pre_ffn, g_post_ffn=m_g_post_ffn, w_in=m_w_in, a_q_norm=m_a_q_norm, a_k_norm=m_a_k_norm,
               b_q_norm=m_b_q_norm, w_q_up=m_w_q_up, b_kv_norm=m_b_kv_norm, w_kv_up=m_w_kv_up,
               c_lower_bounds=m_c_lower_bounds, c_out_norm=m_c_out_norm, w_out=m_w_out, w_ff1=m_w_ff1, w_ff2=m_w_ff2)
    var = dict(c_ctx=v_c_ctx, w_ada=v_w_ada, b_ada=v_b_ada, g_pre_mix=v_g_pre_mix, g_post_mix=v_g_post_mix,
               g_pre_ffn=v_g_pre_ffn, g_post_ffn=v_g_post_ffn, w_in=v_w_in, a_q_norm=v_a_q_norm, a_k_norm=v_a_k_norm,
               b_q_norm=v_b_q_norm, w_q_up=v_w_q_up, b_kv_norm=v_b_kv_norm, w_kv_up=v_w_kv_up,
               c_lower_bounds=v_c_lower_bounds, c_out_norm=v_c_out_norm, w_out=v_w_out, w_ff1=v_w_ff1, w_ff2=v_w_ff2)

    bsz, n_lat, d = x.shape
    n_ctx = ctx.shape[1]
    depth = w_ada.shape[0]
    assert depth == 2 and n_lat % TOKEN_TILE == 0 and n_ctx % TOKEN_TILE == 0 and bsz * N_DEV + 1 <= ADA_ROWS
    ax, ay, ac = (lax.axis_index(a) for a in MESH_AXES)
    chip = 2 * ax + ay
    dev = 2 * chip + ac

    c_all = exchange("gather_c", [c], MESH_AXES, True)[0].reshape(N_DEV * bsz, d)
    big_names = list(BIG)
    mine = [lax.dynamic_index_in_dim(local[n].astype(BF16), ac, axis=0, keepdims=False) for n in big_names]
    over_chips = exchange("gather_w_quad", mine + [c_lower_bounds], ("x", "y"), True)
    both_layers = exchange("gather_w_pair", over_chips[:-1], ("c",), True)
    prm_lo = {n: _unshard(g, BIG[n]) for n, g in zip(big_names, both_layers)}
    prm = {n: w.astype(F32) for n, w in prm_lo.items()}
    prm["c_lower_bounds"] = over_chips[-1].transpose(1, 2, 0, 3).reshape(depth, 2, -1)
    for n in SMALL:
        if n != "c_lower_bounds":
            prm[n] = local[n]

    n_ada = w_ada.shape[2]
    cc = jnp.concatenate([c_all, c_ctx[None, :], jnp.zeros((ADA_ROWS - N_DEV * bsz - 1, d), F32)], axis=0)
    b_blk = lax.dynamic_slice_in_dim(b_ada, chip * n_ada, n_ada, axis=1)[:, None, :]
    mod_part, ada_vjp = jax.vjp(ada_op, cc, w_ada, b_blk)
    mod_full = exchange("gather_mod", [mod_part], ("x", "y"), True)[0].transpose(1, 2, 0, 3).reshape(depth, ADA_ROWS, 6 * d)
    mod_lat = lax.dynamic_slice_in_dim(mod_full, dev * bsz, bsz, axis=1)
    mod_ctx = jnp.broadcast_to(mod_full[:, N_DEV * bsz:N_DEV * bsz + 1], mod_lat.shape)
    mods = jnp.stack([mod_lat, mod_ctx], axis=2).reshape(depth, bsz, 2, 6, d)

    xu = jnp.concatenate([x, ctx], axis=1)
    x_out, fwd_vjp = jax.vjp(lambda xu_, mods_, prm_: _forward(xu_, mods_, prm_, prm_lo, n_lat, n_ctx), xu, mods, prm)
    dx_out, loss_blk = loss_op(x_out, loss_target, n_lat // TOKEN_TILE)
    d_xu, d_mods, d_prm = fwd_vjp(dx_out)
    grad_x = d_xu[:, :n_lat]

    d_mods = d_mods.reshape(depth, bsz, 2, 6 * d)
    pieces = [d_mods] + [d_prm[n] for n in SMALL] + [loss_blk[0:1, 0:1]]
    sizes = [int(np.prod(p.shape)) for p in pieces]
    flat = jnp.concatenate([p.reshape(-1) for p in pieces])
    n_flat = -(-flat.shape[0] // (SUBLANE * LANE)) * SUBLANE * LANE
    flat = jnp.concatenate([flat, jnp.zeros((n_flat - flat.shape[0],), F32)]).reshape(-1, LANE)
    small_all = exchange("gather_small", [flat], MESH_AXES, True)[0]
    small_sum = sum_parts("sum_small", small_all).reshape(-1)
    offs = np.cumsum([0] + sizes)
    summed = {n: small_sum[offs[i + 1]:offs[i + 2]].reshape(d_prm[n].shape) for i, n in enumerate(SMALL)}
    loss = small_sum[offs[-2]]
    dm_all = small_all.reshape(N_DEV, -1)[:, :sizes[0]].reshape(N_DEV, depth, bsz, 2, 6 * d)
    dm_rows = dm_all.transpose(3, 0, 2, 1, 4).reshape(2, N_DEV * bsz, depth, 6 * d)
    d_ctx_row = sum_parts("sum_dmod_ctx", dm_rows[1])
    grad_b_ada = sum_parts("sum_b_ada", dm_rows.reshape(2 * N_DEV * bsz, depth, 6 * d))
    d_rows = jnp.concatenate([dm_rows[0].transpose(1, 0, 2), d_ctx_row[:, None, :],
                              jnp.zeros((depth, ADA_ROWS - N_DEV * bsz - 1, 6 * d), F32)], axis=1)
    d_cc, grad_w_ada, _ = ada_vjp(lax.dynamic_slice_in_dim(d_rows, chip * n_ada, n_ada, axis=2))
    d_cctx_all = exchange("gather_dcctx", [d_cc[N_DEV * bsz:N_DEV * bsz + 1]], MESH_AXES, True)[0]
    grad_c_ctx = sum_parts("sum_dcctx", d_cctx_all[0::2]).reshape(d)

    shard_major = [_shard_major(d_prm[n], BIG[n]).astype(BF16) for n in big_names]
    pair = exchange("rs_pair", shard_major, ("c",), False)
    chip_sum = [sum_parts(f"rs_sum1_{n}", p) for n, p in zip(big_names, pair)]
    quad = exchange("rs_quad", [s.astype(BF16) for s in chip_sum], ("x", "y"), False)
    total = [sum_parts(f"rs_sum2_{n}", q) for n, q in zip(big_names, quad)]
    both = exchange("rs_share", total, ("c",), True)

    grads = dict(summed)
    grads["c_lower_bounds"] = lax.dynamic_slice_in_dim(summed["c_lower_bounds"], chip * c_lower_bounds.shape[2],
                                                       c_lower_bounds.shape[2], axis=2)
    grads.update(c_ctx=grad_c_ctx, w_ada=grad_w_ada, b_ada=grad_b_ada)
    grads.update({n: g for n, g in zip(big_names, both)})

    deltas, new_m, new_v = {}, {}, {}
    for n in WEIGHTS:
        as2d = (lambda a: a[None, :]) if local[n].ndim == 1 else (lambda a: a)
        dl, nm_, nv_ = adamw("adamw_" + n, as2d(local[n]), as2d(grads[n]), as2d(mom[n]), as2d(var[n]))
        deltas[n], new_m[n], new_v[n] = (a.reshape(local[n].shape) for a in (dl, nm_, nv_))
    return (loss, grad_x, *[grads[n] for n in WEIGHTS], *[deltas[n] for n in WEIGHTS],
            *[new_m[n] for n in WEIGHTS], *[new_v[n] for n in WEIGHTS])
```

```python
import functools
from typing import Any, Callable, NamedTuple

import numpy as np
import jax
import jax.numpy as jnp
from jax import lax
from jax.experimental import pallas as pl
from jax.experimental.pallas import tpu as pltpu

F32 = jnp.float32
BF16 = jnp.bfloat16
HIGHEST = lax.Precision.HIGHEST

GRID_W = 64
HEAD_DIM = 64
A_HEADS, A_KV_HEADS = 8, 2
A_GROUP = A_HEADS // A_KV_HEADS
B_HEADS, B_Q_RANK, B_KV_RANK, B_NOPE, B_ROPE, B_V = 4, 192, 128, 64, 32, 64
C_HEADS, C_DK, C_DV = 4, 64, 64
SCAN_CHUNK = 64
SCAN_SUB = 16
ROPE_THETA = 10000.0
EPS = 1e-6
F_TINY = 1e-30
EXP_ARG_MAX = 80.0
ADAM_LR, ADAM_B1, ADAM_B2, ADAM_EPS, ADAM_WD, ADAM_STEP = 0.001, 0.9, 0.999, 1e-08, 0.01, 10

IN_SIZES = (512, 128, 128, 192, 128, 32, 256, 256, 256, 256, 256)
IN_PAD = (512, 128, 128, 256, 128, 128, 256, 256, 256, 256, 256)
IN_OFF = tuple(int(v) for v in np.cumsum((0,) + IN_PAD)[:-1])
D_IN_PAD = int(sum(IN_PAD))

LANE = 128
SUBLANE = 8
TOKEN_TILE = 256
ATT_A_TQ = 64
ATT_A_TQ_FWD = 128
ATT_B_TQ = 256
ATT_FWD_ROW_GROUPS = 4
MM_ROWS = 256
LINEAR_ROWS = 512
MM_TK_TOKENS = 512
MM_TN_OUT = 2560 * 1024
VMEM_LIMIT = 56 * 1024 * 1024
MESH_AXES = ("x", "y", "c")
N_DEV = 8
N_CHIP = 4
ADA_ROWS = 24


class Arg(NamedTuple):
    arr: Any
    block: tuple
    imap: Callable
    kind: str
    first: Callable = None


class Out(NamedTuple):
    shape: tuple
    block: tuple
    imap: Callable


def _cparams():
    return pltpu.CompilerParams(vmem_limit_bytes=VMEM_LIMIT)


def tile_op(name, fn, grid, args, outs, residual=None, passthrough=False):
    n_in, n_out = len(args), len(outs)
    in_specs = [pl.BlockSpec(a.block, a.imap) for a in args]
    out_specs = [pl.BlockSpec(o.block, o.imap) for o in outs]
    out_shape = [jax.ShapeDtypeStruct(o.shape, F32) for o in outs]
    diff = [i for i, a in enumerate(args) if a.kind != "const"]

    def fwd_call(*arrays):
        def body(*refs):
            ids = tuple(pl.program_id(i) for i in range(len(grid)))
            res = list(fn(ids, *[r[...] for r in refs[:n_in]]))
            out_refs = refs[n_in:]
            if residual is not None:
                res[0] = res[0] + refs[n_in][...]
                out_refs = refs[n_in + 1:]
            for r, o in zip(out_refs, res):
                r[...] = o

        specs = in_specs + ([out_specs[0]] if residual is not None else [])
        return pl.pallas_call(body, grid=grid, in_specs=specs, out_specs=out_specs, out_shape=out_shape,
                              name=name + "_fwd", compiler_params=_cparams())(*arrays)

    def bwd_call(arrays, cts):
        def body(*refs):
            ids = tuple(pl.program_id(i) for i in range(len(grid)))
            vals = [r[...] for r in refs[:n_in]]
            ct = tuple(r[...] for r in refs[n_in:n_in + n_out])
            drefs = refs[n_in + n_out:]

            def g(*dv):
                full = list(vals)
                for i, v in zip(diff, dv):
                    full[i] = v
                return tuple(fn(ids, *full))

            _, vjp = jax.vjp(g, *[vals[i] for i in diff])
            ds = list(vjp(ct))
            if passthrough:
                ds[0] = ds[0] + refs[n_in + n_out][...]
                drefs = refs[n_in + n_out + 1:]
            for i, d, r in zip(diff, ds, drefs):
                if args[i].kind == "tile":
                    r[...] = d
                else:
                    is_first = args[i].first(ids)

                    @pl.when(is_first)
                    def _(r=r, d=d):
                        r[...] = d

                    @pl.when(jnp.logical_not(is_first))
                    def _(r=r, d=d):
                        r[...] += d

        d_specs = [in_specs[i] for i in diff]
        d_shape = [jax.ShapeDtypeStruct(arrays[i].shape, F32) for i in diff]
        specs = in_specs + out_specs + ([in_specs[0]] if passthrough else [])
        return pl.pallas_call(body, grid=grid, in_specs=specs, out_specs=d_specs, out_shape=d_shape,
                              name=name + "_bwd", compiler_params=_cparams())(*arrays, *cts)

    assert not passthrough or (diff and diff[0] == 0 and args[0].kind == "tile")

    def results(arrays):
        res = tuple(fwd_call(*arrays))
        return ((arrays[0],) + res) if passthrough else res

    @jax.custom_vjp
    def op(*arrays):
        return results(arrays)

    def op_fwd(*arrays):
        return results(arrays), arrays[:n_in]

    def op_bwd(arrays, cts):
        if passthrough:
            cts = tuple(cts[1:]) + (cts[0],)
        ds = bwd_call(arrays, cts)
        res, k = [], 0
        for i, a in enumerate(args):
            if a.kind == "const":
                res.append(jnp.zeros_like(arrays[i]))
            else:
                res.append(ds[k])
                k += 1
        if residual is not None:
            res.append(cts[0])
        return tuple(res)

    op.defvjp(op_fwd, op_bwd)
    return op(*[a.arr for a in args], *([residual] if residual is not None else []))


def _pick(n, cap):
    if n <= cap:
        return n
    best = None
    for t in range(LANE, cap + 1, LANE):
        if n % t == 0:
            best = t
    assert best is not None, (n, cap)
    return best


_NN = (((1,), (0,)), ((), ()))
_NT = (((1,), (1,)), ((), ()))
_TN = (((0,), (0,)), ((), ()))


def _resident(shape):
    return pl.BlockSpec(shape, lambda *ids: (0,) * len(shape), pipeline_mode=pl.Buffered(1))


def _mm_rows(name, a, w, transposed):
    m, k = a.shape
    n = w.shape[0] if transposed else w.shape[1]
    tm = LINEAR_ROWS
    dims = _NT if transposed else _NN

    def body(a_ref, w_ref, o_ref):
        o_ref[...] = lax.dot_general(a_ref[...].astype(BF16), w_ref[...], dims, preferred_element_type=F32)

    return pl.pallas_call(body, grid=(m // tm,), in_specs=[pl.BlockSpec((tm, k), lambda i: (i, 0)), _resident(w.shape)],
                          out_specs=pl.BlockSpec((tm, n), lambda i: (i, 0)), out_shape=jax.ShapeDtypeStruct((m, n), F32),
                          name=name, compiler_params=_cparams())(a, w)


def _mm_tn(name, a, g):
    t, k = a.shape
    n = g.shape[1]
    tko, tno = k, n
    while tko * tno > MM_TN_OUT:
        if tko >= tno:
            tko //= 2
        else:
            tno //= 2
    assert k % tko == 0 and n % tno == 0 and tko % LANE == 0 and tno % LANE == 0
    tt = _pick(t, MM_TK_TOKENS)

    def body(a_ref, g_ref, o_ref):
        p = lax.dot_general(a_ref[...].astype(BF16), g_ref[...].astype(BF16), _TN, preferred_element_type=F32)
        kk = pl.program_id(2)

        @pl.when(kk == 0)
        def _():
            o_ref[...] = p

        @pl.when(kk != 0)
        def _():
            o_ref[...] += p

    return pl.pallas_call(body, grid=(k // tko, n // tno, t // tt),
                          in_specs=[pl.BlockSpec((tt, tko), lambda i, j, kk: (kk, i)), pl.BlockSpec((tt, tno), lambda i, j, kk: (kk, j))],
                          out_specs=pl.BlockSpec((tko, tno), lambda i, j, kk: (i, j)),
                          out_shape=jax.ShapeDtypeStruct((k, n), F32), name=name, compiler_params=_cparams())(a, g)


def linear(name, a, w, w_lo):
    @jax.custom_vjp
    def op(a, w, w_lo):
        return _mm_rows(name + "_fwd", a, w_lo, False)

    def op_fwd(a, w, w_lo):
        return _mm_rows(name + "_fwd", a, w_lo, False), (a, w_lo)

    def op_bwd(res, g):
        a, w_lo = res
        return _mm_rows(name + "_da", g, w_lo, True), _mm_tn(name + "_dw", a, g), jnp.zeros_like(w_lo)

    op.defvjp(op_fwd, op_bwd)
    return op(a, w, w_lo)


def mlp(name, h, w1, w2, w1_lo, w2_lo):
    m, d = h.shape
    f = w1_lo.shape[1]
    tm = MM_ROWS
    row = pl.BlockSpec((tm, d), lambda i: (i, 0))
    wide = pl.BlockSpec((tm, f), lambda i: (i, 0))

    def fwd_call(h, w1_lo, w2_lo):
        def body(h_ref, w1_ref, w2_ref, o_ref):
            u = lax.dot_general(h_ref[...].astype(BF16), w1_ref[...], _NN, preferred_element_type=F32)
            act = jnp.square(jnp.maximum(u, 0.0))
            o_ref[...] = lax.dot_general(act.astype(BF16), w2_ref[...], _NN, preferred_element_type=F32)

        return pl.pallas_call(body, grid=(m // tm,), in_specs=[row, _resident(w1_lo.shape), _resident(w2_lo.shape)],
                              out_specs=row, out_shape=jax.ShapeDtypeStruct((m, d), F32), name=name + "_fwd",
                              compiler_params=_cparams())(h, w1_lo, w2_lo)

    def bwd_call(h, w1_lo, w2_lo, dy):
        def body(h_ref, dy_ref, w1_ref, w2_ref, dh_ref, act_ref, du_ref):
            u = lax.dot_general(h_ref[...].astype(BF16), w1_ref[...], _NN, preferred_element_type=F32)
            r = jnp.maximum(u, 0.0)
            act_ref[...] = (r * r).astype(BF16)
            dact = lax.dot_general(dy_ref[...].astype(BF16), w2_ref[...], _NT, preferred_element_type=F32)
            du = (dact * (2.0 * r)).astype(BF16)
            du_ref[...] = du
            dh_ref[...] = lax.dot_general(du, w1_ref[...], _NT, preferred_element_type=F32)

        return pl.pallas_call(body, grid=(m // tm,), in_specs=[row, row, _resident(w1_lo.shape), _resident(w2_lo.shape)],
                              out_specs=[row, wide, wide],
                              out_shape=[jax.ShapeDtypeStruct((m, d), F32), jax.ShapeDtypeStruct((m, f), BF16),
                                         jax.ShapeDtypeStruct((m, f), BF16)],
                              name=name + "_bwd", compiler_params=_cparams())(h, dy, w1_lo, w2_lo)

    @jax.custom_vjp
    def op(h, w1, w2, w1_lo, w2_lo):
        return fwd_call(h, w1_lo, w2_lo)

    def op_fwd(h, w1, w2, w1_lo, w2_lo):
        return fwd_call(h, w1_lo, w2_lo), (h, w1_lo, w2_lo)

    def op_bwd(res, dy):
        h, w1_lo, w2_lo = res
        dh, act, du = bwd_call(h, w1_lo, w2_lo, dy)
        return (dh, _mm_tn(name + "_dw1", h, du), _mm_tn(name + "_dw2", act, dy), jnp.zeros_like(w1_lo), jnp.zeros_like(w2_lo))

    op.defvjp(op_fwd, op_bwd)
    return op(h, w1, w2, w1_lo, w2_lo)


def _rms(x, g):
    return x * lax.rsqrt(jnp.mean(x * x, axis=-1, keepdims=True) + EPS) * g


def _sigmoid(z):
    return 1.0 / (1.0 + jnp.exp(jnp.minimum(-z, EXP_ARG_MAX)))


def _silu(z):
    return z * _sigmoid(z)


def _rope(y, cos, sin_signed, swap):
    return y * cos + jnp.dot(y, swap, precision=HIGHEST, preferred_element_type=F32) * sin_signed


def _dot_bf16(a, b, dims=((1,), (0,))):
    return lax.dot_general(a.astype(BF16), b.astype(BF16), (dims, ((), ())), preferred_element_type=F32)


def _gla_step(state, q, k, v, g, reverse):
    c, d = q.shape
    sub = SCAN_SUB
    nb = c // sub
    row = lax.broadcasted_iota(jnp.int32, (c, c), 0)
    col = lax.broadcasted_iota(jnp.int32, (c, c), 1)
    tri = (row <= col) if reverse else (row >= col)
    b = jnp.dot(tri.astype(F32), g, precision=HIGHEST, preferred_element_type=F32)
    o = jnp.dot(q * jnp.exp(b), state, preferred_element_type=F32)
    rs = lax.broadcasted_iota(jnp.int32, (sub, sub), 0)
    cs = lax.broadcasted_iota(jnp.int32, (sub, sub), 1)
    tri_s = ((rs <= cs) if reverse else (rs >= cs)).astype(F32)
    rowc = lax.broadcasted_iota(jnp.int32, (c, 1), 0)
    nonpos = lambda x: jnp.where(x > 0.0, 0.0, x)
    diag = []
    for j in range(nb):
        sl = slice(j * sub, (j + 1) * sub)
        bj, kj, vj, qj = b[sl], k[sl], v[sl], q[sl]
        dec = jnp.exp(nonpos(bj[:, None, :] - bj[None, :, :]))
        sc = jnp.sum(qj[:, None, :] * kj[None, :, :] * dec, axis=-1) * tri_s
        diag.append(jnp.dot(sc, vj, preferred_element_type=F32))
        if (j > 0) if reverse else (j < nb - 1):
            ref = bj[0:1] if reverse else bj[sub - 1:sub]
            qa = q * jnp.exp(nonpos(b - ref))
            ks = kj * jnp.exp(ref - bj)
            scj = lax.dot_general(qa, ks, (((1,), (1,)), ((), ())), precision=HIGHEST, preferred_element_type=F32)
            later = (rowc < j * sub) if reverse else (rowc >= (j + 1) * sub)
            o = o + jnp.dot(jnp.where(later, scj, 0.0), vj, preferred_element_type=F32)
    o = o + jnp.concatenate(diag, axis=0)
    b_end = b[0:1, :] if reverse else b[c - 1:c, :]
    kd = k * jnp.exp(b_end - b)
    new_state = state * jnp.exp(b_end).reshape(d, 1) + lax.dot_general(kd, v, (((0,), (0,)), ((), ())), preferred_element_type=F32)
    return new_state, o


def _zero_ids(ids):
    z = ids[0] == 0
    for i in ids[1:]:
        z = jnp.logical_and(z, i == 0)
    return z


def _token_grid(x, n_lat_tiles):
    bsz, t, d = x.shape
    nt = t // TOKEN_TILE
    row = lambda w: ((None, TOKEN_TILE, w), lambda b, i: (b, i, 0))
    mod_block = (None, None, 6, d)
    mod_imap = lambda b, i: (b, (i >= n_lat_tiles).astype(jnp.int32), 0, 0)
    mod_first = lambda ids: jnp.logical_or(ids[1] == 0, ids[1] == n_lat_tiles)
    return bsz, t, d, nt, row, (mod_block, mod_imap, mod_first)


def premod(name, x, gain, mods, r0, n_lat_tiles):
    bsz, t, d, nt, row, (mb, mi, mf) = _token_grid(x, n_lat_tiles)

    def fn(ids, xb, gb, mod):
        return (_rms(xb, gb) * (1.0 + mod[r0 + 1:r0 + 2]) + mod[r0:r0 + 1],)

    args = [Arg(x, *row(d), "tile"), Arg(gain, (1, d), lambda b, i: (0, 0), "acc", _zero_ids), Arg(mods, mb, mi, "acc", mf)]
    return tile_op(name, fn, (bsz, nt), args, [Out(x.shape, *row(d))], passthrough=True)


def resid(name, x, y, gain, mods, r, n_lat_tiles):
    bsz, t, d, nt, row, (mb, mi, mf) = _token_grid(x, n_lat_tiles)

    def fn(ids, yb, gb, mod):
        return (mod[r:r + 1] * _rms(yb, gb),)

    args = [Arg(y, *row(d), "tile"), Arg(gain, (1, d), lambda b, i: (0, 0), "acc", _zero_ids), Arg(mods, mb, mi, "acc", mf)]
    return tile_op(name, fn, (bsz, nt), args, [Out(x.shape, *row(d))], residual=x)[0]


def _heads_spec(nh, w):
    return (None, nh, TOKEN_TILE, w), lambda b, i: (b, 0, i, 0)


def a_prep(name, x, gain, cos, sin, swap):
    bsz, nh, t, d = x.shape
    tab = ((TOKEN_TILE, d), lambda b, i: (i, 0))

    def fn(ids, xb, gb, cb, sb, pb):
        y = _rms(xb, gb)
        swapped = jnp.dot(y.reshape(nh * TOKEN_TILE, d), pb, precision=HIGHEST, preferred_element_type=F32)
        return (y * cb + swapped.reshape(nh, TOKEN_TILE, d) * sb,)

    args = [Arg(x, *_heads_spec(nh, d), "tile"), Arg(gain, (1, d), lambda b, i: (0, 0), "acc", _zero_ids),
            Arg(cos, *tab, "const"), Arg(sin, *tab, "const"), Arg(swap, (d, d), lambda b, i: (0, 0), "const")]
    return tile_op(name, fn, (bsz, t // TOKEN_TILE), args, [Out(x.shape, *_heads_spec(nh, d))])[0]


def b_prep(name, bqd, bkvd, bkr, bqn, wq_nope, wq_pe, bkvn, wkv_nope, wkv_v, cos_q, sin_q, swap_q, cos_k, sin_k, swap_k):
    bsz, t, _ = bqd.shape
    row = lambda w: ((None, TOKEN_TILE, w), lambda b, i: (b, i, 0))
    whole = lambda a: (a.shape, lambda b, i: (0,) * a.ndim)
    tab = lambda w: ((TOKEN_TILE, w), lambda b, i: (i, 0))

    def fn(ids, qd, kvd, kr, qn, wqn, wqp, kvn, wkn, wkv, cq, sq, pq, ck, sk, pk):
        hq = _rms(qd, qn)
        hkv = _rms(kvd, kvn)
        return (_dot_bf16(hq, wqn), _rope(_dot_bf16(hq, wqp), cq, sq, pq), _dot_bf16(hkv, wkn), _dot_bf16(hkv, wkv),
                _rope(kr, ck, sk, pk))

    params = [bqn, wq_nope, wq_pe, bkvn, wkv_nope, wkv_v]
    args = [Arg(bqd, *row(B_Q_RANK), "tile"), Arg(bkvd, *row(B_KV_RANK), "tile"), Arg(bkr, *row(B_ROPE), "tile")]
    args += [Arg(p, *whole(p), "acc", _zero_ids) for p in params]
    args += [Arg(cos_q, *tab(cos_q.shape[1]), "const"), Arg(sin_q, *tab(cos_q.shape[1]), "const"), Arg(swap_q, *whole(swap_q), "const"),
             Arg(cos_k, *tab(B_ROPE), "const"), Arg(sin_k, *tab(B_ROPE), "const"), Arg(swap_k, *whole(swap_k), "const")]
    widths = (B_HEADS * B_NOPE, B_HEADS * B_ROPE, B_HEADS * B_NOPE, B_HEADS * B_V, B_ROPE)
    outs = [Out((bsz, t, w), *row(w)) for w in widths]
    return tile_op(name, fn, (bsz, t // TOKEN_TILE), args, outs)


def c_lower(name, clb, layer):
    nh, depth, _, d = clb.shape

    def fn(ids, lbs):
        lb = [lbs[:, j] for j in range(depth)]
        m = lb[0]
        for j in range(1, depth):
            m = jnp.maximum(m, lb[j])
        e = [jnp.exp(lb[j] - m) for j in range(depth)]
        tot = e[0]
        for j in range(1, depth):
            tot = tot + e[j]
        p = [ej / tot for ej in e]
        cum = p[0]
        for j in range(1, layer + 1):
            cum = cum + p[j]
        return (cum - p[0],)

    whole = lambda shape: (shape, lambda i: (0,) * len(shape))
    return tile_op(name, fn, (1,), [Arg(clb, *whole(clb.shape), "tile")], [Out((nh, 2, d), *whole((nh, 2, d)))])[0]


def c_readout(name, o_f, o_b, gate, gain):
    bsz, nh, t, d = o_f.shape
    spec = _heads_spec(nh, d)

    def fn(ids, of, ob, gt, gn):
        return (_rms(of + ob, gn) * _silu(gt),)

    args = [Arg(o_f, *spec, "tile"), Arg(o_b, *spec, "tile"), Arg(gate, *spec, "tile"),
            Arg(gain, (1, d), lambda b, i: (0, 0), "acc", _zero_ids)]
    return tile_op(name, fn, (bsz, t // TOKEN_TILE), args, [Out(o_f.shape, *spec)])[0]


def _att_rows_fwd(qq, kts_lo, vt_lo, scale, fold_scale, n_grp):
    rows = qq[0].shape[0]
    gr = rows // n_grp
    outs, lses = [], []
    for r in range(n_grp):
        s = None
        for q2, kt in zip(qq, kts_lo):
            part = lax.dot_general(q2[r * gr:(r + 1) * gr], kt, _NN, preferred_element_type=F32)
            s = part if s is None else s + part
        if not fold_scale:
            s = s * scale
        m = jnp.max(s, axis=-1, keepdims=True)
        e = jnp.exp(s - m)
        l = jnp.sum(e, axis=-1, keepdims=True)
        outs.append(lax.dot_general(e.astype(BF16), vt_lo, _NT, preferred_element_type=F32) * (1.0 / l))
        lses.append(m + jnp.log(l))
    return jnp.concatenate(outs, axis=0), jnp.concatenate(lses, axis=0)


def _att_rows_bwd(qq, kts_lo, vt_lo, o2, lse, do2, scale, fold_scale):
    s = None
    for q2, kt in zip(qq, kts_lo):
        part = lax.dot_general(q2, kt, _NN, preferred_element_type=F32)
        s = part if s is None else s + part
    if not fold_scale:
        s = s * scale
    p = jnp.exp(s - lse)
    delta = jnp.sum(do2 * o2, axis=-1, keepdims=True)
    do_lo = do2.astype(BF16)
    ds = p * (lax.dot_general(do_lo, vt_lo, _NN, preferred_element_type=F32) - delta)
    if not fold_scale:
        ds = ds * scale
    ds_lo = ds.astype(BF16)
    dvt = lax.dot_general(do_lo, p.astype(BF16), _TN, preferred_element_type=F32)
    dqs, dkts = [], []
    for q2, kt in zip(qq, kts_lo):
        dq = lax.dot_general(ds_lo, kt, _NT, preferred_element_type=F32)
        dqs.append(dq * scale if fold_scale else dq)
        dkts.append(lax.dot_general(q2, ds_lo, _TN, preferred_element_type=F32))
    return dqs, dkts, dvt


def mix_pair(name, qs, kts, vt, scale, fold_scale, sub_fwd, sub_bwd, n_lat, sq, sz, sv, lb, reverse, n_lat_chunks):
    bsz, hk, grp, t, _ = qs[0].shape
    dv = vt.shape[2]
    n_parts = len(qs)
    nh, d = sq.shape[1], sq.shape[3]
    c = SCAN_CHUNK
    n = t // c
    tph = n // hk
    tq = t // tph
    assert tph * hk == n and tq * tph == t and tq % sub_fwd == 0 and tq % sub_bwd == 0 and n_lat % tq == 0
    chains = [(b, h) for b in range(bsz) for h in range(nh)]
    widths = [q.shape[-1] for q in qs]

    def chunk_of(j):
        return (n - 1 - j) if reverse else lax.rem(j + n_lat_chunks, n)

    def chain(state, q_raw, z, v, lo_b):
        f = lo_b + (1.0 - lo_b) * _sigmoid(z)
        return _gla_step(state, _silu(q_raw), (1.0 - lo_b) * _sigmoid(-z), v, jnp.log(jnp.maximum(f, F_TINY)), reverse)

    lb_spec = pl.BlockSpec(lb.shape, lambda j: (0, 0, 0))
    head_of = lambda j: j // tph
    tile_of = lambda j: lax.rem(j, tph)
    per_tile = lambda w: pl.BlockSpec((bsz, None, grp, tq, w), lambda j: (0, head_of(j), 0, tile_of(j), 0))
    per_head = lambda w: pl.BlockSpec((bsz, None, w, t), lambda j: (0, head_of(j), 0, 0))
    q_specs = [per_tile(w) for w in widths]
    kt_specs = [per_head(w) for w in widths]
    att_in = q_specs + kt_specs + [per_head(dv)]
    o_shape = jax.ShapeDtypeStruct((bsz, hk, grp, t, dv), F32)
    lse_shape = jax.ShapeDtypeStruct((bsz, hk, grp, t, LANE), F32)
    st_shape = jax.ShapeDtypeStruct((bsz, nh, n, d, d), F32)
    ranges = ((lambda j: tile_of(j) * tq < n_lat, 0), (lambda j: tile_of(j) * tq >= n_lat, n_lat))

    def load_q(q_refs, b, r0, sub):
        qq = []
        for q_ref, w in zip(q_refs, widths):
            q2 = q_ref[b, :, r0:r0 + sub, :].reshape(grp * sub, w)
            qq.append((q2 * scale if fold_scale else q2).astype(BF16))
        return qq

    def fwd_call(*arrays):
        def body(*refs):
            q_refs, kt_refs, vt_ref = refs[:n_parts], refs[n_parts:2 * n_parts], refs[2 * n_parts]
            sq_ref, sz_ref, sv_ref, lb_ref = refs[2 * n_parts + 1:2 * n_parts + 5]
            o_ref, lse_ref, so_ref, states_ref, st = refs[2 * n_parts + 5:]
            j = pl.program_id(0)

            @pl.when(j == 0)
            def _():
                st[...] = jnp.zeros_like(st)

            def step(col0):
                for bh in chains:
                    s = st[bh]
                    states_ref[bh] = s
                    ns, o = chain(s, sq_ref[bh], sz_ref[bh], sv_ref[bh], lb_ref[bh[1]])
                    st[bh] = ns
                    so_ref[bh] = o
                for b in range(bsz):
                    kts_lo = [r[b, :, col0:] for r in kt_refs]
                    vt_lo = vt_ref[b, :, col0:]
                    for r0 in range(0, tq, sub_fwd):
                        o, lse = _att_rows_fwd(load_q(q_refs, b, r0, sub_fwd), kts_lo, vt_lo, scale, fold_scale, ATT_FWD_ROW_GROUPS)
                        o_ref[b, :, r0:r0 + sub_fwd, :] = o.reshape(grp, sub_fwd, dv)
                        lse_ref[b, :, r0:r0 + sub_fwd, :] = jnp.broadcast_to(lse, (grp * sub_fwd, LANE)).reshape(grp, sub_fwd, LANE)

            for cond, col0 in ranges:
                pl.when(cond(j))(functools.partial(step, col0))

        scan_spec = pl.BlockSpec((bsz, nh, c, d), lambda j: (0, 0, chunk_of(j), 0))
        return pl.pallas_call(
            body, grid=(n,), in_specs=att_in + [scan_spec] * 3 + [lb_spec],
            out_specs=[per_tile(dv), per_tile(LANE), scan_spec, pl.BlockSpec((bsz, nh, None, d, d), lambda j: (0, 0, j, 0, 0))],
            out_shape=[o_shape, lse_shape, jax.ShapeDtypeStruct(sq.shape, F32), st_shape],
            scratch_shapes=[pltpu.VMEM((bsz, nh, d, d), F32)], name=name + "_fwd", compiler_params=_cparams())(*arrays)

    def bwd_call(att_lo, scan_in, o, lse, states, do, dso):
        def body(*refs):
            q_refs, kt_refs, vt_ref = refs[:n_parts], refs[n_parts:2 * n_parts], refs[2 * n_parts]
            k0 = 2 * n_parts + 1
            sq_ref, sz_ref, sv_ref, lb_ref, s_ref = refs[k0:k0 + 5]
            o_ref, lse_ref, do_ref, dso_ref = refs[k0 + 5:k0 + 9]
            d_refs = refs[k0 + 9:]
            dq_refs, dkt_refs, dvt_ref = d_refs[:n_parts], d_refs[n_parts:2 * n_parts], d_refs[2 * n_parts]
            dsq_ref, dsz_ref, dsv_ref, dlb_ref, dst = d_refs[2 * n_parts + 1:]
            jj = pl.program_id(0)

            @pl.when(jj == 0)
            def _():
                dst[...] = jnp.zeros_like(dst)
                dlb_ref[...] = jnp.zeros_like(dlb_ref)

            @pl.when(tile_of(jj) == 0)
            def _():
                for ref in list(dkt_refs) + [dvt_ref]:
                    ref[...] = jnp.zeros_like(ref)

            def step(col0):
                for bh in chains:
                    _, vjp = jax.vjp(chain, s_ref[bh], sq_ref[bh], sz_ref[bh], sv_ref[bh], lb_ref[bh[1]])
                    ds, dq, dz, dv_, dlb = vjp((dst[bh], dso_ref[bh]))
                    dst[bh] = ds
                    dsq_ref[bh] = dq
                    dsz_ref[bh] = dz
                    dsv_ref[bh] = dv_
                    dlb_ref[bh[1]] += dlb
                for b in range(bsz):
                    kts_lo = [r[b, :, col0:] for r in kt_refs]
                    vt_lo = vt_ref[b, :, col0:]
                    for r0 in range(0, tq, sub_bwd):
                        rows = grp * sub_bwd
                        rsl = slice(r0, r0 + sub_bwd)
                        dqs, dkts, dvt = _att_rows_bwd(
                            load_q(q_refs, b, r0, sub_bwd), kts_lo, vt_lo, o_ref[b, :, rsl, :].reshape(rows, dv),
                            lse_ref[b, :, rsl, :].reshape(rows, LANE)[:, 0:1], do_ref[b, :, rsl, :].reshape(rows, dv), scale, fold_scale)
                        for dq_ref, dq, w in zip(dq_refs, dqs, widths):
                            dq_ref[b, :, rsl, :] = dq.reshape(grp, sub_bwd, w)
                        for dkt_ref, dkt in zip(dkt_refs, dkts):
                            dkt_ref[b, :, col0:] += dkt
                        dvt_ref[b, :, col0:] += dvt

            for cond, col0 in ranges:
                pl.when(cond(jj))(functools.partial(step, col0))

        scan_spec = pl.BlockSpec((bsz, nh, c, d), lambda jj: (0, 0, chunk_of(n - 1 - jj), 0))
        st_spec = pl.BlockSpec((bsz, nh, None, d, d), lambda jj: (0, 0, n - 1 - jj, 0, 0))
        in_specs = att_in + [scan_spec] * 3 + [lb_spec, st_spec, per_tile(dv), per_tile(LANE), per_tile(dv), scan_spec]
        d_shape = ([jax.ShapeDtypeStruct(a.shape, F32) for a in att_lo] + [jax.ShapeDtypeStruct(sq.shape, F32)] * 3
                   + [jax.ShapeDtypeStruct(lb.shape, F32)])
        return pl.pallas_call(body, grid=(n,), in_specs=in_specs, out_specs=att_in + [scan_spec] * 3 + [lb_spec], out_shape=d_shape,
                              scratch_shapes=[pltpu.VMEM((bsz, nh, d, d), F32)], name=name + "_bwd",
                              compiler_params=_cparams())(*att_lo, *scan_in, states, o, lse, do, dso)

    n_att = 2 * n_parts + 1

    def lo(arrays):
        return arrays[:n_parts] + tuple(a.astype(BF16) for a in arrays[n_parts:n_att])

    @jax.custom_vjp
    def op(*arrays):
        res = fwd_call(*lo(arrays), *arrays[n_att:])
        return res[0], res[2]

    def op_fwd(*arrays):
        att_lo, scan_in = lo(arrays), arrays[n_att:]
        o, lse, so, states = fwd_call(*att_lo, *scan_in)
        return (o, so), (att_lo, scan_in, o, lse, states)

    def op_bwd(res, cts):
        att_lo, scan_in, o, lse, states = res
        return tuple(bwd_call(att_lo, scan_in, o, lse, states, cts[0], cts[1]))

    op.defvjp(op_fwd, op_bwd)
    return op(*qs, *kts, vt, sq, sz, sv, lb)


def ada_op(cc, w, b):
    depth, d, n = w.shape

    def fn(ids, ccb, wb, bb):
        return (_dot_bf16(_silu(ccb), wb) + bb,)

    args = [Arg(cc, cc.shape, lambda l: (0, 0), "acc", lambda ids: ids[0] == 0),
            Arg(w, (None, d, n), lambda l: (l, 0, 0), "tile"), Arg(b, (None, 1, n), lambda l: (l, 0, 0), "tile")]
    return tile_op("ada", fn, (depth,), args, [Out((depth, cc.shape[0], n), (None, cc.shape[0], n), lambda l: (l, 0, 0))])[0]


def loss_op(xu, target, n_lat_tiles):
    bsz, t, d = xu.shape

    def body(x_ref, t_ref, dx_ref, l_ref):
        b, i = pl.program_id(0), pl.program_id(1)

        @pl.when(jnp.logical_and(b == 0, i == 0))
        def _():
            l_ref[...] = jnp.zeros_like(l_ref)

        @pl.when(i < n_lat_tiles)
        def _():
            err = x_ref[...] - t_ref[...]
            dx_ref[...] = err * (1.0 / d)
            l_ref[...] += 0.5 * jnp.sum(jnp.mean(err * err, axis=-1))

        @pl.when(i >= n_lat_tiles)
        def _():
            dx_ref[...] = jnp.zeros_like(dx_ref)

    row = pl.BlockSpec((None, TOKEN_TILE, d), lambda b, i: (b, i, 0))
    t_spec = pl.BlockSpec((None, TOKEN_TILE, d), lambda b, i: (b, jnp.minimum(i, n_lat_tiles - 1), 0))
    return pl.pallas_call(body, grid=(bsz, t // TOKEN_TILE), in_specs=[row, t_spec],
                          out_specs=[row, pl.BlockSpec((SUBLANE, LANE), lambda b, i: (0, 0))],
                          out_shape=[jax.ShapeDtypeStruct(xu.shape, F32), jax.ShapeDtypeStruct((SUBLANE, LANE), F32)],
                          name="loss", compiler_params=_cparams())(xu, target)


def _row_tile(rows, row_bytes, budget=4 << 20, step=SUBLANE):
    if rows * row_bytes <= budget:
        return rows
    best = None
    for t in range(step, rows, step):
        if rows % t == 0 and t * row_bytes <= budget:
            best = t
    return best if best is not None else rows


def _as3d(x):
    p = x.shape[0]
    c = x.shape[-1]
    return x.reshape(p, -1, c)


def sum_parts(name, x):
    x3 = _as3d(x)
    p, r, c = x3.shape
    tr = _row_tile(r, p * c * 4, step=SUBLANE * (4 // x.dtype.itemsize))

    def body(x_ref, o_ref):
        s = x_ref[0].astype(F32)
        for j in range(1, p):
            s = s + x_ref[j].astype(F32)
        o_ref[...] = s

    out = pl.pallas_call(body, grid=(r // tr,), in_specs=[pl.BlockSpec((p, tr, c), lambda i: (0, i, 0))],
                         out_specs=pl.BlockSpec((tr, c), lambda i: (i, 0)), out_shape=jax.ShapeDtypeStruct((r, c), F32),
                         name=name, compiler_params=_cparams())(x3)
    return out.reshape(x.shape[1:])


def adamw(name, w, g, m, v):
    shape = w.shape
    c = shape[-1]
    to2d = lambda a: a.reshape(-1, c)
    r = to2d(w).shape[0]
    tr = _row_tile(r, 7 * c * 4, budget=6 << 20)

    def body(w_ref, g_ref, m_ref, v_ref, d_ref, nm_ref, nv_ref):
        gg = g_ref[...]
        nm = ADAM_B1 * m_ref[...] + (1.0 - ADAM_B1) * gg
        nv = ADAM_B2 * v_ref[...] + (1.0 - ADAM_B2) * jnp.square(gg)
        m_hat = nm / (1.0 - ADAM_B1 ** ADAM_STEP)
        v_hat = nv / (1.0 - ADAM_B2 ** ADAM_STEP)
        d_ref[...] = -ADAM_LR * (m_hat / (jnp.sqrt(v_hat) + ADAM_EPS) + ADAM_WD * w_ref[...])
        nm_ref[...] = nm
        nv_ref[...] = nv

    spec = pl.BlockSpec((tr, c), lambda i: (i, 0))
    outs = pl.pallas_call(body, grid=(r // tr,), in_specs=[spec] * 4, out_specs=[spec] * 3,
                          out_shape=[jax.ShapeDtypeStruct((r, c), F32)] * 3, name=name, compiler_params=_cparams())(
        to2d(w), to2d(g), to2d(m), to2d(v))
    return tuple(o.reshape(shape) for o in outs)


def exchange(name, srcs, group, same):
    p = 2 ** len(group)
    n = len(srcs)
    out_shape = [jax.ShapeDtypeStruct(((p,) + s.shape) if same else s.shape, s.dtype) for s in srcs]

    def index_of(coords):
        idx = 0
        for a in group:
            idx = idx * 2 + coords[a]
        return idx

    def body(*refs):
        src_refs, out_refs = refs[:n], refs[n:2 * n]
        send_sems, recv_sems = refs[2 * n:]
        pos = {a: lax.axis_index(a) for a in MESH_AXES}
        me = index_of(pos)
        peers = []
        for rel in range(1, p):
            coords = dict(pos)
            for bit, a in enumerate(reversed(group)):
                if (rel >> bit) & 1:
                    coords[a] = 1 - coords[a]
            peers.append((coords, index_of(coords)))

        def src_for(a, idx):
            return src_refs[a] if same else src_refs[a].at[idx]

        sends, recvs = [], []
        for a in range(n):
            for r, (coords, idx) in enumerate(peers):
                dev = tuple(coords[ax] for ax in MESH_AXES)
                send = pltpu.make_async_remote_copy(src_ref=src_for(a, idx), dst_ref=out_refs[a].at[me],
                                                    send_sem=send_sems.at[a, r], recv_sem=recv_sems.at[a, r],
                                                    device_id=dev, device_id_type=pl.DeviceIdType.MESH)
                send.start()
                sends.append(send)
                recvs.append(pltpu.make_async_remote_copy(src_ref=src_for(a, idx), dst_ref=out_refs[a].at[idx],
                                                          send_sem=send_sems.at[a, r], recv_sem=recv_sems.at[a, r],
                                                          device_id=dev, device_id_type=pl.DeviceIdType.MESH))
        for cp in sends:
            cp.wait_send()
        for cp in recvs:
            cp.wait_recv()

    any_spec = pl.BlockSpec(memory_space=pl.ANY)
    outs = pl.pallas_call(body, in_specs=[any_spec] * n, out_specs=[any_spec] * n, out_shape=out_shape,
                          scratch_shapes=[pltpu.SemaphoreType.DMA((n, p - 1)), pltpu.SemaphoreType.DMA((n, p - 1))],
                          name=name)(*srcs)
    me = index_of({a: lax.axis_index(a) for a in MESH_AXES})
    own = [s if same else lax.dynamic_index_in_dim(s, me, axis=0, keepdims=False) for s in srcs]
    return [lax.dynamic_update_index_in_dim(o, w, me, axis=0) for o, w in zip(outs, own)]


def _rope_tables(n_lat, n_ctx, rot_dim, heads):
    n_freq = rot_dim // 4
    tok = jnp.arange(n_lat, dtype=jnp.int32)
    inv = ROPE_THETA ** (-jnp.arange(n_freq, dtype=F32) / n_freq)
    ang = jnp.concatenate([(tok // GRID_W).astype(F32)[:, None] * inv, (tok % GRID_W).astype(F32)[:, None] * inv], axis=-1)
    cos, sin = jnp.cos(ang), jnp.sin(ang)
    cos = jnp.concatenate([jnp.concatenate([cos, cos], -1), jnp.ones((n_ctx, rot_dim), F32)], 0)
    sin = jnp.concatenate([jnp.concatenate([-sin, sin], -1), jnp.zeros((n_ctx, rot_dim), F32)], 0)
    half = rot_dim // 2
    w = heads * rot_dim
    j = np.arange(w)
    src = (j // rot_dim) * rot_dim + (j % rot_dim + half) % rot_dim
    swap = np.zeros((w, w), np.float32)
    swap[src, j] = 1.0
    return jnp.tile(cos, (1, heads)), jnp.tile(sin, (1, heads)), jnp.asarray(swap)


def _to_heads(a, nh):
    b, t, _ = a.shape
    return a.reshape(b, t, nh, -1).transpose(0, 2, 1, 3)


def _to_heads_t(a, nh):
    b, t, _ = a.shape
    return a.reshape(b, t, nh, -1).transpose(0, 2, 3, 1)


def _from_heads(a):
    b, nh, t, w = a.shape
    return a.transpose(0, 2, 1, 3).reshape(b, t, nh * w)


def _pad_w_in(w_in):
    parts, off = [], 0
    for size, pad in zip(IN_SIZES, IN_PAD):
        parts.append(w_in[:, off:off + size])
        if pad > size:
            parts.append(jnp.zeros((w_in.shape[0], pad - size), w_in.dtype))
        off += size
    return jnp.concatenate(parts, axis=1)


def _forward(xu, mods, prm, prm_lo, n_lat, n_ctx):
    bsz, t, d = xu.shape
    depth = mods.shape[0]
    n_lat_tiles = n_lat // TOKEN_TILE
    rope_a = _rope_tables(n_lat, n_ctx, HEAD_DIM, 1)
    rope_bq = _rope_tables(n_lat, n_ctx, B_ROPE, B_HEADS)
    rope_bk = _rope_tables(n_lat, n_ctx, B_ROPE, 1)
    clb = prm["c_lower_bounds"].reshape(depth, 2, C_HEADS, C_DK).transpose(2, 0, 1, 3)
    for l in range(depth):
        nm = lambda s: f"l{l}_{s}"
        vec = lambda name: prm[name][l][None, :]
        xu, h = premod(nm("premix"), xu, vec("g_pre_mix"), mods[l], 0, n_lat_tiles)
        z = linear(nm("w_in"), h.reshape(bsz * t, d), _pad_w_in(prm["w_in"][l]), _pad_w_in(prm_lo["w_in"][l])).reshape(bsz, t, D_IN_PAD)
        seg = lambda i: z[:, :, IN_OFF[i]:IN_OFF[i] + IN_SIZES[i]]
        aq = a_prep(nm("aq"), _to_heads(seg(0), A_HEADS), vec("a_q_norm"), *rope_a)
        ak = a_prep(nm("ak"), _to_heads(seg(1), A_KV_HEADS), vec("a_k_norm"), *rope_a)
        av = _to_heads(seg(2), A_KV_HEADS)
        lower = c_lower(nm("c_lower"), clb, l)
        cq, cv = _to_heads(seg(6), C_HEADS), _to_heads(seg(9), C_HEADS)
        ya, o_b = mix_pair(nm("mix_a"), [aq.reshape(bsz, A_KV_HEADS, A_GROUP, t, HEAD_DIM)], [ak.transpose(0, 1, 3, 2)],
                           av.transpose(0, 1, 3, 2), HEAD_DIM ** -0.5, True, ATT_A_TQ_FWD, ATT_A_TQ, n_lat,
                           cq, _to_heads(seg(8), C_HEADS), cv, lower[:, 1:2], True, n_lat // SCAN_CHUNK)
        ya = _from_heads(ya.reshape(bsz, A_HEADS, t, HEAD_DIM))
        wq = prm["w_q_up"][l].reshape(B_Q_RANK, B_HEADS, B_NOPE + B_ROPE)
        wkv = prm["w_kv_up"][l].reshape(B_KV_RANK, B_HEADS, B_NOPE + B_V)
        qn, qp, kn, bv, kp = b_prep(
            nm("b_prep"), seg(3), seg(4), seg(5), vec("b_q_norm"),
            wq[:, :, :B_NOPE].reshape(B_Q_RANK, -1), wq[:, :, B_NOPE:].reshape(B_Q_RANK, -1), vec("b_kv_norm"),
            wkv[:, :, :B_NOPE].reshape(B_KV_RANK, -1), wkv[:, :, B_NOPE:].reshape(B_KV_RANK, -1), *rope_bq, *rope_bk)
        qb = jnp.concatenate([_to_heads(qn, B_HEADS), _to_heads(qp, B_HEADS)], axis=-1)[:, :, None]
        kbt = jnp.concatenate([_to_heads_t(kn, B_HEADS),
                               jnp.broadcast_to(kp.transpose(0, 2, 1)[:, None], (bsz, B_HEADS, B_ROPE, t))], axis=2)
        yb, o_f = mix_pair(nm("mix_b"), [qb], [kbt], _to_heads_t(bv, B_HEADS), (B_NOPE + B_ROPE) ** -0.5, False, ATT_B_TQ,
                           ATT_B_TQ, n_lat, cq, _to_heads(seg(7), C_HEADS), cv, lower[:, 0:1], False, n_lat // SCAN_CHUNK)
        yb = _from_heads(yb[:, :, 0])
        yc = _from_heads(c_readout(nm("c_out"), o_f, o_b, _to_heads(seg(10), C_HEADS), vec("c_out_norm")))
        y = linear(nm("w_out"), jnp.concatenate([ya, yb, yc], axis=-1).reshape(bsz * t, d), prm["w_out"][l],
                   prm_lo["w_out"][l]).reshape(bsz, t, d)
        x1 = resid(nm("res_mix"), xu, y, vec("g_post_mix"), mods[l], 2, n_lat_tiles)
        x1, h2 = premod(nm("preffn"), x1, vec("g_pre_ffn"), mods[l], 3, n_lat_tiles)
        f = mlp(nm("mlp"), h2.reshape(bsz * t, d), prm["w_ff1"][l], prm["w_ff2"][l], prm_lo["w_ff1"][l],
                prm_lo["w_ff2"][l]).reshape(bsz, t, d)
        xu = resid(nm("res_ffn"), x1, f, vec("g_post_ffn"), mods[l], 5, n_lat_tiles)
    return xu


BIG = {"w_in": 2, "w_q_up": 2, "w_kv_up": 2, "w_out": 1, "w_ff1": 2, "w_ff2": 1}
SMALL = ("g_pre_mix", "g_post_mix", "g_pre_ffn", "g_post_ffn", "a_q_norm", "a_k_norm", "b_q_norm", "b_kv_norm",
         "c_lower_bounds", "c_out_norm")
WEIGHTS = ("c_ctx", "w_ada", "b_ada", "g_pre_mix", "g_post_mix", "g_pre_ffn", "g_post_ffn", "w_in", "a_q_norm", "a_k_norm",
           "b_q_norm", "w_q_up", "b_kv_norm", "w_kv_up", "c_lower_bounds", "c_out_norm", "w_out", "w_ff1", "w_ff2")


def _unshard(g, axis):
    depth, _, r, c = g.shape
    if axis == 1:
        return g.reshape(depth, N_CHIP * r, c)
    return g.transpose(0, 2, 1, 3).reshape(depth, r, N_CHIP * c)


def _shard_major(g, axis):
    depth, r, c = g.shape
    if axis == 1:
        return g.reshape(depth, N_CHIP, r // N_CHIP, c)
    return g.reshape(depth, r, N_CHIP, c // N_CHIP).transpose(0, 2, 1, 3)


def kernel(x, c, ctx, c_ctx, w_ada, b_ada, g_pre_mix, g_post_mix, g_pre_ffn, g_post_ffn, w_in, a_q_norm, a_k_norm, b_q_norm, w_q_up, b_kv_norm, w_kv_up, c_lower_bounds, c_out_norm, w_out, w_ff1, w_ff2, loss_target, m_c_ctx, m_w_ada, m_b_ada, m_g_pre_mix, m_g_post_mix, m_g_pre_ffn, m_g_post_ffn, m_w_in, m_a_q_norm, m_a_k_norm, m_b_q_norm, m_w_q_up, m_b_kv_norm, m_w_kv_up, m_c_lower_bounds, m_c_out_norm, m_w_out, m_w_ff1, m_w_ff2, v_c_ctx, v_w_ada, v_b_ada, v_g_pre_mix, v_g_post_mix, v_g_pre_ffn, v_g_post_ffn, v_w_in, v_a_q_norm, v_a_k_norm, v_b_q_norm, v_w_q_up, v_b_kv_norm, v_w_kv_up, v_c_lower_bounds, v_c_out_norm, v_w_out, v_w_ff1, v_w_ff2):
    local = dict(c_ctx=c_ctx, w_ada=w_ada, b_ada=b_ada, g_pre_mix=g_pre_mix, g_post_mix=g_post_mix, g_pre_ffn=g_pre_ffn,
                 g_post_ffn=g_post_ffn, w_in=w_in, a_q_norm=a_q_norm, a_k_norm=a_k_norm, b_q_norm=b_q_norm, w_q_up=w_q_up,
                 b_kv_norm=b_kv_norm, w_kv_up=w_kv_up, c_lower_bounds=c_lower_bounds, c_out_norm=c_out_norm, w_out=w_out,
                 w_ff1=w_ff1, w_ff2=w_ff2)
    mom = dict(c_ctx=m_c_ctx, w_ada=m_w_ada, b_ada=m_b_ada, g_pre_mix=m_g_pre_mix, g_post_mix=m_g_post_mix,
               g_pre_ffn=m_g_pre_ffn, g_post_ffn=m_g_post_ffn, w_in=m_w_in, a_q_norm=m_a_q_norm, a_k_norm=m_a_k_norm,
               b_q_norm=m_b_q_norm, w_q_up=m_w_q_up, b_kv_norm=m_b_kv_norm, w_kv_up=m_w_kv_up,
               c_lower_bounds=m_c_lower_bounds, c_out_norm=m_c_out_norm, w_out=m_w_out, w_ff1=m_w_ff1, w_ff2=m_w_ff2)
    var = dict(c_ctx=v_c_ctx, w_ada=v_w_ada, b_ada=v_b_ada, g_pre_mix=v_g_pre_mix, g_post_mix=v_g_post_mix,
               g_pre_ffn=v_g_pre_ffn, g_post_ffn=v_g_post_ffn, w_in=v_w_in, a_q_norm=v_a_q_norm, a_k_norm=v_a_k_norm,
               b_q_norm=v_b_q_norm, w_q_up=v_w_q_up, b_kv_norm=v_b_kv_norm, w_kv_up=v_w_kv_up,
               c_lower_bounds=v_c_lower_bounds, c_out_norm=v_c_out_norm, w_out=v_w_out, w_ff1=v_w_ff1, w_ff2=v_w_ff2)

    bsz, n_lat, d = x.shape
    n_ctx = ctx.shape[1]
    depth = w_ada.shape[0]
    assert depth == 2 and n_lat % TOKEN_TILE == 0 and n_ctx % TOKEN_TILE == 0 and bsz * N_DEV + 1 <= ADA_ROWS
    ax, ay, ac = (lax.axis_index(a) for a in MESH_AXES)
    chip = 2 * ax + ay
    dev = 2 * chip + ac

    c_all = exchange("gather_c", [c], MESH_AXES, True)[0].reshape(N_DEV * bsz, d)
    big_names = list(BIG)
    mine = [lax.dynamic_index_in_dim(local[n].astype(BF16), ac, axis=0, keepdims=False) for n in big_names]
    over_chips = exchange("gather_w_quad", mine + [c_lower_bounds], ("x", "y"), True)
    both_layers = exchange("gather_w_pair", over_chips[:-1], ("c",), True)
    prm_lo = {n: _unshard(g, BIG[n]) for n, g in zip(big_names, both_layers)}
    prm = {n: w.astype(F32) for n, w in prm_lo.items()}
    prm["c_lower_bounds"] = over_chips[-1].transpose(1, 2, 0, 3).reshape(depth, 2, -1)
    for n in SMALL:
        if n != "c_lower_bounds":
            prm[n] = local[n]

    n_ada = w_ada.shape[2]
    cc = jnp.concatenate([c_all, c_ctx[None, :], jnp.zeros((ADA_ROWS - N_DEV * bsz - 1, d), F32)], axis=0)
    b_blk = lax.dynamic_slice_in_dim(b_ada, chip * n_ada, n_ada, axis=1)[:, None, :]
    mod_part, ada_vjp = jax.vjp(ada_op, cc, w_ada, b_blk)
    mod_full = exchange("gather_mod", [mod_part], ("x", "y"), True)[0].transpose(1, 2, 0, 3).reshape(depth, ADA_ROWS, 6 * d)
    mod_lat = lax.dynamic_slice_in_dim(mod_full, dev * bsz, bsz, axis=1)
    mod_ctx = jnp.broadcast_to(mod_full[:, N_DEV * bsz:N_DEV * bsz + 1], mod_lat.shape)
    mods = jnp.stack([mod_lat, mod_ctx], axis=2).reshape(depth, bsz, 2, 6, d)

    xu = jnp.concatenate([x, ctx], axis=1)
    x_out, fwd_vjp = jax.vjp(lambda xu_, mods_, prm_: _forward(xu_, mods_, prm_, prm_lo, n_lat, n_ctx), xu, mods, prm)
    dx_out, loss_blk = loss_op(x_out, loss_target, n_lat // TOKEN_TILE)
    d_xu, d_mods, d_prm = fwd_vjp(dx_out)
    grad_x = d_xu[:, :n_lat]

    d_mods = d_mods.reshape(depth, bsz, 2, 6 * d)
    pieces = [d_mods] + [d_prm[n] for n in SMALL] + [loss_blk[0:1, 0:1]]
    sizes = [int(np.prod(p.shape)) for p in pieces]
    flat = jnp.concatenate([p.reshape(-1) for p in pieces])
    n_flat = -(-flat.shape[0] // (SUBLANE * LANE)) * SUBLANE * LANE
    flat = jnp.concatenate([flat, jnp.zeros((n_flat - flat.shape[0],), F32)]).reshape(-1, LANE)
    small_all = exchange("gather_small", [flat], MESH_AXES, True)[0]
    small_sum = sum_parts("sum_small", small_all).reshape(-1)
    offs = np.cumsum([0] + sizes)
    summed = {n: small_sum[offs[i + 1]:offs[i + 2]].reshape(d_prm[n].shape) for i, n in enumerate(SMALL)}
    loss = small_sum[offs[-2]]
    dm_all = small_all.reshape(N_DEV, -1)[:, :sizes[0]].reshape(N_DEV, depth, bsz, 2, 6 * d)
    dm_rows = dm_all.transpose(3, 0, 2, 1, 4).reshape(2, N_DEV * bsz, depth, 6 * d)
    d_ctx_row = sum_parts("sum_dmod_ctx", dm_rows[1])
    grad_b_ada = sum_parts("sum_b_ada", dm_rows.reshape(2 * N_DEV * bsz, depth, 6 * d))
    d_rows = jnp.concatenate([dm_rows[0].transpose(1, 0, 2), d_ctx_row[:, None, :],
                              jnp.zeros((depth, ADA_ROWS - N_DEV * bsz - 1, 6 * d), F32)], axis=1)
    d_cc, grad_w_ada, _ = ada_vjp(lax.dynamic_slice_in_dim(d_rows, chip * n_ada, n_ada, axis=2))
    d_cctx_all = exchange("gather_dcctx", [d_cc[N_DEV * bsz:N_DEV * bsz + 1]], MESH_AXES, True)[0]
    grad_c_ctx = sum_parts("sum_dcctx", d_cctx_all[0::2]).reshape(d)

    shard_major = [_shard_major(d_prm[n], BIG[n]).astype(BF16) for n in big_names]
    pair = exchange("rs_pair", shard_major, ("c",), False)
    chip_sum = [sum_parts(f"rs_sum1_{n}", p) for n, p in zip(big_names, pair)]
    quad = exchange("rs_quad", [s.astype(BF16) for s in chip_sum], ("x", "y"), False)
    total = [sum_parts(f"rs_sum2_{n}", q) for n, q in zip(big_names, quad)]
    both = exchange("rs_share", total, ("c",), True)

    grads = dict(summed)
    grads["c_lower_bounds"] = lax.dynamic_slice_in_dim(summed["c_lower_bounds"], chip * c_lower_bounds.shape[2],
                                                       c_lower_bounds.shape[2], axis=2)
    grads.update(c_ctx=grad_c_ctx, w_ada=grad_w_ada, b_ada=grad_b_ada)
    grads.update({n: g for n, g in zip(big_names, both)})

    deltas, new_m, new_v = {}, {}, {}
    for n in WEIGHTS:
        as2d = (lambda a: a[None, :]) if local[n].ndim == 1 else (lambda a: a)
        dl, nm_, nv_ = adamw("adamw_" + n, as2d(local[n]), as2d(grads[n]), as2d(mom[n]), as2d(var[n]))
        deltas[n], new_m[n], new_v[n] = (a.reshape(local[n].shape) for a in (dl, nm_, nv_))
    return (loss, grad_x, *[grads[n] for n in WEIGHTS], *[deltas[n] for n in WEIGHTS],
            *[new_m[n] for n in WEIGHTS], *[new_v[n] for n in WEIGHTS])
```
